```python
import jax, jax.numpy as jnp
from jax import lax
import numpy as np

D_MODEL = 2048
BATCH = 8
SEQ = 2048
DEPTH = 1

MEM_LEN = 256
HEAD_DIM = 64
RWKV_WIDTH = D_MODEL // 2
RWKV_HEADS = RWKV_WIDTH // HEAD_DIM
DECAY_LORA = max(32, int(round(1.8 * RWKV_WIDTH ** 0.5 / 32)) * 32)
AAA_LORA = max(32, int(round(1.8 * RWKV_WIDTH ** 0.5 / 32)) * 32)
GATE_LORA = max(32, int(round(0.6 * RWKV_WIDTH ** 0.8 / 32)) * 32)
GN_EPS = 64e-5
SWA_WIDTH = D_MODEL - RWKV_WIDTH
SWA_Q_HEADS = SWA_WIDTH // HEAD_DIM
SWA_KV_HEADS = max(1, SWA_Q_HEADS // 8)
SWA_GROUP = SWA_Q_HEADS // SWA_KV_HEADS
WINDOW = 128
BLOCK = 128
ROPE_THETA = 10000.0
SHIFT_COLS = 3 * RWKV_WIDTH + DECAY_LORA + AAA_LORA + GATE_LORA
SWA_COLS = SWA_WIDTH + 2 * SWA_KV_HEADS * HEAD_DIM
IN_COLS = SHIFT_COLS + SWA_COLS
XATTN_HEADS = 4
XATTN_HEAD_DIM = D_MODEL // XATTN_HEADS
D_FF = ((8 * D_MODEL // 3 + 255) // 256) * 256
RMS_EPS = 1e-6
NEG_INF = -1e30

kernel_name = 'hymba_rwkv7_swa_sink_macaron_layer'


def _rmsnorm(x, g):
    xf = x.astype(jnp.float32)
    y = xf * lax.rsqrt(jnp.mean(xf * xf, axis=-1, keepdims=True) + RMS_EPS)
    return (y * g.astype(jnp.float32)).astype(x.dtype)


def _swiglu(h, w_gate, w_up, w_down):
    return (jax.nn.silu(h @ w_gate) * (h @ w_up)) @ w_down


def _rope(t, pos):
    hd = t.shape[-1]
    inv_freq = ROPE_THETA ** (-jnp.arange(0, hd, 2, dtype=jnp.float32) / hd)
    ang = pos[:, None] * inv_freq[None, :]
    cos = jnp.cos(ang)[None, :, None, :]
    sin = jnp.sin(ang)[None, :, None, :]
    tf = t.astype(jnp.float32)
    t1, t2 = tf[..., : hd // 2], tf[..., hd // 2:]
    return jnp.concatenate([t1 * cos - t2 * sin, t2 * cos + t1 * sin], axis=-1).astype(t.dtype)


def _rwkv7_time_mix(z, w0, decay_up, a0, aaa_up, gate_up, k_k, k_a, r_k, lnx_w, lnx_b):
    out_dtype = z.dtype
    f32 = jnp.float32
    z = z.astype(f32)
    B, T, _ = z.shape
    C, H, N = RWKV_WIDTH, RWKV_HEADS, HEAD_DIM
    cuts = [C, 2 * C, 3 * C, 3 * C + DECAY_LORA, 3 * C + DECAY_LORA + AAA_LORA]
    r, k, v, wd, ad, gd = jnp.split(z, cuts, axis=-1)
    w = -jax.nn.softplus(-(w0.astype(f32) + jnp.tanh(wd) @ decay_up.astype(f32))) - 0.5
    a = jax.nn.sigmoid(a0.astype(f32) + ad @ aaa_up.astype(f32))
    g = jax.nn.sigmoid(gd) @ gate_up.astype(f32)
    kk = (k * k_k.astype(f32)).reshape(B, T, H, N)
    kk = kk / jnp.maximum(jnp.sqrt(jnp.sum(kk * kk, axis=-1, keepdims=True)), 1e-12)
    k = k * (1.0 + (a - 1.0) * k_a.astype(f32))
    heads = lambda t: t.reshape(B, T, H, N)
    r_h, k_h, v_h, a_h = heads(r), heads(k), heads(v), heads(a)
    decay = jnp.exp(-jnp.exp(heads(w)))
    tm = lambda t: jnp.swapaxes(t, 0, 1)
    seq_in = (tm(r_h), tm(decay), tm(k_h), tm(v_h), tm(-kk), tm(kk * a_h))

    def step(S, inp):
        r_t, w_t, k_t, v_t, a_t, b_t = inp
        sa = jnp.einsum('bhij,bhj->bhi', S, a_t)
        S = S * w_t[:, :, None, :] + sa[..., None] * b_t[:, :, None, :] + v_t[..., None] * k_t[:, :, None, :]
        return S, jnp.einsum('bhij,bhj->bhi', S, r_t)

    S0 = jnp.zeros((B, H, N, N), f32)
    _, y = lax.scan(step, S0, seq_in)
    y = tm(y)
    mu = jnp.mean(y, axis=-1, keepdims=True)
    var = jnp.mean(jnp.square(y - mu), axis=-1, keepdims=True)
    y = (y - mu) * lax.rsqrt(var + GN_EPS) * lnx_w.astype(f32).reshape(H, N) + lnx_b.astype(f32).reshape(H, N)
    y = y + jnp.sum(r_h * k_h * r_k.astype(f32), axis=-1, keepdims=True) * v_h
    return (y.reshape(B, T, C) * g).astype(out_dtype)


def _swa_gqa_sinks(q, k, v, sinks):
    B, T, _, hd = q.shape
    nb = T // BLOCK
    qb = q.reshape(B, nb, BLOCK, SWA_KV_HEADS, SWA_GROUP, hd)

    def band(t):
        tb = t.reshape(B, nb, BLOCK, SWA_KV_HEADS, hd)
        prev = jnp.pad(tb[:, :-1], ((0, 0), (1, 0), (0, 0), (0, 0), (0, 0)))
        return jnp.concatenate([prev, tb], axis=2)

    kb, vb = band(k), band(v)
    s = jnp.einsum('bnqhgd,bnkhd->bnhgqk', qb, kb).astype(jnp.float32) * (hd ** -0.5)
    blk = jnp.arange(nb)[:, None]
    qpos = blk * BLOCK + jnp.arange(BLOCK)[None, :]
    kpos = (blk - 1) * BLOCK + jnp.arange(2 * BLOCK)[None, :]
    diff = qpos[:, :, None] - kpos[:, None, :]
    valid = (diff >= 0) & (diff < WINDOW) & (kpos[:, None, :] >= 0)
    s = jnp.where(valid[None, :, None, None], s, NEG_INF)
    sink = jnp.broadcast_to(sinks.astype(jnp.float32).reshape(SWA_KV_HEADS, SWA_GROUP)[None, None, :, :, None, None],
                            s.shape[:-1] + (1,))
    p = jax.nn.softmax(jnp.concatenate([s, sink], axis=-1), axis=-1)[..., :-1]
    o = jnp.einsum('bnhgqk,bnkhd->bnqhgd', p.astype(v.dtype), vb)
    return o.reshape(B, T, SWA_Q_HEADS * hd)


def _memory_cross_attn(h, mem_n, w_xq, w_xkv, w_xo):
    B, T, _ = h.shape
    M = mem_n.shape[1]
    q = (h @ w_xq).reshape(B, T, XATTN_HEADS, XATTN_HEAD_DIM)
    k, v = jnp.split(mem_n @ w_xkv, 2, axis=-1)
    k = k.reshape(B, M, XATTN_HEADS, XATTN_HEAD_DIM)
    v = v.reshape(B, M, XATTN_HEADS, XATTN_HEAD_DIM)
    s = jnp.einsum('bthd,bmhd->bhtm', q, k).astype(jnp.float32) * (XATTN_HEAD_DIM ** -0.5)
    p = jax.nn.softmax(s, axis=-1).astype(v.dtype)
    o = jnp.einsum('bhtm,bmhd->bthd', p, v).reshape(B, T, D_MODEL)
    return o @ w_xo


def _fwd_setup_inputs(seed: int = 0) -> dict:
    key = jax.random.key(seed)
    ks = iter(jax.random.split(key, 40))
    f32 = jnp.float32
    L, D, C = DEPTH, D_MODEL, RWKV_WIDTH
    nrm = lambda shape, scale: jax.random.normal(next(ks), shape, f32) * scale
    uni = lambda shape, lo, hi: jax.random.uniform(next(ks), shape, f32, lo, hi)
    gain = lambda shape: 1.0 + nrm(shape, 0.02)
    return {
        'x': nrm((BATCH, SEQ, D), 1.0),
        'mem': nrm((BATCH, MEM_LEN, D), 1.0),
        'f1_norm': gain((L, D)),
        'f1_gate': nrm((L, D, D_FF), D ** -0.5),
        'f1_up': nrm((L, D, D_FF), D ** -0.5),
        'f1_down': nrm((L, D_FF, D), D_FF ** -0.5),
        'mix_norm': gain((L, D)),
        'w_in': nrm((L, D, IN_COLS), D ** -0.5),
        'b_in_attn': nrm((L, SWA_COLS), 0.02),
        'rw_mu': uni((L, SHIFT_COLS), 0.0, 1.0),
        'rw_w0': uni((L, C), -6.0, -1.0),
        'rw_decay_up': nrm((L, DECAY_LORA, C), 0.1),
        'rw_a0': nrm((L, C), 0.1),
        'rw_aaa_up': nrm((L, AAA_LORA, C), 0.5 * AAA_LORA ** -0.5),
        'rw_gate_up': nrm((L, GATE_LORA, C), GATE_LORA ** -0.5),
        'rw_k_k': 0.85 + nrm((L, C), 0.02),
        'rw_k_a': 1.0 + nrm((L, C), 0.02),
        'rw_r_k': nrm((L, RWKV_HEADS, HEAD_DIM), 0.1),
        'rw_lnx_w': gain((L, C)),
        'rw_lnx_b': nrm((L, C), 0.02),
        'attn_sinks': nrm((L, SWA_Q_HEADS), 0.5),
        'w_out': nrm((L, D, D), D ** -0.5),
        'b_out': nrm((L, D), 0.02),
        'xa_norm': gain((L, D)),
        'mem_norm': gain((L, D)),
        'w_xq': nrm((L, D, D), D ** -0.5),
        'w_xkv': nrm((L, D, 2 * D), D ** -0.5),
        'w_xo': nrm((L, D, D), D ** -0.5),
        'f2_norm': gain((L, D)),
        'f2_gate': nrm((L, D, D_FF), D ** -0.5),
        'f2_up': nrm((L, D, D_FF), D ** -0.5),
        'f2_down': nrm((L, D_FF, D), D_FF ** -0.5),
        'final_norm': gain((D,)),
    }


def _fwd_reference(x, mem, f1_norm, f1_gate, f1_up, f1_down, mix_norm, w_in, b_in_attn, rw_mu, rw_w0,
              rw_decay_up, rw_a0, rw_aaa_up, rw_gate_up, rw_k_k, rw_k_a, rw_r_k, rw_lnx_w, rw_lnx_b,
              attn_sinks, w_out, b_out, xa_norm, mem_norm, w_xq, w_xkv, w_xo, f2_norm, f2_gate, f2_up,
              f2_down, final_norm):
    B, T, _ = x.shape
    pos = jnp.arange(T, dtype=jnp.float32)
    kv_w = SWA_KV_HEADS * HEAD_DIM
    for l in range(DEPTH):
        x = x + 0.5 * _swiglu(_rmsnorm(x, f1_norm[l]), f1_gate[l], f1_up[l], f1_down[l])
        h = _rmsnorm(x, mix_norm[l])
        proj = h @ w_in[l]
        zr = proj[..., :SHIFT_COLS]
        zr_prev = jnp.pad(zr[:, :-1], ((0, 0), (1, 0), (0, 0)))
        zr = zr + (zr_prev - zr) * rw_mu[l]
        za = proj[..., SHIFT_COLS:] + b_in_attn[l]
        q = za[..., :SWA_WIDTH].reshape(B, T, SWA_Q_HEADS, HEAD_DIM)
        k = za[..., SWA_WIDTH:SWA_WIDTH + kv_w].reshape(B, T, SWA_KV_HEADS, HEAD_DIM)
        v = za[..., SWA_WIDTH + kv_w:].reshape(B, T, SWA_KV_HEADS, HEAD_DIM)
        y_rwkv = _rwkv7_time_mix(zr, rw_w0[l], rw_decay_up[l], rw_a0[l], rw_aaa_up[l], rw_gate_up[l],
                                 rw_k_k[l], rw_k_a[l], rw_r_k[l], rw_lnx_w[l], rw_lnx_b[l])
        y_swa = _swa_gqa_sinks(_rope(q, pos), _rope(k, pos), v, attn_sinks[l])
        x = x + jnp.concatenate([y_rwkv, y_swa], axis=-1) @ w_out[l] + b_out[l]
        x = x + _memory_cross_attn(_rmsnorm(x, xa_norm[l]), _rmsnorm(mem, mem_norm[l]), w_xq[l], w_xkv[l], w_xo[l])
        x = x + 0.5 * _swiglu(_rmsnorm(x, f2_norm[l]), f2_gate[l], f2_up[l], f2_down[l])
    return _rmsnorm(x, final_norm)


import jax as _jax
import jax.numpy as _jnp

TWIN_FORMAT = 'train_step'
FWD_PARAMS = ['x', 'mem', 'f1_norm', 'f1_gate', 'f1_up', 'f1_down', 'mix_norm', 'w_in', 'b_in_attn', 'rw_mu', 'rw_w0', 'rw_decay_up', 'rw_a0', 'rw_aaa_up', 'rw_gate_up', 'rw_k_k', 'rw_k_a', 'rw_r_k', 'rw_lnx_w', 'rw_lnx_b', 'attn_sinks', 'w_out', 'b_out', 'xa_norm', 'mem_norm', 'w_xq', 'w_xkv', 'w_xo', 'f2_norm', 'f2_gate', 'f2_up', 'f2_down', 'final_norm']
TWIN_WEIGHTS = ['f1_norm', 'f1_gate', 'f1_up', 'f1_down', 'mix_norm', 'w_in', 'b_in_attn', 'rw_mu', 'rw_w0', 'rw_decay_up', 'rw_a0', 'rw_aaa_up', 'rw_gate_up', 'rw_k_k', 'rw_k_a', 'rw_r_k', 'rw_lnx_w', 'rw_lnx_b', 'attn_sinks', 'w_out', 'b_out', 'xa_norm', 'mem_norm', 'w_xq', 'w_xkv', 'w_xo', 'f2_norm', 'f2_gate', 'f2_up', 'f2_down', 'final_norm']
TWIN_DIFF_INPUT = 'x'
TWIN_INPUTS = ['x', 'mem', 'f1_norm', 'f1_gate', 'f1_up', 'f1_down', 'mix_norm', 'w_in', 'b_in_attn', 'rw_mu', 'rw_w0', 'rw_decay_up', 'rw_a0', 'rw_aaa_up', 'rw_gate_up', 'rw_k_k', 'rw_k_a', 'rw_r_k', 'rw_lnx_w', 'rw_lnx_b', 'attn_sinks', 'w_out', 'b_out', 'xa_norm', 'mem_norm', 'w_xq', 'w_xkv', 'w_xo', 'f2_norm', 'f2_gate', 'f2_up', 'f2_down', 'final_norm', 'loss_target', 'm_f1_norm', 'm_f1_gate', 'm_f1_up', 'm_f1_down', 'm_mix_norm', 'm_w_in', 'm_b_in_attn', 'm_rw_mu', 'm_rw_w0', 'm_rw_decay_up', 'm_rw_a0', 'm_rw_aaa_up', 'm_rw_gate_up', 'm_rw_k_k', 'm_rw_k_a', 'm_rw_r_k', 'm_rw_lnx_w', 'm_rw_lnx_b', 'm_attn_sinks', 'm_w_out', 'm_b_out', 'm_xa_norm', 'm_mem_norm', 'm_w_xq', 'm_w_xkv', 'm_w_xo', 'm_f2_norm', 'm_f2_gate', 'm_f2_up', 'm_f2_down', 'm_final_norm', 'v_f1_norm', 'v_f1_gate', 'v_f1_up', 'v_f1_down', 'v_mix_norm', 'v_w_in', 'v_b_in_attn', 'v_rw_mu', 'v_rw_w0', 'v_rw_decay_up', 'v_rw_a0', 'v_rw_aaa_up', 'v_rw_gate_up', 'v_rw_k_k', 'v_rw_k_a', 'v_rw_r_k', 'v_rw_lnx_w', 'v_rw_lnx_b', 'v_attn_sinks', 'v_w_out', 'v_b_out', 'v_xa_norm', 'v_mem_norm', 'v_w_xq', 'v_w_xkv', 'v_w_xo', 'v_f2_norm', 'v_f2_gate', 'v_f2_up', 'v_f2_down', 'v_final_norm']
TWIN_OUTPUTS = ['loss', 'grad_x', 'grad_f1_norm', 'grad_f1_gate', 'grad_f1_up', 'grad_f1_down', 'grad_mix_norm', 'grad_w_in', 'grad_b_in_attn', 'grad_rw_mu', 'grad_rw_w0', 'grad_rw_decay_up', 'grad_rw_a0', 'grad_rw_aaa_up', 'grad_rw_gate_up', 'grad_rw_k_k', 'grad_rw_k_a', 'grad_rw_r_k', 'grad_rw_lnx_w', 'grad_rw_lnx_b', 'grad_attn_sinks', 'grad_w_out', 'grad_b_out', 'grad_xa_norm', 'grad_mem_norm', 'grad_w_xq', 'grad_w_xkv', 'grad_w_xo', 'grad_f2_norm', 'grad_f2_gate', 'grad_f2_up', 'grad_f2_down', 'grad_final_norm', 'delta_f1_norm', 'delta_f1_gate', 'delta_f1_up', 'delta_f1_down', 'delta_mix_norm', 'delta_w_in', 'delta_b_in_attn', 'delta_rw_mu', 'delta_rw_w0', 'delta_rw_decay_up', 'delta_rw_a0', 'delta_rw_aaa_up', 'delta_rw_gate_up', 'delta_rw_k_k', 'delta_rw_k_a', 'delta_rw_r_k', 'delta_rw_lnx_w', 'delta_rw_lnx_b', 'delta_attn_sinks', 'delta_w_out', 'delta_b_out', 'delta_xa_norm', 'delta_mem_norm', 'delta_w_xq', 'delta_w_xkv', 'delta_w_xo', 'delta_f2_norm', 'delta_f2_gate', 'delta_f2_up', 'delta_f2_down', 'delta_final_norm', 'new_m_f1_norm', 'new_m_f1_gate', 'new_m_f1_up', 'new_m_f1_down', 'new_m_mix_norm', 'new_m_w_in', 'new_m_b_in_attn', 'new_m_rw_mu', 'new_m_rw_w0', 'new_m_rw_decay_up', 'new_m_rw_a0', 'new_m_rw_aaa_up', 'new_m_rw_gate_up', 'new_m_rw_k_k', 'new_m_rw_k_a', 'new_m_rw_r_k', 'new_m_rw_lnx_w', 'new_m_rw_lnx_b', 'new_m_attn_sinks', 'new_m_w_out', 'new_m_b_out', 'new_m_xa_norm', 'new_m_mem_norm', 'new_m_w_xq', 'new_m_w_xkv', 'new_m_w_xo', 'new_m_f2_norm', 'new_m_f2_gate', 'new_m_f2_up', 'new_m_f2_down', 'new_m_final_norm', 'new_v_f1_norm', 'new_v_f1_gate', 'new_v_f1_up', 'new_v_f1_down', 'new_v_mix_norm', 'new_v_w_in', 'new_v_b_in_attn', 'new_v_rw_mu', 'new_v_rw_w0', 'new_v_rw_decay_up', 'new_v_rw_a0', 'new_v_rw_aaa_up', 'new_v_rw_gate_up', 'new_v_rw_k_k', 'new_v_rw_k_a', 'new_v_rw_r_k', 'new_v_rw_lnx_w', 'new_v_rw_lnx_b', 'new_v_attn_sinks', 'new_v_w_out', 'new_v_b_out', 'new_v_xa_norm', 'new_v_mem_norm', 'new_v_w_xq', 'new_v_w_xkv', 'new_v_w_xo', 'new_v_f2_norm', 'new_v_f2_gate', 'new_v_f2_up', 'new_v_f2_down', 'new_v_final_norm']
TWIN_LEAF_KINDS = {'loss': 'loss', 'grad_x': 'grad_x', 'grad_f1_norm': 'grad_w', 'grad_f1_gate': 'grad_w', 'grad_f1_up': 'grad_w', 'grad_f1_down': 'grad_w', 'grad_mix_norm': 'grad_w', 'grad_w_in': 'grad_w', 'grad_b_in_attn': 'grad_w', 'grad_rw_mu': 'grad_w', 'grad_rw_w0': 'grad_w', 'grad_rw_decay_up': 'grad_w', 'grad_rw_a0': 'grad_w', 'grad_rw_aaa_up': 'grad_w', 'grad_rw_gate_up': 'grad_w', 'grad_rw_k_k': 'grad_w', 'grad_rw_k_a': 'grad_w', 'grad_rw_r_k': 'grad_w', 'grad_rw_lnx_w': 'grad_w', 'grad_rw_lnx_b': 'grad_w', 'grad_attn_sinks': 'grad_w', 'grad_w_out': 'grad_w', 'grad_b_out': 'grad_w', 'grad_xa_norm': 'grad_w', 'grad_mem_norm': 'grad_w', 'grad_w_xq': 'grad_w', 'grad_w_xkv': 'grad_w', 'grad_w_xo': 'grad_w', 'grad_f2_norm': 'grad_w', 'grad_f2_gate': 'grad_w', 'grad_f2_up': 'grad_w', 'grad_f2_down': 'grad_w', 'grad_final_norm': 'grad_w', 'delta_f1_norm': 'delta_w', 'delta_f1_gate': 'delta_w', 'delta_f1_up': 'delta_w', 'delta_f1_down': 'delta_w', 'delta_mix_norm': 'delta_w', 'delta_w_in': 'delta_w', 'delta_b_in_attn': 'delta_w', 'delta_rw_mu': 'delta_w', 'delta_rw_w0': 'delta_w', 'delta_rw_decay_up': 'delta_w', 'delta_rw_a0': 'delta_w', 'delta_rw_aaa_up': 'delta_w', 'delta_rw_gate_up': 'delta_w', 'delta_rw_k_k': 'delta_w', 'delta_rw_k_a': 'delta_w', 'delta_rw_r_k': 'delta_w', 'delta_rw_lnx_w': 'delta_w', 'delta_rw_lnx_b': 'delta_w', 'delta_attn_sinks': 'delta_w', 'delta_w_out': 'delta_w', 'delta_b_out': 'delta_w', 'delta_xa_norm': 'delta_w', 'delta_mem_norm': 'delta_w', 'delta_w_xq': 'delta_w', 'delta_w_xkv': 'delta_w', 'delta_w_xo': 'delta_w', 'delta_f2_norm': 'delta_w', 'delta_f2_gate': 'delta_w', 'delta_f2_up': 'delta_w', 'delta_f2_down': 'delta_w', 'delta_final_norm': 'delta_w', 'new_m_f1_norm': 'new_m', 'new_m_f1_gate': 'new_m', 'new_m_f1_up': 'new_m', 'new_m_f1_down': 'new_m', 'new_m_mix_norm': 'new_m', 'new_m_w_in': 'new_m', 'new_m_b_in_attn': 'new_m', 'new_m_rw_mu': 'new_m', 'new_m_rw_w0': 'new_m', 'new_m_rw_decay_up': 'new_m', 'new_m_rw_a0': 'new_m', 'new_m_rw_aaa_up': 'new_m', 'new_m_rw_gate_up': 'new_m', 'new_m_rw_k_k': 'new_m', 'new_m_rw_k_a': 'new_m', 'new_m_rw_r_k': 'new_m', 'new_m_rw_lnx_w': 'new_m', 'new_m_rw_lnx_b': 'new_m', 'new_m_attn_sinks': 'new_m', 'new_m_w_out': 'new_m', 'new_m_b_out': 'new_m', 'new_m_xa_norm': 'new_m', 'new_m_mem_norm': 'new_m', 'new_m_w_xq': 'new_m', 'new_m_w_xkv': 'new_m', 'new_m_w_xo': 'new_m', 'new_m_f2_norm': 'new_m', 'new_m_f2_gate': 'new_m', 'new_m_f2_up': 'new_m', 'new_m_f2_down': 'new_m', 'new_m_final_norm': 'new_m', 'new_v_f1_norm': 'new_v', 'new_v_f1_gate': 'new_v', 'new_v_f1_up': 'new_v', 'new_v_f1_down': 'new_v', 'new_v_mix_norm': 'new_v', 'new_v_w_in': 'new_v', 'new_v_b_in_attn': 'new_v', 'new_v_rw_mu': 'new_v', 'new_v_rw_w0': 'new_v', 'new_v_rw_decay_up': 'new_v', 'new_v_rw_a0': 'new_v', 'new_v_rw_aaa_up': 'new_v', 'new_v_rw_gate_up': 'new_v', 'new_v_rw_k_k': 'new_v', 'new_v_rw_k_a': 'new_v', 'new_v_rw_r_k': 'new_v', 'new_v_rw_lnx_w': 'new_v', 'new_v_rw_lnx_b': 'new_v', 'new_v_attn_sinks': 'new_v', 'new_v_w_out': 'new_v', 'new_v_b_out': 'new_v', 'new_v_xa_norm': 'new_v', 'new_v_mem_norm': 'new_v', 'new_v_w_xq': 'new_v', 'new_v_w_xkv': 'new_v', 'new_v_w_xo': 'new_v', 'new_v_f2_norm': 'new_v', 'new_v_f2_gate': 'new_v', 'new_v_f2_up': 'new_v', 'new_v_f2_down': 'new_v', 'new_v_final_norm': 'new_v'}


def _forward(args):
    return _fwd_reference(*[args[k] for k in FWD_PARAMS])


def _output_shape():
    out = _jax.eval_shape(lambda: _forward(_fwd_setup_inputs(0)))
    return out.shape, out.dtype

N_MICROBATCH = 1
ADAM_LR = 0.001
ADAM_B1 = 0.9
ADAM_B2 = 0.999
ADAM_EPS = 1e-08
ADAM_WD = 0.01
ADAM_STEP = 10
PER_EXAMPLE_BATCH_AXIS = {'x': 0, 'mem': 0, 'loss_target': 0}
SHARED_INPUTS = []
_WEIGHT_DTYPES = {'f1_norm': _jnp.float32, 'f1_gate': _jnp.float32, 'f1_up': _jnp.float32, 'f1_down': _jnp.float32, 'mix_norm': _jnp.float32, 'w_in': _jnp.float32, 'b_in_attn': _jnp.float32, 'rw_mu': _jnp.float32, 'rw_w0': _jnp.float32, 'rw_decay_up': _jnp.float32, 'rw_a0': _jnp.float32, 'rw_aaa_up': _jnp.float32, 'rw_gate_up': _jnp.float32, 'rw_k_k': _jnp.float32, 'rw_k_a': _jnp.float32, 'rw_r_k': _jnp.float32, 'rw_lnx_w': _jnp.float32, 'rw_lnx_b': _jnp.float32, 'attn_sinks': _jnp.float32, 'w_out': _jnp.float32, 'b_out': _jnp.float32, 'xa_norm': _jnp.float32, 'mem_norm': _jnp.float32, 'w_xq': _jnp.float32, 'w_xkv': _jnp.float32, 'w_xo': _jnp.float32, 'f2_norm': _jnp.float32, 'f2_gate': _jnp.float32, 'f2_up': _jnp.float32, 'f2_down': _jnp.float32, 'final_norm': _jnp.float32}
MOMENT_SCALE = {'f1_norm': 3.102041e-02, 'f1_gate': 1.320933e-02, 'f1_up': 1.279748e-02, 'f1_down': 2.121588e-02, 'mix_norm': 4.316997e-02, 'w_in': 2.907174e-02, 'b_in_attn': 6.647655e-02, 'rw_mu': 5.311618e-02, 'rw_w0': 1.399248e-02, 'rw_decay_up': 1.537321e-03, 'rw_a0': 1.298560e-02, 'rw_aaa_up': 1.266374e-02, 'rw_gate_up': 3.290658e-02, 'rw_k_k': 3.575679e-02, 'rw_k_a': 3.448419e-02, 'rw_r_k': 6.855211e-02, 'rw_lnx_w': 3.136835e-02, 'rw_lnx_b': 3.199370e-02, 'attn_sinks': 1.039261e-02, 'w_out': 2.372268e-02, 'b_out': 6.068353e-02, 'xa_norm': 5.971867e-03, 'mem_norm': 8.601782e-03, 'w_xq': 5.916124e-03, 'w_xkv': 5.964428e-03, 'w_xo': 5.987472e-03, 'f2_norm': 2.293826e-02, 'f2_gate': 1.000080e-02, 'f2_up': 9.705854e-03, 'f2_down': 1.607341e-02, 'final_norm': 7.996605e+00}


def _to_microbatches(a, axis):
    t = _jnp.moveaxis(a, axis, 0)
    t = t.reshape((N_MICROBATCH, t.shape[0] // N_MICROBATCH) + t.shape[1:])
    return _jnp.moveaxis(t, 1, axis + 1)


def setup_inputs(seed: int = 0) -> dict:
    inp = _fwd_setup_inputs(seed)
    key = _jax.random.fold_in(_jax.random.key(seed), 7919)
    shape, _ = _output_shape()
    out = dict(inp)
    out["loss_target"] = _jax.random.normal(_jax.random.fold_in(key, 0), shape, _jnp.float32)
    for i, name in enumerate(TWIN_WEIGHTS):
        w = inp[name].astype(_jnp.float32)
        if MOMENT_SCALE is None:
            s = _jnp.sqrt(_jnp.mean(_jnp.square(w)) + 1e-30)
        else:
            s = MOMENT_SCALE[name]
        km, kv = _jax.random.split(_jax.random.fold_in(key, i + 1))
        out[name] = w
        out["m_" + name] = s * _jax.random.normal(km, w.shape, _jnp.float32)
        out["v_" + name] = (s * s) * _jax.random.uniform(kv, w.shape, _jnp.float32, 0.5, 1.5)
    if N_MICROBATCH > 1:
        for name, axis in PER_EXAMPLE_BATCH_AXIS.items():
            out[name] = _to_microbatches(out[name], axis)
    return {'x': out['x'], 'mem': out['mem'], 'f1_norm': out['f1_norm'], 'f1_gate': out['f1_gate'], 'f1_up': out['f1_up'], 'f1_down': out['f1_down'], 'mix_norm': out['mix_norm'], 'w_in': out['w_in'], 'b_in_attn': out['b_in_attn'], 'rw_mu': out['rw_mu'], 'rw_w0': out['rw_w0'], 'rw_decay_up': out['rw_decay_up'], 'rw_a0': out['rw_a0'], 'rw_aaa_up': out['rw_aaa_up'], 'rw_gate_up': out['rw_gate_up'], 'rw_k_k': out['rw_k_k'], 'rw_k_a': out['rw_k_a'], 'rw_r_k': out['rw_r_k'], 'rw_lnx_w': out['rw_lnx_w'], 'rw_lnx_b': out['rw_lnx_b'], 'attn_sinks': out['attn_sinks'], 'w_out': out['w_out'], 'b_out': out['b_out'], 'xa_norm': out['xa_norm'], 'mem_norm': out['mem_norm'], 'w_xq': out['w_xq'], 'w_xkv': out['w_xkv'], 'w_xo': out['w_xo'], 'f2_norm': out['f2_norm'], 'f2_gate': out['f2_gate'], 'f2_up': out['f2_up'], 'f2_down': out['f2_down'], 'final_norm': out['final_norm'], 'loss_target': out['loss_target'], 'm_f1_norm': out['m_f1_norm'], 'm_f1_gate': out['m_f1_gate'], 'm_f1_up': out['m_f1_up'], 'm_f1_down': out['m_f1_down'], 'm_mix_norm': out['m_mix_norm'], 'm_w_in': out['m_w_in'], 'm_b_in_attn': out['m_b_in_attn'], 'm_rw_mu': out['m_rw_mu'], 'm_rw_w0': out['m_rw_w0'], 'm_rw_decay_up': out['m_rw_decay_up'], 'm_rw_a0': out['m_rw_a0'], 'm_rw_aaa_up': out['m_rw_aaa_up'], 'm_rw_gate_up': out['m_rw_gate_up'], 'm_rw_k_k': out['m_rw_k_k'], 'm_rw_k_a': out['m_rw_k_a'], 'm_rw_r_k': out['m_rw_r_k'], 'm_rw_lnx_w': out['m_rw_lnx_w'], 'm_rw_lnx_b': out['m_rw_lnx_b'], 'm_attn_sinks': out['m_attn_sinks'], 'm_w_out': out['m_w_out'], 'm_b_out': out['m_b_out'], 'm_xa_norm': out['m_xa_norm'], 'm_mem_norm': out['m_mem_norm'], 'm_w_xq': out['m_w_xq'], 'm_w_xkv': out['m_w_xkv'], 'm_w_xo': out['m_w_xo'], 'm_f2_norm': out['m_f2_norm'], 'm_f2_gate': out['m_f2_gate'], 'm_f2_up': out['m_f2_up'], 'm_f2_down': out['m_f2_down'], 'm_final_norm': out['m_final_norm'], 'v_f1_norm': out['v_f1_norm'], 'v_f1_gate': out['v_f1_gate'], 'v_f1_up': out['v_f1_up'], 'v_f1_down': out['v_f1_down'], 'v_mix_norm': out['v_mix_norm'], 'v_w_in': out['v_w_in'], 'v_b_in_attn': out['v_b_in_attn'], 'v_rw_mu': out['v_rw_mu'], 'v_rw_w0': out['v_rw_w0'], 'v_rw_decay_up': out['v_rw_decay_up'], 'v_rw_a0': out['v_rw_a0'], 'v_rw_aaa_up': out['v_rw_aaa_up'], 'v_rw_gate_up': out['v_rw_gate_up'], 'v_rw_k_k': out['v_rw_k_k'], 'v_rw_k_a': out['v_rw_k_a'], 'v_rw_r_k': out['v_rw_r_k'], 'v_rw_lnx_w': out['v_rw_lnx_w'], 'v_rw_lnx_b': out['v_rw_lnx_b'], 'v_attn_sinks': out['v_attn_sinks'], 'v_w_out': out['v_w_out'], 'v_b_out': out['v_b_out'], 'v_xa_norm': out['v_xa_norm'], 'v_mem_norm': out['v_mem_norm'], 'v_w_xq': out['v_w_xq'], 'v_w_xkv': out['v_w_xkv'], 'v_w_xo': out['v_w_xo'], 'v_f2_norm': out['v_f2_norm'], 'v_f2_gate': out['v_f2_gate'], 'v_f2_up': out['v_f2_up'], 'v_f2_down': out['v_f2_down'], 'v_final_norm': out['v_final_norm']}


def _loss(weights, diff, rest, loss_target):
    with _jax.named_scope("forward"):
        args = {**rest, TWIN_DIFF_INPUT: diff, **{k: w.astype(_WEIGHT_DTYPES[k]) for k, w in weights.items()}}
        y = _forward(args)
    with _jax.named_scope("loss_head"):
        err = _jnp.square(y.astype(_jnp.float32) - loss_target)
        return 0.5 * _jnp.sum(_jnp.mean(err, axis=-1)) if err.ndim else 0.5 * err


def _adamw(w, g, m, v):
    m = ADAM_B1 * m + (1.0 - ADAM_B1) * g
    v = ADAM_B2 * v + (1.0 - ADAM_B2) * _jnp.square(g)
    m_hat = m / (1.0 - ADAM_B1 ** ADAM_STEP)
    v_hat = v / (1.0 - ADAM_B2 ** ADAM_STEP)
    delta = -ADAM_LR * (m_hat / (_jnp.sqrt(v_hat) + ADAM_EPS) + ADAM_WD * w)
    return delta, m, v


def reference(x, mem, f1_norm, f1_gate, f1_up, f1_down, mix_norm, w_in, b_in_attn, rw_mu, rw_w0, rw_decay_up, rw_a0, rw_aaa_up, rw_gate_up, rw_k_k, rw_k_a, rw_r_k, rw_lnx_w, rw_lnx_b, attn_sinks, w_out, b_out, xa_norm, mem_norm, w_xq, w_xkv, w_xo, f2_norm, f2_gate, f2_up, f2_down, final_norm, loss_target, m_f1_norm, m_f1_gate, m_f1_up, m_f1_down, m_mix_norm, m_w_in, m_b_in_attn, m_rw_mu, m_rw_w0, m_rw_decay_up, m_rw_a0, m_rw_aaa_up, m_rw_gate_up, m_rw_k_k, m_rw_k_a, m_rw_r_k, m_rw_lnx_w, m_rw_lnx_b, m_attn_sinks, m_w_out, m_b_out, m_xa_norm, m_mem_norm, m_w_xq, m_w_xkv, m_w_xo, m_f2_norm, m_f2_gate, m_f2_up, m_f2_down, m_final_norm, v_f1_norm, v_f1_gate, v_f1_up, v_f1_down, v_mix_norm, v_w_in, v_b_in_attn, v_rw_mu, v_rw_w0, v_rw_decay_up, v_rw_a0, v_rw_aaa_up, v_rw_gate_up, v_rw_k_k, v_rw_k_a, v_rw_r_k, v_rw_lnx_w, v_rw_lnx_b, v_attn_sinks, v_w_out, v_b_out, v_xa_norm, v_mem_norm, v_w_xq, v_w_xkv, v_w_xo, v_f2_norm, v_f2_gate, v_f2_up, v_f2_down, v_final_norm):
    given = dict(x=x, mem=mem, f1_norm=f1_norm, f1_gate=f1_gate, f1_up=f1_up, f1_down=f1_down, mix_norm=mix_norm, w_in=w_in, b_in_attn=b_in_attn, rw_mu=rw_mu, rw_w0=rw_w0, rw_decay_up=rw_decay_up, rw_a0=rw_a0, rw_aaa_up=rw_aaa_up, rw_gate_up=rw_gate_up, rw_k_k=rw_k_k, rw_k_a=rw_k_a, rw_r_k=rw_r_k, rw_lnx_w=rw_lnx_w, rw_lnx_b=rw_lnx_b, attn_sinks=attn_sinks, w_out=w_out, b_out=b_out, xa_norm=xa_norm, mem_norm=mem_norm, w_xq=w_xq, w_xkv=w_xkv, w_xo=w_xo, f2_norm=f2_norm, f2_gate=f2_gate, f2_up=f2_up, f2_down=f2_down, final_norm=final_norm, loss_target=loss_target, m_f1_norm=m_f1_norm, m_f1_gate=m_f1_gate, m_f1_up=m_f1_up, m_f1_down=m_f1_down, m_mix_norm=m_mix_norm, m_w_in=m_w_in, m_b_in_attn=m_b_in_attn, m_rw_mu=m_rw_mu, m_rw_w0=m_rw_w0, m_rw_decay_up=m_rw_decay_up, m_rw_a0=m_rw_a0, m_rw_aaa_up=m_rw_aaa_up, m_rw_gate_up=m_rw_gate_up, m_rw_k_k=m_rw_k_k, m_rw_k_a=m_rw_k_a, m_rw_r_k=m_rw_r_k, m_rw_lnx_w=m_rw_lnx_w, m_rw_lnx_b=m_rw_lnx_b, m_attn_sinks=m_attn_sinks, m_w_out=m_w_out, m_b_out=m_b_out, m_xa_norm=m_xa_norm, m_mem_norm=m_mem_norm, m_w_xq=m_w_xq, m_w_xkv=m_w_xkv, m_w_xo=m_w_xo, m_f2_norm=m_f2_norm, m_f2_gate=m_f2_gate, m_f2_up=m_f2_up, m_f2_down=m_f2_down, m_final_norm=m_final_norm, v_f1_norm=v_f1_norm, v_f1_gate=v_f1_gate, v_f1_up=v_f1_up, v_f1_down=v_f1_down, v_mix_norm=v_mix_norm, v_w_in=v_w_in, v_b_in_attn=v_b_in_attn, v_rw_mu=v_rw_mu, v_rw_w0=v_rw_w0, v_rw_decay_up=v_rw_decay_up, v_rw_a0=v_rw_a0, v_rw_aaa_up=v_rw_aaa_up, v_rw_gate_up=v_rw_gate_up, v_rw_k_k=v_rw_k_k, v_rw_k_a=v_rw_k_a, v_rw_r_k=v_rw_r_k, v_rw_lnx_w=v_rw_lnx_w, v_rw_lnx_b=v_rw_lnx_b, v_attn_sinks=v_attn_sinks, v_w_out=v_w_out, v_b_out=v_b_out, v_xa_norm=v_xa_norm, v_mem_norm=v_mem_norm, v_w_xq=v_w_xq, v_w_xkv=v_w_xkv, v_w_xo=v_w_xo, v_f2_norm=v_f2_norm, v_f2_gate=v_f2_gate, v_f2_up=v_f2_up, v_f2_down=v_f2_down, v_final_norm=v_final_norm)
    weights = {n: given[n] for n in TWIN_WEIGHTS}
    shared = {n: given[n] for n in SHARED_INPUTS}
    per_example = {n: given[n] for n in ['x', 'mem']}
    grad_fn = _jax.value_and_grad(_loss, argnums=(0, 1))

    def one_microbatch(ex, loss_target):
        ex = dict(ex)
        diff = ex.pop(TWIN_DIFF_INPUT)
        return grad_fn(weights, diff, {**shared, **ex}, loss_target)

    if N_MICROBATCH == 1:
        loss, (grad_w, grad_x) = one_microbatch(per_example, given["loss_target"])
    else:
        def body(carry, xs):
            loss_sum, grad_sum = carry
            l_k, (gw_k, gx_k) = one_microbatch(xs[0], xs[1])
            with _jax.named_scope("update"):
                return (loss_sum + l_k, _jax.tree.map(_jnp.add, grad_sum, gw_k)), gx_k

        init = (_jnp.zeros((), _jnp.float32), _jax.tree.map(_jnp.zeros_like, weights))
        (loss, grad_w), grad_x = _jax.lax.scan(body, init, (per_example, given["loss_target"]))
    with _jax.named_scope("update"):
        delta_w, new_m, new_v = {}, {}, {}
        for n in TWIN_WEIGHTS:
            delta_w[n], new_m[n], new_v[n] = _adamw(weights[n], grad_w[n], given["m_" + n], given["v_" + n])
    return (loss, grad_x, *[grad_w[n] for n in TWIN_WEIGHTS], *[delta_w[n] for n in TWIN_WEIGHTS],
            *[new_m[n] for n in TWIN_WEIGHTS], *[new_v[n] for n in TWIN_WEIGHTS])
```

```python
import functools
import math

import jax
import jax.numpy as jnp
from jax import lax
from jax.experimental import pallas as pl
from jax.experimental.pallas import tpu as pltpu

f32 = jnp.float32
bf16 = jnp.bfloat16
MXU_DTYPE = jnp.bfloat16

HEAD_DIM = 64
SWA_BLOCK = 128
ROPE_THETA = 10000.0
XATTN_HEADS = 4
RMS_EPS = 1e-6
GN_EPS = 64e-5
NEG_INF = -1e30
RWKV_CHUNK = 64

ADAM_LR = 0.001
ADAM_B1 = 0.9
ADAM_B2 = 0.999
ADAM_EPS = 1e-08
ADAM_WD = 0.01
ADAM_STEP = 10

N_DEV = 8
LANE = 128
VMEM_LIMIT_BYTES = 56 * 1024 * 1024
MM_VMEM_BUDGET = 40 * 1024 * 1024
MESH = pl.DeviceIdType.MESH


def _params(sem):
    return pltpu.CompilerParams(dimension_semantics=sem, vmem_limit_bytes=VMEM_LIMIT_BYTES)


class _Comm:
    def __init__(self, ins, outs, n_remote, n_local, start, finish):
        self.ins, self.outs, self.n_remote, self.n_local = list(ins), list(outs), n_remote, max(n_local, 1)
        self.start, self.finish = start, finish


def _pcall(body, comm=None, **kw):
    kw.setdefault("compiler_params", pltpu.CompilerParams(vmem_limit_bytes=VMEM_LIMIT_BYTES))
    if comm is None:
        return pl.pallas_call(body, **kw)
    single = not isinstance(kw["out_shape"], (list, tuple))
    out_shape = [kw["out_shape"]] if single else list(kw["out_shape"])
    out_specs = [kw["out_specs"]] if single else list(kw["out_specs"])
    in_specs, scratch, grid = list(kw["in_specs"]), list(kw.get("scratch_shapes", ())), tuple(kw.get("grid", ()))
    n_in, n_out, n_ci, n_co, n_scr = len(in_specs), len(out_shape), len(comm.ins), len(comm.outs), len(scratch)

    def wrapped(*refs):
        ins, c_ins = refs[:n_in], refs[n_in:n_in + n_ci]
        outs = refs[n_in + n_ci:n_in + n_ci + n_out]
        c_outs = refs[n_in + n_ci + n_out:n_in + n_ci + n_out + n_co]
        rest = refs[n_in + n_ci + n_out + n_co:]
        scr, sems = rest[:n_scr], rest[n_scr:]
        if grid:
            ids = [pl.program_id(k) for k in range(len(grid))]
            first = functools.reduce(jnp.logical_and, [i == 0 for i in ids])
            last = functools.reduce(jnp.logical_and, [i == g - 1 for i, g in zip(ids, grid)])
            pl.when(first)(lambda: comm.start(c_ins, c_outs, *sems))
            body(*ins, *outs, *scr)
            pl.when(last)(lambda: comm.finish(c_ins, c_outs, *sems))
        else:
            comm.start(c_ins, c_outs, *sems)
            body(*ins, *outs, *scr)
            comm.finish(c_ins, c_outs, *sems)

    any_spec = pl.BlockSpec(memory_space=pl.ANY)
    kw.update(in_specs=in_specs + [any_spec] * n_ci, out_specs=out_specs + [any_spec] * n_co,
              out_shape=out_shape + comm.outs,
              scratch_shapes=scratch + [pltpu.SemaphoreType.DMA((comm.n_remote,)), pltpu.SemaphoreType.DMA((comm.n_remote,)),
                                        pltpu.SemaphoreType.DMA((comm.n_local,))])
    if grid:
        kw["compiler_params"] = _params(("arbitrary",) * len(grid))
    call = pl.pallas_call(wrapped, **kw)

    def run(*args):
        res = call(*args, *comm.ins)
        return (res[0] if single else list(res[:n_out])), list(res[n_out:])

    return run


def _comm_only(name, comm):
    return _pcall(lambda: None, comm=comm, name=name, in_specs=[], out_specs=[], out_shape=[])()[1]


def _dims(kind, ndim):
    o = ndim - 2
    batch = ((0,), (0,)) if o else ((), ())
    c = {"nn": ((1 + o,), (o,)), "nt": ((1 + o,), (1 + o,)), "tn": ((o,), (o,))}[kind]
    return (c, batch)


def _dot_raw(x, y, kind):
    return lax.dot_general(x.astype(MXU_DTYPE), y.astype(MXU_DTYPE), _dims(kind, x.ndim), preferred_element_type=f32)


@functools.partial(jax.custom_vjp, nondiff_argnums=(2,))
def _dot(x, y, kind):
    return _dot_raw(x, y, kind)


def _dot_fwd(x, y, kind):
    return _dot_raw(x, y, kind), (x, y)


def _dot_bwd(kind, res, g):
    x, y = res
    if kind == "nn":
        dx, dy = _dot(g, y, "nt"), _dot(x, g, "tn")
    elif kind == "nt":
        dx, dy = _dot(g, y, "nn"), _dot(g, x, "tn")
    else:
        dx, dy = _dot(y, g, "nt"), _dot(x, g, "nn")
    return dx.astype(x.dtype), dy.astype(y.dtype)


_dot.defvjp(_dot_fwd, _dot_bwd)


def _split3(x):
    a = x.astype(bf16)
    r = x - a.astype(f32)
    b = r.astype(bf16)
    c = (r - b.astype(f32)).astype(bf16)
    return a, b, c


def _rms(x, g):
    x = x.astype(f32)
    return x * lax.rsqrt(jnp.mean(x * x, axis=-1, keepdims=True) + RMS_EPS) * g


def _sigmoid(x):
    return 1.0 / (1.0 + jnp.exp(-x))


def _softplus(x):
    return jnp.maximum(x, 0.0) + jnp.log(1.0 + jnp.exp(-jnp.abs(x)))


def _rows(name, fn, row_ins, full_ins, row_outs, acc_outs=(), tm=None, comm=None):
    width = lambda a: a.shape[1] if a.ndim == 2 else a.shape[0] * a.shape[2]
    rows = row_ins[0].shape[0] if row_ins[0].ndim == 2 else row_ins[0].shape[1]
    if tm is None:
        tm = _row_tile(rows, max([width(a) for a in row_ins] + [o[0] for o in row_outs]))
    tm = min(tm, rows)
    assert rows % tm == 0, (name, rows, tm)
    n_in = len(row_ins) + len(full_ins)
    n_o, n_a = len(row_outs), len(acc_outs)

    def load(k, ref):
        if k < len(row_ins) and row_ins[k].ndim == 3:
            return jnp.concatenate([ref[h] for h in range(ref.shape[0])], axis=-1)
        return ref[...]

    def body(*refs):
        vals = [load(k, r) for k, r in enumerate(refs[:n_in])]
        outs = fn(*vals)
        o_refs = refs[n_in:n_in + n_o]
        a_refs = refs[n_in + n_o:]
        for k in range(n_o):
            if len(row_outs[k]) == 3:
                n = row_outs[k][2]
                for h in range(row_outs[k][0] // n):
                    o_refs[k][h] = outs[k][:, h * n:(h + 1) * n].astype(o_refs[k].dtype)
            else:
                o_refs[k][...] = outs[k].astype(o_refs[k].dtype)
        if n_a:
            first = pl.program_id(0) == 0

            @pl.when(first)
            def _():
                for k in range(n_a):
                    a_refs[k][...] = outs[n_o + k].astype(a_refs[k].dtype)

            @pl.when(jnp.logical_not(first))
            def _():
                for k in range(n_a):
                    a_refs[k][...] += outs[n_o + k].astype(a_refs[k].dtype)

    by_rows = lambda cols: pl.BlockSpec((tm, cols), lambda i: (i, 0))
    by_heads = lambda h, n: pl.BlockSpec((h, tm, n), lambda i: (0, i, 0))
    in_specs = [by_rows(a.shape[1]) if a.ndim == 2 else by_heads(a.shape[0], a.shape[2]) for a in row_ins]
    in_specs += [pl.BlockSpec(a.shape, lambda i, nd=a.ndim: (0,) * nd) for a in full_ins]
    out_specs = [by_rows(o[0]) if len(o) == 2 else by_heads(o[0] // o[2], o[2]) for o in row_outs]
    out_specs += [pl.BlockSpec(s, lambda i, nd=len(s): (0,) * nd) for s, _ in acc_outs]
    out_shape = [jax.ShapeDtypeStruct((rows, o[0]) if len(o) == 2 else (o[0] // o[2], rows, o[2]), o[1]) for o in row_outs]
    out_shape += [jax.ShapeDtypeStruct(s, d) for s, d in acc_outs]
    return _pcall(body, comm=comm, name=name, grid=(rows // tm,), in_specs=in_specs, out_specs=out_specs, out_shape=out_shape,
                  compiler_params=_params(("arbitrary",)))(*row_ins, *full_ins)


def _pick(n, cands):
    for c in cands:
        if n % c == 0:
            return c
    return n


def _mm(name, a, b, mode, out_dtype, scale=1.0, res=None, bias=None, comm=None, out_blocks=None):
    b_blocks = b.ndim == 3
    if b_blocks:
        assert mode == "nn"
        (m, k), (nb, k2, tn) = a.shape, b.shape
        n = nb * tn
    elif mode == "nn":
        (m, k), (k2, n) = a.shape, b.shape
    elif mode == "nt":
        (m, k), (n, k2) = a.shape, b.shape
    else:
        (k, m), (k2, n) = a.shape, b.shape
    assert k == k2, (name, a.shape, b.shape, mode)
    if not b_blocks:
        tn = n // out_blocks if out_blocks else _pick(n, (512, 256, 128))
    tm = _pick(m, (1024, 512, 256, 128))

    def need(tm_):
        by = tm_ * k * a.dtype.itemsize + tn * k * b.dtype.itemsize + tm_ * tn * (jnp.dtype(out_dtype).itemsize + 4)
        if res is not None:
            by += tm_ * tn * res.dtype.itemsize
        return 2 * by

    while need(tm) > MM_VMEM_BUDGET and tm % 256 == 0:
        tm //= 2
    dims = _dims(mode, 2)

    def body(*refs):
        bv = refs[1][0] if b_blocks else refs[1][...]
        acc = lax.dot_general(refs[0][...].astype(MXU_DTYPE), bv.astype(MXU_DTYPE), dims, preferred_element_type=f32)
        if scale != 1.0:
            acc = acc * scale
        pos = 2
        if bias is not None:
            acc = acc + refs[pos][...]
            pos += 1
        if res is not None:
            acc = acc + refs[pos][...].astype(f32)
            pos += 1
        if out_blocks:
            refs[pos][0] = acc.astype(out_dtype)
        else:
            refs[pos][...] = acc.astype(out_dtype)

    a_spec = pl.BlockSpec((k, tm), lambda i, j: (0, i)) if mode == "tn" else pl.BlockSpec((tm, k), lambda i, j: (i, 0))
    if b_blocks:
        b_spec = pl.BlockSpec((1, k, tn), lambda i, j: (j, 0, 0))
    else:
        b_spec = pl.BlockSpec((tn, k), lambda i, j: (j, 0)) if mode == "nt" else pl.BlockSpec((k, tn), lambda i, j: (0, j))
    in_specs, args = [a_spec, b_spec], [a, b]
    if bias is not None:
        in_specs.append(pl.BlockSpec((1, tn), lambda i, j: (0, j)))
        args.append(bias)
    if res is not None:
        in_specs.append(pl.BlockSpec((tm, tn), lambda i, j: (i, j)))
        args.append(res)
    if out_blocks:
        out_spec, out_shape = pl.BlockSpec((1, tm, tn), lambda i, j: (j, i, 0)), jax.ShapeDtypeStruct((out_blocks, m, tn), out_dtype)
    else:
        out_spec, out_shape = pl.BlockSpec((tm, tn), lambda i, j: (i, j)), jax.ShapeDtypeStruct((m, n), out_dtype)
    return _pcall(body, comm=comm, name=name, grid=(m // tm, n // tn), in_specs=in_specs, out_specs=out_spec, out_shape=out_shape,
                  compiler_params=_params(("parallel", "parallel")))(*args)


def _position():
    return lax.axis_index("x"), lax.axis_index("y"), lax.axis_index("c")


def _gather_comm(shards):
    n = len(shards)

    def plan(x_refs, o_refs, send_sems, recv_sems, local_sems):
        x, y, c = _position()
        me, sibling = (x, y, c), (x, y, 1 - c)
        chips = [(1 - x, y), (x, 1 - y), (1 - x, 1 - y)]

        def slot(px, py, pc):
            return 4 * px + 2 * py + pc

        def copy(t, k, block, to, src=None):
            dst = o_refs[t].at[slot(*block)]
            return pltpu.make_async_remote_copy(src_ref=dst if src is None else src, dst_ref=dst,
                                                send_sem=send_sems.at[7 * t + k], recv_sem=recv_sems.at[7 * t + k],
                                                device_id=to, device_id_type=MESH)

        mine = [pltpu.make_async_copy(x_refs[t], o_refs[t].at[slot(*me)], local_sems.at[t]) for t in range(n)]
        first = []
        for t in range(n):
            first.append(copy(t, 0, me, sibling, src=x_refs[t]))
            first += [copy(t, 1 + j, me, (*chip, c), src=x_refs[t]) for j, chip in enumerate(chips)]
        return me, sibling, chips, c, copy, mine, first

    def start(*refs):
        _, _, _, _, _, mine, first = plan(*refs)
        for cp in mine + first:
            cp.start()

    def finish(*refs):
        me, sibling, chips, c, copy, mine, first = plan(*refs)
        passed = []
        for t in range(n):
            for j, chip in enumerate(chips):
                copy(t, 1 + j, (*chip, c), me).wait_recv()
                cp = copy(t, 4 + j, (*chip, c), sibling)
                cp.start()
                passed.append(cp)
        for t in range(n):
            copy(t, 0, sibling, me).wait_recv()
            for j, chip in enumerate(chips):
                copy(t, 4 + j, (*chip, 1 - c), me).wait_recv()
        for cp in first + passed:
            cp.wait_send()
        for cp in mine:
            cp.wait()

    outs = [jax.ShapeDtypeStruct((N_DEV,) + s.shape, s.dtype) for s in shards]
    return _Comm(shards, outs, 7 * n, n, start, finish)


HBM_SPEC = pl.BlockSpec(memory_space=pltpu.HBM)
SEM_SPEC = pl.BlockSpec(memory_space=pltpu.SEMAPHORE)
DATAFLOW = pltpu.SideEffectType.DATAFLOW_SIDE_EFFECTING


def _chip_copies(shard_refs, land_refs, send_sems, recv_sems):
    x, y, c = _position()
    chips = [(1 - x, y), (x, 1 - y), (1 - x, 1 - y)]
    return [pltpu.make_async_remote_copy(src_ref=shard_refs[t], dst_ref=land_refs[t].at[j], send_sem=send_sems.at[3 * t + j],
                                         recv_sem=recv_sems.at[3 * t + j], device_id=(px, py, c), device_id_type=MESH)
            for t in range(len(shard_refs)) for j, (px, py) in enumerate(chips)]


def _chip_sends_start(name, groups):
    flat = [s for g in groups for s in g]
    n, ng = len(flat), len(groups)

    def body(*refs):
        s_in, l_in, sems = refs[:n], refs[n:2 * n], refs[2 * n:2 * n + 2 * ng]
        off = 0
        for gi, g in enumerate(groups):
            for cp in _chip_copies(s_in[off:off + len(g)], l_in[off:off + len(g)], sems[2 * gi], sems[2 * gi + 1]):
                cp.start()
            off += len(g)
        refs[-1][...] = jnp.zeros(refs[-1].shape, f32)

    lands = [lax.empty((3,) + s.shape, s.dtype) for s in flat]
    args = [pltpu.with_memory_space_constraint(a, pltpu.HBM) for a in flat + lands]
    out_shape = [pltpu.SemaphoreType.DMA((3 * len(g),)) for g in groups for _ in range(2)]
    out_shape += [pltpu.HBM(a.shape, a.dtype) for a in args] + [jax.ShapeDtypeStruct((8, LANE), f32)]
    res = pl.pallas_call(body, name=name, out_shape=out_shape, in_specs=[HBM_SPEC] * (2 * n),
                         out_specs=[SEM_SPEC] * (2 * ng) + [HBM_SPEC] * (2 * n) + [pl.BlockSpec(memory_space=pltpu.VMEM)],
                         input_output_aliases={i: 2 * ng + i for i in range(2 * n)},
                         compiler_params=pltpu.CompilerParams(has_side_effects=DATAFLOW))(*args)
    started, off = [], 0
    for gi, g in enumerate(groups):
        k = len(g)
        started.append((res[2 * gi], res[2 * gi + 1], list(res[2 * ng + off:2 * ng + off + k]),
                        list(res[2 * ng + n + off:2 * ng + n + off + k])))
        off += k
    return started, res[-1]


def _chip_sends_wait(name, send_sems, recv_sems, shards, lands, after):
    n = len(shards)

    def body(*refs):
        for cp in _chip_copies(refs[:n], refs[n:2 * n], refs[2 * n], refs[2 * n + 1]):
            cp.wait_send()
            cp.wait_recv()

    res = pl.pallas_call(body, name=name, out_shape=[pltpu.HBM(a.shape, a.dtype) for a in shards + lands],
                         in_specs=[HBM_SPEC] * (2 * n) + [SEM_SPEC, SEM_SPEC, pl.BlockSpec(memory_space=pl.ANY)],
                         out_specs=[HBM_SPEC] * (2 * n), input_output_aliases={i: i for i in range(2 * n)},
                         compiler_params=pltpu.CompilerParams(has_side_effects=DATAFLOW))(*shards, *lands, send_sems, recv_sems, after)
    return list(res[:n]), list(res[n:])


def _finish_gather_comm(shards, lands):
    n = len(shards)

    def plan(refs):
        s_refs, l_refs, o_refs = refs[0][:n], refs[0][n:], refs[1]
        send_sems, recv_sems, local_sems = refs[2:]
        x, y, c = _position()
        chips = [(1 - x, y), (x, 1 - y), (1 - x, 1 - y)]
        local, remote = [], []
        for t in range(n):
            pieces = [(s_refs[t], 4 * x + 2 * y)] + [(l_refs[t].at[j], 4 * px + 2 * py) for j, (px, py) in enumerate(chips)]
            for k, (src, chip_slot) in enumerate(pieces):
                local.append(pltpu.make_async_copy(src, o_refs[t].at[chip_slot + c], local_sems.at[4 * t + k]))
                remote.append(pltpu.make_async_remote_copy(src_ref=src, dst_ref=o_refs[t].at[chip_slot + c],
                                                           send_sem=send_sems.at[4 * t + k], recv_sem=recv_sems.at[4 * t + k],
                                                           device_id=(x, y, 1 - c), device_id_type=MESH))
        return local, remote

    def start(*refs):
        local, remote = plan(refs)
        for cp in local + remote:
            cp.start()

    def finish(*refs):
        local, remote = plan(refs)
        for cp in local + remote:
            cp.wait()

    outs = [jax.ShapeDtypeStruct((N_DEV,) + s.shape, s.dtype) for s in shards]
    return _Comm(list(shards) + list(lands), outs, 4 * n, 4 * n, start, finish)


def _sibling_comm(blocks):
    n = len(blocks)

    def copies(g_refs, o_refs, send_sems, recv_sems, _):
        x, y, c = _position()
        return [pltpu.make_async_remote_copy(src_ref=g_refs[t].at[2 * q + 1 - c], dst_ref=o_refs[t].at[q],
                                             send_sem=send_sems.at[4 * t + q], recv_sem=recv_sems.at[4 * t + q],
                                             device_id=(x, y, 1 - c), device_id_type=MESH)
                for t in range(n) for q in range(4)]

    def start(*refs):
        for cp in copies(*refs):
            cp.start()

    def finish(*refs):
        for cp in copies(*refs):
            cp.wait()

    outs = [jax.ShapeDtypeStruct((4,) + g.shape[1:], g.dtype) for g in blocks]
    return _Comm(blocks, outs, 4 * n, 0, start, finish)


def _chips_comm(parts):
    n = len(parts)

    def copies(p_refs, o_refs, send_sems, recv_sems, _):
        x, y, c = _position()
        chips = [(1 - x, y), (x, 1 - y), (1 - x, 1 - y)]
        return [pltpu.make_async_remote_copy(src_ref=p_refs[t].at[2 * px + py], dst_ref=o_refs[t].at[j],
                                             send_sem=send_sems.at[3 * t + j], recv_sem=recv_sems.at[3 * t + j],
                                             device_id=(px, py, c), device_id_type=MESH)
                for t in range(n) for j, (px, py) in enumerate(chips)]

    def start(*refs):
        for cp in copies(*refs):
            cp.start()

    def finish(*refs):
        for cp in copies(*refs):
            cp.wait()

    outs = [jax.ShapeDtypeStruct((3,) + p.shape[1:], p.dtype) for p in parts]
    return _Comm(parts, outs, 3 * n, 0, start, finish)


ROW_TILE_BYTES = 2 << 20


def _row_tile(r, cols, itemsize=4):
    fits = [t for t in range(8, r + 1, 8) if r % t == 0 and t * cols * itemsize <= ROW_TILE_BYTES]
    return max(fits) if fits else r


def _pair_add(name, g, got, c_idx):
    _, r, cc = g.shape
    tr = _row_tile(r, cc, g.dtype.itemsize)

    def body(c_ref, g_ref, o_ref, out_ref):
        out_ref[...] = (g_ref[...].astype(f32) + o_ref[...].astype(f32)).astype(out_ref.dtype)

    g5 = g.reshape(4, 2, r, cc)
    spec = pltpu.PrefetchScalarGridSpec(
        num_scalar_prefetch=1, grid=(4, r // tr),
        in_specs=[pl.BlockSpec((1, 1, tr, cc), lambda q, i, c_ref: (q, c_ref[0], i, 0)),
                  pl.BlockSpec((1, 1, tr, cc), lambda q, i, c_ref: (q, 0, i, 0))],
        out_specs=pl.BlockSpec((1, 1, tr, cc), lambda q, i, c_ref: (q, 0, i, 0)))
    out = _pcall(body, name=name, grid_spec=spec, out_shape=jax.ShapeDtypeStruct((4, 1, r, cc), g.dtype),
                 compiler_params=_params(("arbitrary", "arbitrary")))(c_idx, g5, got.reshape(4, 1, r, cc))
    return out.reshape(4, r, cc)


def _adam_math(w, g, m, v):
    m2 = ADAM_B1 * m + (1.0 - ADAM_B1) * g
    v2 = ADAM_B2 * v + (1.0 - ADAM_B2) * (g * g)
    m_hat = m2 / (1.0 - ADAM_B1 ** ADAM_STEP)
    v_hat = v2 / (1.0 - ADAM_B2 ** ADAM_STEP)
    delta = -ADAM_LR * (m_hat / (jnp.sqrt(v_hat) + ADAM_EPS) + ADAM_WD * w)
    return delta, m2, v2


def _adam_sharded(name, w, m, v, part, got, chip_idx):
    r, cc = w.shape
    tr = _row_tile(r, cc)

    def body(q_ref, w_ref, m_ref, v_ref, p_ref, o_ref, g_out, d_out, m_out, v_out):
        g = p_ref[0].astype(f32)
        for j in range(3):
            g = g + o_ref[j].astype(f32)
        d, m2, v2 = _adam_math(w_ref[...], g, m_ref[...], v_ref[...])
        g_out[...] = g
        d_out[...] = d
        m_out[...] = m2
        v_out[...] = v2

    row = pl.BlockSpec((tr, cc), lambda i, q_ref: (i, 0))
    spec = pltpu.PrefetchScalarGridSpec(
        num_scalar_prefetch=1, grid=(r // tr,),
        in_specs=[row, row, row, pl.BlockSpec((1, tr, cc), lambda i, q_ref: (q_ref[0], i, 0)),
                  pl.BlockSpec((3, tr, cc), lambda i, q_ref: (0, i, 0))],
        out_specs=[row, row, row, row])
    sh = jax.ShapeDtypeStruct((r, cc), f32)
    return _pcall(body, name=name, grid_spec=spec, out_shape=[sh, sh, sh, sh],
                  compiler_params=_params(("arbitrary",)))(chip_idx, w, m, v, part, got)


def _adam_small(name, w, m, v, parts):
    def body(w_ref, m_ref, v_ref, p_ref, g_out, d_out, m_out, v_out):
        g = p_ref[0]
        for b in range(1, N_DEV):
            g = g + p_ref[b]
        d, m2, v2 = _adam_math(w_ref[...], g, m_ref[...], v_ref[...])
        g_out[...] = g
        d_out[...] = d
        m_out[...] = m2
        v_out[...] = v2

    sh = jax.ShapeDtypeStruct(w.shape, f32)
    return _pcall(body, name=name, out_shape=[sh, sh, sh, sh])(w, m, v, parts)


def _swa_math(n, qa, qb, kap, kac, kbp, kbc, vp, vc, cq, sq, cp, sp, sink):
    g, blk, half = qa.shape
    c3, s3 = cq[None], sq[None]
    q1 = (qa * c3 - qb * s3).reshape(g * blk, half)
    q2 = (qb * c3 + qa * s3).reshape(g * blk, half)
    ck, sk = jnp.concatenate([cp, cq], axis=0), jnp.concatenate([sp, sq], axis=0)
    k1, k2 = jnp.concatenate([kap[0], kac[0]], axis=0), jnp.concatenate([kbp[0], kbc[0]], axis=0)
    k1r, k2r = k1 * ck - k2 * sk, k2 * ck + k1 * sk
    vv = jnp.concatenate([vp[0], vc[0]], axis=0)
    s = (_dot(q1, k1r, "nt") + _dot(q2, k2r, "nt")) * (HEAD_DIM ** -0.5)
    s = s.reshape(g, blk, 2 * blk)
    qi = lax.broadcasted_iota(jnp.int32, (blk, 2 * blk), 0)
    kj = lax.broadcasted_iota(jnp.int32, (blk, 2 * blk), 1)
    valid = (kj > qi) & (kj <= qi + blk) & ((kj >= blk) | (n > 0))
    s = jnp.where(valid[None], s, NEG_INF)
    sink3 = sink.reshape(g, 1, 1)
    mx = jnp.maximum(jnp.max(s, axis=-1, keepdims=True), sink3)
    e = jnp.exp(s - mx)
    z = jnp.sum(e, axis=-1, keepdims=True) + jnp.exp(sink3 - mx)
    p = (e / z).reshape(g * blk, 2 * blk)
    return _dot(p, vv, "nn").reshape(g, blk, 2 * half)


def _swa_specs(g, blk, half):
    prev = lambda n: jnp.maximum(n - 1, 0)
    q_spec = pl.BlockSpec((g, blk, half), lambda h, n: (h, n, 0))
    kc = pl.BlockSpec((1, blk, half), lambda h, n: (h, n, 0))
    kp = pl.BlockSpec((1, blk, half), lambda h, n: (h, prev(n), 0))
    vc = pl.BlockSpec((1, blk, 2 * half), lambda h, n: (h, n, 0))
    vp = pl.BlockSpec((1, blk, 2 * half), lambda h, n: (h, prev(n), 0))
    tc = pl.BlockSpec((blk, half), lambda h, n: (n, 0))
    tp = pl.BlockSpec((blk, half), lambda h, n: (prev(n), 0))
    sink = pl.BlockSpec((1, g, 1), lambda h, n: (h, 0, 0))
    o_spec = pl.BlockSpec((g, blk, 2 * half), lambda h, n: (h, n, 0))
    return q_spec, kc, kp, vc, vp, tc, tp, sink, o_spec


def _swa_fwd(qa, qb, ka, kb, v, cos, sin, sinks, comm=None):
    hq, t, half = qa.shape
    kv = ka.shape[0]
    g, blk = hq // kv, SWA_BLOCK
    q_spec, kc, kp, vc, vp, tc, tp, sink, o_spec = _swa_specs(g, blk, half)

    def body(qa_r, qb_r, kap, kac, kbp, kbc, vp_r, vc_r, cq, sq, cp, sp, sink_r, o_r):
        o_r[...] = _swa_math(pl.program_id(1), qa_r[...], qb_r[...], kap[...], kac[...], kbp[...], kbc[...], vp_r[...],
                             vc_r[...], cq[...], sq[...], cp[...], sp[...], sink_r[...]).astype(o_r.dtype)

    return _pcall(body, comm=comm, name="swa_fwd", grid=(kv, t // blk),
                  in_specs=[q_spec, q_spec, kp, kc, kp, kc, vp, vc, tc, tc, tp, tp, sink], out_specs=o_spec,
                  out_shape=jax.ShapeDtypeStruct((hq, t, 2 * half), f32),
                  compiler_params=_params(("parallel", "arbitrary")))(qa, qb, ka, ka, kb, kb, v, v, cos, sin, cos, sin, sinks)


def _swa_bwd(qa, qb, ka, kb, v, cos, sin, sinks, do, comm=None):
    hq, t, half = qa.shape
    kv = ka.shape[0]
    g, blk = hq // kv, SWA_BLOCK
    q_spec, kc, kp, vc, vp, tc, tp, sink, o_spec = _swa_specs(g, blk, half)

    def body(qa_r, qb_r, kap, kac, kbp, kbc, vp_r, vc_r, cq, sq, cp, sp, sink_r, do_r,
             dqa, dqb, dkap, dkac, dkbp, dkbc, dvp, dvc, dsink):
        n = pl.program_id(1)
        tabs = (cq[...], sq[...], cp[...], sp[...])
        fn = lambda a, b, c_, d, e, f_, g_, h_, s_: _swa_math(n, a, b, c_, d, e, f_, g_, h_, *tabs, s_)
        _, vjp = jax.vjp(fn, qa_r[...], qb_r[...], kap[...], kac[...], kbp[...], kbc[...], vp_r[...], vc_r[...], sink_r[...])
        grads = vjp(do_r[...])
        for ref, val in zip((dqa, dqb, dkap, dkac, dkbp, dkbc, dvp, dvc), grads[:8]):
            ref[...] = val

        @pl.when(n == 0)
        def _():
            dsink[...] = grads[8]

        @pl.when(n > 0)
        def _():
            dsink[...] += grads[8]

    sh = lambda a: jax.ShapeDtypeStruct(a.shape, f32)
    return _pcall(body, comm=comm, name="swa_bwd", grid=(kv, t // blk),
                  in_specs=[q_spec, q_spec, kp, kc, kp, kc, vp, vc, tc, tc, tp, tp, sink, o_spec],
                  out_specs=[q_spec, q_spec, kc, kc, kc, kc, vc, vc, sink],
                  out_shape=[sh(qa), sh(qb), sh(ka), sh(ka), sh(kb), sh(kb), sh(v), sh(v), sh(sinks)],
                  compiler_params=_params(("parallel", "arbitrary")))(qa, qb, ka, ka, kb, kb, v, v, cos, sin, cos, sin, sinks, do)


def _swa_split(za, hq, kv):
    t = za.shape[0]
    half = HEAD_DIM // 2
    tm = _row_tile(t, za.shape[1])

    def body(z_r, qa, qb, ka, kb, v):
        z = z_r[...]
        for h in range(hq):
            qa[h] = z[:, HEAD_DIM * h:HEAD_DIM * h + half]
            qb[h] = z[:, HEAD_DIM * h + half:HEAD_DIM * (h + 1)]
        for h in range(kv):
            o = HEAD_DIM * (hq + h)
            ka[h] = z[:, o:o + half]
            kb[h] = z[:, o + half:o + HEAD_DIM]
            o = HEAD_DIM * (hq + kv + h)
            v[h] = z[:, o:o + HEAD_DIM]

    spec = lambda n, w: pl.BlockSpec((n, tm, w), lambda i: (0, i, 0))
    sh = lambda n, w: jax.ShapeDtypeStruct((n, t, w), f32)
    return _pcall(body, name="swa_split", grid=(t // tm,), in_specs=[pl.BlockSpec((tm, za.shape[1]), lambda i: (i, 0))],
                  out_specs=[spec(hq, half), spec(hq, half), spec(kv, half), spec(kv, half), spec(kv, HEAD_DIM)],
                  out_shape=[sh(hq, half), sh(hq, half), sh(kv, half), sh(kv, half), sh(kv, HEAD_DIM)],
                  compiler_params=_params(("parallel",)))(za)


def _swa_merge(dqa, dqb, dkac, dkap, dkbc, dkbp, dvc, dvp):
    hq, t, half = dqa.shape
    kv = dkac.shape[0]
    blk = SWA_BLOCK
    nb = t // blk
    cols = HEAD_DIM * (hq + 2 * kv)

    def body(qa, qb, kac, kap, kbc, kbp, vc, vp, z_o, s_o):
        i = pl.program_id(0)
        more = (i < nb - 1).astype(f32)
        pieces = []
        for h in range(hq):
            pieces += [qa[h], qb[h]]
        for h in range(kv):
            pieces += [kac[h] + more * kap[h], kbc[h] + more * kbp[h]]
        for h in range(kv):
            pieces.append(vc[h] + more * vp[h])
        z = jnp.concatenate(pieces, axis=-1)
        z_o[...] = z
        colsum = jnp.sum(z, axis=0, keepdims=True)

        @pl.when(i == 0)
        def _():
            s_o[...] = colsum

        @pl.when(i > 0)
        def _():
            s_o[...] += colsum

    cur = lambda n, w: pl.BlockSpec((n, blk, w), lambda i: (0, i, 0))
    nxt = lambda n, w: pl.BlockSpec((n, blk, w), lambda i: (0, jnp.minimum(i + 1, nb - 1), 0))
    return _pcall(body, name="swa_merge", grid=(nb,),
                  in_specs=[cur(hq, half), cur(hq, half), cur(kv, half), nxt(kv, half), cur(kv, half), nxt(kv, half),
                            cur(kv, HEAD_DIM), nxt(kv, HEAD_DIM)],
                  out_specs=[pl.BlockSpec((blk, cols), lambda i: (i, 0)), pl.BlockSpec((1, cols), lambda i: (0, 0))],
                  out_shape=[jax.ShapeDtypeStruct((t, cols), f32), jax.ShapeDtypeStruct((1, cols), f32)],
                  compiler_params=_params(("arbitrary",)))(dqa, dqb, dkac, dkap, dkbc, dkbp, dvc, dvp)


def _xattn_math(q, k, v):
    s = _dot(q, k, "nt") * (q.shape[-1] ** -0.5)
    e = jnp.exp(s - jnp.max(s, axis=-1, keepdims=True))
    p = e / jnp.sum(e, axis=-1, keepdims=True)
    return _dot(p, v, "nn")


def _xattn_fwd(q, kvm):
    t, d = q.shape
    mlen = kvm.shape[0]
    hd = d // XATTN_HEADS
    tq = min(512, t)

    def body(q_r, k_r, v_r, o_r):
        o_r[...] = _xattn_math(q_r[...], k_r[...], v_r[...]).astype(o_r.dtype)

    return _pcall(body, name="xattn_fwd", grid=(XATTN_HEADS, t // tq),
                  in_specs=[pl.BlockSpec((tq, hd), lambda h, i: (i, h)), pl.BlockSpec((mlen, hd), lambda h, i: (0, h)),
                            pl.BlockSpec((mlen, hd), lambda h, i: (0, XATTN_HEADS + h))],
                  out_specs=pl.BlockSpec((tq, hd), lambda h, i: (i, h)), out_shape=jax.ShapeDtypeStruct((t, d), bf16),
                  compiler_params=_params(("parallel", "parallel")))(q, kvm, kvm)


def _xattn_bwd(q, kvm, do):
    t, d = q.shape
    mlen = kvm.shape[0]
    hd = d // XATTN_HEADS
    tq = min(512, t)

    def body(q_r, k_r, v_r, do_r, dq, dk, dv):
        _, vjp = jax.vjp(_xattn_math, q_r[...].astype(f32), k_r[...].astype(f32), v_r[...].astype(f32))
        gq, gk, gv = vjp(do_r[...].astype(f32))
        dq[...] = gq.astype(dq.dtype)
        first = pl.program_id(1) == 0

        @pl.when(first)
        def _():
            dk[...] = gk
            dv[...] = gv

        @pl.when(jnp.logical_not(first))
        def _():
            dk[...] += gk
            dv[...] += gv

    qs = pl.BlockSpec((tq, hd), lambda h, i: (i, h))
    ms = pl.BlockSpec((mlen, hd), lambda h, i: (0, h))
    return _pcall(body, name="xattn_bwd", grid=(XATTN_HEADS, t // tq),
                  in_specs=[qs, ms, pl.BlockSpec((mlen, hd), lambda h, i: (0, XATTN_HEADS + h)), qs],
                  out_specs=[qs, ms, ms],
                  out_shape=[jax.ShapeDtypeStruct((t, d), bf16), jax.ShapeDtypeStruct((mlen, d), f32),
                             jax.ShapeDtypeStruct((mlen, d), f32)],
                  compiler_params=_params(("parallel", "arbitrary")))(q, kvm, kvm, do)


def _chunk_cumsum(lw, reverse=False):
    h, l, _ = lw.shape
    i = lax.broadcasted_iota(jnp.int32, (l, l), 0)
    j = lax.broadcasted_iota(jnp.int32, (l, l), 1)
    tri = jnp.broadcast_to(((i <= j) if reverse else (i >= j)).astype(bf16)[None], (h, l, l))
    out = jnp.zeros(lw.shape, f32)
    for piece in _split3(lw):
        out = out + lax.dot_general(tri, piece, _dims("nn", 3), preferred_element_type=f32)
    return out


def _rwkv_chunk(s0, r, k, v, a, lw, cl, k_k, k_a, r_k, ln_w, ln_b):
    l = r.shape[1]
    kk = k * k_k
    kk = kk / jnp.maximum(jnp.sqrt(jnp.sum(kk * kk, axis=-1, keepdims=True)), 1e-12)
    km = k * (1.0 + (a - 1.0) * k_a)
    av, bv = -kk, kk * a
    p_incl, p_excl, p_inv = jnp.exp(cl), jnp.exp(cl - lw), jnp.exp(-cl)
    at, bh, kh, rt = av * p_excl, bv * p_inv, km * p_inv, r * p_incl
    i = lax.broadcasted_iota(jnp.int32, (l, l), 0)
    j = lax.broadcasted_iota(jnp.int32, (l, l), 1)
    strict, incl = (i > j)[None], (i >= j)[None]
    a_ab = jnp.where(strict, _dot(at, bh, "nt"), 0.0)
    a_ak = jnp.where(strict, _dot(at, kh, "nt"), 0.0)
    a_rb = jnp.where(incl, _dot(rt, bh, "nt"), 0.0)
    a_rk = jnp.where(incl, _dot(rt, kh, "nt"), 0.0)
    rhs = _dot(at, s0, "nt") + _dot(a_ak, v, "nn")
    inv = a_ab + (i == j)[None].astype(f32)
    pw = a_ab
    for _ in range(int(math.log2(l)) - 1):
        pw = _dot(pw, pw, "nn")
        inv = inv + _dot(inv, pw, "nn")
    sa = _dot(inv, rhs, "nn")
    y = _dot(rt, s0, "nt") + _dot(a_rk, v, "nn") + _dot(a_rb, sa, "nn")
    p_last = p_incl[:, l - 1:l, :]
    s_end = s0 * p_last + _dot(v, kh * p_last, "tn") + _dot(sa, bh * p_last, "tn")
    mu = jnp.mean(y, axis=-1, keepdims=True)
    var = jnp.mean(jnp.square(y - mu), axis=-1, keepdims=True)
    out = (y - mu) * lax.rsqrt(var + GN_EPS) * ln_w + ln_b
    out = out + jnp.sum(r * km * r_k, axis=-1, keepdims=True) * v
    return out, s_end


def _rwkv_fwd(r, k, v, a, lw, heads, comm=None):
    h, t, n = r.shape
    l = min(RWKV_CHUNK, t)
    nc = t // l
    seq = pl.BlockSpec((h, l, n), lambda c: (0, c, 0))
    par = pl.BlockSpec((h, 1, n), lambda c: (0, 0, 0))

    def body(r_r, k_r, v_r, a_r, lw_r, p0, p1, p2, p3, p4, y_r, ck_r, s_scr):
        @pl.when(pl.program_id(0) == 0)
        def _():
            s_scr[...] = jnp.zeros_like(s_scr)

        s0 = s_scr[...]
        ck_r[0] = s0
        lw_v = lw_r[...]
        out, s_end = _rwkv_chunk(s0, r_r[...], k_r[...], v_r[...], a_r[...], lw_v, _chunk_cumsum(lw_v),
                                 p0[...], p1[...], p2[...], p3[...], p4[...])
        y_r[...] = out
        s_scr[...] = s_end

    return _pcall(body, comm=comm, name="rwkv_fwd", grid=(nc,), in_specs=[seq] * 5 + [par] * 5,
                  out_specs=[seq, pl.BlockSpec((1, h, n, n), lambda c: (c, 0, 0, 0))],
                  out_shape=[jax.ShapeDtypeStruct((h, t, n), f32), jax.ShapeDtypeStruct((nc, h, n, n), f32)],
                  scratch_shapes=[pltpu.VMEM((h, n, n), f32)],
                  compiler_params=_params(("arbitrary",)))(r, k, v, a, lw, *heads)


def _rwkv_bwd(r, k, v, a, lw, heads, ck, dy, comm=None):
    h, t, n = r.shape
    l = min(RWKV_CHUNK, t)
    nc = t // l
    seq = pl.BlockSpec((h, l, n), lambda c: (0, nc - 1 - c, 0))
    par = pl.BlockSpec((h, 1, n), lambda c: (0, 0, 0))

    def body(r_r, k_r, v_r, a_r, lw_r, p0, p1, p2, p3, p4, ck_r, dy_r,
             dr, dk, dv, da, dlw, g0, g1, g2, g3, g4, ds_scr):
        first = pl.program_id(0) == 0

        @pl.when(first)
        def _():
            ds_scr[...] = jnp.zeros_like(ds_scr)

        lw_v = lw_r[...]
        _, vjp = jax.vjp(_rwkv_chunk, ck_r[0], r_r[...], k_r[...], v_r[...], a_r[...], lw_v, _chunk_cumsum(lw_v),
                         p0[...], p1[...], p2[...], p3[...], p4[...])
        grads = vjp((dy_r[...], ds_scr[...]))
        ds_scr[...] = grads[0]
        dr[...] = grads[1]
        dk[...] = grads[2]
        dv[...] = grads[3]
        da[...] = grads[4]
        dlw[...] = grads[5] + _chunk_cumsum(grads[6], reverse=True)
        acc = (g0, g1, g2, g3, g4)

        @pl.when(first)
        def _():
            for ref, val in zip(acc, grads[7:]):
                ref[...] = val

        @pl.when(jnp.logical_not(first))
        def _():
            for ref, val in zip(acc, grads[7:]):
                ref[...] += val

    seq_sh = jax.ShapeDtypeStruct((h, t, n), f32)
    par_sh = jax.ShapeDtypeStruct((h, 1, n), f32)
    return _pcall(body, comm=comm, name="rwkv_bwd", grid=(nc,),
                  in_specs=[seq] * 5 + [par] * 5 + [pl.BlockSpec((1, h, n, n), lambda c: (nc - 1 - c, 0, 0, 0)), seq],
                  out_specs=[seq] * 5 + [par] * 5, out_shape=[seq_sh] * 5 + [par_sh] * 5,
                  scratch_shapes=[pltpu.VMEM((h, n, n), f32)],
                  compiler_params=_params(("arbitrary",)))(r, k, v, a, lw, *heads, ck, dy)


def _rwkv_pre_math(c, lp, p_rkv, p_rkv_prev, p_l, p_l_prev, mu_rkv, mu_l, w0, a0, decay_up, aaa_up, gate_up):
    dlp, alp, _ = lp
    z = p_rkv + (p_rkv_prev - p_rkv) * mu_rkv
    zl = p_l + (p_l_prev - p_l) * mu_l
    r, k, v = z[:, :c], z[:, c:2 * c], z[:, 2 * c:]
    wd, ad, gd = zl[:, :dlp], zl[:, dlp:dlp + alp], zl[:, dlp + alp:]
    w = -_softplus(-(w0 + _dot(jnp.tanh(wd), decay_up, "nn"))) - 0.5
    a = _sigmoid(a0 + _dot(ad, aaa_up, "nn"))
    g = _dot(_sigmoid(gd), gate_up, "nn")
    return r, k, v, -jnp.exp(w), a, g


def _pad_to(a, n, axis):
    if a.shape[axis] == n:
        return a
    pad = [(0, 0)] * a.ndim
    pad[axis] = (0, n - a.shape[axis])
    return jnp.pad(a, pad)


def _up128(n):
    return -(-n // LANE) * LANE


def _shift_down(p):
    return jnp.concatenate([jnp.zeros((1, p.shape[1]), p.dtype), p[:-1]], axis=0)


def _shift_up(p):
    return jnp.concatenate([p[1:], jnp.zeros((1, p.shape[1]), p.dtype)], axis=0)


def _swiglu(g, u):
    return jax.nn.silu(g) * u


def _ffn_hidden(name, h, w_gate, w_up, comm=None):
    t, d = h.shape
    nb, _, n = w_gate.shape
    tm = _pick(t, (1024, 512, 256, 128))

    def body(h_r, wg_r, wu_r, g_o, u_o, a_o):
        hv = h_r[...]
        g = lax.dot_general(hv, wg_r[0], _dims("nn", 2), preferred_element_type=f32)
        u = lax.dot_general(hv, wu_r[0], _dims("nn", 2), preferred_element_type=f32)
        g_o[0] = g.astype(bf16)
        u_o[0] = u.astype(bf16)
        a_o[0] = _swiglu(g, u).astype(bf16)

    w_spec = pl.BlockSpec((1, d, n), lambda i, j: (j, 0, 0))
    o_spec = pl.BlockSpec((1, tm, n), lambda i, j: (j, i, 0))
    sh = jax.ShapeDtypeStruct((nb, t, n), bf16)
    return _pcall(body, comm=comm, name=name, grid=(t // tm, nb),
                  in_specs=[pl.BlockSpec((tm, d), lambda i, j: (i, 0)), w_spec, w_spec],
                  out_specs=[o_spec, o_spec, o_spec], out_shape=[sh, sh, sh],
                  compiler_params=_params(("parallel", "arbitrary")))(h, w_gate, w_up)


def _ffn_out(name, act, w_down, x, comm=None):
    nb, t, n = act.shape
    d = w_down.shape[2]
    tm, tn = _pick(t, (512, 256, 128)), _pick(d, (512, 256, 128))

    def body(a_r, w_r, x_r, o_r):
        acc = x_r[...]
        for j in range(nb):
            acc = acc + 0.5 * lax.dot_general(a_r[j], w_r[j], _dims("nn", 2), preferred_element_type=f32)
        o_r[...] = acc

    return _pcall(body, comm=comm, name=name, grid=(t // tm, d // tn),
                  in_specs=[pl.BlockSpec((nb, tm, n), lambda i, j: (0, i, 0)), pl.BlockSpec((nb, n, tn), lambda i, j: (0, 0, j)),
                            pl.BlockSpec((tm, tn), lambda i, j: (i, j))],
                  out_specs=pl.BlockSpec((tm, tn), lambda i, j: (i, j)), out_shape=jax.ShapeDtypeStruct((t, d), f32),
                  compiler_params=_params(("parallel", "parallel")))(act, w_down, x)


def _ffn_dhidden(name, dout, w_down, gate, up, comm=None):
    t, d = dout.shape
    nb, n, _ = w_down.shape
    tm = _pick(t, (512, 256, 128))

    def body(d_r, w_r, g_r, u_r, dg_o, du_o):
        dact = 0.5 * lax.dot_general(d_r[...].astype(MXU_DTYPE), w_r[0], _dims("nt", 2), preferred_element_type=f32)
        _, vjp = jax.vjp(_swiglu, g_r[0].astype(f32), u_r[0].astype(f32))
        dg, du = vjp(dact)
        dg_o[0] = dg.astype(bf16)
        du_o[0] = du.astype(bf16)

    o_spec = pl.BlockSpec((1, tm, n), lambda i, j: (j, i, 0))
    sh = jax.ShapeDtypeStruct((nb, t, n), bf16)
    return _pcall(body, comm=comm, name=name, grid=(t // tm, nb),
                  in_specs=[pl.BlockSpec((tm, d), lambda i, j: (i, 0)), pl.BlockSpec((1, n, d), lambda i, j: (j, 0, 0)), o_spec, o_spec],
                  out_specs=[o_spec, o_spec], out_shape=[sh, sh],
                  compiler_params=_params(("parallel", "arbitrary")))(dout, w_down, gate, up)


def _ffn_dw_down(name, act, dout, comm=None):
    nb, t, n = act.shape
    d = dout.shape[1]
    tn = _pick(d, (1024, 512, 256, 128))

    def body(a_r, d_r, o_r):
        acc = lax.dot_general(a_r[0], d_r[...].astype(MXU_DTYPE), _dims("tn", 2), preferred_element_type=f32)
        o_r[0] = (0.5 * acc).astype(bf16)

    return _pcall(body, comm=comm, name=name, grid=(nb, d // tn),
                  in_specs=[pl.BlockSpec((1, t, n), lambda j, i: (j, 0, 0)), pl.BlockSpec((t, tn), lambda j, i: (0, i))],
                  out_specs=pl.BlockSpec((1, n, tn), lambda j, i: (j, 0, i)), out_shape=jax.ShapeDtypeStruct((nb, n, d), bf16),
                  compiler_params=_params(("parallel", "parallel")))(act, dout)


def _ffn_dw_hidden(name, h, dgate, dup, comm=None):
    t, d = h.shape
    nb, _, n = dgate.shape
    tm = _pick(d, (1024, 512, 256, 128))

    def body(h_r, g_r, u_r, dg_o, du_o):
        hv = h_r[...]
        dg_o[0] = lax.dot_general(hv, g_r[0], _dims("tn", 2), preferred_element_type=f32).astype(bf16)
        du_o[0] = lax.dot_general(hv, u_r[0], _dims("tn", 2), preferred_element_type=f32).astype(bf16)

    g_spec = pl.BlockSpec((1, t, n), lambda j, i: (j, 0, 0))
    o_spec = pl.BlockSpec((1, tm, n), lambda j, i: (j, i, 0))
    sh = jax.ShapeDtypeStruct((nb, d, n), bf16)
    return _pcall(body, comm=comm, name=name, grid=(nb, d // tm),
                  in_specs=[pl.BlockSpec((t, tm), lambda j, i: (0, i)), g_spec, g_spec],
                  out_specs=[o_spec, o_spec], out_shape=[sh, sh],
                  compiler_params=_params(("parallel", "parallel")))(h, dgate, dup)


def _ffn_dh(name, dhid, w, res=None, comm=None):
    nb, t, n = dhid.shape
    d = w.shape[1]
    tm, tn = _pick(t, (512, 256, 128)), _pick(d, (512, 256, 128))

    def body(*refs):
        acc = refs[2][...] if res is not None else jnp.zeros((tm, tn), f32)
        for j in range(nb):
            acc = acc + lax.dot_general(refs[0][j], refs[1][j], _dims("nt", 2), preferred_element_type=f32)
        refs[-1][...] = acc

    in_specs = [pl.BlockSpec((nb, tm, n), lambda i, j: (0, i, 0)), pl.BlockSpec((nb, tn, n), lambda i, j: (0, j, 0))]
    args = [dhid, w]
    if res is not None:
        in_specs.append(pl.BlockSpec((tm, tn), lambda i, j: (i, j)))
        args.append(res)
    return _pcall(body, comm=comm, name=name, grid=(t // tm, d // tn), in_specs=in_specs,
                  out_specs=pl.BlockSpec((tm, tn), lambda i, j: (i, j)), out_shape=jax.ShapeDtypeStruct((t, d), f32),
                  compiler_params=_params(("parallel", "parallel")))(*args)


def _lora_bounds(c, lora):
    dl, al, gl = lora
    o1 = 3 * c
    o2, o3 = o1 + dl, o1 + dl + al
    return o1, o2, o3, o3 + gl, (_up128(dl), _up128(al), _up128(gl))


def _win_split(g8, c, lora):
    nb, d, n = g8.shape
    o1, o2, o3, o4, (dlp, alp, glp) = _lora_bounds(c, lora)
    tm = _row_tile(d, nb * n, g8.dtype.itemsize)

    def body(x, rkv_o, lora_o, swa_o):
        w = jnp.concatenate([x[j] for j in range(nb)], axis=-1)
        pad = lambda p, m: p if p.shape[1] == m else jnp.concatenate([p, jnp.zeros((p.shape[0], m - p.shape[1]), p.dtype)], axis=-1)
        rkv_o[...] = w[:, :o1]
        lora_o[...] = jnp.concatenate([pad(w[:, o1:o2], dlp), pad(w[:, o2:o3], alp), pad(w[:, o3:o4], glp)], axis=-1)
        swa_o[...] = w[:, o4:]

    widths = (o1, dlp + alp + glp, nb * n - o4)
    return _pcall(body, name="w_in_split", grid=(d // tm,), in_specs=[pl.BlockSpec((nb, tm, n), lambda i: (0, i, 0))],
                  out_specs=[pl.BlockSpec((tm, wd), lambda i: (i, 0)) for wd in widths],
                  out_shape=[jax.ShapeDtypeStruct((d, wd), g8.dtype) for wd in widths],
                  compiler_params=_params(("parallel",)))(g8)


def _win_merge(dw_rkv, dw_lora, dw_swa, c, lora, nb):
    d = dw_rkv.shape[0]
    o1, o2, o3, o4, (dlp, alp, glp) = _lora_bounds(c, lora)
    dl, al, gl = lora
    total = o4 + dw_swa.shape[1]
    n = total // nb
    tm = _row_tile(d, total, dw_rkv.dtype.itemsize)

    def body(a, b, s, o):
        bv = b[...]
        w = jnp.concatenate([a[...], bv[:, :dl], bv[:, dlp:dlp + al], bv[:, dlp + alp:dlp + alp + gl], s[...]], axis=-1)
        for j in range(nb):
            o[j] = w[:, n * j:n * (j + 1)]

    ins = [dw_rkv, dw_lora, dw_swa]
    return _pcall(body, name="w_in_merge", grid=(d // tm,), in_specs=[pl.BlockSpec((tm, a.shape[1]), lambda i: (i, 0)) for a in ins],
                  out_specs=pl.BlockSpec((nb, tm, n), lambda i: (0, i, 0)), out_shape=jax.ShapeDtypeStruct((nb, d, n), dw_rkv.dtype),
                  compiler_params=_params(("parallel",)))(*ins)


def _norm_bwd(name, x, g_norm, dh, dres, comm=None):
    d = x.shape[1]

    def fn(xb, dhb, drb, g):
        _, vjp = jax.vjp(_rms, xb, g)
        dx, dg = vjp(dhb)
        return drb + dx, dg

    return _rows(name, fn, [x, dh, dres], [g_norm], [(d, f32)], [((1, d), f32)], comm=comm)


def _colsum(name, a):
    return _rows(name, lambda ab: (jnp.sum(ab.astype(f32), axis=0, keepdims=True),), [a], [], [], [((1, a.shape[1]), f32)])[0]


def kernel(x, mem, f1_norm, f1_gate, f1_up, f1_down, mix_norm, w_in, b_in_attn, rw_mu, rw_w0, rw_decay_up, rw_a0, rw_aaa_up, rw_gate_up, rw_k_k, rw_k_a, rw_r_k, rw_lnx_w, rw_lnx_b, attn_sinks, w_out, b_out, xa_norm, mem_norm, w_xq, w_xkv, w_xo, f2_norm, f2_gate, f2_up, f2_down, final_norm, loss_target, m_f1_norm, m_f1_gate, m_f1_up, m_f1_down, m_mix_norm, m_w_in, m_b_in_attn, m_rw_mu, m_rw_w0, m_rw_decay_up, m_rw_a0, m_rw_aaa_up, m_rw_gate_up, m_rw_k_k, m_rw_k_a, m_rw_r_k, m_rw_lnx_w, m_rw_lnx_b, m_attn_sinks, m_w_out, m_b_out, m_xa_norm, m_mem_norm, m_w_xq, m_w_xkv, m_w_xo, m_f2_norm, m_f2_gate, m_f2_up, m_f2_down, m_final_norm, v_f1_norm, v_f1_gate, v_f1_up, v_f1_down, v_mix_norm, v_w_in, v_b_in_attn, v_rw_mu, v_rw_w0, v_rw_decay_up, v_rw_a0, v_rw_aaa_up, v_rw_gate_up, v_rw_k_k, v_rw_k_a, v_rw_r_k, v_rw_lnx_w, v_rw_lnx_b, v_attn_sinks, v_w_out, v_b_out, v_xa_norm, v_mem_norm, v_w_xq, v_w_xkv, v_w_xo, v_f2_norm, v_f2_gate, v_f2_up, v_f2_down, v_final_norm):
    names = ["f1_norm", "f1_gate", "f1_up", "f1_down", "mix_norm", "w_in", "b_in_attn", "rw_mu", "rw_w0", "rw_decay_up",
             "rw_a0", "rw_aaa_up", "rw_gate_up", "rw_k_k", "rw_k_a", "rw_r_k", "rw_lnx_w", "rw_lnx_b", "attn_sinks", "w_out",
             "b_out", "xa_norm", "mem_norm", "w_xq", "w_xkv", "w_xo", "f2_norm", "f2_gate", "f2_up", "f2_down", "final_norm"]
    env = dict(locals())
    w_of = {k: env[k] for k in names}
    m_of = {k: env["m_" + k] for k in names}
    v_of = {k: env["v_" + k] for k in names}
    col_sharded = ["f1_gate", "f1_up", "w_in", "rw_decay_up", "rw_aaa_up", "rw_gate_up", "w_xkv", "f2_gate", "f2_up"]
    row_sharded = ["f1_down", "w_out", "w_xq", "w_xo", "f2_down"]
    sharded = col_sharded + row_sharded
    small = [k for k in names if k not in sharded]

    x0, mem0, tgt = x[0], mem[0], loss_target[0]
    t, d = x0.shape
    c = rw_w0.shape[-1]
    heads = c // HEAD_DIM
    dl, al, gl = rw_decay_up.shape[1], rw_aaa_up.shape[1], rw_gate_up.shape[1]
    dlp, alp, glp = _up128(dl), _up128(al), _up128(gl)
    swa_w = d - c
    hq, kvh = swa_w // HEAD_DIM, (b_in_attn.shape[-1] - swa_w) // (2 * HEAD_DIM)
    my_x, my_y, my_c = _position()
    c_idx = jnp.reshape(my_c, (1,)).astype(jnp.int32)
    chip_idx = jnp.reshape(2 * my_x + my_y, (1,)).astype(jnp.int32)

    shard2d = {k: w_of[k][0] for k in sharded}
    cast = {k: _rows("cast_" + k, lambda a: (a,), [shard2d[k]], [], [(shard2d[k].shape[1], bf16)], tm=_row_tile(*shard2d[k].shape))[0]
            for k in sharded}
    ffn1_keys, ffn2_keys = ["f1_gate", "f1_up", "f1_down"], ["f2_gate", "f2_up", "f2_down"]
    in_keys = ["w_in", "rw_decay_up", "rw_aaa_up", "rw_gate_up"]
    kept_in_blocks = ffn1_keys + ffn2_keys + ["w_in", "w_xkv"]

    def whole(k, g8):
        if k in kept_in_blocks:
            return g8
        if k in col_sharded:
            return g8.transpose(1, 0, 2).reshape(g8.shape[1], N_DEV * g8.shape[2])
        return g8.reshape(N_DEV * g8.shape[1], g8.shape[2])

    order = [ffn1_keys[:2], ["f1_down"], in_keys, ["w_out"], ["w_xq"], ["f2_gate"], ["f2_up"], ["w_xkv"], ["w_xo"], ["f2_down"]]
    started, token = _chip_sends_start("gather_start", [[cast[k] for k in g] for g in order])

    def gather_of(keys, after):
        gi = order.index(keys)
        return _finish_gather_comm(*_chip_sends_wait("gather_wait_%d" % gi, *started[gi], after))

    def wholes(keys, gathered):
        return {k: whole(k, g8) for k, g8 in zip(keys, gathered)}

    (h1,) = _rows("f1_norm", lambda xb, g, _: (_rms(xb, g),), [x0], [f1_norm, token], [(d, bf16)])
    full = wholes(ffn1_keys[:2], _comm_only("gather_f1", gather_of(ffn1_keys[:2], h1)))
    (gate1, up1, act1), gathered = _ffn_hidden("f1_hidden", h1, full["f1_gate"], full["f1_up"], comm=gather_of(["f1_down"], h1))
    full.update(wholes(["f1_down"], gathered))
    x1, gathered = _ffn_out("f1_out", act1, full["f1_down"], x0, comm=gather_of(in_keys, act1))
    full.update(wholes(in_keys, gathered))
    ffn1_saved = (h1, gate1, up1, act1)
    w_rkv, w_lora, w_swa = _win_split(full["w_in"], c, (dl, al, gl))
    o1, o2, o3, shift_cols, _ = _lora_bounds(c, (dl, al, gl))
    mu_rkv = rw_mu[:, :3 * c]
    mu_l = jnp.concatenate([_pad_to(rw_mu[:, o1:o2], dlp, 1), _pad_to(rw_mu[:, o2:o3], alp, 1),
                            _pad_to(rw_mu[:, o3:shift_cols], glp, 1)], axis=1)
    decay_up = _pad_to(full["rw_decay_up"], dlp, 0).astype(f32)
    aaa_up = _pad_to(full["rw_aaa_up"], alp, 0).astype(f32)
    gate_up = _pad_to(full["rw_gate_up"], glp, 0).astype(f32)
    head_pars = [p.reshape(heads, 1, HEAD_DIM) for p in (rw_k_k, rw_k_a, rw_r_k, rw_lnx_w, rw_lnx_b)]
    final_g = final_norm.reshape(1, d)

    (h2,) = _rows("mix_norm", lambda xb, g: (_rms(xb, g),), [x1], [mix_norm], [(d, bf16)])
    p_rkv, gathered = _mm("in_rkv", h2, w_rkv, "nn", f32, comm=gather_of(["w_out"], h2))
    full.update(wholes(["w_out"], gathered))
    p_l = _mm("in_lora", h2, w_lora, "nn", f32)
    za = _mm("in_swa", h2, w_swa, "nn", f32, bias=b_in_attn)
    pre_fn = functools.partial(_rwkv_pre_math, c, (dlp, alp, glp))
    pre_rows = [p_rkv, _shift_down(p_rkv), p_l, _shift_down(p_l)]
    pre_full = [mu_rkv, mu_l, rw_w0, rw_a0, decay_up, aaa_up, gate_up]
    (r_h, k_h, v_h, lw_h, a_h, g_t), gathered = _rows("rwkv_pre", pre_fn, pre_rows, pre_full,
                                                       [(c, f32, HEAD_DIM)] * 5 + [(c, f32)], tm=128, comm=gather_of(["w_xq"], p_l))
    full.update(wholes(["w_xq"], gathered))
    seqs = [r_h, k_h, v_h, a_h, lw_h]
    (y_heads, checkpoints), gathered = _rwkv_fwd(*seqs, head_pars, comm=gather_of(["f2_gate"], r_h))
    full.update(wholes(["f2_gate"], gathered))

    pos = jnp.arange(t, dtype=f32)
    inv_freq = ROPE_THETA ** (-jnp.arange(0, HEAD_DIM, 2, dtype=f32) / HEAD_DIM)
    ang = pos[:, None] * inv_freq[None, :]
    cos, sin = jnp.cos(ang), jnp.sin(ang)
    sinks3 = attn_sinks.reshape(kvh, hq // kvh, 1)
    swa_in = (*_swa_split(za, hq, kvh), cos, sin, sinks3)
    y_swa_heads, gathered = _swa_fwd(*swa_in, comm=gather_of(["f2_up"], swa_in[0]))
    full.update(wholes(["f2_up"], gathered))
    (ycat,) = _rows("mix_cat", lambda yb, gb, sb: (jnp.concatenate([yb * gb, sb], axis=1),), [y_heads, g_t, y_swa_heads], [],
                    [(d, bf16)])
    x2, gathered = _mm("mix_out", ycat, full["w_out"], "nn", f32, res=x1, bias=b_out, comm=gather_of(["w_xkv"], ycat))
    full.update(wholes(["w_xkv"], gathered))

    (h3,) = _rows("xa_norm", lambda xb, g: (_rms(xb, g),), [x2], [xa_norm], [(d, bf16)])
    (mem_n,) = _rows("mem_norm", lambda xb, g: (_rms(xb, g),), [mem0], [mem_norm], [(d, bf16)])
    q_x, gathered = _mm("xa_q", h3, full["w_xq"], "nn", bf16, comm=gather_of(["w_xo"], h3))
    full.update(wholes(["w_xo"], gathered))
    kv_x = _mm("xa_kv", mem_n, full["w_xkv"], "nn", bf16)
    o_x = _xattn_fwd(q_x, kv_x)
    x3 = _mm("xa_out", o_x, full["w_xo"], "nn", f32, res=x2)
    (h4,) = _rows("f2_norm", lambda xb, g: (_rms(xb, g),), [x3], [f2_norm], [(d, bf16)])
    (gate2, up2, act2), gathered = _ffn_hidden("f2_hidden", h4, full["f2_gate"], full["f2_up"], comm=gather_of(["f2_down"], h4))
    full.update(wholes(["f2_down"], gathered))
    x4 = _ffn_out("f2_out", act2, full["f2_down"], x3)
    ffn2_saved = (h4, gate2, up2, act2)

    def loss_fn(xb, tb, g):
        def per_row(xv, gv):
            return 0.5 * jnp.mean(jnp.square(_rms(xv, gv) - tb), axis=-1, keepdims=True)

        lrow, vjp = jax.vjp(per_row, xb, g)
        dxb, dgb = vjp(jnp.ones_like(lrow))
        return dxb, dgb, jnp.sum(lrow, axis=0, keepdims=True)

    dx4, d_final, loss_part = _rows("loss", loss_fn, [x4, tgt], [final_g], [(d, f32)], [((1, d), f32), ((1, 1), f32)])
    loss = lax.psum(loss_part[0, 0], ("x", "y", "c"))

    grads, small_g, out = {}, {"final_norm": d_final}, {}

    def pair_sums_of(tag, keys, carrier=None):
        blocks = []
        for k in keys:
            g2 = grads[k]
            rr, cc = shard2d[k].shape
            if k in kept_in_blocks:
                blocks.append(g2)
            else:
                blocks.append(g2.reshape(g2.shape[0], N_DEV, cc).transpose(1, 0, 2) if k in col_sharded else g2.reshape(N_DEV, rr, cc))
        if carrier is None:
            from_sibling = _comm_only("grads_to_sibling_" + tag, _sibling_comm(blocks))
        else:
            carried, from_sibling = carrier(_sibling_comm(blocks))
        pairs = [_pair_add("pair_add_" + k, b, o, c_idx) for k, b, o in zip(keys, blocks, from_sibling)]
        return pairs if carrier is None else (pairs, carried)

    def update(keys, pair_sums, from_chips):
        for k, part, others in zip(keys, pair_sums, from_chips):
            res = _adam_sharded("adam_" + k, shard2d[k], m_of[k][0], v_of[k][0], part, others, chip_idx)
            out[k] = [a.reshape(w_of[k].shape) for a in res]

    def ffn_backward(tag, keys, xin, g_norm, saved, dout, first_comm, ride_on_dh):
        h, gate, up, act = saved
        k_gate, k_up, k_down = keys
        if first_comm is None:
            grads[k_down], carried = _ffn_dw_down(tag + "_dw_down", act, dout), None
        else:
            grads[k_down], carried = _ffn_dw_down(tag + "_dw_down", act, dout, comm=first_comm)
        down_pairs, (dgate, dup) = pair_sums_of(
            k_down, [k_down], lambda cm: _ffn_dhidden(tag + "_dhidden", dout, full[k_down], gate, up, comm=cm))
        (grads[k_gate], grads[k_up]), from_chips = _ffn_dw_hidden(tag + "_dw_hidden", h, dgate, dup, comm=_chips_comm(down_pairs))
        update([k_down], down_pairs, from_chips)
        if ride_on_dh:
            hidden_pairs = pair_sums_of(tag + "_hidden", [k_gate, k_up])
            dh, from_chips = _ffn_dh(tag + "_dh1", dgate, full[k_gate], comm=_chips_comm(hidden_pairs[:1]))
            update([k_gate], hidden_pairs[:1], from_chips)
            dh, from_chips = _ffn_dh(tag + "_dh2", dup, full[k_up], res=dh, comm=_chips_comm(hidden_pairs[1:]))
            update([k_up], hidden_pairs[1:], from_chips)
            hidden_pairs = None
        else:
            hidden_pairs, dh = pair_sums_of(tag + "_hidden", [k_gate, k_up],
                                            lambda cm: _ffn_dh(tag + "_dh1", dgate, full[k_gate], comm=cm))
            dh = _ffn_dh(tag + "_dh2", dup, full[k_up], res=dh)
        dx, dg_norm = _norm_bwd(tag + "_dnorm", xin, g_norm, dh, dout)
        return dx, dg_norm, hidden_pairs, carried

    xa_keys = ["w_xq", "w_xkv", "w_xo"]
    dx3, small_g["f2_norm"], ffn2_pairs, _ = ffn_backward("f2b", ffn2_keys, x3, f2_norm, ffn2_saved, dx4, None, False)

    do_x = _mm("xa_do", dx3, full["w_xo"], "nt", bf16)
    grads["w_xo"] = _mm("xa_dwo", o_x, dx3, "tn", bf16)
    dq_x, dk_x, dv_x = _xattn_bwd(q_x, kv_x, do_x)
    grads["w_xq"] = _mm("xa_dwq", h3, dq_x, "tn", bf16)
    dh3 = _mm("xa_dh", dq_x, full["w_xq"], "nt", f32)
    dkv_x = jnp.concatenate([dk_x, dv_x], axis=1)
    grads["w_xkv"] = _mm("xa_dwkv", mem_n, dkv_x, "tn", bf16, out_blocks=N_DEV)
    dkv_blocks = dkv_x.astype(bf16).reshape(dkv_x.shape[0], N_DEV, -1).transpose(1, 0, 2)
    dmem_n = _ffn_dh("xa_dmem", dkv_blocks, full["w_xkv"])
    _, small_g["mem_norm"] = _norm_bwd("mem_dnorm", mem0, mem_norm, dmem_n, jnp.zeros_like(mem0))
    xa_pairs, (dx2, small_g["xa_norm"]) = pair_sums_of(
        "xa", xa_keys, lambda cm: _norm_bwd("xa_dnorm", x2, xa_norm, dh3, dx3, comm=cm))

    dycat = _mm("mix_dy", dx2, full["w_out"], "nt", f32)
    grads["w_out"] = _mm("mix_dwout", ycat, dx2, "tn", bf16)
    small_g["b_out"] = _colsum("mix_dbout", dx2)
    out_pairs, (dy_heads, dg_t, do_sw) = pair_sums_of("out", ["w_out"], lambda cm: _rows(
        "mix_dgate", lambda db, yb, gb: (db[:, :c] * gb, db[:, :c] * yb, db[:, c:]), [dycat, y_heads, g_t], [],
        [(c, f32, HEAD_DIM), (c, f32), (swa_w, f32, HEAD_DIM)], comm=cm))
    rw_grads, from_chips = _rwkv_bwd(*seqs, head_pars, checkpoints, dy_heads, comm=_chips_comm(ffn2_pairs))
    update(ffn2_keys[:2], ffn2_pairs, from_chips)
    dr_h, dk_h, dv_h, da_h, dlw_h = rw_grads[:5]
    for nm, gh in zip(("rw_k_k", "rw_k_a", "rw_r_k", "rw_lnx_w", "rw_lnx_b"), rw_grads[5:]):
        small_g[nm] = gh.reshape(w_of[nm].shape)

    def pre_bwd(*args):
        _, vjp = jax.vjp(pre_fn, *args[:4], *args[10:])
        return vjp(tuple(args[4:10]))

    pre_cts = [dr_h, dk_h, dv_h, dlw_h, da_h, dg_t]
    pre_out = _rows("rwkv_pre_bwd", pre_bwd, pre_rows + pre_cts, pre_full,
                    [(3 * c, f32), (3 * c, f32), (dlp + alp + glp, f32), (dlp + alp + glp, f32)],
                    [(p.shape, f32) for p in pre_full], tm=128)
    dp_rkv = pre_out[0] + _shift_up(pre_out[1])
    dp_l = pre_out[2] + _shift_up(pre_out[3])
    dmu_rkv, dmu_l, small_g["rw_w0"], small_g["rw_a0"], d_decay_up, d_aaa_up, d_gate_up = pre_out[4:]
    small_g["rw_mu"] = jnp.concatenate([dmu_rkv, dmu_l[:, :dl], dmu_l[:, dlp:dlp + al], dmu_l[:, dlp + alp:dlp + alp + gl]], axis=1)
    grads["rw_decay_up"] = d_decay_up[:dl].astype(bf16)
    grads["rw_aaa_up"] = d_aaa_up[:al].astype(bf16)
    grads["rw_gate_up"] = d_gate_up[:gl].astype(bf16)

    sw, from_chips = _swa_bwd(*swa_in, do_sw, comm=_chips_comm(xa_pairs))
    update(xa_keys, xa_pairs, from_chips)
    small_g["attn_sinks"] = sw[8].reshape(attn_sinks.shape)
    dza, small_g["b_in_attn"] = _swa_merge(sw[0], sw[1], sw[3], sw[2], sw[5], sw[4], sw[7], sw[6])

    dw_rkv = _mm("in_dwrkv", h2, dp_rkv, "tn", bf16)
    dw_l = _mm("in_dwlora", h2, dp_l, "tn", bf16)
    dw_swa = _mm("in_dwswa", h2, dza, "tn", bf16)
    grads["w_in"] = _win_merge(dw_rkv, dw_l, dw_swa, c, (dl, al, gl), N_DEV)
    dh2, from_chips = _mm("in_dh1", dp_rkv, w_rkv, "nt", f32, comm=_chips_comm(out_pairs))
    update(["w_out"], out_pairs, from_chips)
    dh2 = _mm("in_dh2", dp_l, w_lora, "nt", f32, res=dh2)
    in_pairs, dh2 = pair_sums_of("in", in_keys, lambda cm: _mm("in_dh3", dza, w_swa, "nt", f32, res=dh2, comm=cm))
    dx1, small_g["mix_norm"] = _norm_bwd("mix_dnorm", x1, mix_norm, dh2, dx2)

    dx0, small_g["f1_norm"], _, from_chips = ffn_backward("f1b", ffn1_keys, x0, f1_norm, ffn1_saved, dx1, _chips_comm(in_pairs), True)
    update(in_keys, in_pairs, from_chips)

    sizes = [int(w_of[k].size) for k in small]
    total = sum(sizes)
    cols = -(-total // (8 * LANE)) * LANE

    def pack(parts_of):
        flat = jnp.concatenate([parts_of[k].reshape(-1).astype(f32) for k in small])
        return _pad_to(flat, 8 * cols, 0).reshape(8, cols)

    (all_parts,) = _comm_only("gather_small_grads", _gather_comm([pack(small_g)]))
    res = _adam_small("adam_small", pack(w_of), pack(m_of), pack(v_of), all_parts)
    offs = 0
    flat_res = [a.reshape(-1) for a in res]
    for k, sz in zip(small, sizes):
        out[k] = [a[offs:offs + sz].reshape(w_of[k].shape) for a in flat_res]
        offs += sz

    outs = [loss, dx0.reshape(x.shape)]
    for j in range(4):
        outs += [out[k][j] for k in names]
    return tuple(outs)
```

```python
import functools
import math

import jax
import jax.numpy as jnp
from jax import lax
from jax.experimental import pallas as pl
from jax.experimental.pallas import tpu as pltpu

f32 = jnp.float32
bf16 = jnp.bfloat16
MXU_DTYPE = jnp.bfloat16

HEAD_DIM = 64
SWA_BLOCK = 128
ROPE_THETA = 10000.0
XATTN_HEADS = 4
RMS_EPS = 1e-6
GN_EPS = 64e-5
NEG_INF = -1e30
RWKV_CHUNK = 64

ADAM_LR = 0.001
ADAM_B1 = 0.9
ADAM_B2 = 0.999
ADAM_EPS = 1e-08
ADAM_WD = 0.01
ADAM_STEP = 10

N_DEV = 8
LANE = 128
VMEM_LIMIT_BYTES = 56 * 1024 * 1024
MM_VMEM_BUDGET = 40 * 1024 * 1024
MESH = pl.DeviceIdType.MESH


def _params(sem):
    return pltpu.CompilerParams(dimension_semantics=sem, vmem_limit_bytes=VMEM_LIMIT_BYTES)


class _Comm:
    def __init__(self, ins, outs, n_remote, n_local, start, finish):
        self.ins, self.outs, self.n_remote, self.n_local = list(ins), list(outs), n_remote, max(n_local, 1)
        self.start, self.finish = start, finish


def _pcall(body, comm=None, **kw):
    kw.setdefault("compiler_params", pltpu.CompilerParams(vmem_limit_bytes=VMEM_LIMIT_BYTES))
    if comm is None:
        return pl.pallas_call(body, **kw)
    single = not isinstance(kw["out_shape"], (list, tuple))
    out_shape = [kw["out_shape"]] if single else list(kw["out_shape"])
    out_specs = [kw["out_specs"]] if single else list(kw["out_specs"])
    in_specs, scratch, grid = list(kw["in_specs"]), list(kw.get("scratch_shapes", ())), tuple(kw.get("grid", ()))
    n_in, n_out, n_ci, n_co, n_scr = len(in_specs), len(out_shape), len(comm.ins), len(comm.outs), len(scratch)

    def wrapped(*refs):
        ins, c_ins = refs[:n_in], refs[n_in:n_in + n_ci]
        outs = refs[n_in + n_ci:n_in + n_ci + n_out]
        c_outs = refs[n_in + n_ci + n_out:n_in + n_ci + n_out + n_co]
        rest = refs[n_in + n_ci + n_out + n_co:]
        scr, sems = rest[:n_scr], rest[n_scr:]
        if grid:
            ids = [pl.program_id(k) for k in range(len(grid))]
            first = functools.reduce(jnp.logical_and, [i == 0 for i in ids])
            last = functools.reduce(jnp.logical_and, [i == g - 1 for i, g in zip(ids, grid)])
            pl.when(first)(lambda: comm.start(c_ins, c_outs, *sems))
            body(*ins, *outs, *scr)
            pl.when(last)(lambda: comm.finish(c_ins, c_outs, *sems))
        else:
            comm.start(c_ins, c_outs, *sems)
            body(*ins, *outs, *scr)
            comm.finish(c_ins, c_outs, *sems)

    any_spec = pl.BlockSpec(memory_space=pl.ANY)
    kw.update(in_specs=in_specs + [any_spec] * n_ci, out_specs=out_specs + [any_spec] * n_co,
              out_shape=out_shape + comm.outs,
              scratch_shapes=scratch + [pltpu.SemaphoreType.DMA((comm.n_remote,)), pltpu.SemaphoreType.DMA((comm.n_remote,)),
                                        pltpu.SemaphoreType.DMA((comm.n_local,))])
    if grid:
        kw["compiler_params"] = _params(("arbitrary",) * len(grid))
    call = pl.pallas_call(wrapped, **kw)

    def run(*args):
        res = call(*args, *comm.ins)
        return (res[0] if single else list(res[:n_out])), list(res[n_out:])

    return run


def _comm_only(name, comm):
    return _pcall(lambda: None, comm=comm, name=name, in_specs=[], out_specs=[], out_shape=[])()[1]


def _dims(kind, ndim):
    o = ndim - 2
    batch = ((0,), (0,)) if o else ((), ())
    c = {"nn": ((1 + o,), (o,)), "nt": ((1 + o,), (1 + o,)), "tn": ((o,), (o,))}[kind]
    return (c, batch)


def _dot_raw(x, y, kind):
    return lax.dot_general(x.astype(MXU_DTYPE), y.astype(MXU_DTYPE), _dims(kind, x.ndim), preferred_element_type=f32)


@functools.partial(jax.custom_vjp, nondiff_argnums=(2,))
def _dot(x, y, kind):
    return _dot_raw(x, y, kind)


def _dot_fwd(x, y, kind):
    return _dot_raw(x, y, kind), (x, y)


def _dot_bwd(kind, res, g):
    x, y = res
    if kind == "nn":
        dx, dy = _dot(g, y, "nt"), _dot(x, g, "tn")
    elif kind == "nt":
        dx, dy = _dot(g, y, "nn"), _dot(g, x, "tn")
    else:
        dx, dy = _dot(y, g, "nt"), _dot(x, g, "nn")
    return dx.astype(x.dtype), dy.astype(y.dtype)


_dot.defvjp(_dot_fwd, _dot_bwd)


def _split3(x):
    a = x.astype(bf16)
    r = x - a.astype(f32)
    b = r.astype(bf16)
    c = (r - b.astype(f32)).astype(bf16)
    return a, b, c


def _rms(x, g):
    x = x.astype(f32)
    return x * lax.rsqrt(jnp.mean(x * x, axis=-1, keepdims=True) + RMS_EPS) * g


def _sigmoid(x):
    return 1.0 / (1.0 + jnp.exp(-x))


def _softplus(x):
    return jnp.maximum(x, 0.0) + jnp.log(1.0 + jnp.exp(-jnp.abs(x)))


def _rows(name, fn, row_ins, full_ins, row_outs, acc_outs=(), tm=None, comm=None):
    width = lambda a: a.shape[1] if a.ndim == 2 else a.shape[0] * a.shape[2]
    rows = row_ins[0].shape[0] if row_ins[0].ndim == 2 else row_ins[0].shape[1]
    if tm is None:
        tm = _row_tile(rows, max([width(a) for a in row_ins] + [o[0] for o in row_outs]))
    tm = min(tm, rows)
    assert rows % tm == 0, (name, rows, tm)
    n_in = len(row_ins) + len(full_ins)
    n_o, n_a = len(row_outs), len(acc_outs)

    def load(k, ref):
        if k < len(row_ins) and row_ins[k].ndim == 3:
            return jnp.concatenate([ref[h] for h in range(ref.shape[0])], axis=-1)
        return ref[...]

    def body(*refs):
        vals = [load(k, r) for k, r in enumerate(refs[:n_in])]
        outs = fn(*vals)
        o_refs = refs[n_in:n_in + n_o]
        a_refs = refs[n_in + n_o:]
        for k in range(n_o):
            if len(row_outs[k]) == 3:
                n = row_outs[k][2]
                for h in range(row_outs[k][0] // n):
                    o_refs[k][h] = outs[k][:, h * n:(h + 1) * n].astype(o_refs[k].dtype)
            else:
                o_refs[k][...] = outs[k].astype(o_refs[k].dtype)
        if n_a:
            first = pl.program_id(0) == 0

            @pl.when(first)
            def _():
                for k in range(n_a):
                    a_refs[k][...] = outs[n_o + k].astype(a_refs[k].dtype)

            @pl.when(jnp.logical_not(first))
            def _():
                for k in range(n_a):
                    a_refs[k][...] += outs[n_o + k].astype(a_refs[k].dtype)

    by_rows = lambda cols: pl.BlockSpec((tm, cols), lambda i: (i, 0))
    by_heads = lambda h, n: pl.BlockSpec((h, tm, n), lambda i: (0, i, 0))
    in_specs = [by_rows(a.shape[1]) if a.ndim == 2 else by_heads(a.shape[0], a.shape[2]) for a in row_ins]
    in_specs += [pl.BlockSpec(a.shape, lambda i, nd=a.ndim: (0,) * nd) for a in full_ins]
    out_specs = [by_rows(o[0]) if len(o) == 2 else by_heads(o[0] // o[2], o[2]) for o in row_outs]
    out_specs += [pl.BlockSpec(s, lambda i, nd=len(s): (0,) * nd) for s, _ in acc_outs]
    out_shape = [jax.ShapeDtypeStruct((rows, o[0]) if len(o) == 2 else (o[0] // o[2], rows, o[2]), o[1]) for o in row_outs]
    out_shape += [jax.ShapeDtypeStruct(s, d) for s, d in acc_outs]
    return _pcall(body, comm=comm, name=name, grid=(rows // tm,), in_specs=in_specs, out_specs=out_specs, out_shape=out_shape,
                  compiler_params=_params(("arbitrary",)))(*row_ins, *full_ins)


def _pick(n, cands):
    for c in cands:
        if n % c == 0:
            return c
    return n


def _mm(name, a, b, mode, out_dtype, scale=1.0, res=None, bias=None, comm=None, out_blocks=None):
    b_blocks = b.ndim == 3
    if b_blocks:
        assert mode == "nn"
        (m, k), (nb, k2, tn) = a.shape, b.shape
        n = nb * tn
    elif mode == "nn":
        (m, k), (k2, n) = a.shape, b.shape
    elif mode == "nt":
        (m, k), (n, k2) = a.shape, b.shape
    else:
        (k, m), (k2, n) = a.shape, b.shape
    assert k == k2, (name, a.shape, b.shape, mode)
    if not b_blocks:
        tn = n // out_blocks if out_blocks else _pick(n, (512, 256, 128))
    tm = _pick(m, (1024, 512, 256, 128))

    def need(tm_):
        by = tm_ * k * a.dtype.itemsize + tn * k * b.dtype.itemsize + tm_ * tn * (jnp.dtype(out_dtype).itemsize + 4)
        if res is not None:
            by += tm_ * tn * res.dtype.itemsize
        return 2 * by

    while need(tm) > MM_VMEM_BUDGET and tm % 256 == 0:
        tm //= 2
    dims = _dims(mode, 2)

    def body(*refs):
        bv = refs[1][0] if b_blocks else refs[1][...]
        acc = lax.dot_general(refs[0][...].astype(MXU_DTYPE), bv.astype(MXU_DTYPE), dims, preferred_element_type=f32)
        if scale != 1.0:
            acc = acc * scale
        pos = 2
        if bias is not None:
            acc = acc + refs[pos][...]
            pos += 1
        if res is not None:
            acc = acc + refs[pos][...].astype(f32)
            pos += 1
        if out_blocks:
            refs[pos][0] = acc.astype(out_dtype)
        else:
            refs[pos][...] = acc.astype(out_dtype)

    a_spec = pl.BlockSpec((k, tm), lambda i, j: (0, i)) if mode == "tn" else pl.BlockSpec((tm, k), lambda i, j: (i, 0))
    if b_blocks:
        b_spec = pl.BlockSpec((1, k, tn), lambda i, j: (j, 0, 0))
    else:
        b_spec = pl.BlockSpec((tn, k), lambda i, j: (j, 0)) if mode == "nt" else pl.BlockSpec((k, tn), lambda i, j: (0, j))
    in_specs, args = [a_spec, b_spec], [a, b]
    if bias is not None:
        in_specs.append(pl.BlockSpec((1, tn), lambda i, j: (0, j)))
        args.append(bias)
    if res is not None:
        in_specs.append(pl.BlockSpec((tm, tn), lambda i, j: (i, j)))
        args.append(res)
    if out_blocks:
        out_spec, out_shape = pl.BlockSpec((1, tm, tn), lambda i, j: (j, i, 0)), jax.ShapeDtypeStruct((out_blocks, m, tn), out_dtype)
    else:
        out_spec, out_shape = pl.BlockSpec((tm, tn), lambda i, j: (i, j)), jax.ShapeDtypeStruct((m, n), out_dtype)
    return _pcall(body, comm=comm, name=name, grid=(m // tm, n // tn), in_specs=in_specs, out_specs=out_spec, out_shape=out_shape,
                  compiler_params=_params(("parallel", "parallel")))(*args)


def _position():
    return lax.axis_index("x"), lax.axis_index("y"), lax.axis_index("c")


def _gather_comm(shards):
    n = len(shards)

    def plan(x_refs, o_refs, send_sems, recv_sems, local_sems):
        x, y, c = _position()
        me, sibling = (x, y, c), (x, y, 1 - c)
        chips = [(1 - x, y), (x, 1 - y), (1 - x, 1 - y)]

        def slot(px, py, pc):
            return 4 * px + 2 * py + pc

        def copy(t, k, block, to, src=None):
            dst = o_refs[t].at[slot(*block)]
            return pltpu.make_async_remote_copy(src_ref=dst if src is None else src, dst_ref=dst,
                                                send_sem=send_sems.at[7 * t + k], recv_sem=recv_sems.at[7 * t + k],
                                                device_id=to, device_id_type=MESH)

        mine = [pltpu.make_async_copy(x_refs[t], o_refs[t].at[slot(*me)], local_sems.at[t]) for t in range(n)]
        first = []
        for t in range(n):
            first.append(copy(t, 0, me, sibling, src=x_refs[t]))
            first += [copy(t, 1 + j, me, (*chip, c), src=x_refs[t]) for j, chip in enumerate(chips)]
        return me, sibling, chips, c, copy, mine, first

    def start(*refs):
        _, _, _, _, _, mine, first = plan(*refs)
        for cp in mine + first:
            cp.start()

    def finish(*refs):
        me, sibling, chips, c, copy, mine, first = plan(*refs)
        passed = []
        for t in range(n):
            for j, chip in enumerate(chips):
                copy(t, 1 + j, (*chip, c), me).wait_recv()
                cp = copy(t, 4 + j, (*chip, c), sibling)
                cp.start()
                passed.append(cp)
        for t in range(n):
            copy(t, 0, sibling, me).wait_recv()
            for j, chip in enumerate(chips):
                copy(t, 4 + j, (*chip, 1 - c), me).wait_recv()
        for cp in first + passed:
            cp.wait_send()
        for cp in mine:
            cp.wait()

    outs = [jax.ShapeDtypeStruct((N_DEV,) + s.shape, s.dtype) for s in shards]
    return _Comm(shards, outs, 7 * n, n, start, finish)


HBM_SPEC = pl.BlockSpec(memory_space=pltpu.HBM)
SEM_SPEC = pl.BlockSpec(memory_space=pltpu.SEMAPHORE)
DATAFLOW = pltpu.SideEffectType.DATAFLOW_SIDE_EFFECTING


def _chip_copies(shard_refs, land_refs, send_sems, recv_sems):
    x, y, c = _position()
    chips = [(1 - x, y), (x, 1 - y), (1 - x, 1 - y)]
    return [pltpu.make_async_remote_copy(src_ref=shard_refs[t], dst_ref=land_refs[t].at[j], send_sem=send_sems.at[3 * t + j],
                                         recv_sem=recv_sems.at[3 * t + j], device_id=(px, py, c), device_id_type=MESH)
            for t in range(len(shard_refs)) for j, (px, py) in enumerate(chips)]


def _chip_sends_start(name, groups, after=None):
    flat = [s for g in groups for s in g]
    n, ng = len(flat), len(groups)
    n_in = 2 * n + (after is not None)

    def body(*refs):
        s_in, l_in, sems = refs[:n], refs[n:2 * n], refs[n_in:n_in + 2 * ng]
        off = 0
        for gi, g in enumerate(groups):
            for cp in _chip_copies(s_in[off:off + len(g)], l_in[off:off + len(g)], sems[2 * gi], sems[2 * gi + 1]):
                cp.start()
            off += len(g)
        refs[-1][...] = jnp.zeros(refs[-1].shape, f32)

    lands = [lax.empty((3,) + s.shape, s.dtype) for s in flat]
    args = [pltpu.with_memory_space_constraint(a, pltpu.HBM) for a in flat + lands]
    out_shape = [pltpu.SemaphoreType.DMA((3 * len(g),)) for g in groups for _ in range(2)]
    out_shape += [pltpu.HBM(a.shape, a.dtype) for a in args] + [jax.ShapeDtypeStruct((8, LANE), f32)]
    extra = [] if after is None else [after]
    res = pl.pallas_call(body, name=name, out_shape=out_shape,
                         in_specs=[HBM_SPEC] * (2 * n) + [pl.BlockSpec(memory_space=pl.ANY)] * len(extra),
                         out_specs=[SEM_SPEC] * (2 * ng) + [HBM_SPEC] * (2 * n) + [pl.BlockSpec(memory_space=pltpu.VMEM)],
                         input_output_aliases={i: 2 * ng + i for i in range(2 * n)},
                         compiler_params=pltpu.CompilerParams(has_side_effects=DATAFLOW))(*args, *extra)
    started, off = [], 0
    for gi, g in enumerate(groups):
        k = len(g)
        started.append((res[2 * gi], res[2 * gi + 1], list(res[2 * ng + off:2 * ng + off + k]),
                        list(res[2 * ng + n + off:2 * ng + n + off + k])))
        off += k
    return started, res[-1]


def _chip_sends_wait(name, send_sems, recv_sems, shards, lands, after):
    n = len(shards)

    def body(*refs):
        for cp in _chip_copies(refs[:n], refs[n:2 * n], refs[2 * n], refs[2 * n + 1]):
            cp.wait_send()
            cp.wait_recv()

    res = pl.pallas_call(body, name=name, out_shape=[pltpu.HBM(a.shape, a.dtype) for a in shards + lands],
                         in_specs=[HBM_SPEC] * (2 * n) + [SEM_SPEC, SEM_SPEC, pl.BlockSpec(memory_space=pl.ANY)],
                         out_specs=[HBM_SPEC] * (2 * n), input_output_aliases={i: i for i in range(2 * n)},
                         compiler_params=pltpu.CompilerParams(has_side_effects=DATAFLOW))(*shards, *lands, send_sems, recv_sems, after)
    return list(res[:n]), list(res[n:])


def _finish_gather_comm(shards, lands, extra=()):
    n = len(shards)

    def plan(refs):
        s_refs, l_refs, o_refs = refs[0][:n], refs[0][n:2 * n], refs[1]
        send_sems, recv_sems, local_sems = refs[2:]
        x, y, c = _position()
        chips = [(1 - x, y), (x, 1 - y), (1 - x, 1 - y)]
        local, remote = [], []
        for t in range(n):
            pieces = [(s_refs[t], 4 * x + 2 * y)] + [(l_refs[t].at[j], 4 * px + 2 * py) for j, (px, py) in enumerate(chips)]
            for k, (src, chip_slot) in enumerate(pieces):
                local.append(pltpu.make_async_copy(src, o_refs[t].at[chip_slot + c], local_sems.at[4 * t + k]))
                remote.append(pltpu.make_async_remote_copy(src_ref=src, dst_ref=o_refs[t].at[chip_slot + c],
                                                           send_sem=send_sems.at[4 * t + k], recv_sem=recv_sems.at[4 * t + k],
                                                           device_id=(x, y, 1 - c), device_id_type=MESH))
        return local, remote

    def start(*refs):
        local, remote = plan(refs)
        for cp in local + remote:
            cp.start()

    def finish(*refs):
        local, remote = plan(refs)
        for cp in local + remote:
            cp.wait()

    outs = [jax.ShapeDtypeStruct((N_DEV,) + s.shape, s.dtype) for s in shards]
    return _Comm(list(shards) + list(lands) + list(extra), outs, 4 * n, 4 * n, start, finish)


def _sibling_comm(blocks):
    n = len(blocks)

    def copies(g_refs, o_refs, send_sems, recv_sems, _):
        x, y, c = _position()
        return [pltpu.make_async_remote_copy(src_ref=g_refs[t].at[2 * q + 1 - c], dst_ref=o_refs[t].at[q],
                                             send_sem=send_sems.at[4 * t + q], recv_sem=recv_sems.at[4 * t + q],
                                             device_id=(x, y, 1 - c), device_id_type=MESH)
                for t in range(n) for q in range(4)]

    def start(*refs):
        for cp in copies(*refs):
            cp.start()

    def finish(*refs):
        for cp in copies(*refs):
            cp.wait()

    outs = [jax.ShapeDtypeStruct((4,) + g.shape[1:], g.dtype) for g in blocks]
    return _Comm(blocks, outs, 4 * n, 0, start, finish)


def _chips_comm(parts):
    n = len(parts)

    def copies(p_refs, o_refs, send_sems, recv_sems, _):
        x, y, c = _position()
        chips = [(1 - x, y), (x, 1 - y), (1 - x, 1 - y)]
        return [pltpu.make_async_remote_copy(src_ref=p_refs[t].at[2 * px + py], dst_ref=o_refs[t].at[j],
                                             send_sem=send_sems.at[3 * t + j], recv_sem=recv_sems.at[3 * t + j],
                                             device_id=(px, py, c), device_id_type=MESH)
                for t in range(n) for j, (px, py) in enumerate(chips)]

    def start(*refs):
        for cp in copies(*refs):
            cp.start()

    def finish(*refs):
        for cp in copies(*refs):
            cp.wait()

    outs = [jax.ShapeDtypeStruct((3,) + p.shape[1:], p.dtype) for p in parts]
    return _Comm(parts, outs, 3 * n, 0, start, finish)


ROW_TILE_BYTES = 2 << 20


def _row_tile(r, cols, itemsize=4):
    fits = [t for t in range(8, r + 1, 8) if r % t == 0 and t * cols * itemsize <= ROW_TILE_BYTES]
    return max(fits) if fits else r


def _pair_add(name, g, got, c_idx):
    _, r, cc = g.shape
    tr = _row_tile(r, cc, g.dtype.itemsize)

    def body(c_ref, g_ref, o_ref, out_ref):
        out_ref[...] = (g_ref[...].astype(f32) + o_ref[...].astype(f32)).astype(out_ref.dtype)

    g5 = g.reshape(4, 2, r, cc)
    spec = pltpu.PrefetchScalarGridSpec(
        num_scalar_prefetch=1, grid=(4, r // tr),
        in_specs=[pl.BlockSpec((1, 1, tr, cc), lambda q, i, c_ref: (q, c_ref[0], i, 0)),
                  pl.BlockSpec((1, 1, tr, cc), lambda q, i, c_ref: (q, 0, i, 0))],
        out_specs=pl.BlockSpec((1, 1, tr, cc), lambda q, i, c_ref: (q, 0, i, 0)))
    out = _pcall(body, name=name, grid_spec=spec, out_shape=jax.ShapeDtypeStruct((4, 1, r, cc), g.dtype),
                 compiler_params=_params(("arbitrary", "arbitrary")))(c_idx, g5, got.reshape(4, 1, r, cc))
    return out.reshape(4, r, cc)


def _adam_math(w, g, m, v):
    m2 = ADAM_B1 * m + (1.0 - ADAM_B1) * g
    v2 = ADAM_B2 * v + (1.0 - ADAM_B2) * (g * g)
    m_hat = m2 / (1.0 - ADAM_B1 ** ADAM_STEP)
    v_hat = v2 / (1.0 - ADAM_B2 ** ADAM_STEP)
    delta = -ADAM_LR * (m_hat / (jnp.sqrt(v_hat) + ADAM_EPS) + ADAM_WD * w)
    return delta, m2, v2


def _adam_sharded(name, w, m, v, part, got, chip_idx):
    r, cc = w.shape
    tr = _row_tile(r, cc)

    def body(q_ref, w_ref, m_ref, v_ref, p_ref, o_ref, g_out, d_out, m_out, v_out):
        g = p_ref[0].astype(f32)
        for j in range(3):
            g = g + o_ref[j].astype(f32)
        d, m2, v2 = _adam_math(w_ref[...], g, m_ref[...], v_ref[...])
        g_out[...] = g
        d_out[...] = d
        m_out[...] = m2
        v_out[...] = v2

    row = pl.BlockSpec((tr, cc), lambda i, q_ref: (i, 0))
    spec = pltpu.PrefetchScalarGridSpec(
        num_scalar_prefetch=1, grid=(r // tr,),
        in_specs=[row, row, row, pl.BlockSpec((1, tr, cc), lambda i, q_ref: (q_ref[0], i, 0)),
                  pl.BlockSpec((3, tr, cc), lambda i, q_ref: (0, i, 0))],
        out_specs=[row, row, row, row])
    sh = jax.ShapeDtypeStruct((r, cc), f32)
    return _pcall(body, name=name, grid_spec=spec, out_shape=[sh, sh, sh, sh],
                  compiler_params=_params(("arbitrary",)))(chip_idx, w, m, v, part, got)


def _adam_small(name, w, m, v, parts):
    def body(w_ref, m_ref, v_ref, p_ref, g_out, d_out, m_out, v_out):
        g = p_ref[0]
        for b in range(1, N_DEV):
            g = g + p_ref[b]
        d, m2, v2 = _adam_math(w_ref[...], g, m_ref[...], v_ref[...])
        g_out[...] = g
        d_out[...] = d
        m_out[...] = m2
        v_out[...] = v2

    sh = jax.ShapeDtypeStruct(w.shape, f32)
    return _pcall(body, name=name, out_shape=[sh, sh, sh, sh])(w, m, v, parts)


def _swa_math(n, qa, qb, kap, kac, kbp, kbc, vp, vc, cq, sq, cp, sp, sink):
    g, blk, half = qa.shape
    c3, s3 = cq[None], sq[None]
    q1 = (qa * c3 - qb * s3).reshape(g * blk, half)
    q2 = (qb * c3 + qa * s3).reshape(g * blk, half)
    ck, sk = jnp.concatenate([cp, cq], axis=0), jnp.concatenate([sp, sq], axis=0)
    k1, k2 = jnp.concatenate([kap[0], kac[0]], axis=0), jnp.concatenate([kbp[0], kbc[0]], axis=0)
    k1r, k2r = k1 * ck - k2 * sk, k2 * ck + k1 * sk
    vv = jnp.concatenate([vp[0], vc[0]], axis=0)
    s = (_dot(q1, k1r, "nt") + _dot(q2, k2r, "nt")) * (HEAD_DIM ** -0.5)
    s = s.reshape(g, blk, 2 * blk)
    qi = lax.broadcasted_iota(jnp.int32, (blk, 2 * blk), 0)
    kj = lax.broadcasted_iota(jnp.int32, (blk, 2 * blk), 1)
    valid = (kj > qi) & (kj <= qi + blk) & ((kj >= blk) | (n > 0))
    s = jnp.where(valid[None], s, NEG_INF)
    sink3 = sink.reshape(g, 1, 1)
    mx = jnp.maximum(jnp.max(s, axis=-1, keepdims=True), sink3)
    e = jnp.exp(s - mx)
    z = jnp.sum(e, axis=-1, keepdims=True) + jnp.exp(sink3 - mx)
    p = (e / z).reshape(g * blk, 2 * blk)
    return _dot(p, vv, "nn").reshape(g, blk, 2 * half)


def _swa_specs(g, blk, half):
    prev = lambda n: jnp.maximum(n - 1, 0)
    q_spec = pl.BlockSpec((g, blk, half), lambda h, n: (h, n, 0))
    kc = pl.BlockSpec((1, blk, half), lambda h, n: (h, n, 0))
    kp = pl.BlockSpec((1, blk, half), lambda h, n: (h, prev(n), 0))
    vc = pl.BlockSpec((1, blk, 2 * half), lambda h, n: (h, n, 0))
    vp = pl.BlockSpec((1, blk, 2 * half), lambda h, n: (h, prev(n), 0))
    tc = pl.BlockSpec((blk, half), lambda h, n: (n, 0))
    tp = pl.BlockSpec((blk, half), lambda h, n: (prev(n), 0))
    sink = pl.BlockSpec((1, g, 1), lambda h, n: (h, 0, 0))
    o_spec = pl.BlockSpec((g, blk, 2 * half), lambda h, n: (h, n, 0))
    return q_spec, kc, kp, vc, vp, tc, tp, sink, o_spec


def _swa_fwd(qa, qb, ka, kb, v, cos, sin, sinks, comm=None):
    hq, t, half = qa.shape
    kv = ka.shape[0]
    g, blk = hq // kv, SWA_BLOCK
    q_spec, kc, kp, vc, vp, tc, tp, sink, o_spec = _swa_specs(g, blk, half)

    def body(qa_r, qb_r, kap, kac, kbp, kbc, vp_r, vc_r, cq, sq, cp, sp, sink_r, o_r):
        o_r[...] = _swa_math(pl.program_id(1), qa_r[...], qb_r[...], kap[...], kac[...], kbp[...], kbc[...], vp_r[...],
                             vc_r[...], cq[...], sq[...], cp[...], sp[...], sink_r[...]).astype(o_r.dtype)

    return _pcall(body, comm=comm, name="swa_fwd", grid=(kv, t // blk),
                  in_specs=[q_spec, q_spec, kp, kc, kp, kc, vp, vc, tc, tc, tp, tp, sink], out_specs=o_spec,
                  out_shape=jax.ShapeDtypeStruct((hq, t, 2 * half), f32),
                  compiler_params=_params(("parallel", "arbitrary")))(qa, qb, ka, ka, kb, kb, v, v, cos, sin, cos, sin, sinks)


def _swa_bwd(qa, qb, ka, kb, v, cos, sin, sinks, do, comm=None):
    hq, t, half = qa.shape
    kv = ka.shape[0]
    g, blk = hq // kv, SWA_BLOCK
    q_spec, kc, kp, vc, vp, tc, tp, sink, o_spec = _swa_specs(g, blk, half)

    def body(qa_r, qb_r, kap, kac, kbp, kbc, vp_r, vc_r, cq, sq, cp, sp, sink_r, do_r,
             dqa, dqb, dkap, dkac, dkbp, dkbc, dvp, dvc, dsink):
        n = pl.program_id(1)
        tabs = (cq[...], sq[...], cp[...], sp[...])
        fn = lambda a, b, c_, d, e, f_, g_, h_, s_: _swa_math(n, a, b, c_, d, e, f_, g_, h_, *tabs, s_)
        _, vjp = jax.vjp(fn, qa_r[...], qb_r[...], kap[...], kac[...], kbp[...], kbc[...], vp_r[...], vc_r[...], sink_r[...])
        grads = vjp(do_r[...])
        for ref, val in zip((dqa, dqb, dkap, dkac, dkbp, dkbc, dvp, dvc), grads[:8]):
            ref[...] = val

        @pl.when(n == 0)
        def _():
            dsink[...] = grads[8]

        @pl.when(n > 0)
        def _():
            dsink[...] += grads[8]

    sh = lambda a: jax.ShapeDtypeStruct(a.shape, f32)
    return _pcall(body, comm=comm, name="swa_bwd", grid=(kv, t // blk),
                  in_specs=[q_spec, q_spec, kp, kc, kp, kc, vp, vc, tc, tc, tp, tp, sink, o_spec],
                  out_specs=[q_spec, q_spec, kc, kc, kc, kc, vc, vc, sink],
                  out_shape=[sh(qa), sh(qb), sh(ka), sh(ka), sh(kb), sh(kb), sh(v), sh(v), sh(sinks)],
                  compiler_params=_params(("parallel", "arbitrary")))(qa, qb, ka, ka, kb, kb, v, v, cos, sin, cos, sin, sinks, do)


def _swa_split(za, hq, kv):
    t = za.shape[0]
    half = HEAD_DIM // 2
    tm = _row_tile(t, za.shape[1])

    def body(z_r, qa, qb, ka, kb, v):
        z = z_r[...]
        for h in range(hq):
            qa[h] = z[:, HEAD_DIM * h:HEAD_DIM * h + half]
            qb[h] = z[:, HEAD_DIM * h + half:HEAD_DIM * (h + 1)]
        for h in range(kv):
            o = HEAD_DIM * (hq + h)
            ka[h] = z[:, o:o + half]
            kb[h] = z[:, o + half:o + HEAD_DIM]
            o = HEAD_DIM * (hq + kv + h)
            v[h] = z[:, o:o + HEAD_DIM]

    spec = lambda n, w: pl.BlockSpec((n, tm, w), lambda i: (0, i, 0))
    sh = lambda n, w: jax.ShapeDtypeStruct((n, t, w), f32)
    return _pcall(body, name="swa_split", grid=(t // tm,), in_specs=[pl.BlockSpec((tm, za.shape[1]), lambda i: (i, 0))],
                  out_specs=[spec(hq, half), spec(hq, half), spec(kv, half), spec(kv, half), spec(kv, HEAD_DIM)],
                  out_shape=[sh(hq, half), sh(hq, half), sh(kv, half), sh(kv, half), sh(kv, HEAD_DIM)],
                  compiler_params=_params(("parallel",)))(za)


def _swa_merge(dqa, dqb, dkac, dkap, dkbc, dkbp, dvc, dvp):
    hq, t, half = dqa.shape
    kv = dkac.shape[0]
    blk = SWA_BLOCK
    nb = t // blk
    cols = HEAD_DIM * (hq + 2 * kv)

    def body(qa, qb, kac, kap, kbc, kbp, vc, vp, z_o, s_o):
        i = pl.program_id(0)
        more = (i < nb - 1).astype(f32)
        pieces = []
        for h in range(hq):
            pieces += [qa[h], qb[h]]
        for h in range(kv):
            pieces += [kac[h] + more * kap[h], kbc[h] + more * kbp[h]]
        for h in range(kv):
            pieces.append(vc[h] + more * vp[h])
        z = jnp.concatenate(pieces, axis=-1)
        z_o[...] = z
        colsum = jnp.sum(z, axis=0, keepdims=True)

        @pl.when(i == 0)
        def _():
            s_o[...] = colsum

        @pl.when(i > 0)
        def _():
            s_o[...] += colsum

    cur = lambda n, w: pl.BlockSpec((n, blk, w), lambda i: (0, i, 0))
    nxt = lambda n, w: pl.BlockSpec((n, blk, w), lambda i: (0, jnp.minimum(i + 1, nb - 1), 0))
    return _pcall(body, name="swa_merge", grid=(nb,),
                  in_specs=[cur(hq, half), cur(hq, half), cur(kv, half), nxt(kv, half), cur(kv, half), nxt(kv, half),
                            cur(kv, HEAD_DIM), nxt(kv, HEAD_DIM)],
                  out_specs=[pl.BlockSpec((blk, cols), lambda i: (i, 0)), pl.BlockSpec((1, cols), lambda i: (0, 0))],
                  out_shape=[jax.ShapeDtypeStruct((t, cols), f32), jax.ShapeDtypeStruct((1, cols), f32)],
                  compiler_params=_params(("arbitrary",)))(dqa, dqb, dkac, dkap, dkbc, dkbp, dvc, dvp)


def _xattn_math(q, k, v):
    s = _dot(q, k, "nt") * (q.shape[-1] ** -0.5)
    e = jnp.exp(s - jnp.max(s, axis=-1, keepdims=True))
    p = e / jnp.sum(e, axis=-1, keepdims=True)
    return _dot(p, v, "nn")


def _xattn_fwd(q, kvm):
    t, d = q.shape
    mlen = kvm.shape[0]
    hd = d // XATTN_HEADS
    tq = min(512, t)

    def body(q_r, k_r, v_r, o_r):
        o_r[...] = _xattn_math(q_r[...], k_r[...], v_r[...]).astype(o_r.dtype)

    return _pcall(body, name="xattn_fwd", grid=(XATTN_HEADS, t // tq),
                  in_specs=[pl.BlockSpec((tq, hd), lambda h, i: (i, h)), pl.BlockSpec((mlen, hd), lambda h, i: (0, h)),
                            pl.BlockSpec((mlen, hd), lambda h, i: (0, XATTN_HEADS + h))],
                  out_specs=pl.BlockSpec((tq, hd), lambda h, i: (i, h)), out_shape=jax.ShapeDtypeStruct((t, d), bf16),
                  compiler_params=_params(("parallel", "parallel")))(q, kvm, kvm)


def _xattn_bwd(q, kvm, do):
    t, d = q.shape
    mlen = kvm.shape[0]
    hd = d // XATTN_HEADS
    tq = min(512, t)

    def body(q_r, k_r, v_r, do_r, dq, dk, dv):
        _, vjp = jax.vjp(_xattn_math, q_r[...].astype(f32), k_r[...].astype(f32), v_r[...].astype(f32))
        gq, gk, gv = vjp(do_r[...].astype(f32))
        dq[...] = gq.astype(dq.dtype)
        first = pl.program_id(1) == 0

        @pl.when(first)
        def _():
            dk[...] = gk
            dv[...] = gv

        @pl.when(jnp.logical_not(first))
        def _():
            dk[...] += gk
            dv[...] += gv

    qs = pl.BlockSpec((tq, hd), lambda h, i: (i, h))
    ms = pl.BlockSpec((mlen, hd), lambda h, i: (0, h))
    return _pcall(body, name="xattn_bwd", grid=(XATTN_HEADS, t // tq),
                  in_specs=[qs, ms, pl.BlockSpec((mlen, hd), lambda h, i: (0, XATTN_HEADS + h)), qs],
                  out_specs=[qs, ms, ms],
                  out_shape=[jax.ShapeDtypeStruct((t, d), bf16), jax.ShapeDtypeStruct((mlen, d), f32),
                             jax.ShapeDtypeStruct((mlen, d), f32)],
                  compiler_params=_params(("parallel", "arbitrary")))(q, kvm, kvm, do)


def _chunk_cumsum(lw, reverse=False):
    h, l, _ = lw.shape
    i = lax.broadcasted_iota(jnp.int32, (l, l), 0)
    j = lax.broadcasted_iota(jnp.int32, (l, l), 1)
    tri = jnp.broadcast_to(((i <= j) if reverse else (i >= j)).astype(bf16)[None], (h, l, l))
    out = jnp.zeros(lw.shape, f32)
    for piece in _split3(lw):
        out = out + lax.dot_general(tri, piece, _dims("nn", 3), preferred_element_type=f32)
    return out


def _rwkv_chunk(s0, r, k, v, a, lw, cl, k_k, k_a, r_k, ln_w, ln_b):
    l = r.shape[1]
    kk = k * k_k
    kk = kk / jnp.maximum(jnp.sqrt(jnp.sum(kk * kk, axis=-1, keepdims=True)), 1e-12)
    km = k * (1.0 + (a - 1.0) * k_a)
    av, bv = -kk, kk * a
    p_incl, p_excl, p_inv = jnp.exp(cl), jnp.exp(cl - lw), jnp.exp(-cl)
    at, bh, kh, rt = av * p_excl, bv * p_inv, km * p_inv, r * p_incl
    i = lax.broadcasted_iota(jnp.int32, (l, l), 0)
    j = lax.broadcasted_iota(jnp.int32, (l, l), 1)
    strict, incl = (i > j)[None], (i >= j)[None]
    a_ab = jnp.where(strict, _dot(at, bh, "nt"), 0.0)
    a_ak = jnp.where(strict, _dot(at, kh, "nt"), 0.0)
    a_rb = jnp.where(incl, _dot(rt, bh, "nt"), 0.0)
    a_rk = jnp.where(incl, _dot(rt, kh, "nt"), 0.0)
    rhs = _dot(at, s0, "nt") + _dot(a_ak, v, "nn")
    inv = a_ab + (i == j)[None].astype(f32)
    pw = a_ab
    for _ in range(int(math.log2(l)) - 1):
        pw = _dot(pw, pw, "nn")
        inv = inv + _dot(inv, pw, "nn")
    sa = _dot(inv, rhs, "nn")
    y = _dot(rt, s0, "nt") + _dot(a_rk, v, "nn") + _dot(a_rb, sa, "nn")
    p_last = p_incl[:, l - 1:l, :]
    s_end = s0 * p_last + _dot(v, kh * p_last, "tn") + _dot(sa, bh * p_last, "tn")
    mu = jnp.mean(y, axis=-1, keepdims=True)
    var = jnp.mean(jnp.square(y - mu), axis=-1, keepdims=True)
    out = (y - mu) * lax.rsqrt(var + GN_EPS) * ln_w + ln_b
    out = out + jnp.sum(r * km * r_k, axis=-1, keepdims=True) * v
    return out, s_end


def _rwkv_fwd(r, k, v, a, lw, heads, comm=None):
    h, t, n = r.shape
    l = min(RWKV_CHUNK, t)
    nc = t // l
    seq = pl.BlockSpec((h, l, n), lambda c: (0, c, 0))
    par = pl.BlockSpec((h, 1, n), lambda c: (0, 0, 0))

    def body(r_r, k_r, v_r, a_r, lw_r, p0, p1, p2, p3, p4, y_r, ck_r, s_scr):
        @pl.when(pl.program_id(0) == 0)
        def _():
            s_scr[...] = jnp.zeros_like(s_scr)

        s0 = s_scr[...]
        ck_r[0] = s0
        lw_v = lw_r[...]
        out, s_end = _rwkv_chunk(s0, r_r[...], k_r[...], v_r[...], a_r[...], lw_v, _chunk_cumsum(lw_v),
                                 p0[...], p1[...], p2[...], p3[...], p4[...])
        y_r[...] = out
        s_scr[...] = s_end

    return _pcall(body, comm=comm, name="rwkv_fwd", grid=(nc,), in_specs=[seq] * 5 + [par] * 5,
                  out_specs=[seq, pl.BlockSpec((1, h, n, n), lambda c: (c, 0, 0, 0))],
                  out_shape=[jax.ShapeDtypeStruct((h, t, n), f32), jax.ShapeDtypeStruct((nc, h, n, n), f32)],
                  scratch_shapes=[pltpu.VMEM((h, n, n), f32)],
                  compiler_params=_params(("arbitrary",)))(r, k, v, a, lw, *heads)


def _rwkv_bwd(r, k, v, a, lw, heads, ck, dy, comm=None):
    h, t, n = r.shape
    l = min(RWKV_CHUNK, t)
    nc = t // l
    seq = pl.BlockSpec((h, l, n), lambda c: (0, nc - 1 - c, 0))
    par = pl.BlockSpec((h, 1, n), lambda c: (0, 0, 0))

    def body(r_r, k_r, v_r, a_r, lw_r, p0, p1, p2, p3, p4, ck_r, dy_r,
             dr, dk, dv, da, dlw, g0, g1, g2, g3, g4, ds_scr):
        first = pl.program_id(0) == 0

        @pl.when(first)
        def _():
            ds_scr[...] = jnp.zeros_like(ds_scr)

        lw_v = lw_r[...]
        _, vjp = jax.vjp(_rwkv_chunk, ck_r[0], r_r[...], k_r[...], v_r[...], a_r[...], lw_v, _chunk_cumsum(lw_v),
                         p0[...], p1[...], p2[...], p3[...], p4[...])
        grads = vjp((dy_r[...], ds_scr[...]))
        ds_scr[...] = grads[0]
        dr[...] = grads[1]
        dk[...] = grads[2]
        dv[...] = grads[3]
        da[...] = grads[4]
        dlw[...] = grads[5] + _chunk_cumsum(grads[6], reverse=True)
        acc = (g0, g1, g2, g3, g4)

        @pl.when(first)
        def _():
            for ref, val in zip(acc, grads[7:]):
                ref[...] = val

        @pl.when(jnp.logical_not(first))
        def _():
            for ref, val in zip(acc, grads[7:]):
                ref[...] += val

    seq_sh = jax.ShapeDtypeStruct((h, t, n), f32)
    par_sh = jax.ShapeDtypeStruct((h, 1, n), f32)
    return _pcall(body, comm=comm, name="rwkv_bwd", grid=(nc,),
                  in_specs=[seq] * 5 + [par] * 5 + [pl.BlockSpec((1, h, n, n), lambda c: (nc - 1 - c, 0, 0, 0)), seq],
                  out_specs=[seq] * 5 + [par] * 5, out_shape=[seq_sh] * 5 + [par_sh] * 5,
                  scratch_shapes=[pltpu.VMEM((h, n, n), f32)],
                  compiler_params=_params(("arbitrary",)))(r, k, v, a, lw, *heads, ck, dy)


def _rwkv_pre_math(c, lp, p_rkv, p_rkv_prev, p_l, p_l_prev, mu_rkv, mu_l, w0, a0, decay_up, aaa_up, gate_up):
    dlp, alp, _ = lp
    z = p_rkv + (p_rkv_prev - p_rkv) * mu_rkv
    zl = p_l + (p_l_prev - p_l) * mu_l
    r, k, v = z[:, :c], z[:, c:2 * c], z[:, 2 * c:]
    wd, ad, gd = zl[:, :dlp], zl[:, dlp:dlp + alp], zl[:, dlp + alp:]
    w = -_softplus(-(w0 + _dot(jnp.tanh(wd), decay_up, "nn"))) - 0.5
    a = _sigmoid(a0 + _dot(ad, aaa_up, "nn"))
    g = _dot(_sigmoid(gd), gate_up, "nn")
    return r, k, v, -jnp.exp(w), a, g


def _pad_to(a, n, axis):
    if a.shape[axis] == n:
        return a
    pad = [(0, 0)] * a.ndim
    pad[axis] = (0, n - a.shape[axis])
    return jnp.pad(a, pad)


def _up128(n):
    return -(-n // LANE) * LANE


def _shift_down(p):
    return jnp.concatenate([jnp.zeros((1, p.shape[1]), p.dtype), p[:-1]], axis=0)


def _shift_up(p):
    return jnp.concatenate([p[1:], jnp.zeros((1, p.shape[1]), p.dtype)], axis=0)


def _swiglu(g, u):
    return jax.nn.silu(g) * u


def _ffn_hidden(name, h, w_gate, w_up, comm=None):
    t, d = h.shape
    nb, _, n = w_gate.shape
    tm = _pick(t, (1024, 512, 256, 128))

    def body(h_r, wg_r, wu_r, g_o, u_o, a_o):
        hv = h_r[...]
        g = lax.dot_general(hv, wg_r[0], _dims("nn", 2), preferred_element_type=f32)
        u = lax.dot_general(hv, wu_r[0], _dims("nn", 2), preferred_element_type=f32)
        g_o[0] = g.astype(bf16)
        u_o[0] = u.astype(bf16)
        a_o[0] = _swiglu(g, u).astype(bf16)

    w_spec = pl.BlockSpec((1, d, n), lambda i, j: (j, 0, 0))
    o_spec = pl.BlockSpec((1, tm, n), lambda i, j: (j, i, 0))
    sh = jax.ShapeDtypeStruct((nb, t, n), bf16)
    return _pcall(body, comm=comm, name=name, grid=(t // tm, nb),
                  in_specs=[pl.BlockSpec((tm, d), lambda i, j: (i, 0)), w_spec, w_spec],
                  out_specs=[o_spec, o_spec, o_spec], out_shape=[sh, sh, sh],
                  compiler_params=_params(("parallel", "arbitrary")))(h, w_gate, w_up)


def _ffn_out(name, act, w_down, x, comm=None):
    nb, t, n = act.shape
    d = w_down.shape[2]
    tm, tn = _pick(t, (512, 256, 128)), _pick(d, (512, 256, 128))

    def body(a_r, w_r, x_r, o_r):
        acc = x_r[...]
        for j in range(nb):
            acc = acc + 0.5 * lax.dot_general(a_r[j], w_r[j], _dims("nn", 2), preferred_element_type=f32)
        o_r[...] = acc

    return _pcall(body, comm=comm, name=name, grid=(t // tm, d // tn),
                  in_specs=[pl.BlockSpec((nb, tm, n), lambda i, j: (0, i, 0)), pl.BlockSpec((nb, n, tn), lambda i, j: (0, 0, j)),
                            pl.BlockSpec((tm, tn), lambda i, j: (i, j))],
                  out_specs=pl.BlockSpec((tm, tn), lambda i, j: (i, j)), out_shape=jax.ShapeDtypeStruct((t, d), f32),
                  compiler_params=_params(("parallel", "parallel")))(act, w_down, x)


def _ffn_dhidden(name, dout, w_down, gate, up, comm=None):
    t, d = dout.shape
    nb, n, _ = w_down.shape
    tm = _pick(t, (512, 256, 128))

    def body(d_r, w_r, g_r, u_r, dg_o, du_o):
        dact = 0.5 * lax.dot_general(d_r[...].astype(MXU_DTYPE), w_r[0], _dims("nt", 2), preferred_element_type=f32)
        _, vjp = jax.vjp(_swiglu, g_r[0].astype(f32), u_r[0].astype(f32))
        dg, du = vjp(dact)
        dg_o[0] = dg.astype(bf16)
        du_o[0] = du.astype(bf16)

    o_spec = pl.BlockSpec((1, tm, n), lambda i, j: (j, i, 0))
    sh = jax.ShapeDtypeStruct((nb, t, n), bf16)
    return _pcall(body, comm=comm, name=name, grid=(t // tm, nb),
                  in_specs=[pl.BlockSpec((tm, d), lambda i, j: (i, 0)), pl.BlockSpec((1, n, d), lambda i, j: (j, 0, 0)), o_spec, o_spec],
                  out_specs=[o_spec, o_spec], out_shape=[sh, sh],
                  compiler_params=_params(("parallel", "arbitrary")))(dout, w_down, gate, up)


def _ffn_dw_down(name, act, dout, comm=None):
    nb, t, n = act.shape
    d = dout.shape[1]
    tn = _pick(d, (1024, 512, 256, 128))

    def body(a_r, d_r, o_r):
        acc = lax.dot_general(a_r[0], d_r[...].astype(MXU_DTYPE), _dims("tn", 2), preferred_element_type=f32)
        o_r[0] = (0.5 * acc).astype(bf16)

    return _pcall(body, comm=comm, name=name, grid=(nb, d // tn),
                  in_specs=[pl.BlockSpec((1, t, n), lambda j, i: (j, 0, 0)), pl.BlockSpec((t, tn), lambda j, i: (0, i))],
                  out_specs=pl.BlockSpec((1, n, tn), lambda j, i: (j, 0, i)), out_shape=jax.ShapeDtypeStruct((nb, n, d), bf16),
                  compiler_params=_params(("parallel", "parallel")))(act, dout)


def _ffn_dw_hidden(name, h, dgate, dup, comm=None):
    t, d = h.shape
    nb, _, n = dgate.shape
    tm = _pick(d, (1024, 512, 256, 128))

    def body(h_r, g_r, u_r, dg_o, du_o):
        hv = h_r[...]
        dg_o[0] = lax.dot_general(hv, g_r[0], _dims("tn", 2), preferred_element_type=f32).astype(bf16)
        du_o[0] = lax.dot_general(hv, u_r[0], _dims("tn", 2), preferred_element_type=f32).astype(bf16)

    g_spec = pl.BlockSpec((1, t, n), lambda j, i: (j, 0, 0))
    o_spec = pl.BlockSpec((1, tm, n), lambda j, i: (j, i, 0))
    sh = jax.ShapeDtypeStruct((nb, d, n), bf16)
    return _pcall(body, comm=comm, name=name, grid=(nb, d // tm),
                  in_specs=[pl.BlockSpec((t, tm), lambda j, i: (0, i)), g_spec, g_spec],
                  out_specs=[o_spec, o_spec], out_shape=[sh, sh],
                  compiler_params=_params(("parallel", "parallel")))(h, dgate, dup)


def _ffn_dh(name, dhid, w, res=None, comm=None):
    nb, t, n = dhid.shape
    d = w.shape[1]
    tm, tn = _pick(t, (512, 256, 128)), _pick(d, (512, 256, 128))

    def body(*refs):
        acc = refs[2][...] if res is not None else jnp.zeros((tm, tn), f32)
        for j in range(nb):
            acc = acc + lax.dot_general(refs[0][j], refs[1][j], _dims("nt", 2), preferred_element_type=f32)
        refs[-1][...] = acc

    in_specs = [pl.BlockSpec((nb, tm, n), lambda i, j: (0, i, 0)), pl.BlockSpec((nb, tn, n), lambda i, j: (0, j, 0))]
    args = [dhid, w]
    if res is not None:
        in_specs.append(pl.BlockSpec((tm, tn), lambda i, j: (i, j)))
        args.append(res)
    return _pcall(body, comm=comm, name=name, grid=(t // tm, d // tn), in_specs=in_specs,
                  out_specs=pl.BlockSpec((tm, tn), lambda i, j: (i, j)), out_shape=jax.ShapeDtypeStruct((t, d), f32),
                  compiler_params=_params(("parallel", "parallel")))(*args)


def _lora_bounds(c, lora):
    dl, al, gl = lora
    o1 = 3 * c
    o2, o3 = o1 + dl, o1 + dl + al
    return o1, o2, o3, o3 + gl, (_up128(dl), _up128(al), _up128(gl))


def _win_split(g8, c, lora):
    nb, d, n = g8.shape
    o1, o2, o3, o4, (dlp, alp, glp) = _lora_bounds(c, lora)
    tm = _row_tile(d, nb * n, g8.dtype.itemsize)

    def body(x, rkv_o, lora_o, swa_o):
        w = jnp.concatenate([x[j] for j in range(nb)], axis=-1)
        pad = lambda p, m: p if p.shape[1] == m else jnp.concatenate([p, jnp.zeros((p.shape[0], m - p.shape[1]), p.dtype)], axis=-1)
        rkv_o[...] = w[:, :o1]
        lora_o[...] = jnp.concatenate([pad(w[:, o1:o2], dlp), pad(w[:, o2:o3], alp), pad(w[:, o3:o4], glp)], axis=-1)
        swa_o[...] = w[:, o4:]

    widths = (o1, dlp + alp + glp, nb * n - o4)
    return _pcall(body, name="w_in_split", grid=(d // tm,), in_specs=[pl.BlockSpec((nb, tm, n), lambda i: (0, i, 0))],
                  out_specs=[pl.BlockSpec((tm, wd), lambda i: (i, 0)) for wd in widths],
                  out_shape=[jax.ShapeDtypeStruct((d, wd), g8.dtype) for wd in widths],
                  compiler_params=_params(("parallel",)))(g8)


def _win_merge(dw_rkv, dw_lora, dw_swa, c, lora, nb):
    d = dw_rkv.shape[0]
    o1, o2, o3, o4, (dlp, alp, glp) = _lora_bounds(c, lora)
    dl, al, gl = lora
    total = o4 + dw_swa.shape[1]
    n = total // nb
    tm = _row_tile(d, total, dw_rkv.dtype.itemsize)

    def body(a, b, s, o):
        bv = b[...]
        w = jnp.concatenate([a[...], bv[:, :dl], bv[:, dlp:dlp + al], bv[:, dlp + alp:dlp + alp + gl], s[...]], axis=-1)
        for j in range(nb):
            o[j] = w[:, n * j:n * (j + 1)]

    ins = [dw_rkv, dw_lora, dw_swa]
    return _pcall(body, name="w_in_merge", grid=(d // tm,), in_specs=[pl.BlockSpec((tm, a.shape[1]), lambda i: (i, 0)) for a in ins],
                  out_specs=pl.BlockSpec((nb, tm, n), lambda i: (0, i, 0)), out_shape=jax.ShapeDtypeStruct((nb, d, n), dw_rkv.dtype),
                  compiler_params=_params(("parallel",)))(*ins)


def _norm_bwd(name, x, g_norm, dh, dres, comm=None):
    d = x.shape[1]

    def fn(xb, dhb, drb, g):
        _, vjp = jax.vjp(_rms, xb, g)
        dx, dg = vjp(dhb)
        return drb + dx, dg

    return _rows(name, fn, [x, dh, dres], [g_norm], [(d, f32)], [((1, d), f32)], comm=comm)


def _colsum(name, a):
    return _rows(name, lambda ab: (jnp.sum(ab.astype(f32), axis=0, keepdims=True),), [a], [], [], [((1, a.shape[1]), f32)])[0]


def kernel(x, mem, f1_norm, f1_gate, f1_up, f1_down, mix_norm, w_in, b_in_attn, rw_mu, rw_w0, rw_decay_up, rw_a0, rw_aaa_up, rw_gate_up, rw_k_k, rw_k_a, rw_r_k, rw_lnx_w, rw_lnx_b, attn_sinks, w_out, b_out, xa_norm, mem_norm, w_xq, w_xkv, w_xo, f2_norm, f2_gate, f2_up, f2_down, final_norm, loss_target, m_f1_norm, m_f1_gate, m_f1_up, m_f1_down, m_mix_norm, m_w_in, m_b_in_attn, m_rw_mu, m_rw_w0, m_rw_decay_up, m_rw_a0, m_rw_aaa_up, m_rw_gate_up, m_rw_k_k, m_rw_k_a, m_rw_r_k, m_rw_lnx_w, m_rw_lnx_b, m_attn_sinks, m_w_out, m_b_out, m_xa_norm, m_mem_norm, m_w_xq, m_w_xkv, m_w_xo, m_f2_norm, m_f2_gate, m_f2_up, m_f2_down, m_final_norm, v_f1_norm, v_f1_gate, v_f1_up, v_f1_down, v_mix_norm, v_w_in, v_b_in_attn, v_rw_mu, v_rw_w0, v_rw_decay_up, v_rw_a0, v_rw_aaa_up, v_rw_gate_up, v_rw_k_k, v_rw_k_a, v_rw_r_k, v_rw_lnx_w, v_rw_lnx_b, v_attn_sinks, v_w_out, v_b_out, v_xa_norm, v_mem_norm, v_w_xq, v_w_xkv, v_w_xo, v_f2_norm, v_f2_gate, v_f2_up, v_f2_down, v_final_norm):
    names = ["f1_norm", "f1_gate", "f1_up", "f1_down", "mix_norm", "w_in", "b_in_attn", "rw_mu", "rw_w0", "rw_decay_up",
             "rw_a0", "rw_aaa_up", "rw_gate_up", "rw_k_k", "rw_k_a", "rw_r_k", "rw_lnx_w", "rw_lnx_b", "attn_sinks", "w_out",
             "b_out", "xa_norm", "mem_norm", "w_xq", "w_xkv", "w_xo", "f2_norm", "f2_gate", "f2_up", "f2_down", "final_norm"]
    env = dict(locals())
    w_of = {k: env[k] for k in names}
    m_of = {k: env["m_" + k] for k in names}
    v_of = {k: env["v_" + k] for k in names}
    col_sharded = ["f1_gate", "f1_up", "w_in", "rw_decay_up", "rw_aaa_up", "rw_gate_up", "w_xkv", "f2_gate", "f2_up"]
    row_sharded = ["f1_down", "w_out", "w_xq", "w_xo", "f2_down"]
    sharded = col_sharded + row_sharded
    small = [k for k in names if k not in sharded]

    x0, mem0, tgt = x[0], mem[0], loss_target[0]
    t, d = x0.shape
    c = rw_w0.shape[-1]
    heads = c // HEAD_DIM
    dl, al, gl = rw_decay_up.shape[1], rw_aaa_up.shape[1], rw_gate_up.shape[1]
    dlp, alp, glp = _up128(dl), _up128(al), _up128(gl)
    swa_w = d - c
    hq, kvh = swa_w // HEAD_DIM, (b_in_attn.shape[-1] - swa_w) // (2 * HEAD_DIM)
    my_x, my_y, my_c = _position()
    c_idx = jnp.reshape(my_c, (1,)).astype(jnp.int32)
    chip_idx = jnp.reshape(2 * my_x + my_y, (1,)).astype(jnp.int32)

    shard2d = {k: w_of[k][0] for k in sharded}
    cast = {k: _rows("cast_" + k, lambda a: (a,), [shard2d[k]], [], [(shard2d[k].shape[1], bf16)], tm=_row_tile(*shard2d[k].shape))[0]
            for k in sharded}
    ffn1_keys, ffn2_keys = ["f1_gate", "f1_up", "f1_down"], ["f2_gate", "f2_up", "f2_down"]
    in_keys = ["w_in", "rw_decay_up", "rw_aaa_up", "rw_gate_up"]
    kept_in_blocks = ffn1_keys + ffn2_keys + ["w_in", "w_xkv"]

    def whole(k, g8):
        if k in kept_in_blocks:
            return g8
        if k in col_sharded:
            return g8.transpose(1, 0, 2).reshape(g8.shape[1], N_DEV * g8.shape[2])
        return g8.reshape(N_DEV * g8.shape[1], g8.shape[2])

    order = [ffn1_keys[:2], ["f1_down"], in_keys, ["w_out"], ["w_xq"], ["f2_gate"], ["f2_up"], ["w_xkv"], ["w_xo"], ["f2_down"]]
    started, tokens = {}, {}

    def start_gather(gi, after=None):
        if gi < len(order):
            (started[gi],), tokens[gi] = _chip_sends_start("gather_start_%d" % gi, [[cast[k] for k in order[gi]]], after)

    def gather_of(keys, after):
        gi = order.index(keys)
        shards, lands = _chip_sends_wait("gather_wait_%d" % gi, *started[gi], after)
        return _finish_gather_comm(shards, lands, extra=[tokens[gi + 1]] if gi + 1 in tokens else [])

    def wholes(keys, gathered):
        return {k: whole(k, g8) for k, g8 in zip(keys, gathered)}

    start_gather(0)
    start_gather(1)
    (h1,) = _rows("f1_norm", lambda xb, g, _: (_rms(xb, g),), [x0], [f1_norm, tokens[0]], [(d, bf16)])
    full = wholes(ffn1_keys[:2], _comm_only("gather_f1", gather_of(ffn1_keys[:2], h1)))
    start_gather(2, full["f1_gate"])
    (gate1, up1, act1), gathered = _ffn_hidden("f1_hidden", h1, full["f1_gate"], full["f1_up"], comm=gather_of(["f1_down"], h1))
    full.update(wholes(["f1_down"], gathered))
    start_gather(3, act1)
    x1, gathered = _ffn_out("f1_out", act1, full["f1_down"], x0, comm=gather_of(in_keys, act1))
    full.update(wholes(in_keys, gathered))
    start_gather(4, x1)
    ffn1_saved = (h1, gate1, up1, act1)
    w_rkv, w_lora, w_swa = _win_split(full["w_in"], c, (dl, al, gl))
    o1, o2, o3, shift_cols, _ = _lora_bounds(c, (dl, al, gl))
    mu_rkv = rw_mu[:, :3 * c]
    mu_l = jnp.concatenate([_pad_to(rw_mu[:, o1:o2], dlp, 1), _pad_to(rw_mu[:, o2:o3], alp, 1),
                            _pad_to(rw_mu[:, o3:shift_cols], glp, 1)], axis=1)
    decay_up = _pad_to(full["rw_decay_up"], dlp, 0).astype(f32)
    aaa_up = _pad_to(full["rw_aaa_up"], alp, 0).astype(f32)
    gate_up = _pad_to(full["rw_gate_up"], glp, 0).astype(f32)
    head_pars = [p.reshape(heads, 1, HEAD_DIM) for p in (rw_k_k, rw_k_a, rw_r_k, rw_lnx_w, rw_lnx_b)]
    final_g = final_norm.reshape(1, d)

    (h2,) = _rows("mix_norm", lambda xb, g: (_rms(xb, g),), [x1], [mix_norm], [(d, bf16)])
    p_rkv, gathered = _mm("in_rkv", h2, w_rkv, "nn", f32, comm=gather_of(["w_out"], h2))
    full.update(wholes(["w_out"], gathered))
    start_gather(5, p_rkv)
    p_l =_mm("in_lora", h2, w_lora, "nn", f32)
    za = _mm("in_swa", h2, w_swa, "nn", f32, bias=b_in_attn)
    pre_fn = functools.partial(_rwkv_pre_math, c, (dlp, alp, glp))
    pre_rows = [p_rkv, _shift_down(p_rkv), p_l, _shift_down(p_l)]
    pre_full = [mu_rkv, mu_l, rw_w0, rw_a0, decay_up, aaa_up, gate_up]
    (r_h, k_h, v_h, lw_h, a_h, g_t), gathered = _rows("rwkv_pre", pre_fn, pre_rows, pre_full,
                                                       [(c, f32, HEAD_DIM)] * 5 + [(c, f32)], tm=128, comm=gather_of(["w_xq"], p_l))
    full.update(wholes(["w_xq"], gathered))
    start_gather(6, r_h)
    seqs = [r_h, k_h, v_h, a_h, lw_h]
    (y_heads, checkpoints), gathered = _rwkv_fwd(*seqs, head_pars, comm=gather_of(["f2_gate"], r_h))
    full.update(wholes(["f2_gate"], gathered))
    start_gather(7, y_heads)

    pos = jnp.arange(t, dtype=f32)
    inv_freq = ROPE_THETA ** (-jnp.arange(0, HEAD_DIM, 2, dtype=f32) / HEAD_DIM)
    ang = pos[:, None] * inv_freq[None, :]
    cos, sin = jnp.cos(ang), jnp.sin(ang)
    sinks3 = attn_sinks.reshape(kvh, hq // kvh, 1)
    swa_in = (*_swa_split(za, hq, kvh), cos, sin, sinks3)
    y_swa_heads, gathered = _swa_fwd(*swa_in, comm=gather_of(["f2_up"], swa_in[0]))
    full.update(wholes(["f2_up"], gathered))
    start_gather(8, y_swa_heads)
    (ycat,) = _rows("mix_cat", lambda yb, gb, sb: (jnp.concatenate([yb * gb, sb], axis=1),), [y_heads, g_t, y_swa_heads], [],
                    [(d, bf16)])
    x2, gathered = _mm("mix_out", ycat, full["w_out"], "nn", f32, res=x1, bias=b_out, comm=gather_of(["w_xkv"], ycat))
    full.update(wholes(["w_xkv"], gathered))
    start_gather(9, x2)

    (h3,) = _rows("xa_norm", lambda xb, g: (_rms(xb, g),), [x2], [xa_norm], [(d, bf16)])
    (mem_n,) = _rows("mem_norm", lambda xb, g: (_rms(xb, g),), [mem0], [mem_norm], [(d, bf16)])
    q_x, gathered = _mm("xa_q", h3, full["w_xq"], "nn", bf16, comm=gather_of(["w_xo"], h3))
    full.update(wholes(["w_xo"], gathered))
    kv_x = _mm("xa_kv", mem_n, full["w_xkv"], "nn", bf16)
    o_x = _xattn_fwd(q_x, kv_x)
    x3 = _mm("xa_out", o_x, full["w_xo"], "nn", f32, res=x2)
    (h4,) = _rows("f2_norm", lambda xb, g: (_rms(xb, g),), [x3], [f2_norm], [(d, bf16)])
    (gate2, up2, act2), gathered = _ffn_hidden("f2_hidden", h4, full["f2_gate"], full["f2_up"], comm=gather_of(["f2_down"], h4))
    full.update(wholes(["f2_down"], gathered))
    x4 = _ffn_out("f2_out", act2, full["f2_down"], x3)
    ffn2_saved = (h4, gate2, up2, act2)

    def loss_fn(xb, tb, g):
        def per_row(xv, gv):
            return 0.5 * jnp.mean(jnp.square(_rms(xv, gv) - tb), axis=-1, keepdims=True)

        lrow, vjp = jax.vjp(per_row, xb, g)
        dxb, dgb = vjp(jnp.ones_like(lrow))
        return dxb, dgb, jnp.sum(lrow, axis=0, keepdims=True)

    dx4, d_final, loss_part = _rows("loss", loss_fn, [x4, tgt], [final_g], [(d, f32)], [((1, d), f32), ((1, 1), f32)])
    loss = lax.psum(loss_part[0, 0], ("x", "y", "c"))

    grads, small_g, out = {}, {"final_norm": d_final}, {}

    def pair_sums_of(tag, keys, carrier=None):
        blocks = []
        for k in keys:
            g2 = grads[k]
            rr, cc = shard2d[k].shape
            if k in kept_in_blocks:
                blocks.append(g2)
            else:
                blocks.append(g2.reshape(g2.shape[0], N_DEV, cc).transpose(1, 0, 2) if k in col_sharded else g2.reshape(N_DEV, rr, cc))
        if carrier is None:
            from_sibling = _comm_only("grads_to_sibling_" + tag, _sibling_comm(blocks))
        else:
            carried, from_sibling = carrier(_sibling_comm(blocks))
        pairs = [_pair_add("pair_add_" + k, b, o, c_idx) for k, b, o in zip(keys, blocks, from_sibling)]
        return pairs if carrier is None else (pairs, carried)

    def update(keys, pair_sums, from_chips):
        for k, part, others in zip(keys, pair_sums, from_chips):
            res = _adam_sharded("adam_" + k, shard2d[k], m_of[k][0], v_of[k][0], part, others, chip_idx)
            out[k] = [a.reshape(w_of[k].shape) for a in res]

    def ffn_backward(tag, keys, xin, g_norm, saved, dout, first_comm, ride_on_dh):
        h, gate, up, act = saved
        k_gate, k_up, k_down = keys
        if first_comm is None:
            grads[k_down], carried = _ffn_dw_down(tag + "_dw_down", act, dout), None
        else:
            grads[k_down], carried = _ffn_dw_down(tag + "_dw_down", act, dout, comm=first_comm)
        down_pairs, (dgate, dup) = pair_sums_of(
            k_down, [k_down], lambda cm: _ffn_dhidden(tag + "_dhidden", dout, full[k_down], gate, up, comm=cm))
        (grads[k_gate], grads[k_up]), from_chips = _ffn_dw_hidden(tag + "_dw_hidden", h, dgate, dup, comm=_chips_comm(down_pairs))
        update([k_down], down_pairs, from_chips)
        if ride_on_dh:
            hidden_pairs = pair_sums_of(tag + "_hidden", [k_gate, k_up])
            dh, from_chips = _ffn_dh(tag + "_dh1", dgate, full[k_gate], comm=_chips_comm(hidden_pairs[:1]))
            update([k_gate], hidden_pairs[:1], from_chips)
            dh, from_chips = _ffn_dh(tag + "_dh2", dup, full[k_up], res=dh, comm=_chips_comm(hidden_pairs[1:]))
            update([k_up], hidden_pairs[1:], from_chips)
            hidden_pairs = None
        else:
            hidden_pairs, dh = pair_sums_of(tag + "_hidden", [k_gate, k_up],
                                            lambda cm: _ffn_dh(tag + "_dh1", dgate, full[k_gate], comm=cm))
            dh = _ffn_dh(tag + "_dh2", dup, full[k_up], res=dh)
        dx, dg_norm = _norm_bwd(tag + "_dnorm", xin, g_norm, dh, dout)
        return dx, dg_norm, hidden_pairs, carried

    xa_keys = ["w_xq", "w_xkv", "w_xo"]
    dx3, small_g["f2_norm"], ffn2_pairs, _ = ffn_backward("f2b", ffn2_keys, x3, f2_norm, ffn2_saved, dx4, None, False)

    do_x = _mm("xa_do", dx3, full["w_xo"], "nt", bf16)
    grads["w_xo"] = _mm("xa_dwo", o_x, dx3, "tn", bf16)
    dq_x, dk_x, dv_x = _xattn_bwd(q_x, kv_x, do_x)
    grads["w_xq"] = _mm("xa_dwq", h3, dq_x, "tn", bf16)
    dh3 = _mm("xa_dh", dq_x, full["w_xq"], "nt", f32)
    dkv_x = jnp.concatenate([dk_x, dv_x], axis=1)
    grads["w_xkv"] = _mm("xa_dwkv", mem_n, dkv_x, "tn", bf16, out_blocks=N_DEV)
    dkv_blocks = dkv_x.astype(bf16).reshape(dkv_x.shape[0], N_DEV, -1).transpose(1, 0, 2)
    dmem_n = _ffn_dh("xa_dmem", dkv_blocks, full["w_xkv"])
    _, small_g["mem_norm"] = _norm_bwd("mem_dnorm", mem0, mem_norm, dmem_n, jnp.zeros_like(mem0))
    xa_pairs, (dx2, small_g["xa_norm"]) = pair_sums_of(
        "xa", xa_keys, lambda cm: _norm_bwd("xa_dnorm", x2, xa_norm, dh3, dx3, comm=cm))

    dycat = _mm("mix_dy", dx2, full["w_out"], "nt", f32)
    grads["w_out"] = _mm("mix_dwout", ycat, dx2, "tn", bf16)
    small_g["b_out"] = _colsum("mix_dbout", dx2)
    out_pairs, (dy_heads, dg_t, do_sw) = pair_sums_of("out", ["w_out"], lambda cm: _rows(
        "mix_dgate", lambda db, yb, gb: (db[:, :c] * gb, db[:, :c] * yb, db[:, c:]), [dycat, y_heads, g_t], [],
        [(c, f32, HEAD_DIM), (c, f32), (swa_w, f32, HEAD_DIM)], comm=cm))
    rw_grads, from_chips = _rwkv_bwd(*seqs, head_pars, checkpoints, dy_heads, comm=_chips_comm(ffn2_pairs))
    update(ffn2_keys[:2], ffn2_pairs, from_chips)
    dr_h, dk_h, dv_h, da_h, dlw_h = rw_grads[:5]
    for nm, gh in zip(("rw_k_k", "rw_k_a", "rw_r_k", "rw_lnx_w", "rw_lnx_b"), rw_grads[5:]):
        small_g[nm] = gh.reshape(w_of[nm].shape)

    def pre_bwd(*args):
        _, vjp = jax.vjp(pre_fn, *args[:4], *args[10:])
        return vjp(tuple(args[4:10]))

    pre_cts = [dr_h, dk_h, dv_h, dlw_h, da_h, dg_t]
    pre_out = _rows("rwkv_pre_bwd", pre_bwd, pre_rows + pre_cts, pre_full,
                    [(3 * c, f32), (3 * c, f32), (dlp + alp + glp, f32), (dlp + alp + glp, f32)],
                    [(p.shape, f32) for p in pre_full], tm=128)
    dp_rkv = pre_out[0] + _shift_up(pre_out[1])
    dp_l = pre_out[2] + _shift_up(pre_out[3])
    dmu_rkv, dmu_l, small_g["rw_w0"], small_g["rw_a0"], d_decay_up, d_aaa_up, d_gate_up = pre_out[4:]
    small_g["rw_mu"] = jnp.concatenate([dmu_rkv, dmu_l[:, :dl], dmu_l[:, dlp:dlp + al], dmu_l[:, dlp + alp:dlp + alp + gl]], axis=1)
    grads["rw_decay_up"] = d_decay_up[:dl].astype(bf16)
    grads["rw_aaa_up"] = d_aaa_up[:al].astype(bf16)
    grads["rw_gate_up"] = d_gate_up[:gl].astype(bf16)

    sw, from_chips = _swa_bwd(*swa_in, do_sw, comm=_chips_comm(xa_pairs))
    update(xa_keys, xa_pairs, from_chips)
    small_g["attn_sinks"] = sw[8].reshape(attn_sinks.shape)
    dza, small_g["b_in_attn"] = _swa_merge(sw[0], sw[1], sw[3], sw[2], sw[5], sw[4], sw[7], sw[6])

    dw_rkv = _mm("in_dwrkv", h2, dp_rkv, "tn", bf16)
    dw_l = _mm("in_dwlora", h2, dp_l, "tn", bf16)
    dw_swa = _mm("in_dwswa", h2, dza, "tn", bf16)
    grads["w_in"] = _win_merge(dw_rkv, dw_l, dw_swa, c, (dl, al, gl), N_DEV)
    dh2, from_chips = _mm("in_dh1", dp_rkv, w_rkv, "nt", f32, comm=_chips_comm(out_pairs))
    update(["w_out"], out_pairs, from_chips)
    dh2 = _mm("in_dh2", dp_l, w_lora, "nt", f32, res=dh2)
    in_pairs, dh2 = pair_sums_of("in", in_keys, lambda cm: _mm("in_dh3", dza, w_swa, "nt", f32, res=dh2, comm=cm))
    dx1, small_g["mix_norm"] = _norm_bwd("mix_dnorm", x1, mix_norm, dh2, dx2)

    dx0, small_g["f1_norm"], _, from_chips = ffn_backward("f1b", ffn1_keys, x0, f1_norm, ffn1_saved, dx1, _chips_comm(in_pairs), True)
    update(in_keys, in_pairs, from_chips)

    sizes = [int(w_of[k].size) for k in small]
    total = sum(sizes)
    cols = -(-total // (8 * LANE)) * LANE

    def pack(parts_of):
        flat = jnp.concatenate([parts_of[k].reshape(-1).astype(f32) for k in small])
        return _pad_to(flat, 8 * cols, 0).reshape(8, cols)

    (all_parts,) = _comm_only("gather_small_grads", _gather_comm([pack(small_g)]))
    res = _adam_small("adam_small", pack(w_of), pack(m_of), pack(v_of), all_parts)
    offs = 0
    flat_res = [a.reshape(-1) for a in res]
    for k, sz in zip(small, sizes):
        out[k] = [a[offs:offs + sz].reshape(w_of[k].shape) for a in flat_res]
        offs += sz

    outs = [loss, dx0.reshape(x.shape)]
    for j in range(4):
        outs += [out[k][j] for k in names]
    return tuple(outs)
```

```python
import functools
import math

import jax
import jax.numpy as jnp
from jax import lax
from jax.experimental import pallas as pl
from jax.experimental.pallas import tpu as pltpu

f32 = jnp.float32
bf16 = jnp.bfloat16
MXU_DTYPE = jnp.bfloat16

HEAD_DIM = 64
SWA_BLOCK = 128
ROPE_THETA = 10000.0
XATTN_HEADS = 4
RMS_EPS = 1e-6
GN_EPS = 64e-5
NEG_INF = -1e30
RWKV_CHUNK = 64

ADAM_LR = 0.001
ADAM_B1 = 0.9
ADAM_B2 = 0.999
ADAM_EPS = 1e-08
ADAM_WD = 0.01
ADAM_STEP = 10

N_DEV = 8
LANE = 128
VMEM_LIMIT_BYTES = 56 * 1024 * 1024
MM_VMEM_BUDGET = 40 * 1024 * 1024
MESH = pl.DeviceIdType.MESH


def _params(sem):
    return pltpu.CompilerParams(dimension_semantics=sem, vmem_limit_bytes=VMEM_LIMIT_BYTES)


class _Comm:
    def __init__(self, ins, outs, n_remote, n_local, start, finish):
        self.ins, self.outs, self.n_remote, self.n_local = list(ins), list(outs), n_remote, max(n_local, 1)
        self.start, self.finish = start, finish


def _pcall(body, comm=None, **kw):
    kw.setdefault("compiler_params", pltpu.CompilerParams(vmem_limit_bytes=VMEM_LIMIT_BYTES))
    if comm is None:
        return pl.pallas_call(body, **kw)
    single = not isinstance(kw["out_shape"], (list, tuple))
    out_shape = [kw["out_shape"]] if single else list(kw["out_shape"])
    out_specs = [kw["out_specs"]] if single else list(kw["out_specs"])
    in_specs, scratch, grid = list(kw["in_specs"]), list(kw.get("scratch_shapes", ())), tuple(kw.get("grid", ()))
    n_in, n_out, n_ci, n_co, n_scr = len(in_specs), len(out_shape), len(comm.ins), len(comm.outs), len(scratch)

    def wrapped(*refs):
        ins, c_ins = refs[:n_in], refs[n_in:n_in + n_ci]
        outs = refs[n_in + n_ci:n_in + n_ci + n_out]
        c_outs = refs[n_in + n_ci + n_out:n_in + n_ci + n_out + n_co]
        rest = refs[n_in + n_ci + n_out + n_co:]
        scr, sems = rest[:n_scr], rest[n_scr:]
        if grid:
            ids = [pl.program_id(k) for k in range(len(grid))]
            first = functools.reduce(jnp.logical_and, [i == 0 for i in ids])
            last = functools.reduce(jnp.logical_and, [i == g - 1 for i, g in zip(ids, grid)])
            pl.when(first)(lambda: comm.start(c_ins, c_outs, *sems))
            body(*ins, *outs, *scr)
            pl.when(last)(lambda: comm.finish(c_ins, c_outs, *sems))
        else:
            comm.start(c_ins, c_outs, *sems)
            body(*ins, *outs, *scr)
            comm.finish(c_ins, c_outs, *sems)

    any_spec = pl.BlockSpec(memory_space=pl.ANY)
    kw.update(in_specs=in_specs + [any_spec] * n_ci, out_specs=out_specs + [any_spec] * n_co,
              out_shape=out_shape + comm.outs,
              scratch_shapes=scratch + [pltpu.SemaphoreType.DMA((comm.n_remote,)), pltpu.SemaphoreType.DMA((comm.n_remote,)),
                                        pltpu.SemaphoreType.DMA((comm.n_local,))])
    if grid:
        kw["compiler_params"] = _params(("arbitrary",) * len(grid))
    call = pl.pallas_call(wrapped, **kw)

    def run(*args):
        res = call(*args, *comm.ins)
        return (res[0] if single else list(res[:n_out])), list(res[n_out:])

    return run


def _comm_only(name, comm):
    return _pcall(lambda: None, comm=comm, name=name, in_specs=[], out_specs=[], out_shape=[])()[1]


def _dims(kind, ndim):
    o = ndim - 2
    batch = ((0,), (0,)) if o else ((), ())
    c = {"nn": ((1 + o,), (o,)), "nt": ((1 + o,), (1 + o,)), "tn": ((o,), (o,))}[kind]
    return (c, batch)


def _dot_raw(x, y, kind):
    return lax.dot_general(x.astype(MXU_DTYPE), y.astype(MXU_DTYPE), _dims(kind, x.ndim), preferred_element_type=f32)


@functools.partial(jax.custom_vjp, nondiff_argnums=(2,))
def _dot(x, y, kind):
    return _dot_raw(x, y, kind)


def _dot_fwd(x, y, kind):
    return _dot_raw(x, y, kind), (x, y)


def _dot_bwd(kind, res, g):
    x, y = res
    if kind == "nn":
        dx, dy = _dot(g, y, "nt"), _dot(x, g, "tn")
    elif kind == "nt":
        dx, dy = _dot(g, y, "nn"), _dot(g, x, "tn")
    else:
        dx, dy = _dot(y, g, "nt"), _dot(x, g, "nn")
    return dx.astype(x.dtype), dy.astype(y.dtype)


_dot.defvjp(_dot_fwd, _dot_bwd)


def _split3(x):
    a = x.astype(bf16)
    r = x - a.astype(f32)
    b = r.astype(bf16)
    c = (r - b.astype(f32)).astype(bf16)
    return a, b, c


def _rms(x, g):
    x = x.astype(f32)
    return x * lax.rsqrt(jnp.mean(x * x, axis=-1, keepdims=True) + RMS_EPS) * g


def _sigmoid(x):
    return 1.0 / (1.0 + jnp.exp(-x))


def _softplus(x):
    return jnp.maximum(x, 0.0) + jnp.log(1.0 + jnp.exp(-jnp.abs(x)))


def _rows(name, fn, row_ins, full_ins, row_outs, acc_outs=(), tm=None, comm=None):
    width = lambda a: a.shape[1] if a.ndim == 2 else a.shape[0] * a.shape[2]
    rows = row_ins[0].shape[0] if row_ins[0].ndim == 2 else row_ins[0].shape[1]
    if tm is None:
        tm = _row_tile(rows, max([width(a) for a in row_ins] + [o[0] for o in row_outs]))
    tm = min(tm, rows)
    assert rows % tm == 0, (name, rows, tm)
    n_in = len(row_ins) + len(full_ins)
    n_o, n_a = len(row_outs), len(acc_outs)

    def load(k, ref):
        if k < len(row_ins) and row_ins[k].ndim == 3:
            return jnp.concatenate([ref[h] for h in range(ref.shape[0])], axis=-1)
        return ref[...]

    def body(*refs):
        vals = [load(k, r) for k, r in enumerate(refs[:n_in])]
        outs = fn(*vals)
        o_refs = refs[n_in:n_in + n_o]
        a_refs = refs[n_in + n_o:]
        for k in range(n_o):
            if len(row_outs[k]) == 3:
                n = row_outs[k][2]
                for h in range(row_outs[k][0] // n):
                    o_refs[k][h] = outs[k][:, h * n:(h + 1) * n].astype(o_refs[k].dtype)
            else:
                o_refs[k][...] = outs[k].astype(o_refs[k].dtype)
        if n_a:
            first = pl.program_id(0) == 0

            @pl.when(first)
            def _():
                for k in range(n_a):
                    a_refs[k][...] = outs[n_o + k].astype(a_refs[k].dtype)

            @pl.when(jnp.logical_not(first))
            def _():
                for k in range(n_a):
                    a_refs[k][...] += outs[n_o + k].astype(a_refs[k].dtype)

    by_rows = lambda cols: pl.BlockSpec((tm, cols), lambda i: (i, 0))
    by_heads = lambda h, n: pl.BlockSpec((h, tm, n), lambda i: (0, i, 0))
    in_specs = [by_rows(a.shape[1]) if a.ndim == 2 else by_heads(a.shape[0], a.shape[2]) for a in row_ins]
    in_specs += [pl.BlockSpec(a.shape, lambda i, nd=a.ndim: (0,) * nd) for a in full_ins]
    out_specs = [by_rows(o[0]) if len(o) == 2 else by_heads(o[0] // o[2], o[2]) for o in row_outs]
    out_specs += [pl.BlockSpec(s, lambda i, nd=len(s): (0,) * nd) for s, _ in acc_outs]
    out_shape = [jax.ShapeDtypeStruct((rows, o[0]) if len(o) == 2 else (o[0] // o[2], rows, o[2]), o[1]) for o in row_outs]
    out_shape += [jax.ShapeDtypeStruct(s, d) for s, d in acc_outs]
    return _pcall(body, comm=comm, name=name, grid=(rows // tm,), in_specs=in_specs, out_specs=out_specs, out_shape=out_shape,
                  compiler_params=_params(("arbitrary",)))(*row_ins, *full_ins)


def _pick(n, cands):
    for c in cands:
        if n % c == 0:
            return c
    return n


def _mm(name, a, b, mode, out_dtype, scale=1.0, res=None, bias=None, comm=None, out_blocks=None):
    b_blocks = b.ndim == 3
    if b_blocks:
        assert mode == "nn"
        (m, k), (nb, k2, tn) = a.shape, b.shape
        n = nb * tn
    elif mode == "nn":
        (m, k), (k2, n) = a.shape, b.shape
    elif mode == "nt":
        (m, k), (n, k2) = a.shape, b.shape
    else:
        (k, m), (k2, n) = a.shape, b.shape
    assert k == k2, (name, a.shape, b.shape, mode)
    if not b_blocks:
        tn = n // out_blocks if out_blocks else _pick(n, (512, 256, 128))
    tm = _pick(m, (1024, 512, 256, 128))

    def need(tm_):
        by = tm_ * k * a.dtype.itemsize + tn * k * b.dtype.itemsize + tm_ * tn * (jnp.dtype(out_dtype).itemsize + 4)
        if res is not None:
            by += tm_ * tn * res.dtype.itemsize
        return 2 * by

    while need(tm) > MM_VMEM_BUDGET and tm % 256 == 0:
        tm //= 2
    dims = _dims(mode, 2)

    def body(*refs):
        bv = refs[1][0] if b_blocks else refs[1][...]
        acc = lax.dot_general(refs[0][...].astype(MXU_DTYPE), bv.astype(MXU_DTYPE), dims, preferred_element_type=f32)
        if scale != 1.0:
            acc = acc * scale
        pos = 2
        if bias is not None:
            acc = acc + refs[pos][...]
            pos += 1
        if res is not None:
            acc = acc + refs[pos][...].astype(f32)
            pos += 1
        if out_blocks:
            refs[pos][0] = acc.astype(out_dtype)
        else:
            refs[pos][...] = acc.astype(out_dtype)

    a_spec = pl.BlockSpec((k, tm), lambda i, j: (0, i)) if mode == "tn" else pl.BlockSpec((tm, k), lambda i, j: (i, 0))
    if b_blocks:
        b_spec = pl.BlockSpec((1, k, tn), lambda i, j: (j, 0, 0))
    else:
        b_spec = pl.BlockSpec((tn, k), lambda i, j: (j, 0)) if mode == "nt" else pl.BlockSpec((k, tn), lambda i, j: (0, j))
    in_specs, args = [a_spec, b_spec], [a, b]
    if bias is not None:
        in_specs.append(pl.BlockSpec((1, tn), lambda i, j: (0, j)))
        args.append(bias)
    if res is not None:
        in_specs.append(pl.BlockSpec((tm, tn), lambda i, j: (i, j)))
        args.append(res)
    if out_blocks:
        out_spec, out_shape = pl.BlockSpec((1, tm, tn), lambda i, j: (j, i, 0)), jax.ShapeDtypeStruct((out_blocks, m, tn), out_dtype)
    else:
        out_spec, out_shape = pl.BlockSpec((tm, tn), lambda i, j: (i, j)), jax.ShapeDtypeStruct((m, n), out_dtype)
    return _pcall(body, comm=comm, name=name, grid=(m // tm, n // tn), in_specs=in_specs, out_specs=out_spec, out_shape=out_shape,
                  compiler_params=_params(("parallel", "parallel")))(*args)


def _position():
    return lax.axis_index("x"), lax.axis_index("y"), lax.axis_index("c")


def _gather_comm(shards):
    n = len(shards)

    def plan(x_refs, o_refs, send_sems, recv_sems, local_sems):
        x, y, c = _position()
        me, sibling = (x, y, c), (x, y, 1 - c)
        chips = [(1 - x, y), (x, 1 - y), (1 - x, 1 - y)]

        def slot(px, py, pc):
            return 4 * px + 2 * py + pc

        def copy(t, k, block, to, src=None):
            dst = o_refs[t].at[slot(*block)]
            return pltpu.make_async_remote_copy(src_ref=dst if src is None else src, dst_ref=dst,
                                                send_sem=send_sems.at[7 * t + k], recv_sem=recv_sems.at[7 * t + k],
                                                device_id=to, device_id_type=MESH)

        mine = [pltpu.make_async_copy(x_refs[t], o_refs[t].at[slot(*me)], local_sems.at[t]) for t in range(n)]
        first = []
        for t in range(n):
            first.append(copy(t, 0, me, sibling, src=x_refs[t]))
            first += [copy(t, 1 + j, me, (*chip, c), src=x_refs[t]) for j, chip in enumerate(chips)]
        return me, sibling, chips, c, copy, mine, first

    def start(*refs):
        _, _, _, _, _, mine, first = plan(*refs)
        for cp in mine + first:
            cp.start()

    def finish(*refs):
        me, sibling, chips, c, copy, mine, first = plan(*refs)
        passed = []
        for t in range(n):
            for j, chip in enumerate(chips):
                copy(t, 1 + j, (*chip, c), me).wait_recv()
                cp = copy(t, 4 + j, (*chip, c), sibling)
                cp.start()
                passed.append(cp)
        for t in range(n):
            copy(t, 0, sibling, me).wait_recv()
            for j, chip in enumerate(chips):
                copy(t, 4 + j, (*chip, 1 - c), me).wait_recv()
        for cp in first + passed:
            cp.wait_send()
        for cp in mine:
            cp.wait()

    outs = [jax.ShapeDtypeStruct((N_DEV,) + s.shape, s.dtype) for s in shards]
    return _Comm(shards, outs, 7 * n, n, start, finish)


def _sibling_comm(blocks):
    n = len(blocks)

    def copies(g_refs, o_refs, send_sems, recv_sems, _):
        x, y, c = _position()
        return [pltpu.make_async_remote_copy(src_ref=g_refs[t].at[2 * q + 1 - c], dst_ref=o_refs[t].at[q],
                                             send_sem=send_sems.at[4 * t + q], recv_sem=recv_sems.at[4 * t + q],
                                             device_id=(x, y, 1 - c), device_id_type=MESH)
                for t in range(n) for q in range(4)]

    def start(*refs):
        for cp in copies(*refs):
            cp.start()

    def finish(*refs):
        for cp in copies(*refs):
            cp.wait()

    outs = [jax.ShapeDtypeStruct((4,) + g.shape[1:], g.dtype) for g in blocks]
    return _Comm(blocks, outs, 4 * n, 0, start, finish)


def _chips_comm(parts):
    n = len(parts)

    def copies(p_refs, o_refs, send_sems, recv_sems, _):
        x, y, c = _position()
        chips = [(1 - x, y), (x, 1 - y), (1 - x, 1 - y)]
        return [pltpu.make_async_remote_copy(src_ref=p_refs[t].at[2 * px + py], dst_ref=o_refs[t].at[j],
                                             send_sem=send_sems.at[3 * t + j], recv_sem=recv_sems.at[3 * t + j],
                                             device_id=(px, py, c), device_id_type=MESH)
                for t in range(n) for j, (px, py) in enumerate(chips)]

    def start(*refs):
        for cp in copies(*refs):
            cp.start()

    def finish(*refs):
        for cp in copies(*refs):
            cp.wait()

    outs = [jax.ShapeDtypeStruct((3,) + p.shape[1:], p.dtype) for p in parts]
    return _Comm(parts, outs, 3 * n, 0, start, finish)


ROW_TILE_BYTES = 2 << 20


def _row_tile(r, cols, itemsize=4):
    fits = [t for t in range(8, r + 1, 8) if r % t == 0 and t * cols * itemsize <= ROW_TILE_BYTES]
    return max(fits) if fits else r


def _pair_add(name, g, got, c_idx):
    _, r, cc = g.shape
    tr = _row_tile(r, cc, g.dtype.itemsize)

    def body(c_ref, g_ref, o_ref, out_ref):
        out_ref[...] = (g_ref[...].astype(f32) + o_ref[...].astype(f32)).astype(out_ref.dtype)

    g5 = g.reshape(4, 2, r, cc)
    spec = pltpu.PrefetchScalarGridSpec(
        num_scalar_prefetch=1, grid=(4, r // tr),
        in_specs=[pl.BlockSpec((1, 1, tr, cc), lambda q, i, c_ref: (q, c_ref[0], i, 0)),
                  pl.BlockSpec((1, 1, tr, cc), lambda q, i, c_ref: (q, 0, i, 0))],
        out_specs=pl.BlockSpec((1, 1, tr, cc), lambda q, i, c_ref: (q, 0, i, 0)))
    out = _pcall(body, name=name, grid_spec=spec, out_shape=jax.ShapeDtypeStruct((4, 1, r, cc), g.dtype),
                 compiler_params=_params(("arbitrary", "arbitrary")))(c_idx, g5, got.reshape(4, 1, r, cc))
    return out.reshape(4, r, cc)


def _adam_math(w, g, m, v):
    m2 = ADAM_B1 * m + (1.0 - ADAM_B1) * g
    v2 = ADAM_B2 * v + (1.0 - ADAM_B2) * (g * g)
    m_hat = m2 / (1.0 - ADAM_B1 ** ADAM_STEP)
    v_hat = v2 / (1.0 - ADAM_B2 ** ADAM_STEP)
    delta = -ADAM_LR * (m_hat / (jnp.sqrt(v_hat) + ADAM_EPS) + ADAM_WD * w)
    return delta, m2, v2


def _adam_sharded(name, w, m, v, part, got, chip_idx):
    r, cc = w.shape
    tr = _row_tile(r, cc)

    def body(q_ref, w_ref, m_ref, v_ref, p_ref, o_ref, g_out, d_out, m_out, v_out):
        g = p_ref[0].astype(f32)
        for j in range(3):
            g = g + o_ref[j].astype(f32)
        d, m2, v2 = _adam_math(w_ref[...], g, m_ref[...], v_ref[...])
        g_out[...] = g
        d_out[...] = d
        m_out[...] = m2
        v_out[...] = v2

    row = pl.BlockSpec((tr, cc), lambda i, q_ref: (i, 0))
    spec = pltpu.PrefetchScalarGridSpec(
        num_scalar_prefetch=1, grid=(r // tr,),
        in_specs=[row, row, row, pl.BlockSpec((1, tr, cc), lambda i, q_ref: (q_ref[0], i, 0)),
                  pl.BlockSpec((3, tr, cc), lambda i, q_ref: (0, i, 0))],
        out_specs=[row, row, row, row])
    sh = jax.ShapeDtypeStruct((r, cc), f32)
    return _pcall(body, name=name, grid_spec=spec, out_shape=[sh, sh, sh, sh],
                  compiler_params=_params(("arbitrary",)))(chip_idx, w, m, v, part, got)


def _adam_small(name, w, m, v, parts):
    def body(w_ref, m_ref, v_ref, p_ref, g_out, d_out, m_out, v_out):
        g = p_ref[0]
        for b in range(1, N_DEV):
            g = g + p_ref[b]
        d, m2, v2 = _adam_math(w_ref[...], g, m_ref[...], v_ref[...])
        g_out[...] = g
        d_out[...] = d
        m_out[...] = m2
        v_out[...] = v2

    sh = jax.ShapeDtypeStruct(w.shape, f32)
    return _pcall(body, name=name, out_shape=[sh, sh, sh, sh])(w, m, v, parts)


def _swa_math(n, qa, qb, kap, kac, kbp, kbc, vp, vc, cq, sq, cp, sp, sink):
    g, blk, half = qa.shape
    c3, s3 = cq[None], sq[None]
    q1 = (qa * c3 - qb * s3).reshape(g * blk, half)
    q2 = (qb * c3 + qa * s3).reshape(g * blk, half)
    ck, sk = jnp.concatenate([cp, cq], axis=0), jnp.concatenate([sp, sq], axis=0)
    k1, k2 = jnp.concatenate([kap[0], kac[0]], axis=0), jnp.concatenate([kbp[0], kbc[0]], axis=0)
    k1r, k2r = k1 * ck - k2 * sk, k2 * ck + k1 * sk
    vv = jnp.concatenate([vp[0], vc[0]], axis=0)
    s = (_dot(q1, k1r, "nt") + _dot(q2, k2r, "nt")) * (HEAD_DIM ** -0.5)
    s = s.reshape(g, blk, 2 * blk)
    qi = lax.broadcasted_iota(jnp.int32, (blk, 2 * blk), 0)
    kj = lax.broadcasted_iota(jnp.int32, (blk, 2 * blk), 1)
    valid = (kj > qi) & (kj <= qi + blk) & ((kj >= blk) | (n > 0))
    s = jnp.where(valid[None], s, NEG_INF)
    sink3 = sink.reshape(g, 1, 1)
    mx = jnp.maximum(jnp.max(s, axis=-1, keepdims=True), sink3)
    e = jnp.exp(s - mx)
    z = jnp.sum(e, axis=-1, keepdims=True) + jnp.exp(sink3 - mx)
    p = (e / z).reshape(g * blk, 2 * blk)
    return _dot(p, vv, "nn").reshape(g, blk, 2 * half)


def _swa_specs(g, blk, half):
    prev = lambda n: jnp.maximum(n - 1, 0)
    q_spec = pl.BlockSpec((g, blk, half), lambda h, n: (h, n, 0))
    kc = pl.BlockSpec((1, blk, half), lambda h, n: (h, n, 0))
    kp = pl.BlockSpec((1, blk, half), lambda h, n: (h, prev(n), 0))
    vc = pl.BlockSpec((1, blk, 2 * half), lambda h, n: (h, n, 0))
    vp = pl.BlockSpec((1, blk, 2 * half), lambda h, n: (h, prev(n), 0))
    tc = pl.BlockSpec((blk, half), lambda h, n: (n, 0))
    tp = pl.BlockSpec((blk, half), lambda h, n: (prev(n), 0))
    sink = pl.BlockSpec((1, g, 1), lambda h, n: (h, 0, 0))
    o_spec = pl.BlockSpec((g, blk, 2 * half), lambda h, n: (h, n, 0))
    return q_spec, kc, kp, vc, vp, tc, tp, sink, o_spec


def _swa_fwd(qa, qb, ka, kb, v, cos, sin, sinks, comm=None):
    hq, t, half = qa.shape
    kv = ka.shape[0]
    g, blk = hq // kv, SWA_BLOCK
    q_spec, kc, kp, vc, vp, tc, tp, sink, o_spec = _swa_specs(g, blk, half)

    def body(qa_r, qb_r, kap, kac, kbp, kbc, vp_r, vc_r, cq, sq, cp, sp, sink_r, o_r):
        o_r[...] = _swa_math(pl.program_id(1), qa_r[...], qb_r[...], kap[...], kac[...], kbp[...], kbc[...], vp_r[...],
                             vc_r[...], cq[...], sq[...], cp[...], sp[...], sink_r[...]).astype(o_r.dtype)

    return _pcall(body, comm=comm, name="swa_fwd", grid=(kv, t // blk),
                  in_specs=[q_spec, q_spec, kp, kc, kp, kc, vp, vc, tc, tc, tp, tp, sink], out_specs=o_spec,
                  out_shape=jax.ShapeDtypeStruct((hq, t, 2 * half), f32),
                  compiler_params=_params(("parallel", "arbitrary")))(qa, qb, ka, ka, kb, kb, v, v, cos, sin, cos, sin, sinks)


def _swa_bwd(qa, qb, ka, kb, v, cos, sin, sinks, do, comm=None):
    hq, t, half = qa.shape
    kv = ka.shape[0]
    g, blk = hq // kv, SWA_BLOCK
    q_spec, kc, kp, vc, vp, tc, tp, sink, o_spec = _swa_specs(g, blk, half)

    def body(qa_r, qb_r, kap, kac, kbp, kbc, vp_r, vc_r, cq, sq, cp, sp, sink_r, do_r,
             dqa, dqb, dkap, dkac, dkbp, dkbc, dvp, dvc, dsink):
        n = pl.program_id(1)
        tabs = (cq[...], sq[...], cp[...], sp[...])
        fn = lambda a, b, c_, d, e, f_, g_, h_, s_: _swa_math(n, a, b, c_, d, e, f_, g_, h_, *tabs, s_)
        _, vjp = jax.vjp(fn, qa_r[...], qb_r[...], kap[...], kac[...], kbp[...], kbc[...], vp_r[...], vc_r[...], sink_r[...])
        grads = vjp(do_r[...])
        for ref, val in zip((dqa, dqb, dkap, dkac, dkbp, dkbc, dvp, dvc), grads[:8]):
            ref[...] = val

        @pl.when(n == 0)
        def _():
            dsink[...] = grads[8]

        @pl.when(n > 0)
        def _():
            dsink[...] += grads[8]

    sh = lambda a: jax.ShapeDtypeStruct(a.shape, f32)
    return _pcall(body, comm=comm, name="swa_bwd", grid=(kv, t // blk),
                  in_specs=[q_spec, q_spec, kp, kc, kp, kc, vp, vc, tc, tc, tp, tp, sink, o_spec],
                  out_specs=[q_spec, q_spec, kc, kc, kc, kc, vc, vc, sink],
                  out_shape=[sh(qa), sh(qb), sh(ka), sh(ka), sh(kb), sh(kb), sh(v), sh(v), sh(sinks)],
                  compiler_params=_params(("parallel", "arbitrary")))(qa, qb, ka, ka, kb, kb, v, v, cos, sin, cos, sin, sinks, do)


def _swa_split(za, hq, kv):
    t = za.shape[0]
    half = HEAD_DIM // 2
    tm = _row_tile(t, za.shape[1])

    def body(z_r, qa, qb, ka, kb, v):
        z = z_r[...]
        for h in range(hq):
            qa[h] = z[:, HEAD_DIM * h:HEAD_DIM * h + half]
            qb[h] = z[:, HEAD_DIM * h + half:HEAD_DIM * (h + 1)]
        for h in range(kv):
            o = HEAD_DIM * (hq + h)
            ka[h] = z[:, o:o + half]
            kb[h] = z[:, o + half:o + HEAD_DIM]
            o = HEAD_DIM * (hq + kv + h)
            v[h] = z[:, o:o + HEAD_DIM]

    spec = lambda n, w: pl.BlockSpec((n, tm, w), lambda i: (0, i, 0))
    sh = lambda n, w: jax.ShapeDtypeStruct((n, t, w), f32)
    return _pcall(body, name="swa_split", grid=(t // tm,), in_specs=[pl.BlockSpec((tm, za.shape[1]), lambda i: (i, 0))],
                  out_specs=[spec(hq, half), spec(hq, half), spec(kv, half), spec(kv, half), spec(kv, HEAD_DIM)],
                  out_shape=[sh(hq, half), sh(hq, half), sh(kv, half), sh(kv, half), sh(kv, HEAD_DIM)],
                  compiler_params=_params(("parallel",)))(za)


def _swa_merge(dqa, dqb, dkac, dkap, dkbc, dkbp, dvc, dvp):
    hq, t, half = dqa.shape
    kv = dkac.shape[0]
    blk = SWA_BLOCK
    nb = t // blk
    cols = HEAD_DIM * (hq + 2 * kv)

    def body(qa, qb, kac, kap, kbc, kbp, vc, vp, z_o, s_o):
        i = pl.program_id(0)
        more = (i < nb - 1).astype(f32)
        pieces = []
        for h in range(hq):
            pieces += [qa[h], qb[h]]
        for h in range(kv):
            pieces += [kac[h] + more * kap[h], kbc[h] + more * kbp[h]]
        for h in range(kv):
            pieces.append(vc[h] + more * vp[h])
        z = jnp.concatenate(pieces, axis=-1)
        z_o[...] = z
        colsum = jnp.sum(z, axis=0, keepdims=True)

        @pl.when(i == 0)
        def _():
            s_o[...] = colsum

        @pl.when(i > 0)
        def _():
            s_o[...] += colsum

    cur = lambda n, w: pl.BlockSpec((n, blk, w), lambda i: (0, i, 0))
    nxt = lambda n, w: pl.BlockSpec((n, blk, w), lambda i: (0, jnp.minimum(i + 1, nb - 1), 0))
    return _pcall(body, name="swa_merge", grid=(nb,),
                  in_specs=[cur(hq, half), cur(hq, half), cur(kv, half), nxt(kv, half), cur(kv, half), nxt(kv, half),
                            cur(kv, HEAD_DIM), nxt(kv, HEAD_DIM)],
                  out_specs=[pl.BlockSpec((blk, cols), lambda i: (i, 0)), pl.BlockSpec((1, cols), lambda i: (0, 0))],
                  out_shape=[jax.ShapeDtypeStruct((t, cols), f32), jax.ShapeDtypeStruct((1, cols), f32)],
                  compiler_params=_params(("arbitrary",)))(dqa, dqb, dkac, dkap, dkbc, dkbp, dvc, dvp)


def _xattn_math(q, k, v):
    s = _dot(q, k, "nt") * (q.shape[-1] ** -0.5)
    e = jnp.exp(s - jnp.max(s, axis=-1, keepdims=True))
    p = e / jnp.sum(e, axis=-1, keepdims=True)
    return _dot(p, v, "nn")


def _xattn_fwd(q, kvm):
    t, d = q.shape
    mlen = kvm.shape[0]
    hd = d // XATTN_HEADS
    tq = min(512, t)

    def body(q_r, k_r, v_r, o_r):
        o_r[...] = _xattn_math(q_r[...], k_r[...], v_r[...]).astype(o_r.dtype)

    return _pcall(body, name="xattn_fwd", grid=(XATTN_HEADS, t // tq),
                  in_specs=[pl.BlockSpec((tq, hd), lambda h, i: (i, h)), pl.BlockSpec((mlen, hd), lambda h, i: (0, h)),
                            pl.BlockSpec((mlen, hd), lambda h, i: (0, XATTN_HEADS + h))],
                  out_specs=pl.BlockSpec((tq, hd), lambda h, i: (i, h)), out_shape=jax.ShapeDtypeStruct((t, d), bf16),
                  compiler_params=_params(("parallel", "parallel")))(q, kvm, kvm)


def _xattn_bwd(q, kvm, do):
    t, d = q.shape
    mlen = kvm.shape[0]
    hd = d // XATTN_HEADS
    tq = min(512, t)

    def body(q_r, k_r, v_r, do_r, dq, dk, dv):
        _, vjp = jax.vjp(_xattn_math, q_r[...].astype(f32), k_r[...].astype(f32), v_r[...].astype(f32))
        gq, gk, gv = vjp(do_r[...].astype(f32))
        dq[...] = gq.astype(dq.dtype)
        first = pl.program_id(1) == 0

        @pl.when(first)
        def _():
            dk[...] = gk
            dv[...] = gv

        @pl.when(jnp.logical_not(first))
        def _():
            dk[...] += gk
            dv[...] += gv

    qs = pl.BlockSpec((tq, hd), lambda h, i: (i, h))
    ms = pl.BlockSpec((mlen, hd), lambda h, i: (0, h))
    return _pcall(body, name="xattn_bwd", grid=(XATTN_HEADS, t // tq),
                  in_specs=[qs, ms, pl.BlockSpec((mlen, hd), lambda h, i: (0, XATTN_HEADS + h)), qs],
                  out_specs=[qs, ms, ms],
                  out_shape=[jax.ShapeDtypeStruct((t, d), bf16), jax.ShapeDtypeStruct((mlen, d), f32),
                             jax.ShapeDtypeStruct((mlen, d), f32)],
                  compiler_params=_params(("parallel", "arbitrary")))(q, kvm, kvm, do)


def _chunk_cumsum(lw, reverse=False):
    h, l, _ = lw.shape
    i = lax.broadcasted_iota(jnp.int32, (l, l), 0)
    j = lax.broadcasted_iota(jnp.int32, (l, l), 1)
    tri = jnp.broadcast_to(((i <= j) if reverse else (i >= j)).astype(bf16)[None], (h, l, l))
    out = jnp.zeros(lw.shape, f32)
    for piece in _split3(lw):
        out = out + lax.dot_general(tri, piece, _dims("nn", 3), preferred_element_type=f32)
    return out


def _rwkv_chunk(s0, r, k, v, a, lw, cl, k_k, k_a, r_k, ln_w, ln_b):
    l = r.shape[1]
    kk = k * k_k
    kk = kk / jnp.maximum(jnp.sqrt(jnp.sum(kk * kk, axis=-1, keepdims=True)), 1e-12)
    km = k * (1.0 + (a - 1.0) * k_a)
    av, bv = -kk, kk * a
    p_incl, p_excl, p_inv = jnp.exp(cl), jnp.exp(cl - lw), jnp.exp(-cl)
    at, bh, kh, rt = av * p_excl, bv * p_inv, km * p_inv, r * p_incl
    i = lax.broadcasted_iota(jnp.int32, (l, l), 0)
    j = lax.broadcasted_iota(jnp.int32, (l, l), 1)
    strict, incl = (i > j)[None], (i >= j)[None]
    a_ab = jnp.where(strict, _dot(at, bh, "nt"), 0.0)
    a_ak = jnp.where(strict, _dot(at, kh, "nt"), 0.0)
    a_rb = jnp.where(incl, _dot(rt, bh, "nt"), 0.0)
    a_rk = jnp.where(incl, _dot(rt, kh, "nt"), 0.0)
    rhs = _dot(at, s0, "nt") + _dot(a_ak, v, "nn")
    inv = a_ab + (i == j)[None].astype(f32)
    pw = a_ab
    for _ in range(int(math.log2(l)) - 1):
        pw = _dot(pw, pw, "nn")
        inv = inv + _dot(inv, pw, "nn")
    sa = _dot(inv, rhs, "nn")
    y = _dot(rt, s0, "nt") + _dot(a_rk, v, "nn") + _dot(a_rb, sa, "nn")
    p_last = p_incl[:, l - 1:l, :]
    s_end = s0 * p_last + _dot(v, kh * p_last, "tn") + _dot(sa, bh * p_last, "tn")
    mu = jnp.mean(y, axis=-1, keepdims=True)
    var = jnp.mean(jnp.square(y - mu), axis=-1, keepdims=True)
    out = (y - mu) * lax.rsqrt(var + GN_EPS) * ln_w + ln_b
    out = out + jnp.sum(r * km * r_k, axis=-1, keepdims=True) * v
    return out, s_end


def _rwkv_fwd(r, k, v, a, lw, heads, comm=None):
    h, t, n = r.shape
    l = min(RWKV_CHUNK, t)
    nc = t // l
    seq = pl.BlockSpec((h, l, n), lambda c: (0, c, 0))
    par = pl.BlockSpec((h, 1, n), lambda c: (0, 0, 0))

    def body(r_r, k_r, v_r, a_r, lw_r, p0, p1, p2, p3, p4, y_r, ck_r, s_scr):
        @pl.when(pl.program_id(0) == 0)
        def _():
            s_scr[...] = jnp.zeros_like(s_scr)

        s0 = s_scr[...]
        ck_r[0] = s0
        lw_v = lw_r[...]
        out, s_end = _rwkv_chunk(s0, r_r[...], k_r[...], v_r[...], a_r[...], lw_v, _chunk_cumsum(lw_v),
                                 p0[...], p1[...], p2[...], p3[...], p4[...])
        y_r[...] = out
        s_scr[...] = s_end

    return _pcall(body, comm=comm, name="rwkv_fwd", grid=(nc,), in_specs=[seq] * 5 + [par] * 5,
                  out_specs=[seq, pl.BlockSpec((1, h, n, n), lambda c: (c, 0, 0, 0))],
                  out_shape=[jax.ShapeDtypeStruct((h, t, n), f32), jax.ShapeDtypeStruct((nc, h, n, n), f32)],
                  scratch_shapes=[pltpu.VMEM((h, n, n), f32)],
                  compiler_params=_params(("arbitrary",)))(r, k, v, a, lw, *heads)


def _rwkv_bwd(r, k, v, a, lw, heads, ck, dy, comm=None):
    h, t, n = r.shape
    l = min(RWKV_CHUNK, t)
    nc = t // l
    seq = pl.BlockSpec((h, l, n), lambda c: (0, nc - 1 - c, 0))
    par = pl.BlockSpec((h, 1, n), lambda c: (0, 0, 0))

    def body(r_r, k_r, v_r, a_r, lw_r, p0, p1, p2, p3, p4, ck_r, dy_r,
             dr, dk, dv, da, dlw, g0, g1, g2, g3, g4, ds_scr):
        first = pl.program_id(0) == 0

        @pl.when(first)
        def _():
            ds_scr[...] = jnp.zeros_like(ds_scr)

        lw_v = lw_r[...]
        _, vjp = jax.vjp(_rwkv_chunk, ck_r[0], r_r[...], k_r[...], v_r[...], a_r[...], lw_v, _chunk_cumsum(lw_v),
                         p0[...], p1[...], p2[...], p3[...], p4[...])
        grads = vjp((dy_r[...], ds_scr[...]))
        ds_scr[...] = grads[0]
        dr[...] = grads[1]
        dk[...] = grads[2]
        dv[...] = grads[3]
        da[...] = grads[4]
        dlw[...] = grads[5] + _chunk_cumsum(grads[6], reverse=True)
        acc = (g0, g1, g2, g3, g4)

        @pl.when(first)
        def _():
            for ref, val in zip(acc, grads[7:]):
                ref[...] = val

        @pl.when(jnp.logical_not(first))
        def _():
            for ref, val in zip(acc, grads[7:]):
                ref[...] += val

    seq_sh = jax.ShapeDtypeStruct((h, t, n), f32)
    par_sh = jax.ShapeDtypeStruct((h, 1, n), f32)
    return _pcall(body, comm=comm, name="rwkv_bwd", grid=(nc,),
                  in_specs=[seq] * 5 + [par] * 5 + [pl.BlockSpec((1, h, n, n), lambda c: (nc - 1 - c, 0, 0, 0)), seq],
                  out_specs=[seq] * 5 + [par] * 5, out_shape=[seq_sh] * 5 + [par_sh] * 5,
                  scratch_shapes=[pltpu.VMEM((h, n, n), f32)],
                  compiler_params=_params(("arbitrary",)))(r, k, v, a, lw, *heads, ck, dy)


def _rwkv_pre_math(c, lp, p_rkv, p_rkv_prev, p_l, p_l_prev, mu_rkv, mu_l, w0, a0, decay_up, aaa_up, gate_up):
    dlp, alp, _ = lp
    z = p_rkv + (p_rkv_prev - p_rkv) * mu_rkv
    zl = p_l + (p_l_prev - p_l) * mu_l
    r, k, v = z[:, :c], z[:, c:2 * c], z[:, 2 * c:]
    wd, ad, gd = zl[:, :dlp], zl[:, dlp:dlp + alp], zl[:, dlp + alp:]
    w = -_softplus(-(w0 + _dot(jnp.tanh(wd), decay_up, "nn"))) - 0.5
    a = _sigmoid(a0 + _dot(ad, aaa_up, "nn"))
    g = _dot(_sigmoid(gd), gate_up, "nn")
    return r, k, v, -jnp.exp(w), a, g


def _pad_to(a, n, axis):
    if a.shape[axis] == n:
        return a
    pad = [(0, 0)] * a.ndim
    pad[axis] = (0, n - a.shape[axis])
    return jnp.pad(a, pad)


def _up128(n):
    return -(-n // LANE) * LANE


def _shift_down(p):
    return jnp.concatenate([jnp.zeros((1, p.shape[1]), p.dtype), p[:-1]], axis=0)


def _shift_up(p):
    return jnp.concatenate([p[1:], jnp.zeros((1, p.shape[1]), p.dtype)], axis=0)


def _swiglu(g, u):
    return jax.nn.silu(g) * u


def _ffn_hidden(name, h, w_gate, w_up, comm=None):
    t, d = h.shape
    nb, _, n = w_gate.shape
    tm = _pick(t, (1024, 512, 256, 128))

    def body(h_r, wg_r, wu_r, g_o, u_o, a_o):
        hv = h_r[...]
        g = lax.dot_general(hv, wg_r[0], _dims("nn", 2), preferred_element_type=f32)
        u = lax.dot_general(hv, wu_r[0], _dims("nn", 2), preferred_element_type=f32)
        g_o[0] = g.astype(bf16)
        u_o[0] = u.astype(bf16)
        a_o[0] = _swiglu(g, u).astype(bf16)

    w_spec = pl.BlockSpec((1, d, n), lambda i, j: (j, 0, 0))
    o_spec = pl.BlockSpec((1, tm, n), lambda i, j: (j, i, 0))
    sh = jax.ShapeDtypeStruct((nb, t, n), bf16)
    return _pcall(body, comm=comm, name=name, grid=(t // tm, nb),
                  in_specs=[pl.BlockSpec((tm, d), lambda i, j: (i, 0)), w_spec, w_spec],
                  out_specs=[o_spec, o_spec, o_spec], out_shape=[sh, sh, sh],
                  compiler_params=_params(("parallel", "arbitrary")))(h, w_gate, w_up)


def _ffn_out(name, act, w_down, x, comm=None):
    nb, t, n = act.shape
    d = w_down.shape[2]
    tm, tn = _pick(t, (512, 256, 128)), _pick(d, (512, 256, 128))

    def body(a_r, w_r, x_r, o_r):
        acc = x_r[...]
        for j in range(nb):
            acc = acc + 0.5 * lax.dot_general(a_r[j], w_r[j], _dims("nn", 2), preferred_element_type=f32)
        o_r[...] = acc

    return _pcall(body, comm=comm, name=name, grid=(t // tm, d // tn),
                  in_specs=[pl.BlockSpec((nb, tm, n), lambda i, j: (0, i, 0)), pl.BlockSpec((nb, n, tn), lambda i, j: (0, 0, j)),
                            pl.BlockSpec((tm, tn), lambda i, j: (i, j))],
                  out_specs=pl.BlockSpec((tm, tn), lambda i, j: (i, j)), out_shape=jax.ShapeDtypeStruct((t, d), f32),
                  compiler_params=_params(("parallel", "parallel")))(act, w_down, x)


def _ffn_dhidden(name, dout, w_down, gate, up, comm=None):
    t, d = dout.shape
    nb, n, _ = w_down.shape
    tm = _pick(t, (512, 256, 128))

    def body(d_r, w_r, g_r, u_r, dg_o, du_o):
        dact = 0.5 * lax.dot_general(d_r[...].astype(MXU_DTYPE), w_r[0], _dims("nt", 2), preferred_element_type=f32)
        _, vjp = jax.vjp(_swiglu, g_r[0].astype(f32), u_r[0].astype(f32))
        dg, du = vjp(dact)
        dg_o[0] = dg.astype(bf16)
        du_o[0] = du.astype(bf16)

    o_spec = pl.BlockSpec((1, tm, n), lambda i, j: (j, i, 0))
    sh = jax.ShapeDtypeStruct((nb, t, n), bf16)
    return _pcall(body, comm=comm, name=name, grid=(t // tm, nb),
                  in_specs=[pl.BlockSpec((tm, d), lambda i, j: (i, 0)), pl.BlockSpec((1, n, d), lambda i, j: (j, 0, 0)), o_spec, o_spec],
                  out_specs=[o_spec, o_spec], out_shape=[sh, sh],
                  compiler_params=_params(("parallel", "arbitrary")))(dout, w_down, gate, up)


def _ffn_dw_down(name, act, dout, comm=None):
    nb, t, n = act.shape
    d = dout.shape[1]
    tn = _pick(d, (1024, 512, 256, 128))

    def body(a_r, d_r, o_r):
        acc = lax.dot_general(a_r[0], d_r[...].astype(MXU_DTYPE), _dims("tn", 2), preferred_element_type=f32)
        o_r[0] = (0.5 * acc).astype(bf16)

    return _pcall(body, comm=comm, name=name, grid=(nb, d // tn),
                  in_specs=[pl.BlockSpec((1, t, n), lambda j, i: (j, 0, 0)), pl.BlockSpec((t, tn), lambda j, i: (0, i))],
                  out_specs=pl.BlockSpec((1, n, tn), lambda j, i: (j, 0, i)), out_shape=jax.ShapeDtypeStruct((nb, n, d), bf16),
                  compiler_params=_params(("parallel", "parallel")))(act, dout)


def _ffn_dw_hidden(name, h, dgate, dup, comm=None):
    t, d = h.shape
    nb, _, n = dgate.shape
    tm = _pick(d, (1024, 512, 256, 128))

    def body(h_r, g_r, u_r, dg_o, du_o):
        hv = h_r[...]
        dg_o[0] = lax.dot_general(hv, g_r[0], _dims("tn", 2), preferred_element_type=f32).astype(bf16)
        du_o[0] = lax.dot_general(hv, u_r[0], _dims("tn", 2), preferred_element_type=f32).astype(bf16)

    g_spec = pl.BlockSpec((1, t, n), lambda j, i: (j, 0, 0))
    o_spec = pl.BlockSpec((1, tm, n), lambda j, i: (j, i, 0))
    sh = jax.ShapeDtypeStruct((nb, d, n), bf16)
    return _pcall(body, comm=comm, name=name, grid=(nb, d // tm),
                  in_specs=[pl.BlockSpec((t, tm), lambda j, i: (0, i)), g_spec, g_spec],
                  out_specs=[o_spec, o_spec], out_shape=[sh, sh],
                  compiler_params=_params(("parallel", "parallel")))(h, dgate, dup)


def _ffn_dh(name, dhid, w, res=None, comm=None):
    nb, t, n = dhid.shape
    d = w.shape[1]
    tm, tn = _pick(t, (512, 256, 128)), _pick(d, (512, 256, 128))

    def body(*refs):
        acc = refs[2][...] if res is not None else jnp.zeros((tm, tn), f32)
        for j in range(nb):
            acc = acc + lax.dot_general(refs[0][j], refs[1][j], _dims("nt", 2), preferred_element_type=f32)
        refs[-1][...] = acc

    in_specs = [pl.BlockSpec((nb, tm, n), lambda i, j: (0, i, 0)), pl.BlockSpec((nb, tn, n), lambda i, j: (0, j, 0))]
    args = [dhid, w]
    if res is not None:
        in_specs.append(pl.BlockSpec((tm, tn), lambda i, j: (i, j)))
        args.append(res)
    return _pcall(body, comm=comm, name=name, grid=(t // tm, d // tn), in_specs=in_specs,
                  out_specs=pl.BlockSpec((tm, tn), lambda i, j: (i, j)), out_shape=jax.ShapeDtypeStruct((t, d), f32),
                  compiler_params=_params(("parallel", "parallel")))(*args)


def _lora_bounds(c, lora):
    dl, al, gl = lora
    o1 = 3 * c
    o2, o3 = o1 + dl, o1 + dl + al
    return o1, o2, o3, o3 + gl, (_up128(dl), _up128(al), _up128(gl))


def _win_split(g8, c, lora):
    nb, d, n = g8.shape
    o1, o2, o3, o4, (dlp, alp, glp) = _lora_bounds(c, lora)
    tm = _row_tile(d, nb * n, g8.dtype.itemsize)

    def body(x, rkv_o, lora_o, swa_o):
        w = jnp.concatenate([x[j] for j in range(nb)], axis=-1)
        pad = lambda p, m: p if p.shape[1] == m else jnp.concatenate([p, jnp.zeros((p.shape[0], m - p.shape[1]), p.dtype)], axis=-1)
        rkv_o[...] = w[:, :o1]
        lora_o[...] = jnp.concatenate([pad(w[:, o1:o2], dlp), pad(w[:, o2:o3], alp), pad(w[:, o3:o4], glp)], axis=-1)
        swa_o[...] = w[:, o4:]

    widths = (o1, dlp + alp + glp, nb * n - o4)
    return _pcall(body, name="w_in_split", grid=(d // tm,), in_specs=[pl.BlockSpec((nb, tm, n), lambda i: (0, i, 0))],
                  out_specs=[pl.BlockSpec((tm, wd), lambda i: (i, 0)) for wd in widths],
                  out_shape=[jax.ShapeDtypeStruct((d, wd), g8.dtype) for wd in widths],
                  compiler_params=_params(("parallel",)))(g8)


def _win_merge(dw_rkv, dw_lora, dw_swa, c, lora, nb):
    d = dw_rkv.shape[0]
    o1, o2, o3, o4, (dlp, alp, glp) = _lora_bounds(c, lora)
    dl, al, gl = lora
    total = o4 + dw_swa.shape[1]
    n = total // nb
    tm = _row_tile(d, total, dw_rkv.dtype.itemsize)

    def body(a, b, s, o):
        bv = b[...]
        w = jnp.concatenate([a[...], bv[:, :dl], bv[:, dlp:dlp + al], bv[:, dlp + alp:dlp + alp + gl], s[...]], axis=-1)
        for j in range(nb):
            o[j] = w[:, n * j:n * (j + 1)]

    ins = [dw_rkv, dw_lora, dw_swa]
    return _pcall(body, name="w_in_merge", grid=(d // tm,), in_specs=[pl.BlockSpec((tm, a.shape[1]), lambda i: (i, 0)) for a in ins],
                  out_specs=pl.BlockSpec((nb, tm, n), lambda i: (0, i, 0)), out_shape=jax.ShapeDtypeStruct((nb, d, n), dw_rkv.dtype),
                  compiler_params=_params(("parallel",)))(*ins)


def _norm_bwd(name, x, g_norm, dh, dres, comm=None):
    d = x.shape[1]

    def fn(xb, dhb, drb, g):
        _, vjp = jax.vjp(_rms, xb, g)
        dx, dg = vjp(dhb)
        return drb + dx, dg

    return _rows(name, fn, [x, dh, dres], [g_norm], [(d, f32)], [((1, d), f32)], comm=comm)


def _colsum(name, a):
    return _rows(name, lambda ab: (jnp.sum(ab.astype(f32), axis=0, keepdims=True),), [a], [], [], [((1, a.shape[1]), f32)])[0]


def kernel(x, mem, f1_norm, f1_gate, f1_up, f1_down, mix_norm, w_in, b_in_attn, rw_mu, rw_w0, rw_decay_up, rw_a0, rw_aaa_up, rw_gate_up, rw_k_k, rw_k_a, rw_r_k, rw_lnx_w, rw_lnx_b, attn_sinks, w_out, b_out, xa_norm, mem_norm, w_xq, w_xkv, w_xo, f2_norm, f2_gate, f2_up, f2_down, final_norm, loss_target, m_f1_norm, m_f1_gate, m_f1_up, m_f1_down, m_mix_norm, m_w_in, m_b_in_attn, m_rw_mu, m_rw_w0, m_rw_decay_up, m_rw_a0, m_rw_aaa_up, m_rw_gate_up, m_rw_k_k, m_rw_k_a, m_rw_r_k, m_rw_lnx_w, m_rw_lnx_b, m_attn_sinks, m_w_out, m_b_out, m_xa_norm, m_mem_norm, m_w_xq, m_w_xkv, m_w_xo, m_f2_norm, m_f2_gate, m_f2_up, m_f2_down, m_final_norm, v_f1_norm, v_f1_gate, v_f1_up, v_f1_down, v_mix_norm, v_w_in, v_b_in_attn, v_rw_mu, v_rw_w0, v_rw_decay_up, v_rw_a0, v_rw_aaa_up, v_rw_gate_up, v_rw_k_k, v_rw_k_a, v_rw_r_k, v_rw_lnx_w, v_rw_lnx_b, v_attn_sinks, v_w_out, v_b_out, v_xa_norm, v_mem_norm, v_w_xq, v_w_xkv, v_w_xo, v_f2_norm, v_f2_gate, v_f2_up, v_f2_down, v_final_norm):
    names = ["f1_norm", "f1_gate", "f1_up", "f1_down", "mix_norm", "w_in", "b_in_attn", "rw_mu", "rw_w0", "rw_decay_up",
             "rw_a0", "rw_aaa_up", "rw_gate_up", "rw_k_k", "rw_k_a", "rw_r_k", "rw_lnx_w", "rw_lnx_b", "attn_sinks", "w_out",
             "b_out", "xa_norm", "mem_norm", "w_xq", "w_xkv", "w_xo", "f2_norm", "f2_gate", "f2_up", "f2_down", "final_norm"]
    env = dict(locals())
    w_of = {k: env[k] for k in names}
    m_of = {k: env["m_" + k] for k in names}
    v_of = {k: env["v_" + k] for k in names}
    col_sharded = ["f1_gate", "f1_up", "w_in", "rw_decay_up", "rw_aaa_up", "rw_gate_up", "w_xkv", "f2_gate", "f2_up"]
    row_sharded = ["f1_down", "w_out", "w_xq", "w_xo", "f2_down"]
    sharded = col_sharded + row_sharded
    small = [k for k in names if k not in sharded]

    x0, mem0, tgt = x[0], mem[0], loss_target[0]
    t, d = x0.shape
    c = rw_w0.shape[-1]
    heads = c // HEAD_DIM
    dl, al, gl = rw_decay_up.shape[1], rw_aaa_up.shape[1], rw_gate_up.shape[1]
    dlp, alp, glp = _up128(dl), _up128(al), _up128(gl)
    swa_w = d - c
    hq, kvh = swa_w // HEAD_DIM, (b_in_attn.shape[-1] - swa_w) // (2 * HEAD_DIM)
    my_x, my_y, my_c = _position()
    c_idx = jnp.reshape(my_c, (1,)).astype(jnp.int32)
    chip_idx = jnp.reshape(2 * my_x + my_y, (1,)).astype(jnp.int32)

    shard2d = {k: w_of[k][0] for k in sharded}
    cast = {k: _rows("cast_" + k, lambda a: (a,), [shard2d[k]], [], [(shard2d[k].shape[1], bf16)], tm=_row_tile(*shard2d[k].shape))[0]
            for k in sharded}
    ffn1_keys, ffn2_keys = ["f1_gate", "f1_up", "f1_down"], ["f2_gate", "f2_up", "f2_down"]
    in_keys = ["w_in", "rw_decay_up", "rw_aaa_up", "rw_gate_up"]
    kept_in_blocks = ffn1_keys + ffn2_keys + ["w_in", "w_xkv"]

    def whole(k, g8):
        if k in kept_in_blocks:
            return g8
        if k in col_sharded:
            return g8.transpose(1, 0, 2).reshape(g8.shape[1], N_DEV * g8.shape[2])
        return g8.reshape(N_DEV * g8.shape[1], g8.shape[2])

    def gather_of(keys):
        return _gather_comm([cast[k] for k in keys])

    def wholes(keys, gathered):
        return {k: whole(k, g8) for k, g8 in zip(keys, gathered)}

    (h1,), gathered = _rows("f1_norm", lambda xb, g: (_rms(xb, g),), [x0], [f1_norm], [(d, bf16)], comm=gather_of(ffn1_keys[:2]))
    full = wholes(ffn1_keys[:2], gathered)
    (gate1, up1, act1), gathered = _ffn_hidden("f1_hidden", h1, full["f1_gate"], full["f1_up"], comm=gather_of(["f1_down"]))
    full.update(wholes(["f1_down"], gathered))
    x1, gathered = _ffn_out("f1_out", act1, full["f1_down"], x0, comm=gather_of(in_keys))
    full.update(wholes(in_keys, gathered))
    ffn1_saved = (h1, gate1, up1, act1)
    w_rkv, w_lora, w_swa = _win_split(full["w_in"], c, (dl, al, gl))
    o1, o2, o3, shift_cols, _ = _lora_bounds(c, (dl, al, gl))
    mu_rkv = rw_mu[:, :3 * c]
    mu_l = jnp.concatenate([_pad_to(rw_mu[:, o1:o2], dlp, 1), _pad_to(rw_mu[:, o2:o3], alp, 1),
                            _pad_to(rw_mu[:, o3:shift_cols], glp, 1)], axis=1)
    decay_up = _pad_to(full["rw_decay_up"], dlp, 0).astype(f32)
    aaa_up = _pad_to(full["rw_aaa_up"], alp, 0).astype(f32)
    gate_up = _pad_to(full["rw_gate_up"], glp, 0).astype(f32)
    head_pars = [p.reshape(heads, 1, HEAD_DIM) for p in (rw_k_k, rw_k_a, rw_r_k, rw_lnx_w, rw_lnx_b)]
    final_g = final_norm.reshape(1, d)

    (h2,) = _rows("mix_norm", lambda xb, g: (_rms(xb, g),), [x1], [mix_norm], [(d, bf16)])
    p_rkv, gathered = _mm("in_rkv", h2, w_rkv, "nn", f32, comm=gather_of(["w_out"]))
    full.update(wholes(["w_out"], gathered))
    p_l =_mm("in_lora", h2, w_lora, "nn", f32)
    za = _mm("in_swa", h2, w_swa, "nn", f32, bias=b_in_attn)
    pre_fn = functools.partial(_rwkv_pre_math, c, (dlp, alp, glp))
    pre_rows = [p_rkv, _shift_down(p_rkv), p_l, _shift_down(p_l)]
    pre_full = [mu_rkv, mu_l, rw_w0, rw_a0, decay_up, aaa_up, gate_up]
    (r_h, k_h, v_h, lw_h, a_h, g_t), gathered = _rows("rwkv_pre", pre_fn, pre_rows, pre_full,
                                                       [(c, f32, HEAD_DIM)] * 5 + [(c, f32)], tm=128, comm=gather_of(["w_xq"]))
    full.update(wholes(["w_xq"], gathered))
    seqs = [r_h, k_h, v_h, a_h, lw_h]
    (y_heads, checkpoints), gathered = _rwkv_fwd(*seqs, head_pars, comm=gather_of(["f2_gate"]))
    full.update(wholes(["f2_gate"], gathered))

    pos = jnp.arange(t, dtype=f32)
    inv_freq = ROPE_THETA ** (-jnp.arange(0, HEAD_DIM, 2, dtype=f32) / HEAD_DIM)
    ang = pos[:, None] * inv_freq[None, :]
    cos, sin = jnp.cos(ang), jnp.sin(ang)
    sinks3 = attn_sinks.reshape(kvh, hq // kvh, 1)
    swa_in = (*_swa_split(za, hq, kvh), cos, sin, sinks3)
    y_swa_heads, gathered = _swa_fwd(*swa_in, comm=gather_of(["f2_up"]))
    full.update(wholes(["f2_up"], gathered))
    (ycat,) = _rows("mix_cat", lambda yb, gb, sb: (jnp.concatenate([yb * gb, sb], axis=1),), [y_heads, g_t, y_swa_heads], [],
                    [(d, bf16)])
    x2, gathered = _mm("mix_out", ycat, full["w_out"], "nn", f32, res=x1, bias=b_out, comm=gather_of(["w_xkv"]))
    full.update(wholes(["w_xkv"], gathered))

    (h3,) = _rows("xa_norm", lambda xb, g: (_rms(xb, g),), [x2], [xa_norm], [(d, bf16)])
    (mem_n,) = _rows("mem_norm", lambda xb, g: (_rms(xb, g),), [mem0], [mem_norm], [(d, bf16)])
    q_x, gathered = _mm("xa_q", h3, full["w_xq"], "nn", bf16, comm=gather_of(["w_xo"]))
    full.update(wholes(["w_xo"], gathered))
    kv_x = _mm("xa_kv", mem_n, full["w_xkv"], "nn", bf16)
    o_x = _xattn_fwd(q_x, kv_x)
    x3 = _mm("xa_out", o_x, full["w_xo"], "nn", f32, res=x2)
    (h4,) = _rows("f2_norm", lambda xb, g: (_rms(xb, g),), [x3], [f2_norm], [(d, bf16)])
    (gate2, up2, act2), gathered = _ffn_hidden("f2_hidden", h4, full["f2_gate"], full["f2_up"], comm=gather_of(["f2_down"]))
    full.update(wholes(["f2_down"], gathered))
    x4 = _ffn_out("f2_out", act2, full["f2_down"], x3)
    ffn2_saved = (h4, gate2, up2, act2)

    def loss_fn(xb, tb, g):
        def per_row(xv, gv):
            return 0.5 * jnp.mean(jnp.square(_rms(xv, gv) - tb), axis=-1, keepdims=True)

        lrow, vjp = jax.vjp(per_row, xb, g)
        dxb, dgb = vjp(jnp.ones_like(lrow))
        return dxb, dgb, jnp.sum(lrow, axis=0, keepdims=True)

    dx4, d_final, loss_part = _rows("loss", loss_fn, [x4, tgt], [final_g], [(d, f32)], [((1, d), f32), ((1, 1), f32)])
    loss = lax.psum(loss_part[0, 0], ("x", "y", "c"))

    grads, small_g, out = {}, {"final_norm": d_final}, {}

    def pair_sums_of(tag, keys, carrier=None):
        blocks = []
        for k in keys:
            g2 = grads[k]
            rr, cc = shard2d[k].shape
            if k in kept_in_blocks:
                blocks.append(g2)
            else:
                blocks.append(g2.reshape(g2.shape[0], N_DEV, cc).transpose(1, 0, 2) if k in col_sharded else g2.reshape(N_DEV, rr, cc))
        if carrier is None:
            from_sibling = _comm_only("grads_to_sibling_" + tag, _sibling_comm(blocks))
        else:
            carried, from_sibling = carrier(_sibling_comm(blocks))
        pairs = [_pair_add("pair_add_" + k, b, o, c_idx) for k, b, o in zip(keys, blocks, from_sibling)]
        return pairs if carrier is None else (pairs, carried)

    def update(keys, pair_sums, from_chips):
        for k, part, others in zip(keys, pair_sums, from_chips):
            res = _adam_sharded("adam_" + k, shard2d[k], m_of[k][0], v_of[k][0], part, others, chip_idx)
            out[k] = [a.reshape(w_of[k].shape) for a in res]

    def ffn_backward(tag, keys, xin, g_norm, saved, dout, first_comm, ride_on_dh):
        h, gate, up, act = saved
        k_gate, k_up, k_down = keys
        if first_comm is None:
            grads[k_down], carried = _ffn_dw_down(tag + "_dw_down", act, dout), None
        else:
            grads[k_down], carried = _ffn_dw_down(tag + "_dw_down", act, dout, comm=first_comm)
        down_pairs, (dgate, dup) = pair_sums_of(
            k_down, [k_down], lambda cm: _ffn_dhidden(tag + "_dhidden", dout, full[k_down], gate, up, comm=cm))
        (grads[k_gate], grads[k_up]), from_chips = _ffn_dw_hidden(tag + "_dw_hidden", h, dgate, dup, comm=_chips_comm(down_pairs))
        update([k_down], down_pairs, from_chips)
        if ride_on_dh:
            hidden_pairs = pair_sums_of(tag + "_hidden", [k_gate, k_up])
            dh, from_chips = _ffn_dh(tag + "_dh1", dgate, full[k_gate], comm=_chips_comm(hidden_pairs[:1]))
            update([k_gate], hidden_pairs[:1], from_chips)
            dh, from_chips = _ffn_dh(tag + "_dh2", dup, full[k_up], res=dh, comm=_chips_comm(hidden_pairs[1:]))
            update([k_up], hidden_pairs[1:], from_chips)
            hidden_pairs = None
        else:
            hidden_pairs, dh = pair_sums_of(tag + "_hidden", [k_gate, k_up],
                                            lambda cm: _ffn_dh(tag + "_dh1", dgate, full[k_gate], comm=cm))
            dh = _ffn_dh(tag + "_dh2", dup, full[k_up], res=dh)
        dx, dg_norm = _norm_bwd(tag + "_dnorm", xin, g_norm, dh, dout)
        return dx, dg_norm, hidden_pairs, carried

    xa_keys = ["w_xq", "w_xkv", "w_xo"]
    dx3, small_g["f2_norm"], ffn2_pairs, _ = ffn_backward("f2b", ffn2_keys, x3, f2_norm, ffn2_saved, dx4, None, False)

    do_x = _mm("xa_do", dx3, full["w_xo"], "nt", bf16)
    grads["w_xo"] = _mm("xa_dwo", o_x, dx3, "tn", bf16)
    dq_x, dk_x, dv_x = _xattn_bwd(q_x, kv_x, do_x)
    grads["w_xq"] = _mm("xa_dwq", h3, dq_x, "tn", bf16)
    dh3 = _mm("xa_dh", dq_x, full["w_xq"], "nt", f32)
    dkv_x = jnp.concatenate([dk_x, dv_x], axis=1)
    grads["w_xkv"] = _mm("xa_dwkv", mem_n, dkv_x, "tn", bf16, out_blocks=N_DEV)
    dkv_blocks = dkv_x.astype(bf16).reshape(dkv_x.shape[0], N_DEV, -1).transpose(1, 0, 2)
    dmem_n = _ffn_dh("xa_dmem", dkv_blocks, full["w_xkv"])
    _, small_g["mem_norm"] = _norm_bwd("mem_dnorm", mem0, mem_norm, dmem_n, jnp.zeros_like(mem0))
    xa_pairs, (dx2, small_g["xa_norm"]) = pair_sums_of(
        "xa", xa_keys, lambda cm: _norm_bwd("xa_dnorm", x2, xa_norm, dh3, dx3, comm=cm))

    dycat = _mm("mix_dy", dx2, full["w_out"], "nt", f32)
    grads["w_out"] = _mm("mix_dwout", ycat, dx2, "tn", bf16)
    small_g["b_out"] = _colsum("mix_dbout", dx2)
    out_pairs, (dy_heads, dg_t, do_sw) = pair_sums_of("out", ["w_out"], lambda cm: _rows(
        "mix_dgate", lambda db, yb, gb: (db[:, :c] * gb, db[:, :c] * yb, db[:, c:]), [dycat, y_heads, g_t], [],
        [(c, f32, HEAD_DIM), (c, f32), (swa_w, f32, HEAD_DIM)], comm=cm))
    rw_grads, from_chips = _rwkv_bwd(*seqs, head_pars, checkpoints, dy_heads, comm=_chips_comm(ffn2_pairs))
    update(ffn2_keys[:2], ffn2_pairs, from_chips)
    dr_h, dk_h, dv_h, da_h, dlw_h = rw_grads[:5]
    for nm, gh in zip(("rw_k_k", "rw_k_a", "rw_r_k", "rw_lnx_w", "rw_lnx_b"), rw_grads[5:]):
        small_g[nm] = gh.reshape(w_of[nm].shape)

    def pre_bwd(*args):
        _, vjp = jax.vjp(pre_fn, *args[:4], *args[10:])
        return vjp(tuple(args[4:10]))

    pre_cts = [dr_h, dk_h, dv_h, dlw_h, da_h, dg_t]
    pre_out = _rows("rwkv_pre_bwd", pre_bwd, pre_rows + pre_cts, pre_full,
                    [(3 * c, f32), (3 * c, f32), (dlp + alp + glp, f32), (dlp + alp + glp, f32)],
                    [(p.shape, f32) for p in pre_full], tm=128)
    dp_rkv = pre_out[0] + _shift_up(pre_out[1])
    dp_l = pre_out[2] + _shift_up(pre_out[3])
    dmu_rkv, dmu_l, small_g["rw_w0"], small_g["rw_a0"], d_decay_up, d_aaa_up, d_gate_up = pre_out[4:]
    small_g["rw_mu"] = jnp.concatenate([dmu_rkv, dmu_l[:, :dl], dmu_l[:, dlp:dlp + al], dmu_l[:, dlp + alp:dlp + alp + gl]], axis=1)
    grads["rw_decay_up"] = d_decay_up[:dl].astype(bf16)
    grads["rw_aaa_up"] = d_aaa_up[:al].astype(bf16)
    grads["rw_gate_up"] = d_gate_up[:gl].astype(bf16)

    sw, from_chips = _swa_bwd(*swa_in, do_sw, comm=_chips_comm(xa_pairs))
    update(xa_keys, xa_pairs, from_chips)
    small_g["attn_sinks"] = sw[8].reshape(attn_sinks.shape)
    dza, small_g["b_in_attn"] = _swa_merge(sw[0], sw[1], sw[3], sw[2], sw[5], sw[4], sw[7], sw[6])

    dw_rkv = _mm("in_dwrkv", h2, dp_rkv, "tn", bf16)
    dw_l = _mm("in_dwlora", h2, dp_l, "tn", bf16)
    dw_swa = _mm("in_dwswa", h2, dza, "tn", bf16)
    grads["w_in"] = _win_merge(dw_rkv, dw_l, dw_swa, c, (dl, al, gl), N_DEV)
    dh2, from_chips = _mm("in_dh1", dp_rkv, w_rkv, "nt", f32, comm=_chips_comm(out_pairs))
    update(["w_out"], out_pairs, from_chips)
    dh2 = _mm("in_dh2", dp_l, w_lora, "nt", f32, res=dh2)
    in_pairs, dh2 = pair_sums_of("in", in_keys, lambda cm: _mm("in_dh3", dza, w_swa, "nt", f32, res=dh2, comm=cm))
    dx1, small_g["mix_norm"] = _norm_bwd("mix_dnorm", x1, mix_norm, dh2, dx2)

    dx0, small_g["f1_norm"], _, from_chips = ffn_backward("f1b", ffn1_keys, x0, f1_norm, ffn1_saved, dx1, _chips_comm(in_pairs), True)
    update(in_keys, in_pairs, from_chips)

    sizes = [int(w_of[k].size) for k in small]
    total = sum(sizes)
    cols = -(-total // (8 * LANE)) * LANE

    def pack(parts_of):
        flat = jnp.concatenate([parts_of[k].reshape(-1).astype(f32) for k in small])
        return _pad_to(flat, 8 * cols, 0).reshape(8, cols)

    (all_parts,) = _comm_only("gather_small_grads", _gather_comm([pack(small_g)]))
    res = _adam_small("adam_small", pack(w_of), pack(m_of), pack(v_of), all_parts)
    offs = 0
    flat_res = [a.reshape(-1) for a in res]
    for k, sz in zip(small, sizes):
        out[k] = [a[offs:offs + sz].reshape(w_of[k].shape) for a in flat_res]
        offs += sz

    outs = [loss, dx0.reshape(x.shape)]
    for j in range(4):
        outs += [out[k][j] for k in names]
    return tuple(outs)
```

```python
import functools
import math

import jax
import jax.numpy as jnp
from jax import lax
from jax.experimental import pallas as pl
from jax.experimental.pallas import tpu as pltpu

f32 = jnp.float32
bf16 = jnp.bfloat16
MXU_DTYPE = jnp.bfloat16

HEAD_DIM = 64
SWA_BLOCK = 128
ROPE_THETA = 10000.0
XATTN_HEADS = 4
RMS_EPS = 1e-6
GN_EPS = 64e-5
NEG_INF = -1e30
RWKV_CHUNK = 64

ADAM_LR = 0.001
ADAM_B1 = 0.9
ADAM_B2 = 0.999
ADAM_EPS = 1e-08
ADAM_WD = 0.01
ADAM_STEP = 10

N_DEV = 8
LANE = 128
VMEM_LIMIT_BYTES = 56 * 1024 * 1024
MM_VMEM_BUDGET = 40 * 1024 * 1024
MESH = pl.DeviceIdType.MESH


def _params(sem):
    return pltpu.CompilerParams(dimension_semantics=sem, vmem_limit_bytes=VMEM_LIMIT_BYTES)


class _Comm:
    def __init__(self, ins, outs, n_remote, n_local, start, finish):
        self.ins, self.outs, self.n_remote, self.n_local = list(ins), list(outs), n_remote, max(n_local, 1)
        self.start, self.finish = start, finish


def _pcall(body, comm=None, **kw):
    kw.setdefault("compiler_params", pltpu.CompilerParams(vmem_limit_bytes=VMEM_LIMIT_BYTES))
    if comm is None:
        return pl.pallas_call(body, **kw)
    single = not isinstance(kw["out_shape"], (list, tuple))
    out_shape = [kw["out_shape"]] if single else list(kw["out_shape"])
    out_specs = [kw["out_specs"]] if single else list(kw["out_specs"])
    in_specs, scratch, grid = list(kw["in_specs"]), list(kw.get("scratch_shapes", ())), tuple(kw.get("grid", ()))
    n_in, n_out, n_ci, n_co, n_scr = len(in_specs), len(out_shape), len(comm.ins), len(comm.outs), len(scratch)

    def wrapped(*refs):
        ins, c_ins = refs[:n_in], refs[n_in:n_in + n_ci]
        outs = refs[n_in + n_ci:n_in + n_ci + n_out]
        c_outs = refs[n_in + n_ci + n_out:n_in + n_ci + n_out + n_co]
        rest = refs[n_in + n_ci + n_out + n_co:]
        scr, sems = rest[:n_scr], rest[n_scr:]
        if grid:
            ids = [pl.program_id(k) for k in range(len(grid))]
            first = functools.reduce(jnp.logical_and, [i == 0 for i in ids])
            last = functools.reduce(jnp.logical_and, [i == g - 1 for i, g in zip(ids, grid)])
            pl.when(first)(lambda: comm.start(c_ins, c_outs, *sems))
            body(*ins, *outs, *scr)
            pl.when(last)(lambda: comm.finish(c_ins, c_outs, *sems))
        else:
            comm.start(c_ins, c_outs, *sems)
            body(*ins, *outs, *scr)
            comm.finish(c_ins, c_outs, *sems)

    any_spec = pl.BlockSpec(memory_space=pl.ANY)
    kw.update(in_specs=in_specs + [any_spec] * n_ci, out_specs=out_specs + [any_spec] * n_co,
              out_shape=out_shape + comm.outs,
              scratch_shapes=scratch + [pltpu.SemaphoreType.DMA((comm.n_remote,)), pltpu.SemaphoreType.DMA((comm.n_remote,)),
                                        pltpu.SemaphoreType.DMA((comm.n_local,))])
    if grid:
        kw["compiler_params"] = _params(("arbitrary",) * len(grid))
    call = pl.pallas_call(wrapped, **kw)

    def run(*args):
        res = call(*args, *comm.ins)
        return (res[0] if single else list(res[:n_out])), list(res[n_out:])

    return run


def _fence(name, arrays):
    def body(*refs):
        refs[-1][...] = jnp.zeros(refs[-1].shape, f32)

    return _pcall(body, name=name, in_specs=[pl.BlockSpec(memory_space=pl.ANY)] * len(arrays),
                  out_specs=pl.BlockSpec(memory_space=pltpu.VMEM), out_shape=jax.ShapeDtypeStruct((8, LANE), f32))(*arrays)


def _comm_only(name, comm):
    return _pcall(lambda: None, comm=comm, name=name, in_specs=[], out_specs=[], out_shape=[])()[1]


def _dims(kind, ndim):
    o = ndim - 2
    batch = ((0,), (0,)) if o else ((), ())
    c = {"nn": ((1 + o,), (o,)), "nt": ((1 + o,), (1 + o,)), "tn": ((o,), (o,))}[kind]
    return (c, batch)


def _dot_raw(x, y, kind):
    return lax.dot_general(x.astype(MXU_DTYPE), y.astype(MXU_DTYPE), _dims(kind, x.ndim), preferred_element_type=f32)


@functools.partial(jax.custom_vjp, nondiff_argnums=(2,))
def _dot(x, y, kind):
    return _dot_raw(x, y, kind)


def _dot_fwd(x, y, kind):
    return _dot_raw(x, y, kind), (x, y)


def _dot_bwd(kind, res, g):
    x, y = res
    if kind == "nn":
        dx, dy = _dot(g, y, "nt"), _dot(x, g, "tn")
    elif kind == "nt":
        dx, dy = _dot(g, y, "nn"), _dot(g, x, "tn")
    else:
        dx, dy = _dot(y, g, "nt"), _dot(x, g, "nn")
    return dx.astype(x.dtype), dy.astype(y.dtype)


_dot.defvjp(_dot_fwd, _dot_bwd)


def _split3(x):
    a = x.astype(bf16)
    r = x - a.astype(f32)
    b = r.astype(bf16)
    c = (r - b.astype(f32)).astype(bf16)
    return a, b, c


def _rms(x, g):
    x = x.astype(f32)
    return x * lax.rsqrt(jnp.mean(x * x, axis=-1, keepdims=True) + RMS_EPS) * g


def _sigmoid(x):
    return 1.0 / (1.0 + jnp.exp(-x))


def _softplus(x):
    return jnp.maximum(x, 0.0) + jnp.log(1.0 + jnp.exp(-jnp.abs(x)))


def _rows(name, fn, row_ins, full_ins, row_outs, acc_outs=(), tm=None, comm=None):
    width = lambda a: a.shape[1] if a.ndim == 2 else a.shape[0] * a.shape[2]
    rows = row_ins[0].shape[0] if row_ins[0].ndim == 2 else row_ins[0].shape[1]
    if tm is None:
        tm = _row_tile(rows, max([width(a) for a in row_ins] + [o[0] for o in row_outs]))
    tm = min(tm, rows)
    assert rows % tm == 0, (name, rows, tm)
    n_in = len(row_ins) + len(full_ins)
    n_o, n_a = len(row_outs), len(acc_outs)

    def load(k, ref):
        if k < len(row_ins) and row_ins[k].ndim == 3:
            return jnp.concatenate([ref[h] for h in range(ref.shape[0])], axis=-1)
        return ref[...]

    def body(*refs):
        vals = [load(k, r) for k, r in enumerate(refs[:n_in])]
        outs = fn(*vals)
        o_refs = refs[n_in:n_in + n_o]
        a_refs = refs[n_in + n_o:]
        for k in range(n_o):
            if len(row_outs[k]) == 3:
                n = row_outs[k][2]
                for h in range(row_outs[k][0] // n):
                    o_refs[k][h] = outs[k][:, h * n:(h + 1) * n].astype(o_refs[k].dtype)
            else:
                o_refs[k][...] = outs[k].astype(o_refs[k].dtype)
        if n_a:
            first = pl.program_id(0) == 0

            @pl.when(first)
            def _():
                for k in range(n_a):
                    a_refs[k][...] = outs[n_o + k].astype(a_refs[k].dtype)

            @pl.when(jnp.logical_not(first))
            def _():
                for k in range(n_a):
                    a_refs[k][...] += outs[n_o + k].astype(a_refs[k].dtype)

    by_rows = lambda cols: pl.BlockSpec((tm, cols), lambda i: (i, 0))
    by_heads = lambda h, n: pl.BlockSpec((h, tm, n), lambda i: (0, i, 0))
    in_specs = [by_rows(a.shape[1]) if a.ndim == 2 else by_heads(a.shape[0], a.shape[2]) for a in row_ins]
    in_specs += [pl.BlockSpec(a.shape, lambda i, nd=a.ndim: (0,) * nd) for a in full_ins]
    out_specs = [by_rows(o[0]) if len(o) == 2 else by_heads(o[0] // o[2], o[2]) for o in row_outs]
    out_specs += [pl.BlockSpec(s, lambda i, nd=len(s): (0,) * nd) for s, _ in acc_outs]
    out_shape = [jax.ShapeDtypeStruct((rows, o[0]) if len(o) == 2 else (o[0] // o[2], rows, o[2]), o[1]) for o in row_outs]
    out_shape += [jax.ShapeDtypeStruct(s, d) for s, d in acc_outs]
    return _pcall(body, comm=comm, name=name, grid=(rows // tm,), in_specs=in_specs, out_specs=out_specs, out_shape=out_shape,
                  compiler_params=_params(("arbitrary",)))(*row_ins, *full_ins)


def _pick(n, cands):
    for c in cands:
        if n % c == 0:
            return c
    return n


def _mm(name, a, b, mode, out_dtype, scale=1.0, res=None, bias=None, comm=None, out_blocks=None):
    b_blocks = b.ndim == 3
    if b_blocks:
        assert mode == "nn"
        (m, k), (nb, k2, tn) = a.shape, b.shape
        n = nb * tn
    elif mode == "nn":
        (m, k), (k2, n) = a.shape, b.shape
    elif mode == "nt":
        (m, k), (n, k2) = a.shape, b.shape
    else:
        (k, m), (k2, n) = a.shape, b.shape
    assert k == k2, (name, a.shape, b.shape, mode)
    if not b_blocks:
        tn = n // out_blocks if out_blocks else _pick(n, (512, 256, 128))
    tm = _pick(m, (1024, 512, 256, 128))

    def need(tm_):
        by = tm_ * k * a.dtype.itemsize + tn * k * b.dtype.itemsize + tm_ * tn * (jnp.dtype(out_dtype).itemsize + 4)
        if res is not None:
            by += tm_ * tn * res.dtype.itemsize
        return 2 * by

    while need(tm) > MM_VMEM_BUDGET and tm % 256 == 0:
        tm //= 2
    dims = _dims(mode, 2)

    def body(*refs):
        bv = refs[1][0] if b_blocks else refs[1][...]
        acc = lax.dot_general(refs[0][...].astype(MXU_DTYPE), bv.astype(MXU_DTYPE), dims, preferred_element_type=f32)
        if scale != 1.0:
            acc = acc * scale
        pos = 2
        if bias is not None:
            acc = acc + refs[pos][...]
            pos += 1
        if res is not None:
            acc = acc + refs[pos][...].astype(f32)
            pos += 1
        if out_blocks:
            refs[pos][0] = acc.astype(out_dtype)
        else:
            refs[pos][...] = acc.astype(out_dtype)

    a_spec = pl.BlockSpec((k, tm), lambda i, j: (0, i)) if mode == "tn" else pl.BlockSpec((tm, k), lambda i, j: (i, 0))
    if b_blocks:
        b_spec = pl.BlockSpec((1, k, tn), lambda i, j: (j, 0, 0))
    else:
        b_spec = pl.BlockSpec((tn, k), lambda i, j: (j, 0)) if mode == "nt" else pl.BlockSpec((k, tn), lambda i, j: (0, j))
    in_specs, args = [a_spec, b_spec], [a, b]
    if bias is not None:
        in_specs.append(pl.BlockSpec((1, tn), lambda i, j: (0, j)))
        args.append(bias)
    if res is not None:
        in_specs.append(pl.BlockSpec((tm, tn), lambda i, j: (i, j)))
        args.append(res)
    if out_blocks:
        out_spec, out_shape = pl.BlockSpec((1, tm, tn), lambda i, j: (j, i, 0)), jax.ShapeDtypeStruct((out_blocks, m, tn), out_dtype)
    else:
        out_spec, out_shape = pl.BlockSpec((tm, tn), lambda i, j: (i, j)), jax.ShapeDtypeStruct((m, n), out_dtype)
    return _pcall(body, comm=comm, name=name, grid=(m // tm, n // tn), in_specs=in_specs, out_specs=out_spec, out_shape=out_shape,
                  compiler_params=_params(("parallel", "parallel")))(*args)


def _position():
    return lax.axis_index("x"), lax.axis_index("y"), lax.axis_index("c")


def _gather_comm(shards, after=()):
    n = len(shards)

    def plan(x_refs, o_refs, send_sems, recv_sems, local_sems):
        x, y, c = _position()
        me, sibling = (x, y, c), (x, y, 1 - c)
        chips = [(1 - x, y), (x, 1 - y), (1 - x, 1 - y)]

        def slot(px, py, pc):
            return 4 * px + 2 * py + pc

        def copy(t, k, block, to, src=None):
            dst = o_refs[t].at[slot(*block)]
            return pltpu.make_async_remote_copy(src_ref=dst if src is None else src, dst_ref=dst,
                                                send_sem=send_sems.at[7 * t + k], recv_sem=recv_sems.at[7 * t + k],
                                                device_id=to, device_id_type=MESH)

        mine = [pltpu.make_async_copy(x_refs[t], o_refs[t].at[slot(*me)], local_sems.at[t]) for t in range(n)]
        first = []
        for t in range(n):
            first.append(copy(t, 0, me, sibling, src=x_refs[t]))
            first += [copy(t, 1 + j, me, (*chip, c), src=x_refs[t]) for j, chip in enumerate(chips)]
        return me, sibling, chips, c, copy, mine, first

    def start(*refs):
        _, _, _, _, _, mine, first = plan(*refs)
        for cp in mine + first:
            cp.start()

    def finish(*refs):
        me, sibling, chips, c, copy, mine, first = plan(*refs)
        passed = []
        for t in range(n):
            for j, chip in enumerate(chips):
                copy(t, 1 + j, (*chip, c), me).wait_recv()
                cp = copy(t, 4 + j, (*chip, c), sibling)
                cp.start()
                passed.append(cp)
        for t in range(n):
            copy(t, 0, sibling, me).wait_recv()
            for j, chip in enumerate(chips):
                copy(t, 4 + j, (*chip, 1 - c), me).wait_recv()
        for cp in first + passed:
            cp.wait_send()
        for cp in mine:
            cp.wait()

    outs = [jax.ShapeDtypeStruct((N_DEV,) + s.shape, s.dtype) for s in shards]
    return _Comm(list(shards) + list(after), outs, 7 * n, n, start, finish)


def _sibling_comm(blocks):
    n = len(blocks)

    def copies(g_refs, o_refs, send_sems, recv_sems, _):
        x, y, c = _position()
        return [pltpu.make_async_remote_copy(src_ref=g_refs[t].at[2 * q + 1 - c], dst_ref=o_refs[t].at[q],
                                             send_sem=send_sems.at[4 * t + q], recv_sem=recv_sems.at[4 * t + q],
                                             device_id=(x, y, 1 - c), device_id_type=MESH)
                for t in range(n) for q in range(4)]

    def start(*refs):
        for cp in copies(*refs):
            cp.start()

    def finish(*refs):
        for cp in copies(*refs):
            cp.wait()

    outs = [jax.ShapeDtypeStruct((4,) + g.shape[1:], g.dtype) for g in blocks]
    return _Comm(blocks, outs, 4 * n, 0, start, finish)


def _chips_comm(parts):
    n = len(parts)

    def copies(p_refs, o_refs, send_sems, recv_sems, _):
        x, y, c = _position()
        chips = [(1 - x, y), (x, 1 - y), (1 - x, 1 - y)]
        return [pltpu.make_async_remote_copy(src_ref=p_refs[t].at[2 * px + py], dst_ref=o_refs[t].at[j],
                                             send_sem=send_sems.at[3 * t + j], recv_sem=recv_sems.at[3 * t + j],
                                             device_id=(px, py, c), device_id_type=MESH)
                for t in range(n) for j, (px, py) in enumerate(chips)]

    def start(*refs):
        for cp in copies(*refs):
            cp.start()

    def finish(*refs):
        for cp in copies(*refs):
            cp.wait()

    outs = [jax.ShapeDtypeStruct((3,) + p.shape[1:], p.dtype) for p in parts]
    return _Comm(parts, outs, 3 * n, 0, start, finish)


HBM_SPEC = pl.BlockSpec(memory_space=pltpu.HBM)
SEM_SPEC = pl.BlockSpec(memory_space=pltpu.SEMAPHORE)
DATAFLOW = pltpu.SideEffectType.DATAFLOW_SIDE_EFFECTING


def _chip_exchange_copies(p_refs, o_refs, send_sems, recv_sems):
    x, y, c = _position()
    chips = [(1 - x, y), (x, 1 - y), (1 - x, 1 - y)]
    return [pltpu.make_async_remote_copy(src_ref=p_refs[t].at[2 * px + py], dst_ref=o_refs[t].at[j],
                                         send_sem=send_sems.at[3 * t + j], recv_sem=recv_sems.at[3 * t + j],
                                         device_id=(px, py, c), device_id_type=MESH)
            for t in range(len(p_refs)) for j, (px, py) in enumerate(chips)]


def _chips_start(name, parts, thru):
    n = len(parts)

    def body(*refs):
        for cp in _chip_exchange_copies(refs[:n], refs[n:2 * n], refs[2 * n + 1], refs[2 * n + 2]):
            cp.start()

    lands = [lax.empty((3,) + p.shape[1:], p.dtype) for p in parts]
    args = [pltpu.with_memory_space_constraint(a, pltpu.HBM) for a in list(parts) + lands + [thru]]
    res = pl.pallas_call(body, name=name, in_specs=[HBM_SPEC] * (2 * n + 1),
                         out_specs=[SEM_SPEC, SEM_SPEC] + [HBM_SPEC] * (2 * n + 1),
                         out_shape=[pltpu.SemaphoreType.DMA((3 * n,)), pltpu.SemaphoreType.DMA((3 * n,))]
                         + [pltpu.HBM(a.shape, a.dtype) for a in args],
                         input_output_aliases={i: 2 + i for i in range(2 * n + 1)},
                         compiler_params=pltpu.CompilerParams(has_side_effects=DATAFLOW))(*args)
    return (res[0], res[1], list(res[2:2 + n]), list(res[2 + n:2 + 2 * n])), res[2 + 2 * n]


def _chips_wait(name, send_sems, recv_sems, parts, lands, after):
    n = len(parts)

    def body(*refs):
        for cp in _chip_exchange_copies(refs[:n], refs[n:2 * n], refs[2 * n], refs[2 * n + 1]):
            cp.wait_send()
            cp.wait_recv()

    res = pl.pallas_call(body, name=name, out_shape=[pltpu.HBM(a.shape, a.dtype) for a in parts + lands],
                         in_specs=[HBM_SPEC] * (2 * n) + [SEM_SPEC, SEM_SPEC, pl.BlockSpec(memory_space=pl.ANY)],
                         out_specs=[HBM_SPEC] * (2 * n), input_output_aliases={i: i for i in range(2 * n)},
                         compiler_params=pltpu.CompilerParams(has_side_effects=DATAFLOW))(*parts, *lands, send_sems, recv_sems, after)
    return list(res[:n]), list(res[n:])


ROW_TILE_BYTES = 2 << 20


def _row_tile(r, cols, itemsize=4):
    fits = [t for t in range(8, r + 1, 8) if r % t == 0 and t * cols * itemsize <= ROW_TILE_BYTES]
    return max(fits) if fits else r


def _pair_add(name, g, got, c_idx):
    _, r, cc = g.shape
    tr = _row_tile(r, cc, g.dtype.itemsize)

    def body(c_ref, g_ref, o_ref, out_ref):
        out_ref[...] = (g_ref[...].astype(f32) + o_ref[...].astype(f32)).astype(out_ref.dtype)

    g5 = g.reshape(4, 2, r, cc)
    spec = pltpu.PrefetchScalarGridSpec(
        num_scalar_prefetch=1, grid=(4, r // tr),
        in_specs=[pl.BlockSpec((1, 1, tr, cc), lambda q, i, c_ref: (q, c_ref[0], i, 0)),
                  pl.BlockSpec((1, 1, tr, cc), lambda q, i, c_ref: (q, 0, i, 0))],
        out_specs=pl.BlockSpec((1, 1, tr, cc), lambda q, i, c_ref: (q, 0, i, 0)))
    out = _pcall(body, name=name, grid_spec=spec, out_shape=jax.ShapeDtypeStruct((4, 1, r, cc), g.dtype),
                 compiler_params=_params(("arbitrary", "arbitrary")))(c_idx, g5, got.reshape(4, 1, r, cc))
    return out.reshape(4, r, cc)


def _adam_math(w, g, m, v):
    m2 = ADAM_B1 * m + (1.0 - ADAM_B1) * g
    v2 = ADAM_B2 * v + (1.0 - ADAM_B2) * (g * g)
    m_hat = m2 / (1.0 - ADAM_B1 ** ADAM_STEP)
    v_hat = v2 / (1.0 - ADAM_B2 ** ADAM_STEP)
    delta = -ADAM_LR * (m_hat / (jnp.sqrt(v_hat) + ADAM_EPS) + ADAM_WD * w)
    return delta, m2, v2


def _adam_sharded(name, w, m, v, part, got, chip_idx):
    r, cc = w.shape
    tr = _row_tile(r, cc)

    def body(q_ref, w_ref, m_ref, v_ref, p_ref, o_ref, g_out, d_out, m_out, v_out):
        g = p_ref[0].astype(f32)
        for j in range(3):
            g = g + o_ref[j].astype(f32)
        d, m2, v2 = _adam_math(w_ref[...], g, m_ref[...], v_ref[...])
        g_out[...] = g
        d_out[...] = d
        m_out[...] = m2
        v_out[...] = v2

    row = pl.BlockSpec((tr, cc), lambda i, q_ref: (i, 0))
    spec = pltpu.PrefetchScalarGridSpec(
        num_scalar_prefetch=1, grid=(r // tr,),
        in_specs=[row, row, row, pl.BlockSpec((1, tr, cc), lambda i, q_ref: (q_ref[0], i, 0)),
                  pl.BlockSpec((3, tr, cc), lambda i, q_ref: (0, i, 0))],
        out_specs=[row, row, row, row])
    sh = jax.ShapeDtypeStruct((r, cc), f32)
    return _pcall(body, name=name, grid_spec=spec, out_shape=[sh, sh, sh, sh],
                  compiler_params=_params(("arbitrary",)))(chip_idx, w, m, v, part, got)


def _adam_small(name, w, m, v, parts):
    def body(w_ref, m_ref, v_ref, p_ref, g_out, d_out, m_out, v_out):
        g = p_ref[0]
        for b in range(1, N_DEV):
            g = g + p_ref[b]
        d, m2, v2 = _adam_math(w_ref[...], g, m_ref[...], v_ref[...])
        g_out[...] = g
        d_out[...] = d
        m_out[...] = m2
        v_out[...] = v2

    sh = jax.ShapeDtypeStruct(w.shape, f32)
    return _pcall(body, name=name, out_shape=[sh, sh, sh, sh])(w, m, v, parts)


def _swa_math(n, qa, qb, kap, kac, kbp, kbc, vp, vc, cq, sq, cp, sp, sink):
    g, blk, half = qa.shape
    c3, s3 = cq[None], sq[None]
    q1 = (qa * c3 - qb * s3).reshape(g * blk, half)
    q2 = (qb * c3 + qa * s3).reshape(g * blk, half)
    ck, sk = jnp.concatenate([cp, cq], axis=0), jnp.concatenate([sp, sq], axis=0)
    k1, k2 = jnp.concatenate([kap[0], kac[0]], axis=0), jnp.concatenate([kbp[0], kbc[0]], axis=0)
    k1r, k2r = k1 * ck - k2 * sk, k2 * ck + k1 * sk
    vv = jnp.concatenate([vp[0], vc[0]], axis=0)
    s = (_dot(q1, k1r, "nt") + _dot(q2, k2r, "nt")) * (HEAD_DIM ** -0.5)
    s = s.reshape(g, blk, 2 * blk)
    qi = lax.broadcasted_iota(jnp.int32, (blk, 2 * blk), 0)
    kj = lax.broadcasted_iota(jnp.int32, (blk, 2 * blk), 1)
    valid = (kj > qi) & (kj <= qi + blk) & ((kj >= blk) | (n > 0))
    s = jnp.where(valid[None], s, NEG_INF)
    sink3 = sink.reshape(g, 1, 1)
    mx = jnp.maximum(jnp.max(s, axis=-1, keepdims=True), sink3)
    e = jnp.exp(s - mx)
    z = jnp.sum(e, axis=-1, keepdims=True) + jnp.exp(sink3 - mx)
    p = (e / z).reshape(g * blk, 2 * blk)
    return _dot(p, vv, "nn").reshape(g, blk, 2 * half)


def _swa_specs(hq, kv, blk, half):
    prev = lambda n: jnp.maximum(n - 1, 0)
    q_spec = pl.BlockSpec((hq, blk, half), lambda n: (0, n, 0))
    kc = pl.BlockSpec((kv, blk, half), lambda n: (0, n, 0))
    kp = pl.BlockSpec((kv, blk, half), lambda n: (0, prev(n), 0))
    vc = pl.BlockSpec((kv, blk, 2 * half), lambda n: (0, n, 0))
    vp = pl.BlockSpec((kv, blk, 2 * half), lambda n: (0, prev(n), 0))
    tc = pl.BlockSpec((blk, half), lambda n: (n, 0))
    tp = pl.BlockSpec((blk, half), lambda n: (prev(n), 0))
    sink = pl.BlockSpec((kv, hq // kv, 1), lambda n: (0, 0, 0))
    o_spec = pl.BlockSpec((hq, blk, 2 * half), lambda n: (0, n, 0))
    return q_spec, kc, kp, vc, vp, tc, tp, sink, o_spec


def _swa_fwd(qa, qb, ka, kb, v, cos, sin, sinks, comm=None):
    hq, t, half = qa.shape
    kv = ka.shape[0]
    g, blk = hq // kv, SWA_BLOCK
    q_spec, kc, kp, vc, vp, tc, tp, sink, o_spec = _swa_specs(hq, kv, blk, half)

    def body(qa_r, qb_r, kap, kac, kbp, kbc, vp_r, vc_r, cq, sq, cp, sp, sink_r, o_r):
        tabs = (cq[...], sq[...], cp[...], sp[...])
        for h in range(kv):
            qs, ks = pl.ds(h * g, g), pl.ds(h, 1)
            o_r[qs] = _swa_math(pl.program_id(0), qa_r[qs], qb_r[qs], kap[ks], kac[ks], kbp[ks], kbc[ks], vp_r[ks], vc_r[ks],
                                *tabs, sink_r[ks]).astype(o_r.dtype)

    return _pcall(body, comm=comm, name="swa_fwd", grid=(t // blk,),
                  in_specs=[q_spec, q_spec, kp, kc, kp, kc, vp, vc, tc, tc, tp, tp, sink], out_specs=o_spec,
                  out_shape=jax.ShapeDtypeStruct((hq, t, 2 * half), f32),
                  compiler_params=_params(("arbitrary",)))(qa, qb, ka, ka, kb, kb, v, v, cos, sin, cos, sin, sinks)


def _swa_bwd(qa, qb, ka, kb, v, cos, sin, sinks, do, comm=None):
    hq, t, half = qa.shape
    kv = ka.shape[0]
    g, blk = hq // kv, SWA_BLOCK
    q_spec, kc, kp, vc, vp, tc, tp, sink, o_spec = _swa_specs(hq, kv, blk, half)

    def body(qa_r, qb_r, kap, kac, kbp, kbc, vp_r, vc_r, cq, sq, cp, sp, sink_r, do_r,
             dqa, dqb, dkap, dkac, dkbp, dkbc, dvp, dvc, dsink):
        n = pl.program_id(0)
        tabs = (cq[...], sq[...], cp[...], sp[...])
        fn = lambda a, b, c_, d, e, f_, g_, h_, s_: _swa_math(n, a, b, c_, d, e, f_, g_, h_, *tabs, s_)
        dsinks = []
        for h in range(kv):
            qs, ks = pl.ds(h * g, g), pl.ds(h, 1)
            _, vjp = jax.vjp(fn, qa_r[qs], qb_r[qs], kap[ks], kac[ks], kbp[ks], kbc[ks], vp_r[ks], vc_r[ks], sink_r[ks])
            grads = vjp(do_r[qs])
            dqa[qs] = grads[0]
            dqb[qs] = grads[1]
            for ref, val in zip((dkap, dkac, dkbp, dkbc, dvp, dvc), grads[2:8]):
                ref[ks] = val
            dsinks.append(grads[8])
        dsink_all = jnp.concatenate(dsinks, axis=0)

        @pl.when(n == 0)
        def _():
            dsink[...] = dsink_all

        @pl.when(n > 0)
        def _():
            dsink[...] += dsink_all

    sh = lambda a: jax.ShapeDtypeStruct(a.shape, f32)
    return _pcall(body, comm=comm, name="swa_bwd", grid=(t // blk,),
                  in_specs=[q_spec, q_spec, kp, kc, kp, kc, vp, vc, tc, tc, tp, tp, sink, o_spec],
                  out_specs=[q_spec, q_spec, kc, kc, kc, kc, vc, vc, sink],
                  out_shape=[sh(qa), sh(qb), sh(ka), sh(ka), sh(kb), sh(kb), sh(v), sh(v), sh(sinks)],
                  compiler_params=_params(("arbitrary",)))(qa, qb, ka, ka, kb, kb, v, v, cos, sin, cos, sin, sinks, do)


def _swa_split(za, hq, kv):
    t = za.shape[0]
    half = HEAD_DIM // 2
    tm = _row_tile(t, za.shape[1])

    def body(z_r, qa, qb, ka, kb, v):
        z = z_r[...]
        for h in range(hq):
            qa[h] = z[:, HEAD_DIM * h:HEAD_DIM * h + half]
            qb[h] = z[:, HEAD_DIM * h + half:HEAD_DIM * (h + 1)]
        for h in range(kv):
            o = HEAD_DIM * (hq + h)
            ka[h] = z[:, o:o + half]
            kb[h] = z[:, o + half:o + HEAD_DIM]
            o = HEAD_DIM * (hq + kv + h)
            v[h] = z[:, o:o + HEAD_DIM]

    spec = lambda n, w: pl.BlockSpec((n, tm, w), lambda i: (0, i, 0))
    sh = lambda n, w: jax.ShapeDtypeStruct((n, t, w), f32)
    return _pcall(body, name="swa_split", grid=(t // tm,), in_specs=[pl.BlockSpec((tm, za.shape[1]), lambda i: (i, 0))],
                  out_specs=[spec(hq, half), spec(hq, half), spec(kv, half), spec(kv, half), spec(kv, HEAD_DIM)],
                  out_shape=[sh(hq, half), sh(hq, half), sh(kv, half), sh(kv, half), sh(kv, HEAD_DIM)],
                  compiler_params=_params(("parallel",)))(za)


def _swa_merge(dqa, dqb, dkac, dkap, dkbc, dkbp, dvc, dvp):
    hq, t, half = dqa.shape
    kv = dkac.shape[0]
    blk = SWA_BLOCK
    nb = t // blk
    cols = HEAD_DIM * (hq + 2 * kv)

    def body(qa, qb, kac, kap, kbc, kbp, vc, vp, z_o, s_o):
        i = pl.program_id(0)
        more = (i < nb - 1).astype(f32)
        pieces = []
        for h in range(hq):
            pieces += [qa[h], qb[h]]
        for h in range(kv):
            pieces += [kac[h] + more * kap[h], kbc[h] + more * kbp[h]]
        for h in range(kv):
            pieces.append(vc[h] + more * vp[h])
        z = jnp.concatenate(pieces, axis=-1)
        z_o[...] = z
        colsum = jnp.sum(z, axis=0, keepdims=True)

        @pl.when(i == 0)
        def _():
            s_o[...] = colsum

        @pl.when(i > 0)
        def _():
            s_o[...] += colsum

    cur = lambda n, w: pl.BlockSpec((n, blk, w), lambda i: (0, i, 0))
    nxt = lambda n, w: pl.BlockSpec((n, blk, w), lambda i: (0, jnp.minimum(i + 1, nb - 1), 0))
    return _pcall(body, name="swa_merge", grid=(nb,),
                  in_specs=[cur(hq, half), cur(hq, half), cur(kv, half), nxt(kv, half), cur(kv, half), nxt(kv, half),
                            cur(kv, HEAD_DIM), nxt(kv, HEAD_DIM)],
                  out_specs=[pl.BlockSpec((blk, cols), lambda i: (i, 0)), pl.BlockSpec((1, cols), lambda i: (0, 0))],
                  out_shape=[jax.ShapeDtypeStruct((t, cols), f32), jax.ShapeDtypeStruct((1, cols), f32)],
                  compiler_params=_params(("arbitrary",)))(dqa, dqb, dkac, dkap, dkbc, dkbp, dvc, dvp)


def _xattn_math(q, k, v):
    s = _dot(q, k, "nt") * (q.shape[-1] ** -0.5)
    e = jnp.exp(s - jnp.max(s, axis=-1, keepdims=True))
    p = e / jnp.sum(e, axis=-1, keepdims=True)
    return _dot(p, v, "nn")


def _xattn_fwd(q, kvm):
    t, d = q.shape
    mlen = kvm.shape[0]
    hd = d // XATTN_HEADS
    tq = min(512, t)

    def body(q_r, k_r, v_r, o_r):
        o_r[...] = _xattn_math(q_r[...], k_r[...], v_r[...]).astype(o_r.dtype)

    return _pcall(body, name="xattn_fwd", grid=(XATTN_HEADS, t // tq),
                  in_specs=[pl.BlockSpec((tq, hd), lambda h, i: (i, h)), pl.BlockSpec((mlen, hd), lambda h, i: (0, h)),
                            pl.BlockSpec((mlen, hd), lambda h, i: (0, XATTN_HEADS + h))],
                  out_specs=pl.BlockSpec((tq, hd), lambda h, i: (i, h)), out_shape=jax.ShapeDtypeStruct((t, d), bf16),
                  compiler_params=_params(("parallel", "parallel")))(q, kvm, kvm)


def _xattn_bwd(q, kvm, do):
    t, d = q.shape
    mlen = kvm.shape[0]
    hd = d // XATTN_HEADS
    tq = min(512, t)

    def body(q_r, k_r, v_r, do_r, dq, dk, dv):
        _, vjp = jax.vjp(_xattn_math, q_r[...].astype(f32), k_r[...].astype(f32), v_r[...].astype(f32))
        gq, gk, gv = vjp(do_r[...].astype(f32))
        dq[...] = gq.astype(dq.dtype)
        first = pl.program_id(1) == 0

        @pl.when(first)
        def _():
            dk[...] = gk
            dv[...] = gv

        @pl.when(jnp.logical_not(first))
        def _():
            dk[...] += gk
            dv[...] += gv

    qs = pl.BlockSpec((tq, hd), lambda h, i: (i, h))
    ms = pl.BlockSpec((mlen, hd), lambda h, i: (0, h))
    return _pcall(body, name="xattn_bwd", grid=(XATTN_HEADS, t // tq),
                  in_specs=[qs, ms, pl.BlockSpec((mlen, hd), lambda h, i: (0, XATTN_HEADS + h)), qs],
                  out_specs=[qs, ms, ms],
                  out_shape=[jax.ShapeDtypeStruct((t, d), bf16), jax.ShapeDtypeStruct((mlen, d), f32),
                             jax.ShapeDtypeStruct((mlen, d), f32)],
                  compiler_params=_params(("parallel", "arbitrary")))(q, kvm, kvm, do)


def _chunk_cumsum(lw, reverse=False):
    h, l, _ = lw.shape
    i = lax.broadcasted_iota(jnp.int32, (l, l), 0)
    j = lax.broadcasted_iota(jnp.int32, (l, l), 1)
    tri = jnp.broadcast_to(((i <= j) if reverse else (i >= j)).astype(bf16)[None], (h, l, l))
    out = jnp.zeros(lw.shape, f32)
    for piece in _split3(lw):
        out = out + lax.dot_general(tri, piece, _dims("nn", 3), preferred_element_type=f32)
    return out


def _rwkv_chunk(s0, r, k, v, a, lw, cl, k_k, k_a, r_k, ln_w, ln_b):
    l = r.shape[1]
    kk = k * k_k
    kk = kk / jnp.maximum(jnp.sqrt(jnp.sum(kk * kk, axis=-1, keepdims=True)), 1e-12)
    km = k * (1.0 + (a - 1.0) * k_a)
    av, bv = -kk, kk * a
    p_incl, p_excl, p_inv = jnp.exp(cl), jnp.exp(cl - lw), jnp.exp(-cl)
    at, bh, kh, rt = av * p_excl, bv * p_inv, km * p_inv, r * p_incl
    i = lax.broadcasted_iota(jnp.int32, (l, l), 0)
    j = lax.broadcasted_iota(jnp.int32, (l, l), 1)
    strict, incl = (i > j)[None], (i >= j)[None]
    a_ab = jnp.where(strict, _dot(at, bh, "nt"), 0.0)
    a_ak = jnp.where(strict, _dot(at, kh, "nt"), 0.0)
    a_rb = jnp.where(incl, _dot(rt, bh, "nt"), 0.0)
    a_rk = jnp.where(incl, _dot(rt, kh, "nt"), 0.0)
    rhs = _dot(at, s0, "nt") + _dot(a_ak, v, "nn")
    inv = a_ab + (i == j)[None].astype(f32)
    pw = a_ab
    for _ in range(int(math.log2(l)) - 1):
        pw = _dot(pw, pw, "nn")
        inv = inv + _dot(inv, pw, "nn")
    sa = _dot(inv, rhs, "nn")
    y = _dot(rt, s0, "nt") + _dot(a_rk, v, "nn") + _dot(a_rb, sa, "nn")
    p_last = p_incl[:, l - 1:l, :]
    s_end = s0 * p_last + _dot(v, kh * p_last, "tn") + _dot(sa, bh * p_last, "tn")
    mu = jnp.mean(y, axis=-1, keepdims=True)
    var = jnp.mean(jnp.square(y - mu), axis=-1, keepdims=True)
    out = (y - mu) * lax.rsqrt(var + GN_EPS) * ln_w + ln_b
    out = out + jnp.sum(r * km * r_k, axis=-1, keepdims=True) * v
    return out, s_end


def _rwkv_fwd(r, k, v, a, lw, heads, comm=None):
    h, t, n = r.shape
    l = min(RWKV_CHUNK, t)
    nc = t // l
    seq = pl.BlockSpec((h, l, n), lambda c: (0, c, 0))
    par = pl.BlockSpec((h, 1, n), lambda c: (0, 0, 0))

    def body(r_r, k_r, v_r, a_r, lw_r, p0, p1, p2, p3, p4, y_r, ck_r, s_scr):
        @pl.when(pl.program_id(0) == 0)
        def _():
            s_scr[...] = jnp.zeros_like(s_scr)

        s0 = s_scr[...]
        ck_r[0] = s0
        lw_v = lw_r[...]
        out, s_end = _rwkv_chunk(s0, r_r[...], k_r[...], v_r[...], a_r[...], lw_v, _chunk_cumsum(lw_v),
                                 p0[...], p1[...], p2[...], p3[...], p4[...])
        y_r[...] = out
        s_scr[...] = s_end

    return _pcall(body, comm=comm, name="rwkv_fwd", grid=(nc,), in_specs=[seq] * 5 + [par] * 5,
                  out_specs=[seq, pl.BlockSpec((1, h, n, n), lambda c: (c, 0, 0, 0))],
                  out_shape=[jax.ShapeDtypeStruct((h, t, n), f32), jax.ShapeDtypeStruct((nc, h, n, n), f32)],
                  scratch_shapes=[pltpu.VMEM((h, n, n), f32)],
                  compiler_params=_params(("arbitrary",)))(r, k, v, a, lw, *heads)


def _rwkv_bwd(r, k, v, a, lw, heads, ck, dy, comm=None):
    h, t, n = r.shape
    l = min(RWKV_CHUNK, t)
    nc = t // l
    seq = pl.BlockSpec((h, l, n), lambda c: (0, nc - 1 - c, 0))
    par = pl.BlockSpec((h, 1, n), lambda c: (0, 0, 0))

    def body(r_r, k_r, v_r, a_r, lw_r, p0, p1, p2, p3, p4, ck_r, dy_r,
             dr, dk, dv, da, dlw, g0, g1, g2, g3, g4, ds_scr):
        first = pl.program_id(0) == 0

        @pl.when(first)
        def _():
            ds_scr[...] = jnp.zeros_like(ds_scr)

        lw_v = lw_r[...]
        _, vjp = jax.vjp(_rwkv_chunk, ck_r[0], r_r[...], k_r[...], v_r[...], a_r[...], lw_v, _chunk_cumsum(lw_v),
                         p0[...], p1[...], p2[...], p3[...], p4[...])
        grads = vjp((dy_r[...], ds_scr[...]))
        ds_scr[...] = grads[0]
        dr[...] = grads[1]
        dk[...] = grads[2]
        dv[...] = grads[3]
        da[...] = grads[4]
        dlw[...] = grads[5] + _chunk_cumsum(grads[6], reverse=True)
        acc = (g0, g1, g2, g3, g4)

        @pl.when(first)
        def _():
            for ref, val in zip(acc, grads[7:]):
                ref[...] = val

        @pl.when(jnp.logical_not(first))
        def _():
            for ref, val in zip(acc, grads[7:]):
                ref[...] += val

    seq_sh = jax.ShapeDtypeStruct((h, t, n), f32)
    par_sh = jax.ShapeDtypeStruct((h, 1, n), f32)
    return _pcall(body, comm=comm, name="rwkv_bwd", grid=(nc,),
                  in_specs=[seq] * 5 + [par] * 5 + [pl.BlockSpec((1, h, n, n), lambda c: (nc - 1 - c, 0, 0, 0)), seq],
                  out_specs=[seq] * 5 + [par] * 5, out_shape=[seq_sh] * 5 + [par_sh] * 5,
                  scratch_shapes=[pltpu.VMEM((h, n, n), f32)],
                  compiler_params=_params(("arbitrary",)))(r, k, v, a, lw, *heads, ck, dy)


def _rwkv_pre_math(c, lp, p_rkv, p_rkv_prev, p_l, p_l_prev, mu_rkv, mu_l, w0, a0, decay_up, aaa_up, gate_up):
    dlp, alp, _ = lp
    z = p_rkv + (p_rkv_prev - p_rkv) * mu_rkv
    zl = p_l + (p_l_prev - p_l) * mu_l
    r, k, v = z[:, :c], z[:, c:2 * c], z[:, 2 * c:]
    wd, ad, gd = zl[:, :dlp], zl[:, dlp:dlp + alp], zl[:, dlp + alp:]
    w = -_softplus(-(w0 + _dot(jnp.tanh(wd), decay_up, "nn"))) - 0.5
    a = _sigmoid(a0 + _dot(ad, aaa_up, "nn"))
    g = _dot(_sigmoid(gd), gate_up, "nn")
    return r, k, v, -jnp.exp(w), a, g


def _pad_to(a, n, axis):
    if a.shape[axis] == n:
        return a
    pad = [(0, 0)] * a.ndim
    pad[axis] = (0, n - a.shape[axis])
    return jnp.pad(a, pad)


def _up128(n):
    return -(-n // LANE) * LANE


def _shift_down(p):
    return jnp.concatenate([jnp.zeros((1, p.shape[1]), p.dtype), p[:-1]], axis=0)


def _shift_up(p):
    return jnp.concatenate([p[1:], jnp.zeros((1, p.shape[1]), p.dtype)], axis=0)


def _swiglu(g, u):
    return jax.nn.silu(g) * u


def _ffn_hidden(name, h, w_gate, w_up, comm=None):
    t, d = h.shape
    nb, _, n = w_gate.shape
    tm = _pick(t, (1024, 512, 256, 128))

    def body(h_r, wg_r, wu_r, g_o, u_o, a_o):
        hv = h_r[...]
        g = lax.dot_general(hv, wg_r[0], _dims("nn", 2), preferred_element_type=f32)
        u = lax.dot_general(hv, wu_r[0], _dims("nn", 2), preferred_element_type=f32)
        g_o[0] = g.astype(bf16)
        u_o[0] = u.astype(bf16)
        a_o[0] = _swiglu(g, u).astype(bf16)

    w_spec = pl.BlockSpec((1, d, n), lambda i, j: (j, 0, 0))
    o_spec = pl.BlockSpec((1, tm, n), lambda i, j: (j, i, 0))
    sh = jax.ShapeDtypeStruct((nb, t, n), bf16)
    return _pcall(body, comm=comm, name=name, grid=(t // tm, nb),
                  in_specs=[pl.BlockSpec((tm, d), lambda i, j: (i, 0)), w_spec, w_spec],
                  out_specs=[o_spec, o_spec, o_spec], out_shape=[sh, sh, sh],
                  compiler_params=_params(("parallel", "arbitrary")))(h, w_gate, w_up)


def _ffn_out(name, act, w_down, x, comm=None):
    nb, t, n = act.shape
    d = w_down.shape[2]
    tm, tn = _pick(t, (512, 256, 128)), _pick(d, (512, 256, 128))

    def body(a_r, w_r, x_r, o_r):
        acc = x_r[...]
        for j in range(nb):
            acc = acc + 0.5 * lax.dot_general(a_r[j], w_r[j], _dims("nn", 2), preferred_element_type=f32)
        o_r[...] = acc

    return _pcall(body, comm=comm, name=name, grid=(t // tm, d // tn),
                  in_specs=[pl.BlockSpec((nb, tm, n), lambda i, j: (0, i, 0)), pl.BlockSpec((nb, n, tn), lambda i, j: (0, 0, j)),
                            pl.BlockSpec((tm, tn), lambda i, j: (i, j))],
                  out_specs=pl.BlockSpec((tm, tn), lambda i, j: (i, j)), out_shape=jax.ShapeDtypeStruct((t, d), f32),
                  compiler_params=_params(("parallel", "parallel")))(act, w_down, x)


def _ffn_dhidden(name, dout, w_down, gate, up, comm=None):
    t, d = dout.shape
    nb, n, _ = w_down.shape
    tm = _pick(t, (512, 256, 128))

    def body(d_r, w_r, g_r, u_r, dg_o, du_o):
        dact = 0.5 * lax.dot_general(d_r[...].astype(MXU_DTYPE), w_r[0], _dims("nt", 2), preferred_element_type=f32)
        _, vjp = jax.vjp(_swiglu, g_r[0].astype(f32), u_r[0].astype(f32))
        dg, du = vjp(dact)
        dg_o[0] = dg.astype(bf16)
        du_o[0] = du.astype(bf16)

    o_spec = pl.BlockSpec((1, tm, n), lambda i, j: (j, i, 0))
    sh = jax.ShapeDtypeStruct((nb, t, n), bf16)
    return _pcall(body, comm=comm, name=name, grid=(t // tm, nb),
                  in_specs=[pl.BlockSpec((tm, d), lambda i, j: (i, 0)), pl.BlockSpec((1, n, d), lambda i, j: (j, 0, 0)), o_spec, o_spec],
                  out_specs=[o_spec, o_spec], out_shape=[sh, sh],
                  compiler_params=_params(("parallel", "arbitrary")))(dout, w_down, gate, up)


def _ffn_dw_down(name, act, dout, comm=None):
    nb, t, n = act.shape
    d = dout.shape[1]
    tn = _pick(d, (1024, 512, 256, 128))

    def body(a_r, d_r, o_r):
        acc = lax.dot_general(a_r[0], d_r[...].astype(MXU_DTYPE), _dims("tn", 2), preferred_element_type=f32)
        o_r[0] = (0.5 * acc).astype(bf16)

    return _pcall(body, comm=comm, name=name, grid=(nb, d // tn),
                  in_specs=[pl.BlockSpec((1, t, n), lambda j, i: (j, 0, 0)), pl.BlockSpec((t, tn), lambda j, i: (0, i))],
                  out_specs=pl.BlockSpec((1, n, tn), lambda j, i: (j, 0, i)), out_shape=jax.ShapeDtypeStruct((nb, n, d), bf16),
                  compiler_params=_params(("parallel", "parallel")))(act, dout)


def _ffn_dw_hidden(name, h, dgate, dup, comm=None):
    t, d = h.shape
    nb, _, n = dgate.shape
    tm = _pick(d, (1024, 512, 256, 128))

    def body(h_r, g_r, u_r, dg_o, du_o):
        hv = h_r[...]
        dg_o[0] = lax.dot_general(hv, g_r[0], _dims("tn", 2), preferred_element_type=f32).astype(bf16)
        du_o[0] = lax.dot_general(hv, u_r[0], _dims("tn", 2), preferred_element_type=f32).astype(bf16)

    g_spec = pl.BlockSpec((1, t, n), lambda j, i: (j, 0, 0))
    o_spec = pl.BlockSpec((1, tm, n), lambda j, i: (j, i, 0))
    sh = jax.ShapeDtypeStruct((nb, d, n), bf16)
    return _pcall(body, comm=comm, name=name, grid=(nb, d // tm),
                  in_specs=[pl.BlockSpec((t, tm), lambda j, i: (0, i)), g_spec, g_spec],
                  out_specs=[o_spec, o_spec], out_shape=[sh, sh],
                  compiler_params=_params(("parallel", "parallel")))(h, dgate, dup)


def _ffn_dh(name, dhid, w, res=None, comm=None):
    nb, t, n = dhid.shape
    d = w.shape[1]
    tm, tn = _pick(t, (512, 256, 128)), _pick(d, (512, 256, 128))

    def body(*refs):
        acc = refs[2][...] if res is not None else jnp.zeros((tm, tn), f32)
        for j in range(nb):
            acc = acc + lax.dot_general(refs[0][j], refs[1][j], _dims("nt", 2), preferred_element_type=f32)
        refs[-1][...] = acc

    in_specs = [pl.BlockSpec((nb, tm, n), lambda i, j: (0, i, 0)), pl.BlockSpec((nb, tn, n), lambda i, j: (0, j, 0))]
    args = [dhid, w]
    if res is not None:
        in_specs.append(pl.BlockSpec((tm, tn), lambda i, j: (i, j)))
        args.append(res)
    return _pcall(body, comm=comm, name=name, grid=(t // tm, d // tn), in_specs=in_specs,
                  out_specs=pl.BlockSpec((tm, tn), lambda i, j: (i, j)), out_shape=jax.ShapeDtypeStruct((t, d), f32),
                  compiler_params=_params(("parallel", "parallel")))(*args)


def _lora_bounds(c, lora):
    dl, al, gl = lora
    o1 = 3 * c
    o2, o3 = o1 + dl, o1 + dl + al
    return o1, o2, o3, o3 + gl, (_up128(dl), _up128(al), _up128(gl))


def _win_split(g8, c, lora):
    nb, d, n = g8.shape
    o1, o2, o3, o4, (dlp, alp, glp) = _lora_bounds(c, lora)
    tm = _row_tile(d, nb * n, g8.dtype.itemsize)

    def body(x, rkv_o, lora_o, swa_o):
        w = jnp.concatenate([x[j] for j in range(nb)], axis=-1)
        pad = lambda p, m: p if p.shape[1] == m else jnp.concatenate([p, jnp.zeros((p.shape[0], m - p.shape[1]), p.dtype)], axis=-1)
        rkv_o[...] = w[:, :o1]
        lora_o[...] = jnp.concatenate([pad(w[:, o1:o2], dlp), pad(w[:, o2:o3], alp), pad(w[:, o3:o4], glp)], axis=-1)
        swa_o[...] = w[:, o4:]

    widths = (o1, dlp + alp + glp, nb * n - o4)
    return _pcall(body, name="w_in_split", grid=(d // tm,), in_specs=[pl.BlockSpec((nb, tm, n), lambda i: (0, i, 0))],
                  out_specs=[pl.BlockSpec((tm, wd), lambda i: (i, 0)) for wd in widths],
                  out_shape=[jax.ShapeDtypeStruct((d, wd), g8.dtype) for wd in widths],
                  compiler_params=_params(("parallel",)))(g8)


def _win_merge(dw_rkv, dw_lora, dw_swa, c, lora, nb):
    d = dw_rkv.shape[0]
    o1, o2, o3, o4, (dlp, alp, glp) = _lora_bounds(c, lora)
    dl, al, gl = lora
    total = o4 + dw_swa.shape[1]
    n = total // nb
    tm = _row_tile(d, total, dw_rkv.dtype.itemsize)

    def body(a, b, s, o):
        bv = b[...]
        w = jnp.concatenate([a[...], bv[:, :dl], bv[:, dlp:dlp + al], bv[:, dlp + alp:dlp + alp + gl], s[...]], axis=-1)
        for j in range(nb):
            o[j] = w[:, n * j:n * (j + 1)]

    ins = [dw_rkv, dw_lora, dw_swa]
    return _pcall(body, name="w_in_merge", grid=(d // tm,), in_specs=[pl.BlockSpec((tm, a.shape[1]), lambda i: (i, 0)) for a in ins],
                  out_specs=pl.BlockSpec((nb, tm, n), lambda i: (0, i, 0)), out_shape=jax.ShapeDtypeStruct((nb, d, n), dw_rkv.dtype),
                  compiler_params=_params(("parallel",)))(*ins)


def _norm_bwd(name, x, g_norm, dh, dres, comm=None):
    d = x.shape[1]

    def fn(xb, dhb, drb, g):
        _, vjp = jax.vjp(_rms, xb, g)
        dx, dg = vjp(dhb)
        return drb + dx, dg

    return _rows(name, fn, [x, dh, dres], [g_norm], [(d, f32)], [((1, d), f32)], comm=comm)


def _colsum(name, a):
    return _rows(name, lambda ab: (jnp.sum(ab.astype(f32), axis=0, keepdims=True),), [a], [], [], [((1, a.shape[1]), f32)])[0]


def kernel(x, mem, f1_norm, f1_gate, f1_up, f1_down, mix_norm, w_in, b_in_attn, rw_mu, rw_w0, rw_decay_up, rw_a0, rw_aaa_up, rw_gate_up, rw_k_k, rw_k_a, rw_r_k, rw_lnx_w, rw_lnx_b, attn_sinks, w_out, b_out, xa_norm, mem_norm, w_xq, w_xkv, w_xo, f2_norm, f2_gate, f2_up, f2_down, final_norm, loss_target, m_f1_norm, m_f1_gate, m_f1_up, m_f1_down, m_mix_norm, m_w_in, m_b_in_attn, m_rw_mu, m_rw_w0, m_rw_decay_up, m_rw_a0, m_rw_aaa_up, m_rw_gate_up, m_rw_k_k, m_rw_k_a, m_rw_r_k, m_rw_lnx_w, m_rw_lnx_b, m_attn_sinks, m_w_out, m_b_out, m_xa_norm, m_mem_norm, m_w_xq, m_w_xkv, m_w_xo, m_f2_norm, m_f2_gate, m_f2_up, m_f2_down, m_final_norm, v_f1_norm, v_f1_gate, v_f1_up, v_f1_down, v_mix_norm, v_w_in, v_b_in_attn, v_rw_mu, v_rw_w0, v_rw_decay_up, v_rw_a0, v_rw_aaa_up, v_rw_gate_up, v_rw_k_k, v_rw_k_a, v_rw_r_k, v_rw_lnx_w, v_rw_lnx_b, v_attn_sinks, v_w_out, v_b_out, v_xa_norm, v_mem_norm, v_w_xq, v_w_xkv, v_w_xo, v_f2_norm, v_f2_gate, v_f2_up, v_f2_down, v_final_norm):
    names = ["f1_norm", "f1_gate", "f1_up", "f1_down", "mix_norm", "w_in", "b_in_attn", "rw_mu", "rw_w0", "rw_decay_up",
             "rw_a0", "rw_aaa_up", "rw_gate_up", "rw_k_k", "rw_k_a", "rw_r_k", "rw_lnx_w", "rw_lnx_b", "attn_sinks", "w_out",
             "b_out", "xa_norm", "mem_norm", "w_xq", "w_xkv", "w_xo", "f2_norm", "f2_gate", "f2_up", "f2_down", "final_norm"]
    env = dict(locals())
    w_of = {k: env[k] for k in names}
    m_of = {k: env["m_" + k] for k in names}
    v_of = {k: env["v_" + k] for k in names}
    col_sharded = ["f1_gate", "f1_up", "w_in", "rw_decay_up", "rw_aaa_up", "rw_gate_up", "w_xkv", "f2_gate", "f2_up"]
    row_sharded = ["f1_down", "w_out", "w_xq", "w_xo", "f2_down"]
    sharded = col_sharded + row_sharded
    small = [k for k in names if k not in sharded]

    x0, mem0, tgt = x[0], mem[0], loss_target[0]
    t, d = x0.shape
    c = rw_w0.shape[-1]
    heads = c // HEAD_DIM
    dl, al, gl = rw_decay_up.shape[1], rw_aaa_up.shape[1], rw_gate_up.shape[1]
    dlp, alp, glp = _up128(dl), _up128(al), _up128(gl)
    swa_w = d - c
    hq, kvh = swa_w // HEAD_DIM, (b_in_attn.shape[-1] - swa_w) // (2 * HEAD_DIM)
    my_x, my_y, my_c = _position()
    c_idx = jnp.reshape(my_c, (1,)).astype(jnp.int32)
    chip_idx = jnp.reshape(2 * my_x + my_y, (1,)).astype(jnp.int32)

    shard2d = {k: w_of[k][0] for k in sharded}
    cast = {k: _rows("cast_" + k, lambda a: (a,), [shard2d[k]], [], [(shard2d[k].shape[1], bf16)], tm=_row_tile(*shard2d[k].shape))[0]
            for k in sharded}
    ffn1_keys, ffn2_keys = ["f1_gate", "f1_up", "f1_down"], ["f2_gate", "f2_up", "f2_down"]
    in_keys = ["w_in", "rw_decay_up", "rw_aaa_up", "rw_gate_up"]
    kept_in_blocks = ffn1_keys + ffn2_keys + ["w_in", "w_xkv"]

    def whole(k, g8):
        if k in kept_in_blocks:
            return g8
        if k in col_sharded:
            return g8.transpose(1, 0, 2).reshape(g8.shape[1], N_DEV * g8.shape[2])
        return g8.reshape(N_DEV * g8.shape[1], g8.shape[2])

    def gather_of(keys):
        return _gather_comm([cast[k] for k in keys])

    def wholes(keys, gathered):
        return {k: whole(k, g8) for k, g8 in zip(keys, gathered)}

    (h1,), gathered = _rows("f1_norm", lambda xb, g: (_rms(xb, g),), [x0], [f1_norm], [(d, bf16)], comm=gather_of(ffn1_keys[:2]))
    full = wholes(ffn1_keys[:2], gathered)
    (gate1, up1, act1), gathered = _ffn_hidden("f1_hidden", h1, full["f1_gate"], full["f1_up"], comm=gather_of(["f1_down"]))
    full.update(wholes(["f1_down"], gathered))
    x1, gathered = _ffn_out("f1_out", act1, full["f1_down"], x0, comm=gather_of(in_keys))
    full.update(wholes(in_keys, gathered))
    ffn1_saved = (h1, gate1, up1, act1)
    w_rkv, w_lora, w_swa = _win_split(full["w_in"], c, (dl, al, gl))
    o1, o2, o3, shift_cols, _ = _lora_bounds(c, (dl, al, gl))
    mu_rkv = rw_mu[:, :3 * c]
    mu_l = jnp.concatenate([_pad_to(rw_mu[:, o1:o2], dlp, 1), _pad_to(rw_mu[:, o2:o3], alp, 1),
                            _pad_to(rw_mu[:, o3:shift_cols], glp, 1)], axis=1)
    decay_up = _pad_to(full["rw_decay_up"], dlp, 0).astype(f32)
    aaa_up = _pad_to(full["rw_aaa_up"], alp, 0).astype(f32)
    gate_up = _pad_to(full["rw_gate_up"], glp, 0).astype(f32)
    head_pars = [p.reshape(heads, 1, HEAD_DIM) for p in (rw_k_k, rw_k_a, rw_r_k, rw_lnx_w, rw_lnx_b)]
    final_g = final_norm.reshape(1, d)

    (h2,) = _rows("mix_norm", lambda xb, g: (_rms(xb, g),), [x1], [mix_norm], [(d, bf16)])
    p_rkv, gathered = _mm("in_rkv", h2, w_rkv, "nn", f32, comm=gather_of(["w_out"]))
    full.update(wholes(["w_out"], gathered))
    p_l =_mm("in_lora", h2, w_lora, "nn", f32)
    za = _mm("in_swa", h2, w_swa, "nn", f32, bias=b_in_attn)
    pre_fn = functools.partial(_rwkv_pre_math, c, (dlp, alp, glp))
    pre_rows = [p_rkv, _shift_down(p_rkv), p_l, _shift_down(p_l)]
    pre_full = [mu_rkv, mu_l, rw_w0, rw_a0, decay_up, aaa_up, gate_up]
    (r_h, k_h, v_h, lw_h, a_h, g_t), gathered = _rows("rwkv_pre", pre_fn, pre_rows, pre_full,
                                                       [(c, f32, HEAD_DIM)] * 5 + [(c, f32)], tm=128, comm=gather_of(["w_xq"]))
    full.update(wholes(["w_xq"], gathered))
    seqs = [r_h, k_h, v_h, a_h, lw_h]
    (y_heads, checkpoints), gathered = _rwkv_fwd(*seqs, head_pars, comm=gather_of(["f2_gate"]))
    full.update(wholes(["f2_gate"], gathered))

    pos = jnp.arange(t, dtype=f32)
    inv_freq = ROPE_THETA ** (-jnp.arange(0, HEAD_DIM, 2, dtype=f32) / HEAD_DIM)
    ang = pos[:, None] * inv_freq[None, :]
    cos, sin = jnp.cos(ang), jnp.sin(ang)
    sinks3 = attn_sinks.reshape(kvh, hq // kvh, 1)
    swa_in = (*_swa_split(za, hq, kvh), cos, sin, sinks3)
    y_swa_heads, gathered = _swa_fwd(*swa_in, comm=gather_of(["f2_up"]))
    full.update(wholes(["f2_up"], gathered))
    (ycat,) = _rows("mix_cat", lambda yb, gb, sb: (jnp.concatenate([yb * gb, sb], axis=1),), [y_heads, g_t, y_swa_heads], [],
                    [(d, bf16)])
    x2, gathered = _mm("mix_out", ycat, full["w_out"], "nn", f32, res=x1, bias=b_out, comm=gather_of(["w_xkv"]))
    full.update(wholes(["w_xkv"], gathered))

    (h3,) = _rows("xa_norm", lambda xb, g: (_rms(xb, g),), [x2], [xa_norm], [(d, bf16)])
    (mem_n,) = _rows("mem_norm", lambda xb, g: (_rms(xb, g),), [mem0], [mem_norm], [(d, bf16)])
    q_x, gathered = _mm("xa_q", h3, full["w_xq"], "nn", bf16, comm=gather_of(["w_xo"]))
    full.update(wholes(["w_xo"], gathered))
    kv_x = _mm("xa_kv", mem_n, full["w_xkv"], "nn", bf16)
    o_x = _xattn_fwd(q_x, kv_x)
    x3 = _mm("xa_out", o_x, full["w_xo"], "nn", f32, res=x2)
    (h4,) = _rows("f2_norm", lambda xb, g: (_rms(xb, g),), [x3], [f2_norm], [(d, bf16)])
    (gate2, up2, act2), gathered = _ffn_hidden("f2_hidden", h4, full["f2_gate"], full["f2_up"], comm=gather_of(["f2_down"]))
    full.update(wholes(["f2_down"], gathered))
    x4 = _ffn_out("f2_out", act2, full["f2_down"], x3)
    ffn2_saved = (h4, gate2, up2, act2)

    def loss_fn(xb, tb, g):
        def per_row(xv, gv):
            return 0.5 * jnp.mean(jnp.square(_rms(xv, gv) - tb), axis=-1, keepdims=True)

        lrow, vjp = jax.vjp(per_row, xb, g)
        dxb, dgb = vjp(jnp.ones_like(lrow))
        return dxb, dgb, jnp.sum(lrow, axis=0, keepdims=True)

    dx4, d_final, loss_part = _rows("loss", loss_fn, [x4, tgt], [final_g], [(d, f32)], [((1, d), f32), ((1, 1), f32)])
    loss = lax.psum(loss_part[0, 0], ("x", "y", "c"))

    grads, small_g, out = {}, {"final_norm": d_final}, {}

    def pair_sums_of(tag, keys, carrier=None):
        blocks = []
        for k in keys:
            g2 = grads[k]
            rr, cc = shard2d[k].shape
            if k in kept_in_blocks:
                blocks.append(g2)
            else:
                blocks.append(g2.reshape(g2.shape[0], N_DEV, cc).transpose(1, 0, 2) if k in col_sharded else g2.reshape(N_DEV, rr, cc))
        if carrier is None:
            from_sibling = _comm_only("grads_to_sibling_" + tag, _sibling_comm(blocks))
        else:
            carried, from_sibling = carrier(_sibling_comm(blocks))
        pairs = [_pair_add("pair_add_" + k, b, o, c_idx) for k, b, o in zip(keys, blocks, from_sibling)]
        return pairs if carrier is None else (pairs, carried)

    def update(keys, pair_sums, from_chips):
        for k, part, others in zip(keys, pair_sums, from_chips):
            res = _adam_sharded("adam_" + k, shard2d[k], m_of[k][0], v_of[k][0], part, others, chip_idx)
            out[k] = [a.reshape(w_of[k].shape) for a in res]

    def ffn_backward(tag, keys, xin, g_norm, saved, dout, first_comm, start_exchange):
        h, gate, up, act = saved
        k_gate, k_up, k_down = keys
        if first_comm is None:
            grads[k_down], carried = _ffn_dw_down(tag + "_dw_down", act, dout), None
        else:
            grads[k_down], carried = _ffn_dw_down(tag + "_dw_down", act, dout, comm=first_comm)
        down_pairs, (dgate, dup) = pair_sums_of(
            k_down, [k_down], lambda cm: _ffn_dhidden(tag + "_dhidden", dout, full[k_down], gate, up, comm=cm))
        (grads[k_gate], grads[k_up]), from_chips = _ffn_dw_hidden(tag + "_dw_hidden", h, dgate, dup, comm=_chips_comm(down_pairs))
        update([k_down], down_pairs, from_chips)
        hidden_pairs, dh = pair_sums_of(tag + "_hidden", [k_gate, k_up],
                                        lambda cm: _ffn_dh(tag + "_dh1", dgate, full[k_gate], comm=cm))
        pending = None
        if start_exchange:
            pending, dh = _chips_start("grads_to_chips_start_" + tag, hidden_pairs, dh)
        dh = _ffn_dh(tag + "_dh2", dup, full[k_up], res=dh)
        dx, dg_norm = _norm_bwd(tag + "_dnorm", xin, g_norm, dh, dout)
        return dx, dg_norm, hidden_pairs, carried, pending

    xa_keys = ["w_xq", "w_xkv", "w_xo"]
    dx3, small_g["f2_norm"], ffn2_pairs, _, _ = ffn_backward("f2b", ffn2_keys, x3, f2_norm, ffn2_saved, dx4, None, False)

    do_x = _mm("xa_do", dx3, full["w_xo"], "nt", bf16)
    grads["w_xo"] = _mm("xa_dwo", o_x, dx3, "tn", bf16)
    dq_x, dk_x, dv_x = _xattn_bwd(q_x, kv_x, do_x)
    grads["w_xq"] = _mm("xa_dwq", h3, dq_x, "tn", bf16)
    dh3 = _mm("xa_dh", dq_x, full["w_xq"], "nt", f32)
    dkv_x = jnp.concatenate([dk_x, dv_x], axis=1)
    grads["w_xkv"] = _mm("xa_dwkv", mem_n, dkv_x, "tn", bf16, out_blocks=N_DEV)
    dkv_blocks = dkv_x.astype(bf16).reshape(dkv_x.shape[0], N_DEV, -1).transpose(1, 0, 2)
    dmem_n = _ffn_dh("xa_dmem", dkv_blocks, full["w_xkv"])
    _, small_g["mem_norm"] = _norm_bwd("mem_dnorm", mem0, mem_norm, dmem_n, jnp.zeros_like(mem0))
    xa_pairs, (dx2, small_g["xa_norm"]) = pair_sums_of(
        "xa", xa_keys, lambda cm: _norm_bwd("xa_dnorm", x2, xa_norm, dh3, dx3, comm=cm))

    dycat = _mm("mix_dy", dx2, full["w_out"], "nt", f32)
    grads["w_out"] = _mm("mix_dwout", ycat, dx2, "tn", bf16)
    small_g["b_out"] = _colsum("mix_dbout", dx2)
    out_pairs, (dy_heads, dg_t, do_sw) = pair_sums_of("out", ["w_out"], lambda cm: _rows(
        "mix_dgate", lambda db, yb, gb: (db[:, :c] * gb, db[:, :c] * yb, db[:, c:]), [dycat, y_heads, g_t], [],
        [(c, f32, HEAD_DIM), (c, f32), (swa_w, f32, HEAD_DIM)], comm=cm))
    rw_grads, from_chips = _rwkv_bwd(*seqs, head_pars, checkpoints, dy_heads, comm=_chips_comm(ffn2_pairs))
    update(ffn2_keys[:2], ffn2_pairs, from_chips)
    dr_h, dk_h, dv_h, da_h, dlw_h = rw_grads[:5]
    for nm, gh in zip(("rw_k_k", "rw_k_a", "rw_r_k", "rw_lnx_w", "rw_lnx_b"), rw_grads[5:]):
        small_g[nm] = gh.reshape(w_of[nm].shape)

    def pre_bwd(*args):
        _, vjp = jax.vjp(pre_fn, *args[:4], *args[10:])
        return vjp(tuple(args[4:10]))

    pre_cts = [dr_h, dk_h, dv_h, dlw_h, da_h, dg_t]
    pre_out = _rows("rwkv_pre_bwd", pre_bwd, pre_rows + pre_cts, pre_full,
                    [(3 * c, f32), (3 * c, f32), (dlp + alp + glp, f32), (dlp + alp + glp, f32)],
                    [(p.shape, f32) for p in pre_full], tm=128)
    dp_rkv = pre_out[0] + _shift_up(pre_out[1])
    dp_l = pre_out[2] + _shift_up(pre_out[3])
    dmu_rkv, dmu_l, small_g["rw_w0"], small_g["rw_a0"], d_decay_up, d_aaa_up, d_gate_up = pre_out[4:]
    small_g["rw_mu"] = jnp.concatenate([dmu_rkv, dmu_l[:, :dl], dmu_l[:, dlp:dlp + al], dmu_l[:, dlp + alp:dlp + alp + gl]], axis=1)
    grads["rw_decay_up"] = d_decay_up[:dl].astype(bf16)
    grads["rw_aaa_up"] = d_aaa_up[:al].astype(bf16)
    grads["rw_gate_up"] = d_gate_up[:gl].astype(bf16)

    sw, from_chips = _swa_bwd(*swa_in, do_sw, comm=_chips_comm(xa_pairs))
    update(xa_keys, xa_pairs, from_chips)
    small_g["attn_sinks"] = sw[8].reshape(attn_sinks.shape)
    dza, small_g["b_in_attn"] = _swa_merge(sw[0], sw[1], sw[3], sw[2], sw[5], sw[4], sw[7], sw[6])

    dw_rkv = _mm("in_dwrkv", h2, dp_rkv, "tn", bf16)
    dw_l = _mm("in_dwlora", h2, dp_l, "tn", bf16)
    dw_swa = _mm("in_dwswa", h2, dza, "tn", bf16)
    grads["w_in"] = _win_merge(dw_rkv, dw_l, dw_swa, c, (dl, al, gl), N_DEV)
    dh2, from_chips = _mm("in_dh1", dp_rkv, w_rkv, "nt", f32, comm=_chips_comm(out_pairs))
    update(["w_out"], out_pairs, from_chips)
    dh2 = _mm("in_dh2", dp_l, w_lora, "nt", f32, res=dh2)
    in_pairs, dh2 = pair_sums_of("in", in_keys, lambda cm: _mm("in_dh3", dza, w_swa, "nt", f32, res=dh2, comm=cm))
    dx1, small_g["mix_norm"] = _norm_bwd("mix_dnorm", x1, mix_norm, dh2, dx2)

    dx0, small_g["f1_norm"], _, from_chips, pending = ffn_backward(
        "f1b", ffn1_keys, x0, f1_norm, ffn1_saved, dx1, _chips_comm(in_pairs), True)
    update(in_keys, in_pairs, from_chips)
    others_done = _fence("updates_done", [res[1] for res in out.values()])
    pairs, from_chips = _chips_wait("grads_to_chips_wait_f1b", *pending, others_done)
    update(ffn1_keys[:2], pairs, from_chips)

    sizes = [int(w_of[k].size) for k in small]
    total = sum(sizes)
    cols = -(-total // (8 * LANE)) * LANE

    def pack(parts_of):
        flat = jnp.concatenate([parts_of[k].reshape(-1).astype(f32) for k in small])
        return _pad_to(flat, 8 * cols, 0).reshape(8, cols)

    (all_parts,) = _comm_only("gather_small_grads", _gather_comm([pack(small_g)], after=from_chips))
    res = _adam_small("adam_small", pack(w_of), pack(m_of), pack(v_of), all_parts)
    offs = 0
    flat_res = [a.reshape(-1) for a in res]
    for k, sz in zip(small, sizes):
        out[k] = [a[offs:offs + sz].reshape(w_of[k].shape) for a in flat_res]
        offs += sz

    outs = [loss, dx0.reshape(x.shape)]
    for j in range(4):
        outs += [out[k][j] for k in names]
    return tuple(outs)
```

```python
import functools
import math

import jax
import jax.numpy as jnp
from jax import lax
from jax.experimental import pallas as pl
from jax.experimental.pallas import tpu as pltpu

f32 = jnp.float32
bf16 = jnp.bfloat16
MXU_DTYPE = jnp.bfloat16

HEAD_DIM = 64
SWA_BLOCK = 128
ROPE_THETA = 10000.0
XATTN_HEADS = 4
RMS_EPS = 1e-6
GN_EPS = 64e-5
NEG_INF = -1e30
RWKV_CHUNK = 64

ADAM_LR = 0.001
ADAM_B1 = 0.9
ADAM_B2 = 0.999
ADAM_EPS = 1e-08
ADAM_WD = 0.01
ADAM_STEP = 10

N_DEV = 8
LANE = 128
VMEM_LIMIT_BYTES = 56 * 1024 * 1024
MM_VMEM_BUDGET = 40 * 1024 * 1024
MESH = pl.DeviceIdType.MESH


def _params(sem):
    return pltpu.CompilerParams(dimension_semantics=sem, vmem_limit_bytes=VMEM_LIMIT_BYTES)


class _Comm:
    def __init__(self, ins, outs, n_remote, n_local, start, finish):
        self.ins, self.outs, self.n_remote, self.n_local = list(ins), list(outs), n_remote, max(n_local, 1)
        self.start, self.finish = start, finish


def _pcall(body, comm=None, **kw):
    kw.setdefault("compiler_params", pltpu.CompilerParams(vmem_limit_bytes=VMEM_LIMIT_BYTES))
    if comm is None:
        return pl.pallas_call(body, **kw)
    single = not isinstance(kw["out_shape"], (list, tuple))
    out_shape = [kw["out_shape"]] if single else list(kw["out_shape"])
    out_specs = [kw["out_specs"]] if single else list(kw["out_specs"])
    in_specs, scratch, grid = list(kw["in_specs"]), list(kw.get("scratch_shapes", ())), tuple(kw.get("grid", ()))
    n_in, n_out, n_ci, n_co, n_scr = len(in_specs), len(out_shape), len(comm.ins), len(comm.outs), len(scratch)

    def wrapped(*refs):
        ins, c_ins = refs[:n_in], refs[n_in:n_in + n_ci]
        outs = refs[n_in + n_ci:n_in + n_ci + n_out]
        c_outs = refs[n_in + n_ci + n_out:n_in + n_ci + n_out + n_co]
        rest = refs[n_in + n_ci + n_out + n_co:]
        scr, sems = rest[:n_scr], rest[n_scr:]
        if grid:
            ids = [pl.program_id(k) for k in range(len(grid))]
            first = functools.reduce(jnp.logical_and, [i == 0 for i in ids])
            last = functools.reduce(jnp.logical_and, [i == g - 1 for i, g in zip(ids, grid)])
            pl.when(first)(lambda: comm.start(c_ins, c_outs, *sems))
            body(*ins, *outs, *scr)
            pl.when(last)(lambda: comm.finish(c_ins, c_outs, *sems))
        else:
            comm.start(c_ins, c_outs, *sems)
            body(*ins, *outs, *scr)
            comm.finish(c_ins, c_outs, *sems)

    any_spec = pl.BlockSpec(memory_space=pl.ANY)
    kw.update(in_specs=in_specs + [any_spec] * n_ci, out_specs=out_specs + [any_spec] * n_co,
              out_shape=out_shape + comm.outs,
              scratch_shapes=scratch + [pltpu.SemaphoreType.DMA((comm.n_remote,)), pltpu.SemaphoreType.DMA((comm.n_remote,)),
                                        pltpu.SemaphoreType.DMA((comm.n_local,))])
    if grid:
        kw["compiler_params"] = _params(("arbitrary",) * len(grid))
    call = pl.pallas_call(wrapped, **kw)

    def run(*args):
        res = call(*args, *comm.ins)
        return (res[0] if single else list(res[:n_out])), list(res[n_out:])

    return run


def _fence(name, arrays):
    def body(*refs):
        refs[-1][...] = jnp.zeros(refs[-1].shape, f32)

    return _pcall(body, name=name, in_specs=[pl.BlockSpec(memory_space=pl.ANY)] * len(arrays),
                  out_specs=pl.BlockSpec(memory_space=pltpu.VMEM), out_shape=jax.ShapeDtypeStruct((8, LANE), f32))(*arrays)


def _comm_only(name, comm):
    return _pcall(lambda: None, comm=comm, name=name, in_specs=[], out_specs=[], out_shape=[])()[1]


def _dims(kind, ndim):
    o = ndim - 2
    batch = ((0,), (0,)) if o else ((), ())
    c = {"nn": ((1 + o,), (o,)), "nt": ((1 + o,), (1 + o,)), "tn": ((o,), (o,))}[kind]
    return (c, batch)


def _dot_raw(x, y, kind):
    return lax.dot_general(x.astype(MXU_DTYPE), y.astype(MXU_DTYPE), _dims(kind, x.ndim), preferred_element_type=f32)


@functools.partial(jax.custom_vjp, nondiff_argnums=(2,))
def _dot(x, y, kind):
    return _dot_raw(x, y, kind)


def _dot_fwd(x, y, kind):
    return _dot_raw(x, y, kind), (x, y)


def _dot_bwd(kind, res, g):
    x, y = res
    if kind == "nn":
        dx, dy = _dot(g, y, "nt"), _dot(x, g, "tn")
    elif kind == "nt":
        dx, dy = _dot(g, y, "nn"), _dot(g, x, "tn")
    else:
        dx, dy = _dot(y, g, "nt"), _dot(x, g, "nn")
    return dx.astype(x.dtype), dy.astype(y.dtype)


_dot.defvjp(_dot_fwd, _dot_bwd)


def _split3(x):
    a = x.astype(bf16)
    r = x - a.astype(f32)
    b = r.astype(bf16)
    c = (r - b.astype(f32)).astype(bf16)
    return a, b, c


def _rms(x, g):
    x = x.astype(f32)
    return x * lax.rsqrt(jnp.mean(x * x, axis=-1, keepdims=True) + RMS_EPS) * g


def _sigmoid(x):
    return 1.0 / (1.0 + jnp.exp(-x))


def _softplus(x):
    return jnp.maximum(x, 0.0) + jnp.log(1.0 + jnp.exp(-jnp.abs(x)))


def _rows(name, fn, row_ins, full_ins, row_outs, acc_outs=(), tm=None, comm=None):
    width = lambda a: a.shape[1] if a.ndim == 2 else a.shape[0] * a.shape[2]
    rows = row_ins[0].shape[0] if row_ins[0].ndim == 2 else row_ins[0].shape[1]
    if tm is None:
        tm = _row_tile(rows, max([width(a) for a in row_ins] + [o[0] for o in row_outs]))
    tm = min(tm, rows)
    assert rows % tm == 0, (name, rows, tm)
    n_in = len(row_ins) + len(full_ins)
    n_o, n_a = len(row_outs), len(acc_outs)

    def load(k, ref):
        if k < len(row_ins) and row_ins[k].ndim == 3:
            return jnp.concatenate([ref[h] for h in range(ref.shape[0])], axis=-1)
        return ref[...]

    def body(*refs):
        vals = [load(k, r) for k, r in enumerate(refs[:n_in])]
        outs = fn(*vals)
        o_refs = refs[n_in:n_in + n_o]
        a_refs = refs[n_in + n_o:]
        for k in range(n_o):
            if len(row_outs[k]) == 3:
                n = row_outs[k][2]
                for h in range(row_outs[k][0] // n):
                    o_refs[k][h] = outs[k][:, h * n:(h + 1) * n].astype(o_refs[k].dtype)
            else:
                o_refs[k][...] = outs[k].astype(o_refs[k].dtype)
        if n_a:
            first = pl.program_id(0) == 0

            @pl.when(first)
            def _():
                for k in range(n_a):
                    a_refs[k][...] = outs[n_o + k].astype(a_refs[k].dtype)

            @pl.when(jnp.logical_not(first))
            def _():
                for k in range(n_a):
                    a_refs[k][...] += outs[n_o + k].astype(a_refs[k].dtype)

    by_rows = lambda cols: pl.BlockSpec((tm, cols), lambda i: (i, 0))
    by_heads = lambda h, n: pl.BlockSpec((h, tm, n), lambda i: (0, i, 0))
    in_specs = [by_rows(a.shape[1]) if a.ndim == 2 else by_heads(a.shape[0], a.shape[2]) for a in row_ins]
    in_specs += [pl.BlockSpec(a.shape, lambda i, nd=a.ndim: (0,) * nd) for a in full_ins]
    out_specs = [by_rows(o[0]) if len(o) == 2 else by_heads(o[0] // o[2], o[2]) for o in row_outs]
    out_specs += [pl.BlockSpec(s, lambda i, nd=len(s): (0,) * nd) for s, _ in acc_outs]
    out_shape = [jax.ShapeDtypeStruct((rows, o[0]) if len(o) == 2 else (o[0] // o[2], rows, o[2]), o[1]) for o in row_outs]
    out_shape += [jax.ShapeDtypeStruct(s, d) for s, d in acc_outs]
    return _pcall(body, comm=comm, name=name, grid=(rows // tm,), in_specs=in_specs, out_specs=out_specs, out_shape=out_shape,
                  compiler_params=_params(("arbitrary",)))(*row_ins, *full_ins)


def _pick(n, cands):
    for c in cands:
        if n % c == 0:
            return c
    return n


def _mm(name, a, b, mode, out_dtype, scale=1.0, res=None, bias=None, comm=None, out_blocks=None):
    b_blocks = b.ndim == 3
    if b_blocks:
        assert mode == "nn"
        (m, k), (nb, k2, tn) = a.shape, b.shape
        n = nb * tn
    elif mode == "nn":
        (m, k), (k2, n) = a.shape, b.shape
    elif mode == "nt":
        (m, k), (n, k2) = a.shape, b.shape
    else:
        (k, m), (k2, n) = a.shape, b.shape
    assert k == k2, (name, a.shape, b.shape, mode)
    if not b_blocks:
        tn = n // out_blocks if out_blocks else _pick(n, (512, 256, 128))
    tm = _pick(m, (1024, 512, 256, 128))

    def need(tm_):
        by = tm_ * k * a.dtype.itemsize + tn * k * b.dtype.itemsize + tm_ * tn * (jnp.dtype(out_dtype).itemsize + 4)
        if res is not None:
            by += tm_ * tn * res.dtype.itemsize
        return 2 * by

    while need(tm) > MM_VMEM_BUDGET and tm % 256 == 0:
        tm //= 2
    dims = _dims(mode, 2)

    def body(*refs):
        bv = refs[1][0] if b_blocks else refs[1][...]
        acc = lax.dot_general(refs[0][...].astype(MXU_DTYPE), bv.astype(MXU_DTYPE), dims, preferred_element_type=f32)
        if scale != 1.0:
            acc = acc * scale
        pos = 2
        if bias is not None:
            acc = acc + refs[pos][...]
            pos += 1
        if res is not None:
            acc = acc + refs[pos][...].astype(f32)
            pos += 1
        if out_blocks:
            refs[pos][0] = acc.astype(out_dtype)
        else:
            refs[pos][...] = acc.astype(out_dtype)

    a_spec = pl.BlockSpec((k, tm), lambda i, j: (0, i)) if mode == "tn" else pl.BlockSpec((tm, k), lambda i, j: (i, 0))
    if b_blocks:
        b_spec = pl.BlockSpec((1, k, tn), lambda i, j: (j, 0, 0))
    else:
        b_spec = pl.BlockSpec((tn, k), lambda i, j: (j, 0)) if mode == "nt" else pl.BlockSpec((k, tn), lambda i, j: (0, j))
    in_specs, args = [a_spec, b_spec], [a, b]
    if bias is not None:
        in_specs.append(pl.BlockSpec((1, tn), lambda i, j: (0, j)))
        args.append(bias)
    if res is not None:
        in_specs.append(pl.BlockSpec((tm, tn), lambda i, j: (i, j)))
        args.append(res)
    if out_blocks:
        out_spec, out_shape = pl.BlockSpec((1, tm, tn), lambda i, j: (j, i, 0)), jax.ShapeDtypeStruct((out_blocks, m, tn), out_dtype)
    else:
        out_spec, out_shape = pl.BlockSpec((tm, tn), lambda i, j: (i, j)), jax.ShapeDtypeStruct((m, n), out_dtype)
    return _pcall(body, comm=comm, name=name, grid=(m // tm, n // tn), in_specs=in_specs, out_specs=out_spec, out_shape=out_shape,
                  compiler_params=_params(("parallel", "parallel")))(*args)


def _position():
    return lax.axis_index("x"), lax.axis_index("y"), lax.axis_index("c")


def _gather_comm(shards, after=()):
    n = len(shards)

    def plan(x_refs, o_refs, send_sems, recv_sems, local_sems):
        x, y, c = _position()
        me, sibling = (x, y, c), (x, y, 1 - c)
        chips = [(1 - x, y), (x, 1 - y), (1 - x, 1 - y)]

        def slot(px, py, pc):
            return 4 * px + 2 * py + pc

        def copy(t, k, block, to, src=None):
            dst = o_refs[t].at[slot(*block)]
            return pltpu.make_async_remote_copy(src_ref=dst if src is None else src, dst_ref=dst,
                                                send_sem=send_sems.at[7 * t + k], recv_sem=recv_sems.at[7 * t + k],
                                                device_id=to, device_id_type=MESH)

        mine = [pltpu.make_async_copy(x_refs[t], o_refs[t].at[slot(*me)], local_sems.at[t]) for t in range(n)]
        first = []
        for t in range(n):
            first.append(copy(t, 0, me, sibling, src=x_refs[t]))
            first += [copy(t, 1 + j, me, (*chip, c), src=x_refs[t]) for j, chip in enumerate(chips)]
        return me, sibling, chips, c, copy, mine, first

    def start(*refs):
        _, _, _, _, _, mine, first = plan(*refs)
        for cp in mine + first:
            cp.start()

    def finish(*refs):
        me, sibling, chips, c, copy, mine, first = plan(*refs)
        passed = []
        for t in range(n):
            for j, chip in enumerate(chips):
                copy(t, 1 + j, (*chip, c), me).wait_recv()
                cp = copy(t, 4 + j, (*chip, c), sibling)
                cp.start()
                passed.append(cp)
        for t in range(n):
            copy(t, 0, sibling, me).wait_recv()
            for j, chip in enumerate(chips):
                copy(t, 4 + j, (*chip, 1 - c), me).wait_recv()
        for cp in first + passed:
            cp.wait_send()
        for cp in mine:
            cp.wait()

    outs = [jax.ShapeDtypeStruct((N_DEV,) + s.shape, s.dtype) for s in shards]
    return _Comm(list(shards) + list(after), outs, 7 * n, n, start, finish)


def _sibling_comm(blocks):
    n = len(blocks)

    def copies(g_refs, o_refs, send_sems, recv_sems, _):
        x, y, c = _position()
        return [pltpu.make_async_remote_copy(src_ref=g_refs[t].at[2 * q + 1 - c], dst_ref=o_refs[t].at[q],
                                             send_sem=send_sems.at[4 * t + q], recv_sem=recv_sems.at[4 * t + q],
                                             device_id=(x, y, 1 - c), device_id_type=MESH)
                for t in range(n) for q in range(4)]

    def start(*refs):
        for cp in copies(*refs):
            cp.start()

    def finish(*refs):
        for cp in copies(*refs):
            cp.wait()

    outs = [jax.ShapeDtypeStruct((4,) + g.shape[1:], g.dtype) for g in blocks]
    return _Comm(blocks, outs, 4 * n, 0, start, finish)


def _chips_comm(parts):
    n = len(parts)

    def copies(p_refs, o_refs, send_sems, recv_sems, _):
        x, y, c = _position()
        chips = [(1 - x, y), (x, 1 - y), (1 - x, 1 - y)]
        return [pltpu.make_async_remote_copy(src_ref=p_refs[t].at[2 * px + py], dst_ref=o_refs[t].at[j],
                                             send_sem=send_sems.at[3 * t + j], recv_sem=recv_sems.at[3 * t + j],
                                             device_id=(px, py, c), device_id_type=MESH)
                for t in range(n) for j, (px, py) in enumerate(chips)]

    def start(*refs):
        for cp in copies(*refs):
            cp.start()

    def finish(*refs):
        for cp in copies(*refs):
            cp.wait()

    outs = [jax.ShapeDtypeStruct((3,) + p.shape[1:], p.dtype) for p in parts]
    return _Comm(parts, outs, 3 * n, 0, start, finish)


HBM_SPEC = pl.BlockSpec(memory_space=pltpu.HBM)
SEM_SPEC = pl.BlockSpec(memory_space=pltpu.SEMAPHORE)
DATAFLOW = pltpu.SideEffectType.DATAFLOW_SIDE_EFFECTING


def _chip_exchange_copies(p_refs, o_refs, send_sems, recv_sems):
    x, y, c = _position()
    chips = [(1 - x, y), (x, 1 - y), (1 - x, 1 - y)]
    return [pltpu.make_async_remote_copy(src_ref=p_refs[t].at[2 * px + py], dst_ref=o_refs[t].at[j],
                                         send_sem=send_sems.at[3 * t + j], recv_sem=recv_sems.at[3 * t + j],
                                         device_id=(px, py, c), device_id_type=MESH)
            for t in range(len(p_refs)) for j, (px, py) in enumerate(chips)]


def _chips_start(name, parts, thru):
    n = len(parts)

    def body(*refs):
        for cp in _chip_exchange_copies(refs[:n], refs[n:2 * n], refs[2 * n + 1], refs[2 * n + 2]):
            cp.start()

    lands = [lax.empty((3,) + p.shape[1:], p.dtype) for p in parts]
    args = [pltpu.with_memory_space_constraint(a, pltpu.HBM) for a in list(parts) + lands + [thru]]
    res = pl.pallas_call(body, name=name, in_specs=[HBM_SPEC] * (2 * n + 1),
                         out_specs=[SEM_SPEC, SEM_SPEC] + [HBM_SPEC] * (2 * n + 1),
                         out_shape=[pltpu.SemaphoreType.DMA((3 * n,)), pltpu.SemaphoreType.DMA((3 * n,))]
                         + [pltpu.HBM(a.shape, a.dtype) for a in args],
                         input_output_aliases={i: 2 + i for i in range(2 * n + 1)},
                         compiler_params=pltpu.CompilerParams(has_side_effects=DATAFLOW))(*args)
    return (res[0], res[1], list(res[2:2 + n]), list(res[2 + n:2 + 2 * n])), res[2 + 2 * n]


def _chips_wait(name, send_sems, recv_sems, parts, lands, after):
    n = len(parts)

    def body(*refs):
        for cp in _chip_exchange_copies(refs[:n], refs[n:2 * n], refs[2 * n], refs[2 * n + 1]):
            cp.wait_send()
            cp.wait_recv()

    res = pl.pallas_call(body, name=name, out_shape=[pltpu.HBM(a.shape, a.dtype) for a in parts + lands],
                         in_specs=[HBM_SPEC] * (2 * n) + [SEM_SPEC, SEM_SPEC, pl.BlockSpec(memory_space=pl.ANY)],
                         out_specs=[HBM_SPEC] * (2 * n), input_output_aliases={i: i for i in range(2 * n)},
                         compiler_params=pltpu.CompilerParams(has_side_effects=DATAFLOW))(*parts, *lands, send_sems, recv_sems, after)
    return list(res[:n]), list(res[n:])


ROW_TILE_BYTES = 2 << 20


def _row_tile(r, cols, itemsize=4):
    fits = [t for t in range(8, r + 1, 8) if r % t == 0 and t * cols * itemsize <= ROW_TILE_BYTES]
    return max(fits) if fits else r


def _pair_add(name, g, got, c_idx):
    _, r, cc = g.shape
    tr = _row_tile(r, cc, g.dtype.itemsize)

    def body(c_ref, g_ref, o_ref, out_ref):
        out_ref[...] = (g_ref[...].astype(f32) + o_ref[...].astype(f32)).astype(out_ref.dtype)

    g5 = g.reshape(4, 2, r, cc)
    spec = pltpu.PrefetchScalarGridSpec(
        num_scalar_prefetch=1, grid=(4, r // tr),
        in_specs=[pl.BlockSpec((1, 1, tr, cc), lambda q, i, c_ref: (q, c_ref[0], i, 0)),
                  pl.BlockSpec((1, 1, tr, cc), lambda q, i, c_ref: (q, 0, i, 0))],
        out_specs=pl.BlockSpec((1, 1, tr, cc), lambda q, i, c_ref: (q, 0, i, 0)))
    out = _pcall(body, name=name, grid_spec=spec, out_shape=jax.ShapeDtypeStruct((4, 1, r, cc), g.dtype),
                 compiler_params=_params(("arbitrary", "arbitrary")))(c_idx, g5, got.reshape(4, 1, r, cc))
    return out.reshape(4, r, cc)


def _adam_math(w, g, m, v):
    m2 = ADAM_B1 * m + (1.0 - ADAM_B1) * g
    v2 = ADAM_B2 * v + (1.0 - ADAM_B2) * (g * g)
    m_hat = m2 / (1.0 - ADAM_B1 ** ADAM_STEP)
    v_hat = v2 / (1.0 - ADAM_B2 ** ADAM_STEP)
    delta = -ADAM_LR * (m_hat / (jnp.sqrt(v_hat) + ADAM_EPS) + ADAM_WD * w)
    return delta, m2, v2


def _adam_sharded(name, w, m, v, part, got, chip_idx):
    r, cc = w.shape
    tr = _row_tile(r, cc)

    def body(q_ref, w_ref, m_ref, v_ref, p_ref, o_ref, g_out, d_out, m_out, v_out):
        g = p_ref[0].astype(f32)
        for j in range(3):
            g = g + o_ref[j].astype(f32)
        d, m2, v2 = _adam_math(w_ref[...], g, m_ref[...], v_ref[...])
        g_out[...] = g
        d_out[...] = d
        m_out[...] = m2
        v_out[...] = v2

    row = pl.BlockSpec((tr, cc), lambda i, q_ref: (i, 0))
    spec = pltpu.PrefetchScalarGridSpec(
        num_scalar_prefetch=1, grid=(r // tr,),
        in_specs=[row, row, row, pl.BlockSpec((1, tr, cc), lambda i, q_ref: (q_ref[0], i, 0)),
                  pl.BlockSpec((3, tr, cc), lambda i, q_ref: (0, i, 0))],
        out_specs=[row, row, row, row])
    sh = jax.ShapeDtypeStruct((r, cc), f32)
    return _pcall(body, name=name, grid_spec=spec, out_shape=[sh, sh, sh, sh],
                  compiler_params=_params(("arbitrary",)))(chip_idx, w, m, v, part, got)


def _adam_small(name, w, m, v, parts):
    def body(w_ref, m_ref, v_ref, p_ref, g_out, d_out, m_out, v_out):
        g = p_ref[0]
        for b in range(1, N_DEV):
            g = g + p_ref[b]
        d, m2, v2 = _adam_math(w_ref[...], g, m_ref[...], v_ref[...])
        g_out[...] = g
        d_out[...] = d
        m_out[...] = m2
        v_out[...] = v2

    sh = jax.ShapeDtypeStruct(w.shape, f32)
    return _pcall(body, name=name, out_shape=[sh, sh, sh, sh])(w, m, v, parts)


def _swa_math(n, qa, qb, kap, kac, kbp, kbc, vp, vc, cq, sq, cp, sp, sink):
    g, blk, half = qa.shape
    c3, s3 = cq[None], sq[None]
    q1 = (qa * c3 - qb * s3).reshape(g * blk, half)
    q2 = (qb * c3 + qa * s3).reshape(g * blk, half)
    ck, sk = jnp.concatenate([cp, cq], axis=0), jnp.concatenate([sp, sq], axis=0)
    k1, k2 = jnp.concatenate([kap[0], kac[0]], axis=0), jnp.concatenate([kbp[0], kbc[0]], axis=0)
    k1r, k2r = k1 * ck - k2 * sk, k2 * ck + k1 * sk
    vv = jnp.concatenate([vp[0], vc[0]], axis=0)
    s = (_dot(q1, k1r, "nt") + _dot(q2, k2r, "nt")) * (HEAD_DIM ** -0.5)
    s = s.reshape(g, blk, 2 * blk)
    qi = lax.broadcasted_iota(jnp.int32, (blk, 2 * blk), 0)
    kj = lax.broadcasted_iota(jnp.int32, (blk, 2 * blk), 1)
    valid = (kj > qi) & (kj <= qi + blk) & ((kj >= blk) | (n > 0))
    s = jnp.where(valid[None], s, NEG_INF)
    sink3 = sink.reshape(g, 1, 1)
    mx = jnp.maximum(jnp.max(s, axis=-1, keepdims=True), sink3)
    e = jnp.exp(s - mx)
    z = jnp.sum(e, axis=-1, keepdims=True) + jnp.exp(sink3 - mx)
    p = (e / z).reshape(g * blk, 2 * blk)
    return _dot(p, vv, "nn").reshape(g, blk, 2 * half)


def _swa_specs(hq, kv, blk, half):
    prev = lambda n: jnp.maximum(n - 1, 0)
    q_spec = pl.BlockSpec((hq, blk, half), lambda n: (0, n, 0))
    kc = pl.BlockSpec((kv, blk, half), lambda n: (0, n, 0))
    kp = pl.BlockSpec((kv, blk, half), lambda n: (0, prev(n), 0))
    vc = pl.BlockSpec((kv, blk, 2 * half), lambda n: (0, n, 0))
    vp = pl.BlockSpec((kv, blk, 2 * half), lambda n: (0, prev(n), 0))
    tc = pl.BlockSpec((blk, half), lambda n: (n, 0))
    tp = pl.BlockSpec((blk, half), lambda n: (prev(n), 0))
    sink = pl.BlockSpec((kv, hq // kv, 1), lambda n: (0, 0, 0))
    o_spec = pl.BlockSpec((hq, blk, 2 * half), lambda n: (0, n, 0))
    return q_spec, kc, kp, vc, vp, tc, tp, sink, o_spec


def _swa_fwd(qa, qb, ka, kb, v, cos, sin, sinks, comm=None):
    hq, t, half = qa.shape
    kv = ka.shape[0]
    g, blk = hq // kv, SWA_BLOCK
    q_spec, kc, kp, vc, vp, tc, tp, sink, o_spec = _swa_specs(hq, kv, blk, half)

    def body(qa_r, qb_r, kap, kac, kbp, kbc, vp_r, vc_r, cq, sq, cp, sp, sink_r, o_r):
        tabs = (cq[...], sq[...], cp[...], sp[...])
        for h in range(kv):
            qs, ks = pl.ds(h * g, g), pl.ds(h, 1)
            o_r[qs] = _swa_math(pl.program_id(0), qa_r[qs], qb_r[qs], kap[ks], kac[ks], kbp[ks], kbc[ks], vp_r[ks], vc_r[ks],
                                *tabs, sink_r[ks]).astype(o_r.dtype)

    return _pcall(body, comm=comm, name="swa_fwd", grid=(t // blk,),
                  in_specs=[q_spec, q_spec, kp, kc, kp, kc, vp, vc, tc, tc, tp, tp, sink], out_specs=o_spec,
                  out_shape=jax.ShapeDtypeStruct((hq, t, 2 * half), f32),
                  compiler_params=_params(("arbitrary",)))(qa, qb, ka, ka, kb, kb, v, v, cos, sin, cos, sin, sinks)


def _swa_bwd(qa, qb, ka, kb, v, cos, sin, sinks, do, comm=None):
    hq, t, half = qa.shape
    kv = ka.shape[0]
    g, blk = hq // kv, SWA_BLOCK
    q_spec, kc, kp, vc, vp, tc, tp, sink, o_spec = _swa_specs(hq, kv, blk, half)

    def body(qa_r, qb_r, kap, kac, kbp, kbc, vp_r, vc_r, cq, sq, cp, sp, sink_r, do_r,
             dqa, dqb, dkap, dkac, dkbp, dkbc, dvp, dvc, dsink):
        n = pl.program_id(0)
        tabs = (cq[...], sq[...], cp[...], sp[...])
        fn = lambda a, b, c_, d, e, f_, g_, h_, s_: _swa_math(n, a, b, c_, d, e, f_, g_, h_, *tabs, s_)
        dsinks = []
        for h in range(kv):
            qs, ks = pl.ds(h * g, g), pl.ds(h, 1)
            _, vjp = jax.vjp(fn, qa_r[qs], qb_r[qs], kap[ks], kac[ks], kbp[ks], kbc[ks], vp_r[ks], vc_r[ks], sink_r[ks])
            grads = vjp(do_r[qs])
            dqa[qs] = grads[0]
            dqb[qs] = grads[1]
            for ref, val in zip((dkap, dkac, dkbp, dkbc, dvp, dvc), grads[2:8]):
                ref[ks] = val
            dsinks.append(grads[8])
        dsink_all = jnp.concatenate(dsinks, axis=0)

        @pl.when(n == 0)
        def _():
            dsink[...] = dsink_all

        @pl.when(n > 0)
        def _():
            dsink[...] += dsink_all

    sh = lambda a: jax.ShapeDtypeStruct(a.shape, f32)
    return _pcall(body, comm=comm, name="swa_bwd", grid=(t // blk,),
                  in_specs=[q_spec, q_spec, kp, kc, kp, kc, vp, vc, tc, tc, tp, tp, sink, o_spec],
                  out_specs=[q_spec, q_spec, kc, kc, kc, kc, vc, vc, sink],
                  out_shape=[sh(qa), sh(qb), sh(ka), sh(ka), sh(kb), sh(kb), sh(v), sh(v), sh(sinks)],
                  compiler_params=_params(("arbitrary",)))(qa, qb, ka, ka, kb, kb, v, v, cos, sin, cos, sin, sinks, do)


def _swa_split(za, hq, kv):
    t = za.shape[0]
    half = HEAD_DIM // 2
    tm = _row_tile(t, za.shape[1])

    def body(z_r, qa, qb, ka, kb, v):
        z = z_r[...]
        for h in range(hq):
            qa[h] = z[:, HEAD_DIM * h:HEAD_DIM * h + half]
            qb[h] = z[:, HEAD_DIM * h + half:HEAD_DIM * (h + 1)]
        for h in range(kv):
            o = HEAD_DIM * (hq + h)
            ka[h] = z[:, o:o + half]
            kb[h] = z[:, o + half:o + HEAD_DIM]
            o = HEAD_DIM * (hq + kv + h)
            v[h] = z[:, o:o + HEAD_DIM]

    spec = lambda n, w: pl.BlockSpec((n, tm, w), lambda i: (0, i, 0))
    sh = lambda n, w: jax.ShapeDtypeStruct((n, t, w), f32)
    return _pcall(body, name="swa_split", grid=(t // tm,), in_specs=[pl.BlockSpec((tm, za.shape[1]), lambda i: (i, 0))],
                  out_specs=[spec(hq, half), spec(hq, half), spec(kv, half), spec(kv, half), spec(kv, HEAD_DIM)],
                  out_shape=[sh(hq, half), sh(hq, half), sh(kv, half), sh(kv, half), sh(kv, HEAD_DIM)],
                  compiler_params=_params(("parallel",)))(za)


def _swa_merge(dqa, dqb, dkac, dkap, dkbc, dkbp, dvc, dvp):
    hq, t, half = dqa.shape
    kv = dkac.shape[0]
    blk = SWA_BLOCK
    nb = t // blk
    cols = HEAD_DIM * (hq + 2 * kv)

    def body(qa, qb, kac, kap, kbc, kbp, vc, vp, z_o, s_o):
        i = pl.program_id(0)
        more = (i < nb - 1).astype(f32)
        pieces = []
        for h in range(hq):
            pieces += [qa[h], qb[h]]
        for h in range(kv):
            pieces += [kac[h] + more * kap[h], kbc[h] + more * kbp[h]]
        for h in range(kv):
            pieces.append(vc[h] + more * vp[h])
        z = jnp.concatenate(pieces, axis=-1)
        z_o[...] = z
        colsum = jnp.sum(z, axis=0, keepdims=True)

        @pl.when(i == 0)
        def _():
            s_o[...] = colsum

        @pl.when(i > 0)
        def _():
            s_o[...] += colsum

    cur = lambda n, w: pl.BlockSpec((n, blk, w), lambda i: (0, i, 0))
    nxt = lambda n, w: pl.BlockSpec((n, blk, w), lambda i: (0, jnp.minimum(i + 1, nb - 1), 0))
    return _pcall(body, name="swa_merge", grid=(nb,),
                  in_specs=[cur(hq, half), cur(hq, half), cur(kv, half), nxt(kv, half), cur(kv, half), nxt(kv, half),
                            cur(kv, HEAD_DIM), nxt(kv, HEAD_DIM)],
                  out_specs=[pl.BlockSpec((blk, cols), lambda i: (i, 0)), pl.BlockSpec((1, cols), lambda i: (0, 0))],
                  out_shape=[jax.ShapeDtypeStruct((t, cols), f32), jax.ShapeDtypeStruct((1, cols), f32)],
                  compiler_params=_params(("arbitrary",)))(dqa, dqb, dkac, dkap, dkbc, dkbp, dvc, dvp)


def _xattn_math(q, k, v):
    s = _dot(q, k, "nt") * (q.shape[-1] ** -0.5)
    e = jnp.exp(s - jnp.max(s, axis=-1, keepdims=True))
    p = e / jnp.sum(e, axis=-1, keepdims=True)
    return _dot(p, v, "nn")


def _xattn_fwd(q, kvm):
    t, d = q.shape
    mlen = kvm.shape[0]
    hd = d // XATTN_HEADS
    tq = min(512, t)

    def body(q_r, k_r, v_r, o_r):
        o_r[...] = _xattn_math(q_r[...], k_r[...], v_r[...]).astype(o_r.dtype)

    return _pcall(body, name="xattn_fwd", grid=(XATTN_HEADS, t // tq),
                  in_specs=[pl.BlockSpec((tq, hd), lambda h, i: (i, h)), pl.BlockSpec((mlen, hd), lambda h, i: (0, h)),
                            pl.BlockSpec((mlen, hd), lambda h, i: (0, XATTN_HEADS + h))],
                  out_specs=pl.BlockSpec((tq, hd), lambda h, i: (i, h)), out_shape=jax.ShapeDtypeStruct((t, d), bf16),
                  compiler_params=_params(("parallel", "parallel")))(q, kvm, kvm)


def _xattn_bwd(q, kvm, do):
    t, d = q.shape
    mlen = kvm.shape[0]
    hd = d // XATTN_HEADS
    tq = min(512, t)

    def body(q_r, k_r, v_r, do_r, dq, dk, dv):
        _, vjp = jax.vjp(_xattn_math, q_r[...].astype(f32), k_r[...].astype(f32), v_r[...].astype(f32))
        gq, gk, gv = vjp(do_r[...].astype(f32))
        dq[...] = gq.astype(dq.dtype)
        first = pl.program_id(1) == 0

        @pl.when(first)
        def _():
            dk[...] = gk
            dv[...] = gv

        @pl.when(jnp.logical_not(first))
        def _():
            dk[...] += gk
            dv[...] += gv

    qs = pl.BlockSpec((tq, hd), lambda h, i: (i, h))
    ms = pl.BlockSpec((mlen, hd), lambda h, i: (0, h))
    return _pcall(body, name="xattn_bwd", grid=(XATTN_HEADS, t // tq),
                  in_specs=[qs, ms, pl.BlockSpec((mlen, hd), lambda h, i: (0, XATTN_HEADS + h)), qs],
                  out_specs=[qs, ms, ms],
                  out_shape=[jax.ShapeDtypeStruct((t, d), bf16), jax.ShapeDtypeStruct((mlen, d), f32),
                             jax.ShapeDtypeStruct((mlen, d), f32)],
                  compiler_params=_params(("parallel", "arbitrary")))(q, kvm, kvm, do)


def _chunk_cumsum(lw, reverse=False):
    h, l, _ = lw.shape
    i = lax.broadcasted_iota(jnp.int32, (l, l), 0)
    j = lax.broadcasted_iota(jnp.int32, (l, l), 1)
    tri = jnp.broadcast_to(((i <= j) if reverse else (i >= j)).astype(bf16)[None], (h, l, l))
    out = jnp.zeros(lw.shape, f32)
    for piece in _split3(lw):
        out = out + lax.dot_general(tri, piece, _dims("nn", 3), preferred_element_type=f32)
    return out


def _rwkv_chunk(s0, r, k, v, a, lw, cl, k_k, k_a, r_k, ln_w, ln_b):
    l = r.shape[1]
    kk = k * k_k
    kk = kk / jnp.maximum(jnp.sqrt(jnp.sum(kk * kk, axis=-1, keepdims=True)), 1e-12)
    km = k * (1.0 + (a - 1.0) * k_a)
    av, bv = -kk, kk * a
    p_incl, p_excl, p_inv = jnp.exp(cl), jnp.exp(cl - lw), jnp.exp(-cl)
    at, bh, kh, rt = av * p_excl, bv * p_inv, km * p_inv, r * p_incl
    i = lax.broadcasted_iota(jnp.int32, (l, l), 0)
    j = lax.broadcasted_iota(jnp.int32, (l, l), 1)
    strict, incl = (i > j)[None], (i >= j)[None]
    a_ab = jnp.where(strict, _dot(at, bh, "nt"), 0.0)
    a_ak = jnp.where(strict, _dot(at, kh, "nt"), 0.0)
    a_rb = jnp.where(incl, _dot(rt, bh, "nt"), 0.0)
    a_rk = jnp.where(incl, _dot(rt, kh, "nt"), 0.0)
    rhs = _dot(at, s0, "nt") + _dot(a_ak, v, "nn")
    inv = a_ab + (i == j)[None].astype(f32)
    pw = a_ab
    for _ in range(int(math.log2(l)) - 1):
        pw = _dot(pw, pw, "nn")
        inv = inv + _dot(inv, pw, "nn")
    sa = _dot(inv, rhs, "nn")
    y = _dot(rt, s0, "nt") + _dot(a_rk, v, "nn") + _dot(a_rb, sa, "nn")
    p_last = p_incl[:, l - 1:l, :]
    s_end = s0 * p_last + _dot(v, kh * p_last, "tn") + _dot(sa, bh * p_last, "tn")
    mu = jnp.mean(y, axis=-1, keepdims=True)
    var = jnp.mean(jnp.square(y - mu), axis=-1, keepdims=True)
    out = (y - mu) * lax.rsqrt(var + GN_EPS) * ln_w + ln_b
    out = out + jnp.sum(r * km * r_k, axis=-1, keepdims=True) * v
    return out, s_end


def _rwkv_fwd(r, k, v, a, lw, heads, comm=None):
    h, t, n = r.shape
    l = min(RWKV_CHUNK, t)
    nc = t // l
    seq = pl.BlockSpec((h, l, n), lambda c: (0, c, 0))
    par = pl.BlockSpec((h, 1, n), lambda c: (0, 0, 0))

    def body(r_r, k_r, v_r, a_r, lw_r, p0, p1, p2, p3, p4, y_r, ck_r, s_scr):
        @pl.when(pl.program_id(0) == 0)
        def _():
            s_scr[...] = jnp.zeros_like(s_scr)

        s0 = s_scr[...]
        ck_r[0] = s0
        lw_v = lw_r[...]
        out, s_end = _rwkv_chunk(s0, r_r[...], k_r[...], v_r[...], a_r[...], lw_v, _chunk_cumsum(lw_v),
                                 p0[...], p1[...], p2[...], p3[...], p4[...])
        y_r[...] = out
        s_scr[...] = s_end

    return _pcall(body, comm=comm, name="rwkv_fwd", grid=(nc,), in_specs=[seq] * 5 + [par] * 5,
                  out_specs=[seq, pl.BlockSpec((1, h, n, n), lambda c: (c, 0, 0, 0))],
                  out_shape=[jax.ShapeDtypeStruct((h, t, n), f32), jax.ShapeDtypeStruct((nc, h, n, n), f32)],
                  scratch_shapes=[pltpu.VMEM((h, n, n), f32)],
                  compiler_params=_params(("arbitrary",)))(r, k, v, a, lw, *heads)


def _rwkv_bwd(r, k, v, a, lw, heads, ck, dy, comm=None):
    h, t, n = r.shape
    l = min(RWKV_CHUNK, t)
    nc = t // l
    seq = pl.BlockSpec((h, l, n), lambda c: (0, nc - 1 - c, 0))
    par = pl.BlockSpec((h, 1, n), lambda c: (0, 0, 0))

    def body(r_r, k_r, v_r, a_r, lw_r, p0, p1, p2, p3, p4, ck_r, dy_r,
             dr, dk, dv, da, dlw, g0, g1, g2, g3, g4, ds_scr):
        first = pl.program_id(0) == 0

        @pl.when(first)
        def _():
            ds_scr[...] = jnp.zeros_like(ds_scr)

        lw_v = lw_r[...]
        _, vjp = jax.vjp(_rwkv_chunk, ck_r[0], r_r[...], k_r[...], v_r[...], a_r[...], lw_v, _chunk_cumsum(lw_v),
                         p0[...], p1[...], p2[...], p3[...], p4[...])
        grads = vjp((dy_r[...], ds_scr[...]))
        ds_scr[...] = grads[0]
        dr[...] = grads[1]
        dk[...] = grads[2]
        dv[...] = grads[3]
        da[...] = grads[4]
        dlw[...] = grads[5] + _chunk_cumsum(grads[6], reverse=True)
        acc = (g0, g1, g2, g3, g4)

        @pl.when(first)
        def _():
            for ref, val in zip(acc, grads[7:]):
                ref[...] = val

        @pl.when(jnp.logical_not(first))
        def _():
            for ref, val in zip(acc, grads[7:]):
                ref[...] += val

    seq_sh = jax.ShapeDtypeStruct((h, t, n), f32)
    par_sh = jax.ShapeDtypeStruct((h, 1, n), f32)
    return _pcall(body, comm=comm, name="rwkv_bwd", grid=(nc,),
                  in_specs=[seq] * 5 + [par] * 5 + [pl.BlockSpec((1, h, n, n), lambda c: (nc - 1 - c, 0, 0, 0)), seq],
                  out_specs=[seq] * 5 + [par] * 5, out_shape=[seq_sh] * 5 + [par_sh] * 5,
                  scratch_shapes=[pltpu.VMEM((h, n, n), f32)],
                  compiler_params=_params(("arbitrary",)))(r, k, v, a, lw, *heads, ck, dy)


def _rwkv_pre_math(c, lp, p_rkv, p_rkv_prev, p_l, p_l_prev, mu_rkv, mu_l, w0, a0, decay_up, aaa_up, gate_up):
    dlp, alp, _ = lp
    z = p_rkv + (p_rkv_prev - p_rkv) * mu_rkv
    zl = p_l + (p_l_prev - p_l) * mu_l
    r, k, v = z[:, :c], z[:, c:2 * c], z[:, 2 * c:]
    wd, ad, gd = zl[:, :dlp], zl[:, dlp:dlp + alp], zl[:, dlp + alp:]
    w = -_softplus(-(w0 + _dot(jnp.tanh(wd), decay_up, "nn"))) - 0.5
    a = _sigmoid(a0 + _dot(ad, aaa_up, "nn"))
    g = _dot(_sigmoid(gd), gate_up, "nn")
    return r, k, v, -jnp.exp(w), a, g


def _pad_to(a, n, axis):
    if a.shape[axis] == n:
        return a
    pad = [(0, 0)] * a.ndim
    pad[axis] = (0, n - a.shape[axis])
    return jnp.pad(a, pad)


def _up128(n):
    return -(-n // LANE) * LANE


def _shift_down(p):
    return jnp.concatenate([jnp.zeros((1, p.shape[1]), p.dtype), p[:-1]], axis=0)


def _shift_up(p):
    return jnp.concatenate([p[1:], jnp.zeros((1, p.shape[1]), p.dtype)], axis=0)


def _swiglu(g, u):
    return jax.nn.silu(g) * u


def _ffn_hidden(name, h, w_gate, w_up, comm=None):
    t, d = h.shape
    nb, _, n = w_gate.shape
    tm = _pick(t, (1024, 512, 256, 128))

    def body(h_r, wg_r, wu_r, g_o, u_o, a_o):
        hv = h_r[...]
        g = lax.dot_general(hv, wg_r[0], _dims("nn", 2), preferred_element_type=f32)
        u = lax.dot_general(hv, wu_r[0], _dims("nn", 2), preferred_element_type=f32)
        g_o[0] = g.astype(bf16)
        u_o[0] = u.astype(bf16)
        a_o[0] = _swiglu(g, u).astype(bf16)

    w_spec = pl.BlockSpec((1, d, n), lambda i, j: (j, 0, 0))
    o_spec = pl.BlockSpec((1, tm, n), lambda i, j: (j, i, 0))
    sh = jax.ShapeDtypeStruct((nb, t, n), bf16)
    return _pcall(body, comm=comm, name=name, grid=(t // tm, nb),
                  in_specs=[pl.BlockSpec((tm, d), lambda i, j: (i, 0)), w_spec, w_spec],
                  out_specs=[o_spec, o_spec, o_spec], out_shape=[sh, sh, sh],
                  compiler_params=_params(("parallel", "arbitrary")))(h, w_gate, w_up)


def _ffn_out(name, act, w_down, x, comm=None):
    nb, t, n = act.shape
    d = w_down.shape[2]
    tm, tn = _pick(t, (512, 256, 128)), _pick(d, (512, 256, 128))

    def body(a_r, w_r, x_r, o_r):
        acc = x_r[...]
        for j in range(nb):
            acc = acc + 0.5 * lax.dot_general(a_r[j], w_r[j], _dims("nn", 2), preferred_element_type=f32)
        o_r[...] = acc

    return _pcall(body, comm=comm, name=name, grid=(t // tm, d // tn),
                  in_specs=[pl.BlockSpec((nb, tm, n), lambda i, j: (0, i, 0)), pl.BlockSpec((nb, n, tn), lambda i, j: (0, 0, j)),
                            pl.BlockSpec((tm, tn), lambda i, j: (i, j))],
                  out_specs=pl.BlockSpec((tm, tn), lambda i, j: (i, j)), out_shape=jax.ShapeDtypeStruct((t, d), f32),
                  compiler_params=_params(("parallel", "parallel")))(act, w_down, x)


def _ffn_dhidden(name, dout, w_down, gate, up, comm=None):
    t, d = dout.shape
    nb, n, _ = w_down.shape
    tm = _pick(t, (512, 256, 128))

    def body(d_r, w_r, g_r, u_r, dg_o, du_o):
        dact = 0.5 * lax.dot_general(d_r[...].astype(MXU_DTYPE), w_r[0], _dims("nt", 2), preferred_element_type=f32)
        _, vjp = jax.vjp(_swiglu, g_r[0].astype(f32), u_r[0].astype(f32))
        dg, du = vjp(dact)
        dg_o[0] = dg.astype(bf16)
        du_o[0] = du.astype(bf16)

    o_spec = pl.BlockSpec((1, tm, n), lambda i, j: (j, i, 0))
    sh = jax.ShapeDtypeStruct((nb, t, n), bf16)
    return _pcall(body, comm=comm, name=name, grid=(t // tm, nb),
                  in_specs=[pl.BlockSpec((tm, d), lambda i, j: (i, 0)), pl.BlockSpec((1, n, d), lambda i, j: (j, 0, 0)), o_spec, o_spec],
                  out_specs=[o_spec, o_spec], out_shape=[sh, sh],
                  compiler_params=_params(("parallel", "arbitrary")))(dout, w_down, gate, up)


def _ffn_dw_down(name, act, dout, comm=None):
    nb, t, n = act.shape
    d = dout.shape[1]
    tn = _pick(d, (1024, 512, 256, 128))

    def body(a_r, d_r, o_r):
        acc = lax.dot_general(a_r[0], d_r[...].astype(MXU_DTYPE), _dims("tn", 2), preferred_element_type=f32)
        o_r[0] = (0.5 * acc).astype(bf16)

    return _pcall(body, comm=comm, name=name, grid=(nb, d // tn),
                  in_specs=[pl.BlockSpec((1, t, n), lambda j, i: (j, 0, 0)), pl.BlockSpec((t, tn), lambda j, i: (0, i))],
                  out_specs=pl.BlockSpec((1, n, tn), lambda j, i: (j, 0, i)), out_shape=jax.ShapeDtypeStruct((nb, n, d), bf16),
                  compiler_params=_params(("parallel", "parallel")))(act, dout)


def _ffn_dw_hidden(name, h, dgate, dup, comm=None):
    t, d = h.shape
    nb, _, n = dgate.shape
    tm = _pick(d, (1024, 512, 256, 128))

    def body(h_r, g_r, u_r, dg_o, du_o):
        hv = h_r[...]
        dg_o[0] = lax.dot_general(hv, g_r[0], _dims("tn", 2), preferred_element_type=f32).astype(bf16)
        du_o[0] = lax.dot_general(hv, u_r[0], _dims("tn", 2), preferred_element_type=f32).astype(bf16)

    g_spec = pl.BlockSpec((1, t, n), lambda j, i: (j, 0, 0))
    o_spec = pl.BlockSpec((1, tm, n), lambda j, i: (j, i, 0))
    sh = jax.ShapeDtypeStruct((nb, d, n), bf16)
    return _pcall(body, comm=comm, name=name, grid=(nb, d // tm),
                  in_specs=[pl.BlockSpec((t, tm), lambda j, i: (0, i)), g_spec, g_spec],
                  out_specs=[o_spec, o_spec], out_shape=[sh, sh],
                  compiler_params=_params(("parallel", "parallel")))(h, dgate, dup)


def _ffn_dh(name, dhid, w, res=None, comm=None):
    nb, t, n = dhid.shape
    d = w.shape[1]
    tm, tn = _pick(t, (512, 256, 128)), _pick(d, (512, 256, 128))

    def body(*refs):
        acc = refs[2][...] if res is not None else jnp.zeros((tm, tn), f32)
        for j in range(nb):
            acc = acc + lax.dot_general(refs[0][j], refs[1][j], _dims("nt", 2), preferred_element_type=f32)
        refs[-1][...] = acc

    in_specs = [pl.BlockSpec((nb, tm, n), lambda i, j: (0, i, 0)), pl.BlockSpec((nb, tn, n), lambda i, j: (0, j, 0))]
    args = [dhid, w]
    if res is not None:
        in_specs.append(pl.BlockSpec((tm, tn), lambda i, j: (i, j)))
        args.append(res)
    return _pcall(body, comm=comm, name=name, grid=(t // tm, d // tn), in_specs=in_specs,
                  out_specs=pl.BlockSpec((tm, tn), lambda i, j: (i, j)), out_shape=jax.ShapeDtypeStruct((t, d), f32),
                  compiler_params=_params(("parallel", "parallel")))(*args)


def _lora_bounds(c, lora):
    dl, al, gl = lora
    o1 = 3 * c
    o2, o3 = o1 + dl, o1 + dl + al
    return o1, o2, o3, o3 + gl, (_up128(dl), _up128(al), _up128(gl))


def _win_split(g8, c, lora):
    nb, d, n = g8.shape
    o1, o2, o3, o4, (dlp, alp, glp) = _lora_bounds(c, lora)
    tm = _row_tile(d, nb * n, g8.dtype.itemsize)

    def body(x, rkv_o, lora_o, swa_o):
        w = jnp.concatenate([x[j] for j in range(nb)], axis=-1)
        pad = lambda p, m: p if p.shape[1] == m else jnp.concatenate([p, jnp.zeros((p.shape[0], m - p.shape[1]), p.dtype)], axis=-1)
        rkv_o[...] = w[:, :o1]
        lora_o[...] = jnp.concatenate([pad(w[:, o1:o2], dlp), pad(w[:, o2:o3], alp), pad(w[:, o3:o4], glp)], axis=-1)
        swa_o[...] = w[:, o4:]

    widths = (o1, dlp + alp + glp, nb * n - o4)
    return _pcall(body, name="w_in_split", grid=(d // tm,), in_specs=[pl.BlockSpec((nb, tm, n), lambda i: (0, i, 0))],
                  out_specs=[pl.BlockSpec((tm, wd), lambda i: (i, 0)) for wd in widths],
                  out_shape=[jax.ShapeDtypeStruct((d, wd), g8.dtype) for wd in widths],
                  compiler_params=_params(("parallel",)))(g8)


def _win_merge(dw_rkv, dw_lora, dw_swa, c, lora, nb):
    d = dw_rkv.shape[0]
    o1, o2, o3, o4, (dlp, alp, glp) = _lora_bounds(c, lora)
    dl, al, gl = lora
    total = o4 + dw_swa.shape[1]
    n = total // nb
    tm = _row_tile(d, total, dw_rkv.dtype.itemsize)

    def body(a, b, s, o):
        bv = b[...]
        w = jnp.concatenate([a[...], bv[:, :dl], bv[:, dlp:dlp + al], bv[:, dlp + alp:dlp + alp + gl], s[...]], axis=-1)
        for j in range(nb):
            o[j] = w[:, n * j:n * (j + 1)]

    ins = [dw_rkv, dw_lora, dw_swa]
    return _pcall(body, name="w_in_merge", grid=(d // tm,), in_specs=[pl.BlockSpec((tm, a.shape[1]), lambda i: (i, 0)) for a in ins],
                  out_specs=pl.BlockSpec((nb, tm, n), lambda i: (0, i, 0)), out_shape=jax.ShapeDtypeStruct((nb, d, n), dw_rkv.dtype),
                  compiler_params=_params(("parallel",)))(*ins)


def _norm_bwd(name, x, g_norm, dh, dres, comm=None):
    d = x.shape[1]

    def fn(xb, dhb, drb, g):
        _, vjp = jax.vjp(_rms, xb, g)
        dx, dg = vjp(dhb)
        return drb + dx, dg

    return _rows(name, fn, [x, dh, dres], [g_norm], [(d, f32)], [((1, d), f32)], comm=comm)


def _colsum(name, a):
    return _rows(name, lambda ab: (jnp.sum(ab.astype(f32), axis=0, keepdims=True),), [a], [], [], [((1, a.shape[1]), f32)])[0]


def kernel(x, mem, f1_norm, f1_gate, f1_up, f1_down, mix_norm, w_in, b_in_attn, rw_mu, rw_w0, rw_decay_up, rw_a0, rw_aaa_up, rw_gate_up, rw_k_k, rw_k_a, rw_r_k, rw_lnx_w, rw_lnx_b, attn_sinks, w_out, b_out, xa_norm, mem_norm, w_xq, w_xkv, w_xo, f2_norm, f2_gate, f2_up, f2_down, final_norm, loss_target, m_f1_norm, m_f1_gate, m_f1_up, m_f1_down, m_mix_norm, m_w_in, m_b_in_attn, m_rw_mu, m_rw_w0, m_rw_decay_up, m_rw_a0, m_rw_aaa_up, m_rw_gate_up, m_rw_k_k, m_rw_k_a, m_rw_r_k, m_rw_lnx_w, m_rw_lnx_b, m_attn_sinks, m_w_out, m_b_out, m_xa_norm, m_mem_norm, m_w_xq, m_w_xkv, m_w_xo, m_f2_norm, m_f2_gate, m_f2_up, m_f2_down, m_final_norm, v_f1_norm, v_f1_gate, v_f1_up, v_f1_down, v_mix_norm, v_w_in, v_b_in_attn, v_rw_mu, v_rw_w0, v_rw_decay_up, v_rw_a0, v_rw_aaa_up, v_rw_gate_up, v_rw_k_k, v_rw_k_a, v_rw_r_k, v_rw_lnx_w, v_rw_lnx_b, v_attn_sinks, v_w_out, v_b_out, v_xa_norm, v_mem_norm, v_w_xq, v_w_xkv, v_w_xo, v_f2_norm, v_f2_gate, v_f2_up, v_f2_down, v_final_norm):
    names = ["f1_norm", "f1_gate", "f1_up", "f1_down", "mix_norm", "w_in", "b_in_attn", "rw_mu", "rw_w0", "rw_decay_up",
             "rw_a0", "rw_aaa_up", "rw_gate_up", "rw_k_k", "rw_k_a", "rw_r_k", "rw_lnx_w", "rw_lnx_b", "attn_sinks", "w_out",
             "b_out", "xa_norm", "mem_norm", "w_xq", "w_xkv", "w_xo", "f2_norm", "f2_gate", "f2_up", "f2_down", "final_norm"]
    env = dict(locals())
    w_of = {k: env[k] for k in names}
    m_of = {k: env["m_" + k] for k in names}
    v_of = {k: env["v_" + k] for k in names}
    col_sharded = ["f1_gate", "f1_up", "w_in", "rw_decay_up", "rw_aaa_up", "rw_gate_up", "w_xkv", "f2_gate", "f2_up"]
    row_sharded = ["f1_down", "w_out", "w_xq", "w_xo", "f2_down"]
    sharded = col_sharded + row_sharded
    small = [k for k in names if k not in sharded]

    x0, mem0, tgt = x[0], mem[0], loss_target[0]
    t, d = x0.shape
    c = rw_w0.shape[-1]
    heads = c // HEAD_DIM
    dl, al, gl = rw_decay_up.shape[1], rw_aaa_up.shape[1], rw_gate_up.shape[1]
    dlp, alp, glp = _up128(dl), _up128(al), _up128(gl)
    swa_w = d - c
    hq, kvh = swa_w // HEAD_DIM, (b_in_attn.shape[-1] - swa_w) // (2 * HEAD_DIM)
    my_x, my_y, my_c = _position()
    c_idx = jnp.reshape(my_c, (1,)).astype(jnp.int32)
    chip_idx = jnp.reshape(2 * my_x + my_y, (1,)).astype(jnp.int32)

    shard2d = {k: w_of[k][0] for k in sharded}
    cast = {k: _rows("cast_" + k, lambda a: (a,), [shard2d[k]], [], [(shard2d[k].shape[1], bf16)], tm=_row_tile(*shard2d[k].shape))[0]
            for k in sharded}
    ffn1_keys, ffn2_keys = ["f1_gate", "f1_up", "f1_down"], ["f2_gate", "f2_up", "f2_down"]
    in_keys = ["w_in", "rw_decay_up", "rw_aaa_up", "rw_gate_up"]
    kept_in_blocks = ffn1_keys + ffn2_keys + ["w_in", "w_xkv"]

    def whole(k, g8):
        if k in kept_in_blocks:
            return g8
        if k in col_sharded:
            return g8.transpose(1, 0, 2).reshape(g8.shape[1], N_DEV * g8.shape[2])
        return g8.reshape(N_DEV * g8.shape[1], g8.shape[2])

    def gather_of(keys):
        return _gather_comm([cast[k] for k in keys])

    def wholes(keys, gathered):
        return {k: whole(k, g8) for k, g8 in zip(keys, gathered)}

    (h1,), gathered = _rows("f1_norm", lambda xb, g: (_rms(xb, g),), [x0], [f1_norm], [(d, bf16)], comm=gather_of(ffn1_keys[:2]))
    full = wholes(ffn1_keys[:2], gathered)
    (gate1, up1, act1), gathered = _ffn_hidden("f1_hidden", h1, full["f1_gate"], full["f1_up"], comm=gather_of(["f1_down"]))
    full.update(wholes(["f1_down"], gathered))
    x1, gathered = _ffn_out("f1_out", act1, full["f1_down"], x0, comm=gather_of(in_keys))
    full.update(wholes(in_keys, gathered))
    ffn1_saved = (h1, gate1, up1, act1)
    w_rkv, w_lora, w_swa = _win_split(full["w_in"], c, (dl, al, gl))
    o1, o2, o3, shift_cols, _ = _lora_bounds(c, (dl, al, gl))
    mu_rkv = rw_mu[:, :3 * c]
    mu_l = jnp.concatenate([_pad_to(rw_mu[:, o1:o2], dlp, 1), _pad_to(rw_mu[:, o2:o3], alp, 1),
                            _pad_to(rw_mu[:, o3:shift_cols], glp, 1)], axis=1)
    decay_up = _pad_to(full["rw_decay_up"], dlp, 0).astype(f32)
    aaa_up = _pad_to(full["rw_aaa_up"], alp, 0).astype(f32)
    gate_up = _pad_to(full["rw_gate_up"], glp, 0).astype(f32)
    head_pars = [p.reshape(heads, 1, HEAD_DIM) for p in (rw_k_k, rw_k_a, rw_r_k, rw_lnx_w, rw_lnx_b)]
    final_g = final_norm.reshape(1, d)

    (h2,) = _rows("mix_norm", lambda xb, g: (_rms(xb, g),), [x1], [mix_norm], [(d, bf16)])
    p_rkv, gathered = _mm("in_rkv", h2, w_rkv, "nn", f32, comm=gather_of(["w_out"]))
    full.update(wholes(["w_out"], gathered))
    p_l =_mm("in_lora", h2, w_lora, "nn", f32)
    za = _mm("in_swa", h2, w_swa, "nn", f32, bias=b_in_attn)
    pre_fn = functools.partial(_rwkv_pre_math, c, (dlp, alp, glp))
    pre_rows = [p_rkv, _shift_down(p_rkv), p_l, _shift_down(p_l)]
    pre_full = [mu_rkv, mu_l, rw_w0, rw_a0, decay_up, aaa_up, gate_up]
    (r_h, k_h, v_h, lw_h, a_h, g_t), gathered = _rows("rwkv_pre", pre_fn, pre_rows, pre_full,
                                                       [(c, f32, HEAD_DIM)] * 5 + [(c, f32)], tm=128, comm=gather_of(["w_xq"]))
    full.update(wholes(["w_xq"], gathered))
    seqs = [r_h, k_h, v_h, a_h, lw_h]
    (y_heads, checkpoints), gathered = _rwkv_fwd(*seqs, head_pars, comm=gather_of(["f2_gate"]))
    full.update(wholes(["f2_gate"], gathered))

    pos = jnp.arange(t, dtype=f32)
    inv_freq = ROPE_THETA ** (-jnp.arange(0, HEAD_DIM, 2, dtype=f32) / HEAD_DIM)
    ang = pos[:, None] * inv_freq[None, :]
    cos, sin = jnp.cos(ang), jnp.sin(ang)
    sinks3 = attn_sinks.reshape(kvh, hq // kvh, 1)
    swa_in = (*_swa_split(za, hq, kvh), cos, sin, sinks3)
    y_swa_heads, gathered = _swa_fwd(*swa_in, comm=gather_of(["f2_up"]))
    full.update(wholes(["f2_up"], gathered))
    (ycat,) = _rows("mix_cat", lambda yb, gb, sb: (jnp.concatenate([yb * gb, sb], axis=1),), [y_heads, g_t, y_swa_heads], [],
                    [(d, bf16)])
    x2, gathered = _mm("mix_out", ycat, full["w_out"], "nn", f32, res=x1, bias=b_out, comm=gather_of(["w_xkv"]))
    full.update(wholes(["w_xkv"], gathered))

    (h3,) = _rows("xa_norm", lambda xb, g: (_rms(xb, g),), [x2], [xa_norm], [(d, bf16)])
    (mem_n,) = _rows("mem_norm", lambda xb, g: (_rms(xb, g),), [mem0], [mem_norm], [(d, bf16)])
    q_x, gathered = _mm("xa_q", h3, full["w_xq"], "nn", bf16, comm=gather_of(["w_xo"]))
    full.update(wholes(["w_xo"], gathered))
    kv_x = _mm("xa_kv", mem_n, full["w_xkv"], "nn", bf16)
    o_x = _xattn_fwd(q_x, kv_x)
    x3 = _mm("xa_out", o_x, full["w_xo"], "nn", f32, res=x2)
    (h4,) = _rows("f2_norm", lambda xb, g: (_rms(xb, g),), [x3], [f2_norm], [(d, bf16)])
    (gate2, up2, act2), gathered = _ffn_hidden("f2_hidden", h4, full["f2_gate"], full["f2_up"], comm=gather_of(["f2_down"]))
    full.update(wholes(["f2_down"], gathered))
    x4 = _ffn_out("f2_out", act2, full["f2_down"], x3)
    ffn2_saved = (h4, gate2, up2, act2)

    def loss_fn(xb, tb, g):
        def per_row(xv, gv):
            return 0.5 * jnp.mean(jnp.square(_rms(xv, gv) - tb), axis=-1, keepdims=True)

        lrow, vjp = jax.vjp(per_row, xb, g)
        dxb, dgb = vjp(jnp.ones_like(lrow))
        return dxb, dgb, jnp.sum(lrow, axis=0, keepdims=True)

    dx4, d_final, loss_part = _rows("loss", loss_fn, [x4, tgt], [final_g], [(d, f32)], [((1, d), f32), ((1, 1), f32)])
    loss = lax.psum(loss_part[0, 0], ("x", "y", "c"))

    grads, small_g, out = {}, {"final_norm": d_final}, {}

    def pair_sums_of(tag, keys, carrier=None):
        blocks = []
        for k in keys:
            g2 = grads[k]
            rr, cc = shard2d[k].shape
            if k in kept_in_blocks:
                blocks.append(g2)
            else:
                blocks.append(g2.reshape(g2.shape[0], N_DEV, cc).transpose(1, 0, 2) if k in col_sharded else g2.reshape(N_DEV, rr, cc))
        if carrier is None:
            from_sibling = _comm_only("grads_to_sibling_" + tag, _sibling_comm(blocks))
        else:
            carried, from_sibling = carrier(_sibling_comm(blocks))
        pairs = [_pair_add("pair_add_" + k, b, o, c_idx) for k, b, o in zip(keys, blocks, from_sibling)]
        return pairs if carrier is None else (pairs, carried)

    def update(keys, pair_sums, from_chips):
        for k, part, others in zip(keys, pair_sums, from_chips):
            res = _adam_sharded("adam_" + k, shard2d[k], m_of[k][0], v_of[k][0], part, others, chip_idx)
            out[k] = [a.reshape(w_of[k].shape) for a in res]

    def ffn_backward(tag, keys, xin, g_norm, saved, dout, first_comm, start_exchange):
        h, gate, up, act = saved
        k_gate, k_up, k_down = keys
        if first_comm is None:
            grads[k_down], carried = _ffn_dw_down(tag + "_dw_down", act, dout), None
        else:
            grads[k_down], carried = _ffn_dw_down(tag + "_dw_down", act, dout, comm=first_comm)
        down_pairs, (dgate, dup) = pair_sums_of(
            k_down, [k_down], lambda cm: _ffn_dhidden(tag + "_dhidden", dout, full[k_down], gate, up, comm=cm))
        (grads[k_gate], grads[k_up]), from_chips = _ffn_dw_hidden(tag + "_dw_hidden", h, dgate, dup, comm=_chips_comm(down_pairs))
        update([k_down], down_pairs, from_chips)
        hidden_pairs, dh = pair_sums_of(tag + "_hidden", [k_gate, k_up],
                                        lambda cm: _ffn_dh(tag + "_dh1", dgate, full[k_gate], comm=cm))
        pending = None
        if start_exchange:
            pending, dh = _chips_start("grads_to_chips_start_" + tag, hidden_pairs, dh)
        dh = _ffn_dh(tag + "_dh2", dup, full[k_up], res=dh)
        dx, dg_norm = _norm_bwd(tag + "_dnorm", xin, g_norm, dh, dout)
        return dx, dg_norm, hidden_pairs, carried, pending

    xa_keys = ["w_xq", "w_xkv", "w_xo"]
    dx3, small_g["f2_norm"], ffn2_pairs, _, _ = ffn_backward("f2b", ffn2_keys, x3, f2_norm, ffn2_saved, dx4, None, False)

    do_x = _mm("xa_do", dx3, full["w_xo"], "nt", bf16)
    grads["w_xo"] = _mm("xa_dwo", o_x, dx3, "tn", bf16)
    dq_x, dk_x, dv_x = _xattn_bwd(q_x, kv_x, do_x)
    grads["w_xq"] = _mm("xa_dwq", h3, dq_x, "tn", bf16)
    dh3 = _mm("xa_dh", dq_x, full["w_xq"], "nt", f32)
    dkv_x = jnp.concatenate([dk_x, dv_x], axis=1)
    grads["w_xkv"] = _mm("xa_dwkv", mem_n, dkv_x, "tn", bf16, out_blocks=N_DEV)
    dkv_blocks = dkv_x.astype(bf16).reshape(dkv_x.shape[0], N_DEV, -1).transpose(1, 0, 2)
    dmem_n = _ffn_dh("xa_dmem", dkv_blocks, full["w_xkv"])
    _, small_g["mem_norm"] = _norm_bwd("mem_dnorm", mem0, mem_norm, dmem_n, jnp.zeros_like(mem0))
    xa_pairs, (dx2, small_g["xa_norm"]) = pair_sums_of(
        "xa", xa_keys, lambda cm: _norm_bwd("xa_dnorm", x2, xa_norm, dh3, dx3, comm=cm))

    dycat = _mm("mix_dy", dx2, full["w_out"], "nt", f32)
    grads["w_out"] = _mm("mix_dwout", ycat, dx2, "tn", bf16)
    small_g["b_out"] = _colsum("mix_dbout", dx2)
    out_pairs, (dy_heads, dg_t, do_sw) = pair_sums_of("out", ["w_out"], lambda cm: _rows(
        "mix_dgate", lambda db, yb, gb: (db[:, :c] * gb, db[:, :c] * yb, db[:, c:]), [dycat, y_heads, g_t], [],
        [(c, f32, HEAD_DIM), (c, f32), (swa_w, f32, HEAD_DIM)], comm=cm))
    rw_grads, from_chips = _rwkv_bwd(*seqs, head_pars, checkpoints, dy_heads, comm=_chips_comm(ffn2_pairs))
    update(ffn2_keys[:2], ffn2_pairs, from_chips)
    dr_h, dk_h, dv_h, da_h, dlw_h = rw_grads[:5]
    for nm, gh in zip(("rw_k_k", "rw_k_a", "rw_r_k", "rw_lnx_w", "rw_lnx_b"), rw_grads[5:]):
        small_g[nm] = gh.reshape(w_of[nm].shape)

    def pre_bwd(*args):
        _, vjp = jax.vjp(pre_fn, *args[:4], *args[10:])
        return vjp(tuple(args[4:10]))

    pre_cts = [dr_h, dk_h, dv_h, dlw_h, da_h, dg_t]
    pre_out = _rows("rwkv_pre_bwd", pre_bwd, pre_rows + pre_cts, pre_full,
                    [(3 * c, f32), (3 * c, f32), (dlp + alp + glp, f32), (dlp + alp + glp, f32)],
                    [(p.shape, f32) for p in pre_full], tm=128)
    dp_rkv = pre_out[0] + _shift_up(pre_out[1])
    dp_l = pre_out[2] + _shift_up(pre_out[3])
    dmu_rkv, dmu_l, small_g["rw_w0"], small_g["rw_a0"], d_decay_up, d_aaa_up, d_gate_up = pre_out[4:]
    small_g["rw_mu"] = jnp.concatenate([dmu_rkv, dmu_l[:, :dl], dmu_l[:, dlp:dlp + al], dmu_l[:, dlp + alp:dlp + alp + gl]], axis=1)
    grads["rw_decay_up"] = d_decay_up[:dl].astype(bf16)
    grads["rw_aaa_up"] = d_aaa_up[:al].astype(bf16)
    grads["rw_gate_up"] = d_gate_up[:gl].astype(bf16)

    sw, from_chips = _swa_bwd(*swa_in, do_sw, comm=_chips_comm(xa_pairs))
    update(xa_keys, xa_pairs, from_chips)
    small_g["attn_sinks"] = sw[8].reshape(attn_sinks.shape)
    dza, small_g["b_in_attn"] = _swa_merge(sw[0], sw[1], sw[3], sw[2], sw[5], sw[4], sw[7], sw[6])

    dw_rkv = _mm("in_dwrkv", h2, dp_rkv, "tn", bf16)
    dw_l = _mm("in_dwlora", h2, dp_l, "tn", bf16)
    dw_swa = _mm("in_dwswa", h2, dza, "tn", bf16)
    grads["w_in"] = _win_merge(dw_rkv, dw_l, dw_swa, c, (dl, al, gl), N_DEV)
    dh2, from_chips = _mm("in_dh1", dp_rkv, w_rkv, "nt", f32, comm=_chips_comm(out_pairs))
    update(["w_out"], out_pairs, from_chips)
    dh2 = _mm("in_dh2", dp_l, w_lora, "nt", f32, res=dh2)
    in_pairs, dh2 = pair_sums_of("in", in_keys, lambda cm: _mm("in_dh3", dza, w_swa, "nt", f32, res=dh2, comm=cm))
    dx1, small_g["mix_norm"] = _norm_bwd("mix_dnorm", x1, mix_norm, dh2, dx2)

    dx0, small_g["f1_norm"], _, from_chips, pending = ffn_backward(
        "f1b", ffn1_keys, x0, f1_norm, ffn1_saved, dx1, _chips_comm(in_pairs), True)
    update(in_keys, in_pairs, from_chips)
    others_done = _fence("updates_done", [res[1] for res in out.values()] + [dx0] + list(small_g.values()))
    pairs, from_chips = _chips_wait("grads_to_chips_wait_f1b", *pending, others_done)
    update(ffn1_keys[:2], pairs, from_chips)

    sizes = [int(w_of[k].size) for k in small]
    total = sum(sizes)
    cols = -(-total // (8 * LANE)) * LANE

    def pack(parts_of):
        flat = jnp.concatenate([parts_of[k].reshape(-1).astype(f32) for k in small])
        return _pad_to(flat, 8 * cols, 0).reshape(8, cols)

    (all_parts,) = _comm_only("gather_small_grads", _gather_comm([pack(small_g)], after=from_chips))
    res = _adam_small("adam_small", pack(w_of), pack(m_of), pack(v_of), all_parts)
    offs = 0
    flat_res = [a.reshape(-1) for a in res]
    for k, sz in zip(small, sizes):
        out[k] = [a[offs:offs + sz].reshape(w_of[k].shape) for a in flat_res]
        offs += sz

    outs = [loss, dx0.reshape(x.shape)]
    for j in range(4):
        outs += [out[k][j] for k in names]
    return tuple(outs)
```

```python
import functools
import math

import jax
import jax.numpy as jnp
from jax import lax
from jax.experimental import pallas as pl
from jax.experimental.pallas import tpu as pltpu

f32 = jnp.float32
bf16 = jnp.bfloat16
MXU_DTYPE = jnp.bfloat16

HEAD_DIM = 64
SWA_BLOCK = 128
ROPE_THETA = 10000.0
XATTN_HEADS = 4
RMS_EPS = 1e-6
GN_EPS = 64e-5
NEG_INF = -1e30
RWKV_CHUNK = 64

ADAM_LR = 0.001
ADAM_B1 = 0.9
ADAM_B2 = 0.999
ADAM_EPS = 1e-08
ADAM_WD = 0.01
ADAM_STEP = 10

N_DEV = 8
LANE = 128
VMEM_LIMIT_BYTES = 56 * 1024 * 1024
MM_VMEM_BUDGET = 40 * 1024 * 1024
MESH = pl.DeviceIdType.MESH


def _params(sem):
    return pltpu.CompilerParams(dimension_semantics=sem, vmem_limit_bytes=VMEM_LIMIT_BYTES)


class _Comm:
    def __init__(self, ins, outs, n_remote, n_local, start, finish):
        self.ins, self.outs, self.n_remote, self.n_local = list(ins), list(outs), n_remote, max(n_local, 1)
        self.start, self.finish = start, finish


def _pcall(body, comm=None, **kw):
    kw.setdefault("compiler_params", pltpu.CompilerParams(vmem_limit_bytes=VMEM_LIMIT_BYTES))
    if comm is None:
        return pl.pallas_call(body, **kw)
    single = not isinstance(kw["out_shape"], (list, tuple))
    out_shape = [kw["out_shape"]] if single else list(kw["out_shape"])
    out_specs = [kw["out_specs"]] if single else list(kw["out_specs"])
    in_specs, scratch, grid = list(kw["in_specs"]), list(kw.get("scratch_shapes", ())), tuple(kw.get("grid", ()))
    n_in, n_out, n_ci, n_co, n_scr = len(in_specs), len(out_shape), len(comm.ins), len(comm.outs), len(scratch)

    def wrapped(*refs):
        ins, c_ins = refs[:n_in], refs[n_in:n_in + n_ci]
        outs = refs[n_in + n_ci:n_in + n_ci + n_out]
        c_outs = refs[n_in + n_ci + n_out:n_in + n_ci + n_out + n_co]
        rest = refs[n_in + n_ci + n_out + n_co:]
        scr, sems = rest[:n_scr], rest[n_scr:]
        if grid:
            ids = [pl.program_id(k) for k in range(len(grid))]
            first = functools.reduce(jnp.logical_and, [i == 0 for i in ids])
            last = functools.reduce(jnp.logical_and, [i == g - 1 for i, g in zip(ids, grid)])
            pl.when(first)(lambda: comm.start(c_ins, c_outs, *sems))
            body(*ins, *outs, *scr)
            pl.when(last)(lambda: comm.finish(c_ins, c_outs, *sems))
        else:
            comm.start(c_ins, c_outs, *sems)
            body(*ins, *outs, *scr)
            comm.finish(c_ins, c_outs, *sems)

    any_spec = pl.BlockSpec(memory_space=pl.ANY)
    kw.update(in_specs=in_specs + [any_spec] * n_ci, out_specs=out_specs + [any_spec] * n_co,
              out_shape=out_shape + comm.outs,
              scratch_shapes=scratch + [pltpu.SemaphoreType.DMA((comm.n_remote,)), pltpu.SemaphoreType.DMA((comm.n_remote,)),
                                        pltpu.SemaphoreType.DMA((comm.n_local,))])
    if grid:
        kw["compiler_params"] = _params(("arbitrary",) * len(grid))
    call = pl.pallas_call(wrapped, **kw)

    def run(*args):
        res = call(*args, *comm.ins)
        return (res[0] if single else list(res[:n_out])), list(res[n_out:])

    return run


def _fence(name, arrays):
    def body(*refs):
        refs[-1][...] = jnp.zeros(refs[-1].shape, f32)

    return _pcall(body, name=name, in_specs=[pl.BlockSpec(memory_space=pl.ANY)] * len(arrays),
                  out_specs=pl.BlockSpec(memory_space=pltpu.VMEM), out_shape=jax.ShapeDtypeStruct((8, LANE), f32))(*arrays)


def _comm_only(name, comm):
    return _pcall(lambda: None, comm=comm, name=name, in_specs=[], out_specs=[], out_shape=[])()[1]


def _dims(kind, ndim):
    o = ndim - 2
    batch = ((0,), (0,)) if o else ((), ())
    c = {"nn": ((1 + o,), (o,)), "nt": ((1 + o,), (1 + o,)), "tn": ((o,), (o,))}[kind]
    return (c, batch)


def _dot_raw(x, y, kind):
    return lax.dot_general(x.astype(MXU_DTYPE), y.astype(MXU_DTYPE), _dims(kind, x.ndim), preferred_element_type=f32)


@functools.partial(jax.custom_vjp, nondiff_argnums=(2,))
def _dot(x, y, kind):
    return _dot_raw(x, y, kind)


def _dot_fwd(x, y, kind):
    return _dot_raw(x, y, kind), (x, y)


def _dot_bwd(kind, res, g):
    x, y = res
    if kind == "nn":
        dx, dy = _dot(g, y, "nt"), _dot(x, g, "tn")
    elif kind == "nt":
        dx, dy = _dot(g, y, "nn"), _dot(g, x, "tn")
    else:
        dx, dy = _dot(y, g, "nt"), _dot(x, g, "nn")
    return dx.astype(x.dtype), dy.astype(y.dtype)


_dot.defvjp(_dot_fwd, _dot_bwd)


def _split3(x):
    a = x.astype(bf16)
    r = x - a.astype(f32)
    b = r.astype(bf16)
    c = (r - b.astype(f32)).astype(bf16)
    return a, b, c


def _rms(x, g):
    x = x.astype(f32)
    return x * lax.rsqrt(jnp.mean(x * x, axis=-1, keepdims=True) + RMS_EPS) * g


def _sigmoid(x):
    return 1.0 / (1.0 + jnp.exp(-x))


def _softplus(x):
    return jnp.maximum(x, 0.0) + jnp.log(1.0 + jnp.exp(-jnp.abs(x)))


def _rows(name, fn, row_ins, full_ins, row_outs, acc_outs=(), tm=None, comm=None):
    width = lambda a: a.shape[1] if a.ndim == 2 else a.shape[0] * a.shape[2]
    rows = row_ins[0].shape[0] if row_ins[0].ndim == 2 else row_ins[0].shape[1]
    if tm is None:
        tm = _row_tile(rows, max([width(a) for a in row_ins] + [o[0] for o in row_outs]))
    tm = min(tm, rows)
    assert rows % tm == 0, (name, rows, tm)
    n_in = len(row_ins) + len(full_ins)
    n_o, n_a = len(row_outs), len(acc_outs)

    def load(k, ref):
        if k < len(row_ins) and row_ins[k].ndim == 3:
            return jnp.concatenate([ref[h] for h in range(ref.shape[0])], axis=-1)
        return ref[...]

    def body(*refs):
        vals = [load(k, r) for k, r in enumerate(refs[:n_in])]
        outs = fn(*vals)
        o_refs = refs[n_in:n_in + n_o]
        a_refs = refs[n_in + n_o:]
        for k in range(n_o):
            if len(row_outs[k]) == 3:
                n = row_outs[k][2]
                for h in range(row_outs[k][0] // n):
                    o_refs[k][h] = outs[k][:, h * n:(h + 1) * n].astype(o_refs[k].dtype)
            else:
                o_refs[k][...] = outs[k].astype(o_refs[k].dtype)
        if n_a:
            first = pl.program_id(0) == 0

            @pl.when(first)
            def _():
                for k in range(n_a):
                    a_refs[k][...] = outs[n_o + k].astype(a_refs[k].dtype)

            @pl.when(jnp.logical_not(first))
            def _():
                for k in range(n_a):
                    a_refs[k][...] += outs[n_o + k].astype(a_refs[k].dtype)

    by_rows = lambda cols: pl.BlockSpec((tm, cols), lambda i: (i, 0))
    by_heads = lambda h, n: pl.BlockSpec((h, tm, n), lambda i: (0, i, 0))
    in_specs = [by_rows(a.shape[1]) if a.ndim == 2 else by_heads(a.shape[0], a.shape[2]) for a in row_ins]
    in_specs += [pl.BlockSpec(a.shape, lambda i, nd=a.ndim: (0,) * nd) for a in full_ins]
    out_specs = [by_rows(o[0]) if len(o) == 2 else by_heads(o[0] // o[2], o[2]) for o in row_outs]
    out_specs += [pl.BlockSpec(s, lambda i, nd=len(s): (0,) * nd) for s, _ in acc_outs]
    out_shape = [jax.ShapeDtypeStruct((rows, o[0]) if len(o) == 2 else (o[0] // o[2], rows, o[2]), o[1]) for o in row_outs]
    out_shape += [jax.ShapeDtypeStruct(s, d) for s, d in acc_outs]
    return _pcall(body, comm=comm, name=name, grid=(rows // tm,), in_specs=in_specs, out_specs=out_specs, out_shape=out_shape,
                  compiler_params=_params(("arbitrary",)))(*row_ins, *full_ins)


def _pick(n, cands):
    for c in cands:
        if n % c == 0:
            return c
    return n


def _mm(name, a, b, mode, out_dtype, scale=1.0, res=None, bias=None, comm=None, out_blocks=None):
    b_blocks = b.ndim == 3
    if b_blocks:
        assert mode == "nn"
        (m, k), (nb, k2, tn) = a.shape, b.shape
        n = nb * tn
    elif mode == "nn":
        (m, k), (k2, n) = a.shape, b.shape
    elif mode == "nt":
        (m, k), (n, k2) = a.shape, b.shape
    else:
        (k, m), (k2, n) = a.shape, b.shape
    assert k == k2, (name, a.shape, b.shape, mode)
    if not b_blocks:
        tn = n // out_blocks if out_blocks else _pick(n, (512, 256, 128))
    tm = _pick(m, (1024, 512, 256, 128))

    def need(tm_):
        by = tm_ * k * a.dtype.itemsize + tn * k * b.dtype.itemsize + tm_ * tn * (jnp.dtype(out_dtype).itemsize + 4)
        if res is not None:
            by += tm_ * tn * res.dtype.itemsize
        return 2 * by

    while need(tm) > MM_VMEM_BUDGET and tm % 256 == 0:
        tm //= 2
    dims = _dims(mode, 2)

    def body(*refs):
        bv = refs[1][0] if b_blocks else refs[1][...]
        acc = lax.dot_general(refs[0][...].astype(MXU_DTYPE), bv.astype(MXU_DTYPE), dims, preferred_element_type=f32)
        if scale != 1.0:
            acc = acc * scale
        pos = 2
        if bias is not None:
            acc = acc + refs[pos][...]
            pos += 1
        if res is not None:
            acc = acc + refs[pos][...].astype(f32)
            pos += 1
        if out_blocks:
            refs[pos][0] = acc.astype(out_dtype)
        else:
            refs[pos][...] = acc.astype(out_dtype)

    a_spec = pl.BlockSpec((k, tm), lambda i, j: (0, i)) if mode == "tn" else pl.BlockSpec((tm, k), lambda i, j: (i, 0))
    if b_blocks:
        b_spec = pl.BlockSpec((1, k, tn), lambda i, j: (j, 0, 0))
    else:
        b_spec = pl.BlockSpec((tn, k), lambda i, j: (j, 0)) if mode == "nt" else pl.BlockSpec((k, tn), lambda i, j: (0, j))
    in_specs, args = [a_spec, b_spec], [a, b]
    if bias is not None:
        in_specs.append(pl.BlockSpec((1, tn), lambda i, j: (0, j)))
        args.append(bias)
    if res is not None:
        in_specs.append(pl.BlockSpec((tm, tn), lambda i, j: (i, j)))
        args.append(res)
    if out_blocks:
        out_spec, out_shape = pl.BlockSpec((1, tm, tn), lambda i, j: (j, i, 0)), jax.ShapeDtypeStruct((out_blocks, m, tn), out_dtype)
    else:
        out_spec, out_shape = pl.BlockSpec((tm, tn), lambda i, j: (i, j)), jax.ShapeDtypeStruct((m, n), out_dtype)
    return _pcall(body, comm=comm, name=name, grid=(m // tm, n // tn), in_specs=in_specs, out_specs=out_spec, out_shape=out_shape,
                  compiler_params=_params(("parallel", "parallel")))(*args)


def _position():
    return lax.axis_index("x"), lax.axis_index("y"), lax.axis_index("c")


def _gather_comm(shards, after=()):
    n = len(shards)

    def plan(x_refs, o_refs, send_sems, recv_sems, local_sems):
        x, y, c = _position()
        me, sibling = (x, y, c), (x, y, 1 - c)
        chips = [(1 - x, y), (x, 1 - y), (1 - x, 1 - y)]

        def slot(px, py, pc):
            return 4 * px + 2 * py + pc

        def copy(t, k, block, to, src=None):
            dst = o_refs[t].at[slot(*block)]
            return pltpu.make_async_remote_copy(src_ref=dst if src is None else src, dst_ref=dst,
                                                send_sem=send_sems.at[7 * t + k], recv_sem=recv_sems.at[7 * t + k],
                                                device_id=to, device_id_type=MESH)

        mine = [pltpu.make_async_copy(x_refs[t], o_refs[t].at[slot(*me)], local_sems.at[t]) for t in range(n)]
        first = []
        for t in range(n):
            first.append(copy(t, 0, me, sibling, src=x_refs[t]))
            first += [copy(t, 1 + j, me, (*chip, c), src=x_refs[t]) for j, chip in enumerate(chips)]
        return me, sibling, chips, c, copy, mine, first

    def start(*refs):
        _, _, _, _, _, mine, first = plan(*refs)
        for cp in mine + first:
            cp.start()

    def finish(*refs):
        me, sibling, chips, c, copy, mine, first = plan(*refs)
        passed = []
        for t in range(n):
            for j, chip in enumerate(chips):
                copy(t, 1 + j, (*chip, c), me).wait_recv()
                cp = copy(t, 4 + j, (*chip, c), sibling)
                cp.start()
                passed.append(cp)
        for t in range(n):
            copy(t, 0, sibling, me).wait_recv()
            for j, chip in enumerate(chips):
                copy(t, 4 + j, (*chip, 1 - c), me).wait_recv()
        for cp in first + passed:
            cp.wait_send()
        for cp in mine:
            cp.wait()

    outs = [jax.ShapeDtypeStruct((N_DEV,) + s.shape, s.dtype) for s in shards]
    return _Comm(list(shards) + list(after), outs, 7 * n, n, start, finish)


def _sibling_comm(blocks):
    n = len(blocks)

    def copies(g_refs, o_refs, send_sems, recv_sems, _):
        x, y, c = _position()
        return [pltpu.make_async_remote_copy(src_ref=g_refs[t].at[2 * q + 1 - c], dst_ref=o_refs[t].at[q],
                                             send_sem=send_sems.at[4 * t + q], recv_sem=recv_sems.at[4 * t + q],
                                             device_id=(x, y, 1 - c), device_id_type=MESH)
                for t in range(n) for q in range(4)]

    def start(*refs):
        for cp in copies(*refs):
            cp.start()

    def finish(*refs):
        for cp in copies(*refs):
            cp.wait()

    outs = [jax.ShapeDtypeStruct((4,) + g.shape[1:], g.dtype) for g in blocks]
    return _Comm(blocks, outs, 4 * n, 0, start, finish)


def _chips_comm(parts):
    n = len(parts)

    def copies(p_refs, o_refs, send_sems, recv_sems, _):
        x, y, c = _position()
        chips = [(1 - x, y), (x, 1 - y), (1 - x, 1 - y)]
        return [pltpu.make_async_remote_copy(src_ref=p_refs[t].at[2 * px + py], dst_ref=o_refs[t].at[j],
                                             send_sem=send_sems.at[3 * t + j], recv_sem=recv_sems.at[3 * t + j],
                                             device_id=(px, py, c), device_id_type=MESH)
                for t in range(n) for j, (px, py) in enumerate(chips)]

    def start(*refs):
        for cp in copies(*refs):
            cp.start()

    def finish(*refs):
        for cp in copies(*refs):
            cp.wait()

    outs = [jax.ShapeDtypeStruct((3,) + p.shape[1:], p.dtype) for p in parts]
    return _Comm(parts, outs, 3 * n, 0, start, finish)


HBM_SPEC = pl.BlockSpec(memory_space=pltpu.HBM)
SEM_SPEC = pl.BlockSpec(memory_space=pltpu.SEMAPHORE)
DATAFLOW = pltpu.SideEffectType.DATAFLOW_SIDE_EFFECTING


def _chip_exchange_copies(p_refs, o_refs, send_sems, recv_sems):
    x, y, c = _position()
    chips = [(1 - x, y), (x, 1 - y), (1 - x, 1 - y)]
    return [pltpu.make_async_remote_copy(src_ref=p_refs[t].at[2 * px + py], dst_ref=o_refs[t].at[j],
                                         send_sem=send_sems.at[3 * t + j], recv_sem=recv_sems.at[3 * t + j],
                                         device_id=(px, py, c), device_id_type=MESH)
            for t in range(len(p_refs)) for j, (px, py) in enumerate(chips)]


def _chips_start(name, parts, thru):
    n = len(parts)

    def body(*refs):
        for cp in _chip_exchange_copies(refs[:n], refs[n:2 * n], refs[2 * n + 1], refs[2 * n + 2]):
            cp.start()

    lands = [lax.empty((3,) + p.shape[1:], p.dtype) for p in parts]
    args = [pltpu.with_memory_space_constraint(a, pltpu.HBM) for a in list(parts) + lands + [thru]]
    res = pl.pallas_call(body, name=name, in_specs=[HBM_SPEC] * (2 * n + 1),
                         out_specs=[SEM_SPEC, SEM_SPEC] + [HBM_SPEC] * (2 * n + 1),
                         out_shape=[pltpu.SemaphoreType.DMA((3 * n,)), pltpu.SemaphoreType.DMA((3 * n,))]
                         + [pltpu.HBM(a.shape, a.dtype) for a in args],
                         input_output_aliases={i: 2 + i for i in range(2 * n + 1)},
                         compiler_params=pltpu.CompilerParams(has_side_effects=DATAFLOW))(*args)
    return (res[0], res[1], list(res[2:2 + n]), list(res[2 + n:2 + 2 * n])), res[2 + 2 * n]


def _chips_wait(name, send_sems, recv_sems, parts, lands, after):
    n = len(parts)

    def body(*refs):
        for cp in _chip_exchange_copies(refs[:n], refs[n:2 * n], refs[2 * n], refs[2 * n + 1]):
            cp.wait_send()
            cp.wait_recv()

    res = pl.pallas_call(body, name=name, out_shape=[pltpu.HBM(a.shape, a.dtype) for a in parts + lands],
                         in_specs=[HBM_SPEC] * (2 * n) + [SEM_SPEC, SEM_SPEC, pl.BlockSpec(memory_space=pl.ANY)],
                         out_specs=[HBM_SPEC] * (2 * n), input_output_aliases={i: i for i in range(2 * n)},
                         compiler_params=pltpu.CompilerParams(has_side_effects=DATAFLOW))(*parts, *lands, send_sems, recv_sems, after)
    return list(res[:n]), list(res[n:])


ROW_TILE_BYTES = 2 << 20


def _row_tile(r, cols, itemsize=4):
    fits = [t for t in range(8, r + 1, 8) if r % t == 0 and t * cols * itemsize <= ROW_TILE_BYTES]
    return max(fits) if fits else r


def _pair_add(name, g, got, c_idx):
    _, r, cc = g.shape
    tr = _row_tile(r, cc, g.dtype.itemsize)

    def body(c_ref, g_ref, o_ref, out_ref):
        out_ref[...] = (g_ref[...].astype(f32) + o_ref[...].astype(f32)).astype(out_ref.dtype)

    g5 = g.reshape(4, 2, r, cc)
    spec = pltpu.PrefetchScalarGridSpec(
        num_scalar_prefetch=1, grid=(4, r // tr),
        in_specs=[pl.BlockSpec((1, 1, tr, cc), lambda q, i, c_ref: (q, c_ref[0], i, 0)),
                  pl.BlockSpec((1, 1, tr, cc), lambda q, i, c_ref: (q, 0, i, 0))],
        out_specs=pl.BlockSpec((1, 1, tr, cc), lambda q, i, c_ref: (q, 0, i, 0)))
    out = _pcall(body, name=name, grid_spec=spec, out_shape=jax.ShapeDtypeStruct((4, 1, r, cc), g.dtype),
                 compiler_params=_params(("arbitrary", "arbitrary")))(c_idx, g5, got.reshape(4, 1, r, cc))
    return out.reshape(4, r, cc)


def _adam_math(w, g, m, v):
    m2 = ADAM_B1 * m + (1.0 - ADAM_B1) * g
    v2 = ADAM_B2 * v + (1.0 - ADAM_B2) * (g * g)
    m_hat = m2 / (1.0 - ADAM_B1 ** ADAM_STEP)
    v_hat = v2 / (1.0 - ADAM_B2 ** ADAM_STEP)
    delta = -ADAM_LR * (m_hat / (jnp.sqrt(v_hat) + ADAM_EPS) + ADAM_WD * w)
    return delta, m2, v2


def _adam_sharded(name, w, m, v, part, got, chip_idx):
    r, cc = w.shape
    tr = _row_tile(r, cc)

    def body(q_ref, w_ref, m_ref, v_ref, p_ref, o_ref, g_out, d_out, m_out, v_out):
        g = p_ref[0].astype(f32)
        for j in range(3):
            g = g + o_ref[j].astype(f32)
        d, m2, v2 = _adam_math(w_ref[...], g, m_ref[...], v_ref[...])
        g_out[...] = g
        d_out[...] = d
        m_out[...] = m2
        v_out[...] = v2

    row = pl.BlockSpec((tr, cc), lambda i, q_ref: (i, 0))
    spec = pltpu.PrefetchScalarGridSpec(
        num_scalar_prefetch=1, grid=(r // tr,),
        in_specs=[row, row, row, pl.BlockSpec((1, tr, cc), lambda i, q_ref: (q_ref[0], i, 0)),
                  pl.BlockSpec((3, tr, cc), lambda i, q_ref: (0, i, 0))],
        out_specs=[row, row, row, row])
    sh = jax.ShapeDtypeStruct((r, cc), f32)
    return _pcall(body, name=name, grid_spec=spec, out_shape=[sh, sh, sh, sh],
                  compiler_params=_params(("arbitrary",)))(chip_idx, w, m, v, part, got)


def _adam_small(name, w, m, v, parts):
    def body(w_ref, m_ref, v_ref, p_ref, g_out, d_out, m_out, v_out):
        g = p_ref[0]
        for b in range(1, N_DEV):
            g = g + p_ref[b]
        d, m2, v2 = _adam_math(w_ref[...], g, m_ref[...], v_ref[...])
        g_out[...] = g
        d_out[...] = d
        m_out[...] = m2
        v_out[...] = v2

    sh = jax.ShapeDtypeStruct(w.shape, f32)
    return _pcall(body, name=name, out_shape=[sh, sh, sh, sh])(w, m, v, parts)


def _swa_math(n, qa, qb, kap, kac, kbp, kbc, vp, vc, cq, sq, cp, sp, sink):
    g, blk, half = qa.shape
    c3, s3 = cq[None], sq[None]
    q1 = (qa * c3 - qb * s3).reshape(g * blk, half)
    q2 = (qb * c3 + qa * s3).reshape(g * blk, half)
    ck, sk = jnp.concatenate([cp, cq], axis=0), jnp.concatenate([sp, sq], axis=0)
    k1, k2 = jnp.concatenate([kap[0], kac[0]], axis=0), jnp.concatenate([kbp[0], kbc[0]], axis=0)
    k1r, k2r = k1 * ck - k2 * sk, k2 * ck + k1 * sk
    vv = jnp.concatenate([vp[0], vc[0]], axis=0)
    s = (_dot(q1, k1r, "nt") + _dot(q2, k2r, "nt")) * (HEAD_DIM ** -0.5)
    s = s.reshape(g, blk, 2 * blk)
    qi = lax.broadcasted_iota(jnp.int32, (blk, 2 * blk), 0)
    kj = lax.broadcasted_iota(jnp.int32, (blk, 2 * blk), 1)
    valid = (kj > qi) & (kj <= qi + blk) & ((kj >= blk) | (n > 0))
    s = jnp.where(valid[None], s, NEG_INF)
    sink3 = sink.reshape(g, 1, 1)
    mx = jnp.maximum(jnp.max(s, axis=-1, keepdims=True), sink3)
    e = jnp.exp(s - mx)
    z = jnp.sum(e, axis=-1, keepdims=True) + jnp.exp(sink3 - mx)
    p = (e / z).reshape(g * blk, 2 * blk)
    return _dot(p, vv, "nn").reshape(g, blk, 2 * half)


def _swa_specs(hq, kv, blk, half):
    prev = lambda n: jnp.maximum(n - 1, 0)
    q_spec = pl.BlockSpec((hq, blk, half), lambda n: (0, n, 0))
    kc = pl.BlockSpec((kv, blk, half), lambda n: (0, n, 0))
    kp = pl.BlockSpec((kv, blk, half), lambda n: (0, prev(n), 0))
    vc = pl.BlockSpec((kv, blk, 2 * half), lambda n: (0, n, 0))
    vp = pl.BlockSpec((kv, blk, 2 * half), lambda n: (0, prev(n), 0))
    tc = pl.BlockSpec((blk, half), lambda n: (n, 0))
    tp = pl.BlockSpec((blk, half), lambda n: (prev(n), 0))
    sink = pl.BlockSpec((kv, hq // kv, 1), lambda n: (0, 0, 0))
    o_spec = pl.BlockSpec((hq, blk, 2 * half), lambda n: (0, n, 0))
    return q_spec, kc, kp, vc, vp, tc, tp, sink, o_spec


def _swa_fwd(qa, qb, ka, kb, v, cos, sin, sinks, comm=None):
    hq, t, half = qa.shape
    kv = ka.shape[0]
    g, blk = hq // kv, SWA_BLOCK
    q_spec, kc, kp, vc, vp, tc, tp, sink, o_spec = _swa_specs(hq, kv, blk, half)

    def body(qa_r, qb_r, kap, kac, kbp, kbc, vp_r, vc_r, cq, sq, cp, sp, sink_r, o_r):
        tabs = (cq[...], sq[...], cp[...], sp[...])
        for h in range(kv):
            qs, ks = pl.ds(h * g, g), pl.ds(h, 1)
            o_r[qs] = _swa_math(pl.program_id(0), qa_r[qs], qb_r[qs], kap[ks], kac[ks], kbp[ks], kbc[ks], vp_r[ks], vc_r[ks],
                                *tabs, sink_r[ks]).astype(o_r.dtype)

    return _pcall(body, comm=comm, name="swa_fwd", grid=(t // blk,),
                  in_specs=[q_spec, q_spec, kp, kc, kp, kc, vp, vc, tc, tc, tp, tp, sink], out_specs=o_spec,
                  out_shape=jax.ShapeDtypeStruct((hq, t, 2 * half), f32),
                  compiler_params=_params(("arbitrary",)))(qa, qb, ka, ka, kb, kb, v, v, cos, sin, cos, sin, sinks)


def _swa_bwd(qa, qb, ka, kb, v, cos, sin, sinks, do, comm=None):
    hq, t, half = qa.shape
    kv = ka.shape[0]
    g, blk = hq // kv, SWA_BLOCK
    q_spec, kc, kp, vc, vp, tc, tp, sink, o_spec = _swa_specs(hq, kv, blk, half)

    def body(qa_r, qb_r, kap, kac, kbp, kbc, vp_r, vc_r, cq, sq, cp, sp, sink_r, do_r,
             dqa, dqb, dkap, dkac, dkbp, dkbc, dvp, dvc, dsink):
        n = pl.program_id(0)
        tabs = (cq[...], sq[...], cp[...], sp[...])
        fn = lambda a, b, c_, d, e, f_, g_, h_, s_: _swa_math(n, a, b, c_, d, e, f_, g_, h_, *tabs, s_)
        dsinks = []
        for h in range(kv):
            qs, ks = pl.ds(h * g, g), pl.ds(h, 1)
            _, vjp = jax.vjp(fn, qa_r[qs], qb_r[qs], kap[ks], kac[ks], kbp[ks], kbc[ks], vp_r[ks], vc_r[ks], sink_r[ks])
            grads = vjp(do_r[qs])
            dqa[qs] = grads[0]
            dqb[qs] = grads[1]
            for ref, val in zip((dkap, dkac, dkbp, dkbc, dvp, dvc), grads[2:8]):
                ref[ks] = val
            dsinks.append(grads[8])
        dsink_all = jnp.concatenate(dsinks, axis=0)

        @pl.when(n == 0)
        def _():
            dsink[...] = dsink_all

        @pl.when(n > 0)
        def _():
            dsink[...] += dsink_all

    sh = lambda a: jax.ShapeDtypeStruct(a.shape, f32)
    return _pcall(body, comm=comm, name="swa_bwd", grid=(t // blk,),
                  in_specs=[q_spec, q_spec, kp, kc, kp, kc, vp, vc, tc, tc, tp, tp, sink, o_spec],
                  out_specs=[q_spec, q_spec, kc, kc, kc, kc, vc, vc, sink],
                  out_shape=[sh(qa), sh(qb), sh(ka), sh(ka), sh(kb), sh(kb), sh(v), sh(v), sh(sinks)],
                  compiler_params=_params(("arbitrary",)))(qa, qb, ka, ka, kb, kb, v, v, cos, sin, cos, sin, sinks, do)


def _swa_split(za, hq, kv):
    t = za.shape[0]
    half = HEAD_DIM // 2
    tm = _row_tile(t, za.shape[1])

    def body(z_r, qa, qb, ka, kb, v):
        z = z_r[...]
        for h in range(hq):
            qa[h] = z[:, HEAD_DIM * h:HEAD_DIM * h + half]
            qb[h] = z[:, HEAD_DIM * h + half:HEAD_DIM * (h + 1)]
        for h in range(kv):
            o = HEAD_DIM * (hq + h)
            ka[h] = z[:, o:o + half]
            kb[h] = z[:, o + half:o + HEAD_DIM]
            o = HEAD_DIM * (hq + kv + h)
            v[h] = z[:, o:o + HEAD_DIM]

    spec = lambda n, w: pl.BlockSpec((n, tm, w), lambda i: (0, i, 0))
    sh = lambda n, w: jax.ShapeDtypeStruct((n, t, w), f32)
    return _pcall(body, name="swa_split", grid=(t // tm,), in_specs=[pl.BlockSpec((tm, za.shape[1]), lambda i: (i, 0))],
                  out_specs=[spec(hq, half), spec(hq, half), spec(kv, half), spec(kv, half), spec(kv, HEAD_DIM)],
                  out_shape=[sh(hq, half), sh(hq, half), sh(kv, half), sh(kv, half), sh(kv, HEAD_DIM)],
                  compiler_params=_params(("parallel",)))(za)


def _swa_merge(dqa, dqb, dkac, dkap, dkbc, dkbp, dvc, dvp):
    hq, t, half = dqa.shape
    kv = dkac.shape[0]
    blk = SWA_BLOCK
    nb = t // blk
    cols = HEAD_DIM * (hq + 2 * kv)

    def body(qa, qb, kac, kap, kbc, kbp, vc, vp, z_o, s_o):
        i = pl.program_id(0)
        more = (i < nb - 1).astype(f32)
        pieces = []
        for h in range(hq):
            pieces += [qa[h], qb[h]]
        for h in range(kv):
            pieces += [kac[h] + more * kap[h], kbc[h] + more * kbp[h]]
        for h in range(kv):
            pieces.append(vc[h] + more * vp[h])
        z = jnp.concatenate(pieces, axis=-1)
        z_o[...] = z
        colsum = jnp.sum(z, axis=0, keepdims=True)

        @pl.when(i == 0)
        def _():
            s_o[...] = colsum

        @pl.when(i > 0)
        def _():
            s_o[...] += colsum

    cur = lambda n, w: pl.BlockSpec((n, blk, w), lambda i: (0, i, 0))
    nxt = lambda n, w: pl.BlockSpec((n, blk, w), lambda i: (0, jnp.minimum(i + 1, nb - 1), 0))
    return _pcall(body, name="swa_merge", grid=(nb,),
                  in_specs=[cur(hq, half), cur(hq, half), cur(kv, half), nxt(kv, half), cur(kv, half), nxt(kv, half),
                            cur(kv, HEAD_DIM), nxt(kv, HEAD_DIM)],
                  out_specs=[pl.BlockSpec((blk, cols), lambda i: (i, 0)), pl.BlockSpec((1, cols), lambda i: (0, 0))],
                  out_shape=[jax.ShapeDtypeStruct((t, cols), f32), jax.ShapeDtypeStruct((1, cols), f32)],
                  compiler_params=_params(("arbitrary",)))(dqa, dqb, dkac, dkap, dkbc, dkbp, dvc, dvp)


def _xattn_math(q, k, v):
    s = _dot(q, k, "nt") * (q.shape[-1] ** -0.5)
    e = jnp.exp(s - jnp.max(s, axis=-1, keepdims=True))
    p = e / jnp.sum(e, axis=-1, keepdims=True)
    return _dot(p, v, "nn")


def _xattn_fwd(q, kvm):
    t, d = q.shape
    mlen = kvm.shape[0]
    hd = d // XATTN_HEADS
    tq = min(512, t)

    def body(q_r, k_r, v_r, o_r):
        o_r[...] = _xattn_math(q_r[...], k_r[...], v_r[...]).astype(o_r.dtype)

    return _pcall(body, name="xattn_fwd", grid=(XATTN_HEADS, t // tq),
                  in_specs=[pl.BlockSpec((tq, hd), lambda h, i: (i, h)), pl.BlockSpec((mlen, hd), lambda h, i: (0, h)),
                            pl.BlockSpec((mlen, hd), lambda h, i: (0, XATTN_HEADS + h))],
                  out_specs=pl.BlockSpec((tq, hd), lambda h, i: (i, h)), out_shape=jax.ShapeDtypeStruct((t, d), bf16),
                  compiler_params=_params(("parallel", "parallel")))(q, kvm, kvm)


def _xattn_bwd(q, kvm, do):
    t, d = q.shape
    mlen = kvm.shape[0]
    hd = d // XATTN_HEADS
    tq = min(512, t)

    def body(q_r, k_r, v_r, do_r, dq, dk, dv):
        _, vjp = jax.vjp(_xattn_math, q_r[...].astype(f32), k_r[...].astype(f32), v_r[...].astype(f32))
        gq, gk, gv = vjp(do_r[...].astype(f32))
        dq[...] = gq.astype(dq.dtype)
        first = pl.program_id(1) == 0

        @pl.when(first)
        def _():
            dk[...] = gk
            dv[...] = gv

        @pl.when(jnp.logical_not(first))
        def _():
            dk[...] += gk
            dv[...] += gv

    qs = pl.BlockSpec((tq, hd), lambda h, i: (i, h))
    ms = pl.BlockSpec((mlen, hd), lambda h, i: (0, h))
    return _pcall(body, name="xattn_bwd", grid=(XATTN_HEADS, t // tq),
                  in_specs=[qs, ms, pl.BlockSpec((mlen, hd), lambda h, i: (0, XATTN_HEADS + h)), qs],
                  out_specs=[qs, ms, ms],
                  out_shape=[jax.ShapeDtypeStruct((t, d), bf16), jax.ShapeDtypeStruct((mlen, d), f32),
                             jax.ShapeDtypeStruct((mlen, d), f32)],
                  compiler_params=_params(("parallel", "arbitrary")))(q, kvm, kvm, do)


def _chunk_cumsum(lw, reverse=False):
    h, l, _ = lw.shape
    i = lax.broadcasted_iota(jnp.int32, (l, l), 0)
    j = lax.broadcasted_iota(jnp.int32, (l, l), 1)
    tri = jnp.broadcast_to(((i <= j) if reverse else (i >= j)).astype(bf16)[None], (h, l, l))
    out = jnp.zeros(lw.shape, f32)
    for piece in _split3(lw):
        out = out + lax.dot_general(tri, piece, _dims("nn", 3), preferred_element_type=f32)
    return out


def _rwkv_chunk(s0, r, k, v, a, lw, cl, k_k, k_a, r_k, ln_w, ln_b):
    l = r.shape[1]
    kk = k * k_k
    kk = kk / jnp.maximum(jnp.sqrt(jnp.sum(kk * kk, axis=-1, keepdims=True)), 1e-12)
    km = k * (1.0 + (a - 1.0) * k_a)
    av, bv = -kk, kk * a
    p_incl, p_excl, p_inv = jnp.exp(cl), jnp.exp(cl - lw), jnp.exp(-cl)
    at, bh, kh, rt = av * p_excl, bv * p_inv, km * p_inv, r * p_incl
    i = lax.broadcasted_iota(jnp.int32, (l, l), 0)
    j = lax.broadcasted_iota(jnp.int32, (l, l), 1)
    strict, incl = (i > j)[None], (i >= j)[None]
    a_ab = jnp.where(strict, _dot(at, bh, "nt"), 0.0)
    a_ak = jnp.where(strict, _dot(at, kh, "nt"), 0.0)
    a_rb = jnp.where(incl, _dot(rt, bh, "nt"), 0.0)
    a_rk = jnp.where(incl, _dot(rt, kh, "nt"), 0.0)
    rhs = _dot(at, s0, "nt") + _dot(a_ak, v, "nn")
    inv = a_ab + (i == j)[None].astype(f32)
    pw = a_ab
    for _ in range(int(math.log2(l)) - 1):
        pw = _dot(pw, pw, "nn")
        inv = inv + _dot(inv, pw, "nn")
    sa = _dot(inv, rhs, "nn")
    y = _dot(rt, s0, "nt") + _dot(a_rk, v, "nn") + _dot(a_rb, sa, "nn")
    p_last = p_incl[:, l - 1:l, :]
    s_end = s0 * p_last + _dot(v, kh * p_last, "tn") + _dot(sa, bh * p_last, "tn")
    mu = jnp.mean(y, axis=-1, keepdims=True)
    var = jnp.mean(jnp.square(y - mu), axis=-1, keepdims=True)
    out = (y - mu) * lax.rsqrt(var + GN_EPS) * ln_w + ln_b
    out = out + jnp.sum(r * km * r_k, axis=-1, keepdims=True) * v
    return out, s_end


def _rwkv_fwd(r, k, v, a, lw, heads, comm=None):
    h, t, n = r.shape
    l = min(RWKV_CHUNK, t)
    nc = t // l
    seq = pl.BlockSpec((h, l, n), lambda c: (0, c, 0))
    par = pl.BlockSpec((h, 1, n), lambda c: (0, 0, 0))

    def body(r_r, k_r, v_r, a_r, lw_r, p0, p1, p2, p3, p4, y_r, ck_r, s_scr):
        @pl.when(pl.program_id(0) == 0)
        def _():
            s_scr[...] = jnp.zeros_like(s_scr)

        s0 = s_scr[...]
        ck_r[0] = s0
        lw_v = lw_r[...]
        out, s_end = _rwkv_chunk(s0, r_r[...], k_r[...], v_r[...], a_r[...], lw_v, _chunk_cumsum(lw_v),
                                 p0[...], p1[...], p2[...], p3[...], p4[...])
        y_r[...] = out
        s_scr[...] = s_end

    return _pcall(body, comm=comm, name="rwkv_fwd", grid=(nc,), in_specs=[seq] * 5 + [par] * 5,
                  out_specs=[seq, pl.BlockSpec((1, h, n, n), lambda c: (c, 0, 0, 0))],
                  out_shape=[jax.ShapeDtypeStruct((h, t, n), f32), jax.ShapeDtypeStruct((nc, h, n, n), f32)],
                  scratch_shapes=[pltpu.VMEM((h, n, n), f32)],
                  compiler_params=_params(("arbitrary",)))(r, k, v, a, lw, *heads)


def _rwkv_bwd(r, k, v, a, lw, heads, ck, dy, comm=None):
    h, t, n = r.shape
    l = min(RWKV_CHUNK, t)
    nc = t // l
    seq = pl.BlockSpec((h, l, n), lambda c: (0, nc - 1 - c, 0))
    par = pl.BlockSpec((h, 1, n), lambda c: (0, 0, 0))

    def body(r_r, k_r, v_r, a_r, lw_r, p0, p1, p2, p3, p4, ck_r, dy_r,
             dr, dk, dv, da, dlw, g0, g1, g2, g3, g4, ds_scr):
        first = pl.program_id(0) == 0

        @pl.when(first)
        def _():
            ds_scr[...] = jnp.zeros_like(ds_scr)

        lw_v = lw_r[...]
        _, vjp = jax.vjp(_rwkv_chunk, ck_r[0], r_r[...], k_r[...], v_r[...], a_r[...], lw_v, _chunk_cumsum(lw_v),
                         p0[...], p1[...], p2[...], p3[...], p4[...])
        grads = vjp((dy_r[...], ds_scr[...]))
        ds_scr[...] = grads[0]
        dr[...] = grads[1]
        dk[...] = grads[2]
        dv[...] = grads[3]
        da[...] = grads[4]
        dlw[...] = grads[5] + _chunk_cumsum(grads[6], reverse=True)
        acc = (g0, g1, g2, g3, g4)

        @pl.when(first)
        def _():
            for ref, val in zip(acc, grads[7:]):
                ref[...] = val

        @pl.when(jnp.logical_not(first))
        def _():
            for ref, val in zip(acc, grads[7:]):
                ref[...] += val

    seq_sh = jax.ShapeDtypeStruct((h, t, n), f32)
    par_sh = jax.ShapeDtypeStruct((h, 1, n), f32)
    return _pcall(body, comm=comm, name="rwkv_bwd", grid=(nc,),
                  in_specs=[seq] * 5 + [par] * 5 + [pl.BlockSpec((1, h, n, n), lambda c: (nc - 1 - c, 0, 0, 0)), seq],
                  out_specs=[seq] * 5 + [par] * 5, out_shape=[seq_sh] * 5 + [par_sh] * 5,
                  scratch_shapes=[pltpu.VMEM((h, n, n), f32)],
                  compiler_params=_params(("arbitrary",)))(r, k, v, a, lw, *heads, ck, dy)


def _rwkv_pre_math(c, lp, p_rkv, p_rkv_prev, p_l, p_l_prev, mu_rkv, mu_l, w0, a0, decay_up, aaa_up, gate_up):
    dlp, alp, _ = lp
    z = p_rkv + (p_rkv_prev - p_rkv) * mu_rkv
    zl = p_l + (p_l_prev - p_l) * mu_l
    r, k, v = z[:, :c], z[:, c:2 * c], z[:, 2 * c:]
    wd, ad, gd = zl[:, :dlp], zl[:, dlp:dlp + alp], zl[:, dlp + alp:]
    w = -_softplus(-(w0 + _dot(jnp.tanh(wd), decay_up, "nn"))) - 0.5
    a = _sigmoid(a0 + _dot(ad, aaa_up, "nn"))
    g = _dot(_sigmoid(gd), gate_up, "nn")
    return r, k, v, -jnp.exp(w), a, g


def _pad_to(a, n, axis):
    if a.shape[axis] == n:
        return a
    pad = [(0, 0)] * a.ndim
    pad[axis] = (0, n - a.shape[axis])
    return jnp.pad(a, pad)


def _up128(n):
    return -(-n // LANE) * LANE


def _shift_down(p):
    return jnp.concatenate([jnp.zeros((1, p.shape[1]), p.dtype), p[:-1]], axis=0)


def _shift_up(p):
    return jnp.concatenate([p[1:], jnp.zeros((1, p.shape[1]), p.dtype)], axis=0)


def _swiglu(g, u):
    return jax.nn.silu(g) * u


def _ffn_hidden(name, h, w_gate, w_up, comm=None):
    t, d = h.shape
    nb, _, n = w_gate.shape
    tm = _pick(t, (1024, 512, 256, 128))

    def body(h_r, wg_r, wu_r, g_o, u_o, a_o):
        hv = h_r[...]
        g = lax.dot_general(hv, wg_r[0], _dims("nn", 2), preferred_element_type=f32)
        u = lax.dot_general(hv, wu_r[0], _dims("nn", 2), preferred_element_type=f32)
        g_o[0] = g.astype(bf16)
        u_o[0] = u.astype(bf16)
        a_o[0] = _swiglu(g, u).astype(bf16)

    w_spec = pl.BlockSpec((1, d, n), lambda i, j: (j, 0, 0))
    o_spec = pl.BlockSpec((1, tm, n), lambda i, j: (j, i, 0))
    sh = jax.ShapeDtypeStruct((nb, t, n), bf16)
    return _pcall(body, comm=comm, name=name, grid=(t // tm, nb),
                  in_specs=[pl.BlockSpec((tm, d), lambda i, j: (i, 0)), w_spec, w_spec],
                  out_specs=[o_spec, o_spec, o_spec], out_shape=[sh, sh, sh],
                  compiler_params=_params(("parallel", "arbitrary")))(h, w_gate, w_up)


def _ffn_out(name, act, w_down, x, comm=None):
    nb, t, n = act.shape
    d = w_down.shape[2]
    tm, tn = _pick(t, (512, 256, 128)), _pick(d, (512, 256, 128))

    def body(a_r, w_r, x_r, o_r):
        acc = x_r[...]
        for j in range(nb):
            acc = acc + 0.5 * lax.dot_general(a_r[j], w_r[j], _dims("nn", 2), preferred_element_type=f32)
        o_r[...] = acc

    return _pcall(body, comm=comm, name=name, grid=(t // tm, d // tn),
                  in_specs=[pl.BlockSpec((nb, tm, n), lambda i, j: (0, i, 0)), pl.BlockSpec((nb, n, tn), lambda i, j: (0, 0, j)),
                            pl.BlockSpec((tm, tn), lambda i, j: (i, j))],
                  out_specs=pl.BlockSpec((tm, tn), lambda i, j: (i, j)), out_shape=jax.ShapeDtypeStruct((t, d), f32),
                  compiler_params=_params(("parallel", "parallel")))(act, w_down, x)


def _ffn_dhidden(name, dout, w_down, gate, up, comm=None):
    t, d = dout.shape
    nb, n, _ = w_down.shape
    tm = _pick(t, (512, 256, 128))

    def body(d_r, w_r, g_r, u_r, dg_o, du_o):
        dact = 0.5 * lax.dot_general(d_r[...].astype(MXU_DTYPE), w_r[0], _dims("nt", 2), preferred_element_type=f32)
        _, vjp = jax.vjp(_swiglu, g_r[0].astype(f32), u_r[0].astype(f32))
        dg, du = vjp(dact)
        dg_o[0] = dg.astype(bf16)
        du_o[0] = du.astype(bf16)

    o_spec = pl.BlockSpec((1, tm, n), lambda i, j: (j, i, 0))
    sh = jax.ShapeDtypeStruct((nb, t, n), bf16)
    return _pcall(body, comm=comm, name=name, grid=(t // tm, nb),
                  in_specs=[pl.BlockSpec((tm, d), lambda i, j: (i, 0)), pl.BlockSpec((1, n, d), lambda i, j: (j, 0, 0)), o_spec, o_spec],
                  out_specs=[o_spec, o_spec], out_shape=[sh, sh],
                  compiler_params=_params(("parallel", "arbitrary")))(dout, w_down, gate, up)


def _ffn_dw_down(name, act, dout, comm=None):
    nb, t, n = act.shape
    d = dout.shape[1]
    tn = _pick(d, (1024, 512, 256, 128))

    def body(a_r, d_r, o_r):
        acc = lax.dot_general(a_r[0], d_r[...].astype(MXU_DTYPE), _dims("tn", 2), preferred_element_type=f32)
        o_r[0] = (0.5 * acc).astype(bf16)

    return _pcall(body, comm=comm, name=name, grid=(nb, d // tn),
                  in_specs=[pl.BlockSpec((1, t, n), lambda j, i: (j, 0, 0)), pl.BlockSpec((t, tn), lambda j, i: (0, i))],
                  out_specs=pl.BlockSpec((1, n, tn), lambda j, i: (j, 0, i)), out_shape=jax.ShapeDtypeStruct((nb, n, d), bf16),
                  compiler_params=_params(("parallel", "parallel")))(act, dout)


def _ffn_dw_hidden(name, h, dgate, dup, comm=None):
    t, d = h.shape
    nb, _, n = dgate.shape
    tm = _pick(d, (1024, 512, 256, 128))

    def body(h_r, g_r, u_r, dg_o, du_o):
        hv = h_r[...]
        dg_o[0] = lax.dot_general(hv, g_r[0], _dims("tn", 2), preferred_element_type=f32).astype(bf16)
        du_o[0] = lax.dot_general(hv, u_r[0], _dims("tn", 2), preferred_element_type=f32).astype(bf16)

    g_spec = pl.BlockSpec((1, t, n), lambda j, i: (j, 0, 0))
    o_spec = pl.BlockSpec((1, tm, n), lambda j, i: (j, i, 0))
    sh = jax.ShapeDtypeStruct((nb, d, n), bf16)
    return _pcall(body, comm=comm, name=name, grid=(nb, d // tm),
                  in_specs=[pl.BlockSpec((t, tm), lambda j, i: (0, i)), g_spec, g_spec],
                  out_specs=[o_spec, o_spec], out_shape=[sh, sh],
                  compiler_params=_params(("parallel", "parallel")))(h, dgate, dup)


def _ffn_dh(name, dhid, w, res=None, comm=None):
    nb, t, n = dhid.shape
    d = w.shape[1]
    tm, tn = _pick(t, (512, 256, 128)), _pick(d, (512, 256, 128))

    def body(*refs):
        acc = refs[2][...] if res is not None else jnp.zeros((tm, tn), f32)
        for j in range(nb):
            acc = acc + lax.dot_general(refs[0][j], refs[1][j], _dims("nt", 2), preferred_element_type=f32)
        refs[-1][...] = acc

    in_specs = [pl.BlockSpec((nb, tm, n), lambda i, j: (0, i, 0)), pl.BlockSpec((nb, tn, n), lambda i, j: (0, j, 0))]
    args = [dhid, w]
    if res is not None:
        in_specs.append(pl.BlockSpec((tm, tn), lambda i, j: (i, j)))
        args.append(res)
    return _pcall(body, comm=comm, name=name, grid=(t // tm, d // tn), in_specs=in_specs,
                  out_specs=pl.BlockSpec((tm, tn), lambda i, j: (i, j)), out_shape=jax.ShapeDtypeStruct((t, d), f32),
                  compiler_params=_params(("parallel", "parallel")))(*args)


def _lora_bounds(c, lora):
    dl, al, gl = lora
    o1 = 3 * c
    o2, o3 = o1 + dl, o1 + dl + al
    return o1, o2, o3, o3 + gl, (_up128(dl), _up128(al), _up128(gl))


def _win_split(g8, c, lora):
    nb, d, n = g8.shape
    o1, o2, o3, o4, (dlp, alp, glp) = _lora_bounds(c, lora)
    tm = _row_tile(d, nb * n, g8.dtype.itemsize)

    def body(x, rkv_o, lora_o, swa_o):
        w = jnp.concatenate([x[j] for j in range(nb)], axis=-1)
        pad = lambda p, m: p if p.shape[1] == m else jnp.concatenate([p, jnp.zeros((p.shape[0], m - p.shape[1]), p.dtype)], axis=-1)
        rkv_o[...] = w[:, :o1]
        lora_o[...] = jnp.concatenate([pad(w[:, o1:o2], dlp), pad(w[:, o2:o3], alp), pad(w[:, o3:o4], glp)], axis=-1)
        swa_o[...] = w[:, o4:]

    widths = (o1, dlp + alp + glp, nb * n - o4)
    return _pcall(body, name="w_in_split", grid=(d // tm,), in_specs=[pl.BlockSpec((nb, tm, n), lambda i: (0, i, 0))],
                  out_specs=[pl.BlockSpec((tm, wd), lambda i: (i, 0)) for wd in widths],
                  out_shape=[jax.ShapeDtypeStruct((d, wd), g8.dtype) for wd in widths],
                  compiler_params=_params(("parallel",)))(g8)


def _win_merge(dw_rkv, dw_lora, dw_swa, c, lora, nb):
    d = dw_rkv.shape[0]
    o1, o2, o3, o4, (dlp, alp, glp) = _lora_bounds(c, lora)
    dl, al, gl = lora
    total = o4 + dw_swa.shape[1]
    n = total // nb
    tm = _row_tile(d, total, dw_rkv.dtype.itemsize)

    def body(a, b, s, o):
        bv = b[...]
        w = jnp.concatenate([a[...], bv[:, :dl], bv[:, dlp:dlp + al], bv[:, dlp + alp:dlp + alp + gl], s[...]], axis=-1)
        for j in range(nb):
            o[j] = w[:, n * j:n * (j + 1)]

    ins = [dw_rkv, dw_lora, dw_swa]
    return _pcall(body, name="w_in_merge", grid=(d // tm,), in_specs=[pl.BlockSpec((tm, a.shape[1]), lambda i: (i, 0)) for a in ins],
                  out_specs=pl.BlockSpec((nb, tm, n), lambda i: (0, i, 0)), out_shape=jax.ShapeDtypeStruct((nb, d, n), dw_rkv.dtype),
                  compiler_params=_params(("parallel",)))(*ins)


def _norm_bwd(name, x, g_norm, dh, dres, comm=None):
    d = x.shape[1]

    def fn(xb, dhb, drb, g):
        _, vjp = jax.vjp(_rms, xb, g)
        dx, dg = vjp(dhb)
        return drb + dx, dg

    return _rows(name, fn, [x, dh, dres], [g_norm], [(d, f32)], [((1, d), f32)], comm=comm)


def _colsum(name, a):
    return _rows(name, lambda ab: (jnp.sum(ab.astype(f32), axis=0, keepdims=True),), [a], [], [], [((1, a.shape[1]), f32)])[0]


def kernel(x, mem, f1_norm, f1_gate, f1_up, f1_down, mix_norm, w_in, b_in_attn, rw_mu, rw_w0, rw_decay_up, rw_a0, rw_aaa_up, rw_gate_up, rw_k_k, rw_k_a, rw_r_k, rw_lnx_w, rw_lnx_b, attn_sinks, w_out, b_out, xa_norm, mem_norm, w_xq, w_xkv, w_xo, f2_norm, f2_gate, f2_up, f2_down, final_norm, loss_target, m_f1_norm, m_f1_gate, m_f1_up, m_f1_down, m_mix_norm, m_w_in, m_b_in_attn, m_rw_mu, m_rw_w0, m_rw_decay_up, m_rw_a0, m_rw_aaa_up, m_rw_gate_up, m_rw_k_k, m_rw_k_a, m_rw_r_k, m_rw_lnx_w, m_rw_lnx_b, m_attn_sinks, m_w_out, m_b_out, m_xa_norm, m_mem_norm, m_w_xq, m_w_xkv, m_w_xo, m_f2_norm, m_f2_gate, m_f2_up, m_f2_down, m_final_norm, v_f1_norm, v_f1_gate, v_f1_up, v_f1_down, v_mix_norm, v_w_in, v_b_in_attn, v_rw_mu, v_rw_w0, v_rw_decay_up, v_rw_a0, v_rw_aaa_up, v_rw_gate_up, v_rw_k_k, v_rw_k_a, v_rw_r_k, v_rw_lnx_w, v_rw_lnx_b, v_attn_sinks, v_w_out, v_b_out, v_xa_norm, v_mem_norm, v_w_xq, v_w_xkv, v_w_xo, v_f2_norm, v_f2_gate, v_f2_up, v_f2_down, v_final_norm):
    names = ["f1_norm", "f1_gate", "f1_up", "f1_down", "mix_norm", "w_in", "b_in_attn", "rw_mu", "rw_w0", "rw_decay_up",
             "rw_a0", "rw_aaa_up", "rw_gate_up", "rw_k_k", "rw_k_a", "rw_r_k", "rw_lnx_w", "rw_lnx_b", "attn_sinks", "w_out",
             "b_out", "xa_norm", "mem_norm", "w_xq", "w_xkv", "w_xo", "f2_norm", "f2_gate", "f2_up", "f2_down", "final_norm"]
    env = dict(locals())
    w_of = {k: env[k] for k in names}
    m_of = {k: env["m_" + k] for k in names}
    v_of = {k: env["v_" + k] for k in names}
    col_sharded = ["f1_gate", "f1_up", "w_in", "rw_decay_up", "rw_aaa_up", "rw_gate_up", "w_xkv", "f2_gate", "f2_up"]
    row_sharded = ["f1_down", "w_out", "w_xq", "w_xo", "f2_down"]
    sharded = col_sharded + row_sharded
    small = [k for k in names if k not in sharded]

    x0, mem0, tgt = x[0], mem[0], loss_target[0]
    t, d = x0.shape
    c = rw_w0.shape[-1]
    heads = c // HEAD_DIM
    dl, al, gl = rw_decay_up.shape[1], rw_aaa_up.shape[1], rw_gate_up.shape[1]
    dlp, alp, glp = _up128(dl), _up128(al), _up128(gl)
    swa_w = d - c
    hq, kvh = swa_w // HEAD_DIM, (b_in_attn.shape[-1] - swa_w) // (2 * HEAD_DIM)
    my_x, my_y, my_c = _position()
    c_idx = jnp.reshape(my_c, (1,)).astype(jnp.int32)
    chip_idx = jnp.reshape(2 * my_x + my_y, (1,)).astype(jnp.int32)

    shard2d = {k: w_of[k][0] for k in sharded}
    cast = {k: _rows("cast_" + k, lambda a: (a,), [shard2d[k]], [], [(shard2d[k].shape[1], bf16)], tm=_row_tile(*shard2d[k].shape))[0]
            for k in sharded}
    ffn1_keys, ffn2_keys = ["f1_gate", "f1_up", "f1_down"], ["f2_gate", "f2_up", "f2_down"]
    in_keys = ["w_in", "rw_decay_up", "rw_aaa_up", "rw_gate_up"]
    kept_in_blocks = ffn1_keys + ffn2_keys + ["w_in", "w_xkv"]

    def whole(k, g8):
        if k in kept_in_blocks:
            return g8
        if k in col_sharded:
            return g8.transpose(1, 0, 2).reshape(g8.shape[1], N_DEV * g8.shape[2])
        return g8.reshape(N_DEV * g8.shape[1], g8.shape[2])

    def gather_of(keys):
        return _gather_comm([cast[k] for k in keys])

    def wholes(keys, gathered):
        return {k: whole(k, g8) for k, g8 in zip(keys, gathered)}

    (h1,), gathered = _rows("f1_norm", lambda xb, g: (_rms(xb, g),), [x0], [f1_norm], [(d, bf16)], comm=gather_of(ffn1_keys[:2]))
    full = wholes(ffn1_keys[:2], gathered)
    (gate1, up1, act1), gathered = _ffn_hidden("f1_hidden", h1, full["f1_gate"], full["f1_up"], comm=gather_of(["f1_down"]))
    full.update(wholes(["f1_down"], gathered))
    x1, gathered = _ffn_out("f1_out", act1, full["f1_down"], x0, comm=gather_of(in_keys))
    full.update(wholes(in_keys, gathered))
    ffn1_saved = (h1, gate1, up1, act1)
    w_rkv, w_lora, w_swa = _win_split(full["w_in"], c, (dl, al, gl))
    o1, o2, o3, shift_cols, _ = _lora_bounds(c, (dl, al, gl))
    mu_rkv = rw_mu[:, :3 * c]
    mu_l = jnp.concatenate([_pad_to(rw_mu[:, o1:o2], dlp, 1), _pad_to(rw_mu[:, o2:o3], alp, 1),
                            _pad_to(rw_mu[:, o3:shift_cols], glp, 1)], axis=1)
    decay_up = _pad_to(full["rw_decay_up"], dlp, 0).astype(f32)
    aaa_up = _pad_to(full["rw_aaa_up"], alp, 0).astype(f32)
    gate_up = _pad_to(full["rw_gate_up"], glp, 0).astype(f32)
    head_pars = [p.reshape(heads, 1, HEAD_DIM) for p in (rw_k_k, rw_k_a, rw_r_k, rw_lnx_w, rw_lnx_b)]
    final_g = final_norm.reshape(1, d)

    (h2,) = _rows("mix_norm", lambda xb, g: (_rms(xb, g),), [x1], [mix_norm], [(d, bf16)])
    p_rkv, gathered = _mm("in_rkv", h2, w_rkv, "nn", f32, comm=gather_of(["w_out"]))
    full.update(wholes(["w_out"], gathered))
    p_l =_mm("in_lora", h2, w_lora, "nn", f32)
    za = _mm("in_swa", h2, w_swa, "nn", f32, bias=b_in_attn)
    pre_fn = functools.partial(_rwkv_pre_math, c, (dlp, alp, glp))
    pre_rows = [p_rkv, _shift_down(p_rkv), p_l, _shift_down(p_l)]
    pre_full = [mu_rkv, mu_l, rw_w0, rw_a0, decay_up, aaa_up, gate_up]
    (r_h, k_h, v_h, lw_h, a_h, g_t), gathered = _rows("rwkv_pre", pre_fn, pre_rows, pre_full,
                                                       [(c, f32, HEAD_DIM)] * 5 + [(c, f32)], tm=128, comm=gather_of(["w_xq"]))
    full.update(wholes(["w_xq"], gathered))
    seqs = [r_h, k_h, v_h, a_h, lw_h]
    (y_heads, checkpoints), gathered = _rwkv_fwd(*seqs, head_pars, comm=gather_of(["f2_gate"]))
    full.update(wholes(["f2_gate"], gathered))

    pos = jnp.arange(t, dtype=f32)
    inv_freq = ROPE_THETA ** (-jnp.arange(0, HEAD_DIM, 2, dtype=f32) / HEAD_DIM)
    ang = pos[:, None] * inv_freq[None, :]
    cos, sin = jnp.cos(ang), jnp.sin(ang)
    sinks3 = attn_sinks.reshape(kvh, hq // kvh, 1)
    swa_in = (*_swa_split(za, hq, kvh), cos, sin, sinks3)
    y_swa_heads, gathered = _swa_fwd(*swa_in, comm=gather_of(["f2_up"]))
    full.update(wholes(["f2_up"], gathered))
    (ycat,) = _rows("mix_cat", lambda yb, gb, sb: (jnp.concatenate([yb * gb, sb], axis=1),), [y_heads, g_t, y_swa_heads], [],
                    [(d, bf16)])
    x2, gathered = _mm("mix_out", ycat, full["w_out"], "nn", f32, res=x1, bias=b_out, comm=gather_of(["w_xkv"]))
    full.update(wholes(["w_xkv"], gathered))

    (h3,) = _rows("xa_norm", lambda xb, g: (_rms(xb, g),), [x2], [xa_norm], [(d, bf16)])
    (mem_n,) = _rows("mem_norm", lambda xb, g: (_rms(xb, g),), [mem0], [mem_norm], [(d, bf16)])
    q_x, gathered = _mm("xa_q", h3, full["w_xq"], "nn", bf16, comm=gather_of(["w_xo"]))
    full.update(wholes(["w_xo"], gathered))
    kv_x = _mm("xa_kv", mem_n, full["w_xkv"], "nn", bf16)
    o_x = _xattn_fwd(q_x, kv_x)
    x3 = _mm("xa_out", o_x, full["w_xo"], "nn", f32, res=x2)
    (h4,) = _rows("f2_norm", lambda xb, g: (_rms(xb, g),), [x3], [f2_norm], [(d, bf16)])
    (gate2, up2, act2), gathered = _ffn_hidden("f2_hidden", h4, full["f2_gate"], full["f2_up"], comm=gather_of(["f2_down"]))
    full.update(wholes(["f2_down"], gathered))
    x4 = _ffn_out("f2_out", act2, full["f2_down"], x3)
    ffn2_saved = (h4, gate2, up2, act2)

    def loss_fn(xb, tb, g):
        def per_row(xv, gv):
            return 0.5 * jnp.mean(jnp.square(_rms(xv, gv) - tb), axis=-1, keepdims=True)

        lrow, vjp = jax.vjp(per_row, xb, g)
        dxb, dgb = vjp(jnp.ones_like(lrow))
        return dxb, dgb, jnp.sum(lrow, axis=0, keepdims=True)

    dx4, d_final, loss_part = _rows("loss", loss_fn, [x4, tgt], [final_g], [(d, f32)], [((1, d), f32), ((1, 1), f32)])
    loss = lax.psum(loss_part[0, 0], ("x", "y", "c"))

    grads, small_g, out = {}, {"final_norm": d_final}, {}

    def pair_sums_of(tag, keys, carrier=None):
        blocks = []
        for k in keys:
            g2 = grads[k]
            rr, cc = shard2d[k].shape
            if k in kept_in_blocks:
                blocks.append(g2)
            else:
                blocks.append(g2.reshape(g2.shape[0], N_DEV, cc).transpose(1, 0, 2) if k in col_sharded else g2.reshape(N_DEV, rr, cc))
        if carrier is None:
            from_sibling = _comm_only("grads_to_sibling_" + tag, _sibling_comm(blocks))
        else:
            carried, from_sibling = carrier(_sibling_comm(blocks))
        pairs = [_pair_add("pair_add_" + k, b, o, c_idx) for k, b, o in zip(keys, blocks, from_sibling)]
        return pairs if carrier is None else (pairs, carried)

    def update(keys, pair_sums, from_chips):
        for k, part, others in zip(keys, pair_sums, from_chips):
            res = _adam_sharded("adam_" + k, shard2d[k], m_of[k][0], v_of[k][0], part, others, chip_idx)
            out[k] = [a.reshape(w_of[k].shape) for a in res]

    def ffn_backward(tag, keys, xin, g_norm, saved, dout, first_comm, start_exchange):
        h, gate, up, act = saved
        k_gate, k_up, k_down = keys
        if first_comm is None:
            grads[k_down], carried = _ffn_dw_down(tag + "_dw_down", act, dout), None
        else:
            grads[k_down], carried = _ffn_dw_down(tag + "_dw_down", act, dout, comm=first_comm)
        down_pairs, (dgate, dup) = pair_sums_of(
            k_down, [k_down], lambda cm: _ffn_dhidden(tag + "_dhidden", dout, full[k_down], gate, up, comm=cm))
        (grads[k_gate], grads[k_up]), from_chips = _ffn_dw_hidden(tag + "_dw_hidden", h, dgate, dup, comm=_chips_comm(down_pairs))
        update([k_down], down_pairs, from_chips)
        hidden_pairs, dh = pair_sums_of(tag + "_hidden", [k_gate, k_up],
                                        lambda cm: _ffn_dh(tag + "_dh1", dgate, full[k_gate], comm=cm))
        pending = None
        if start_exchange:
            pending, dh = _chips_start("grads_to_chips_start_" + tag, hidden_pairs, dh)
        dh = _ffn_dh(tag + "_dh2", dup, full[k_up], res=dh)
        dx, dg_norm = _norm_bwd(tag + "_dnorm", xin, g_norm, dh, dout)
        return dx, dg_norm, hidden_pairs, carried, pending

    xa_keys = ["w_xq", "w_xkv", "w_xo"]
    dx3, small_g["f2_norm"], ffn2_pairs, _, _ = ffn_backward("f2b", ffn2_keys, x3, f2_norm, ffn2_saved, dx4, None, False)

    do_x = _mm("xa_do", dx3, full["w_xo"], "nt", bf16)
    grads["w_xo"] = _mm("xa_dwo", o_x, dx3, "tn", bf16)
    dq_x, dk_x, dv_x = _xattn_bwd(q_x, kv_x, do_x)
    grads["w_xq"] = _mm("xa_dwq", h3, dq_x, "tn", bf16)
    dh3 = _mm("xa_dh", dq_x, full["w_xq"], "nt", f32)
    dkv_x = jnp.concatenate([dk_x, dv_x], axis=1)
    grads["w_xkv"] = _mm("xa_dwkv", mem_n, dkv_x, "tn", bf16, out_blocks=N_DEV)
    dkv_blocks = dkv_x.astype(bf16).reshape(dkv_x.shape[0], N_DEV, -1).transpose(1, 0, 2)
    dmem_n = _ffn_dh("xa_dmem", dkv_blocks, full["w_xkv"])
    _, small_g["mem_norm"] = _norm_bwd("mem_dnorm", mem0, mem_norm, dmem_n, jnp.zeros_like(mem0))
    xa_pairs, (dx2, small_g["xa_norm"]) = pair_sums_of(
        "xa", xa_keys, lambda cm: _norm_bwd("xa_dnorm", x2, xa_norm, dh3, dx3, comm=cm))

    dycat = _mm("mix_dy", dx2, full["w_out"], "nt", f32)
    grads["w_out"] = _mm("mix_dwout", ycat, dx2, "tn", bf16)
    small_g["b_out"] = _colsum("mix_dbout", dx2)
    out_pairs, (dy_heads, dg_t, do_sw) = pair_sums_of("out", ["w_out"], lambda cm: _rows(
        "mix_dgate", lambda db, yb, gb: (db[:, :c] * gb, db[:, :c] * yb, db[:, c:]), [dycat, y_heads, g_t], [],
        [(c, f32, HEAD_DIM), (c, f32), (swa_w, f32, HEAD_DIM)], comm=cm))
    rw_grads, from_chips = _rwkv_bwd(*seqs, head_pars, checkpoints, dy_heads, comm=_chips_comm(ffn2_pairs))
    update(ffn2_keys[:2], ffn2_pairs, from_chips)
    dr_h, dk_h, dv_h, da_h, dlw_h = rw_grads[:5]
    for nm, gh in zip(("rw_k_k", "rw_k_a", "rw_r_k", "rw_lnx_w", "rw_lnx_b"), rw_grads[5:]):
        small_g[nm] = gh.reshape(w_of[nm].shape)

    def pre_bwd(*args):
        _, vjp = jax.vjp(pre_fn, *args[:4], *args[10:])
        return vjp(tuple(args[4:10]))

    pre_cts = [dr_h, dk_h, dv_h, dlw_h, da_h, dg_t]
    pre_out = _rows("rwkv_pre_bwd", pre_bwd, pre_rows + pre_cts, pre_full,
                    [(3 * c, f32), (3 * c, f32), (dlp + alp + glp, f32), (dlp + alp + glp, f32)],
                    [(p.shape, f32) for p in pre_full], tm=128)
    dp_rkv = pre_out[0] + _shift_up(pre_out[1])
    dp_l = pre_out[2] + _shift_up(pre_out[3])
    dmu_rkv, dmu_l, small_g["rw_w0"], small_g["rw_a0"], d_decay_up, d_aaa_up, d_gate_up = pre_out[4:]
    small_g["rw_mu"] = jnp.concatenate([dmu_rkv, dmu_l[:, :dl], dmu_l[:, dlp:dlp + al], dmu_l[:, dlp + alp:dlp + alp + gl]], axis=1)
    grads["rw_decay_up"] = d_decay_up[:dl].astype(bf16)
    grads["rw_aaa_up"] = d_aaa_up[:al].astype(bf16)
    grads["rw_gate_up"] = d_gate_up[:gl].astype(bf16)

    sw, from_chips = _swa_bwd(*swa_in, do_sw, comm=_chips_comm(xa_pairs))
    update(xa_keys, xa_pairs, from_chips)
    small_g["attn_sinks"] = sw[8].reshape(attn_sinks.shape)
    dza, small_g["b_in_attn"] = _swa_merge(sw[0], sw[1], sw[3], sw[2], sw[5], sw[4], sw[7], sw[6])

    dw_rkv = _mm("in_dwrkv", h2, dp_rkv, "tn", bf16)
    dw_l = _mm("in_dwlora", h2, dp_l, "tn", bf16)
    dw_swa = _mm("in_dwswa", h2, dza, "tn", bf16)
    grads["w_in"] = _win_merge(dw_rkv, dw_l, dw_swa, c, (dl, al, gl), N_DEV)
    dh2, from_chips = _mm("in_dh1", dp_rkv, w_rkv, "nt", f32, comm=_chips_comm(out_pairs))
    update(["w_out"], out_pairs, from_chips)
    dh2 = _mm("in_dh2", dp_l, w_lora, "nt", f32, res=dh2)
    in_pairs, dh2 = pair_sums_of("in", in_keys, lambda cm: _mm("in_dh3", dza, w_swa, "nt", f32, res=dh2, comm=cm))
    pending_in, dh2 = _chips_start("grads_to_chips_start_in", in_pairs, dh2)
    dx1, small_g["mix_norm"] = _norm_bwd("mix_dnorm", x1, mix_norm, dh2, dx2)

    dx0, small_g["f1_norm"], _, _, pending = ffn_backward("f1b", ffn1_keys, x0, f1_norm, ffn1_saved, dx1, None, True)
    others_done = _fence("updates_done", [res[1] for res in out.values()] + [dx0] + list(small_g.values()))
    in_pairs, from_chips = _chips_wait("grads_to_chips_wait_in", *pending_in, others_done)
    update(in_keys, in_pairs, from_chips)
    pairs, from_chips = _chips_wait("grads_to_chips_wait_f1b", *pending, others_done)
    update(ffn1_keys[:2], pairs, from_chips)

    sizes = [int(w_of[k].size) for k in small]
    total = sum(sizes)
    cols = -(-total // (8 * LANE)) * LANE

    def pack(parts_of):
        flat = jnp.concatenate([parts_of[k].reshape(-1).astype(f32) for k in small])
        return _pad_to(flat, 8 * cols, 0).reshape(8, cols)

    (all_parts,) = _comm_only("gather_small_grads", _gather_comm([pack(small_g)], after=from_chips))
    res = _adam_small("adam_small", pack(w_of), pack(m_of), pack(v_of), all_parts)
    offs = 0
    flat_res = [a.reshape(-1) for a in res]
    for k, sz in zip(small, sizes):
        out[k] = [a[offs:offs + sz].reshape(w_of[k].shape) for a in flat_res]
        offs += sz

    outs = [loss, dx0.reshape(x.shape)]
    for j in range(4):
        outs += [out[k][j] for k in names]
    return tuple(outs)
```

```python
import functools
import math

import jax
import jax.numpy as jnp
from jax import lax
from jax.experimental import pallas as pl
from jax.experimental.pallas import tpu as pltpu

f32 = jnp.float32
bf16 = jnp.bfloat16
MXU_DTYPE = jnp.bfloat16

HEAD_DIM = 64
SWA_BLOCK = 128
ROPE_THETA = 10000.0
XATTN_HEADS = 4
RMS_EPS = 1e-6
GN_EPS = 64e-5
NEG_INF = -1e30
RWKV_CHUNK = 64

ADAM_LR = 0.001
ADAM_B1 = 0.9
ADAM_B2 = 0.999
ADAM_EPS = 1e-08
ADAM_WD = 0.01
ADAM_STEP = 10

N_DEV = 8
LANE = 128
VMEM_LIMIT_BYTES = 56 * 1024 * 1024
MM_VMEM_BUDGET = 40 * 1024 * 1024
MESH = pl.DeviceIdType.MESH


def _params(sem):
    return pltpu.CompilerParams(dimension_semantics=sem, vmem_limit_bytes=VMEM_LIMIT_BYTES)


class _Comm:
    def __init__(self, ins, outs, n_remote, n_local, start, finish):
        self.ins, self.outs, self.n_remote, self.n_local = list(ins), list(outs), n_remote, max(n_local, 1)
        self.start, self.finish = start, finish


def _pcall(body, comm=None, **kw):
    kw.setdefault("compiler_params", pltpu.CompilerParams(vmem_limit_bytes=VMEM_LIMIT_BYTES))
    if comm is None:
        return pl.pallas_call(body, **kw)
    single = not isinstance(kw["out_shape"], (list, tuple))
    out_shape = [kw["out_shape"]] if single else list(kw["out_shape"])
    out_specs = [kw["out_specs"]] if single else list(kw["out_specs"])
    in_specs, scratch, grid = list(kw["in_specs"]), list(kw.get("scratch_shapes", ())), tuple(kw.get("grid", ()))
    n_in, n_out, n_ci, n_co, n_scr = len(in_specs), len(out_shape), len(comm.ins), len(comm.outs), len(scratch)

    def wrapped(*refs):
        ins, c_ins = refs[:n_in], refs[n_in:n_in + n_ci]
        outs = refs[n_in + n_ci:n_in + n_ci + n_out]
        c_outs = refs[n_in + n_ci + n_out:n_in + n_ci + n_out + n_co]
        rest = refs[n_in + n_ci + n_out + n_co:]
        scr, sems = rest[:n_scr], rest[n_scr:]
        if grid:
            ids = [pl.program_id(k) for k in range(len(grid))]
            first = functools.reduce(jnp.logical_and, [i == 0 for i in ids])
            last = functools.reduce(jnp.logical_and, [i == g - 1 for i, g in zip(ids, grid)])
            pl.when(first)(lambda: comm.start(c_ins, c_outs, *sems))
            body(*ins, *outs, *scr)
            pl.when(last)(lambda: comm.finish(c_ins, c_outs, *sems))
        else:
            comm.start(c_ins, c_outs, *sems)
            body(*ins, *outs, *scr)
            comm.finish(c_ins, c_outs, *sems)

    any_spec = pl.BlockSpec(memory_space=pl.ANY)
    kw.update(in_specs=in_specs + [any_spec] * n_ci, out_specs=out_specs + [any_spec] * n_co,
              out_shape=out_shape + comm.outs,
              scratch_shapes=scratch + [pltpu.SemaphoreType.DMA((comm.n_remote,)), pltpu.SemaphoreType.DMA((comm.n_remote,)),
                                        pltpu.SemaphoreType.DMA((comm.n_local,))])
    if grid:
        kw["compiler_params"] = _params(("arbitrary",) * len(grid))
    call = pl.pallas_call(wrapped, **kw)

    def run(*args):
        res = call(*args, *comm.ins)
        return (res[0] if single else list(res[:n_out])), list(res[n_out:])

    return run


def _fence(name, arrays):
    def body(*refs):
        refs[-1][...] = jnp.zeros(refs[-1].shape, f32)

    return _pcall(body, name=name, in_specs=[pl.BlockSpec(memory_space=pl.ANY)] * len(arrays),
                  out_specs=pl.BlockSpec(memory_space=pltpu.VMEM), out_shape=jax.ShapeDtypeStruct((8, LANE), f32))(*arrays)


def _comm_only(name, comm):
    return _pcall(lambda: None, comm=comm, name=name, in_specs=[], out_specs=[], out_shape=[])()[1]


def _dims(kind, ndim):
    o = ndim - 2
    batch = ((0,), (0,)) if o else ((), ())
    c = {"nn": ((1 + o,), (o,)), "nt": ((1 + o,), (1 + o,)), "tn": ((o,), (o,))}[kind]
    return (c, batch)


def _dot_raw(x, y, kind):
    return lax.dot_general(x.astype(MXU_DTYPE), y.astype(MXU_DTYPE), _dims(kind, x.ndim), preferred_element_type=f32)


@functools.partial(jax.custom_vjp, nondiff_argnums=(2,))
def _dot(x, y, kind):
    return _dot_raw(x, y, kind)


def _dot_fwd(x, y, kind):
    return _dot_raw(x, y, kind), (x, y)


def _dot_bwd(kind, res, g):
    x, y = res
    if kind == "nn":
        dx, dy = _dot(g, y, "nt"), _dot(x, g, "tn")
    elif kind == "nt":
        dx, dy = _dot(g, y, "nn"), _dot(g, x, "tn")
    else:
        dx, dy = _dot(y, g, "nt"), _dot(x, g, "nn")
    return dx.astype(x.dtype), dy.astype(y.dtype)


_dot.defvjp(_dot_fwd, _dot_bwd)


def _split3(x):
    a = x.astype(bf16)
    r = x - a.astype(f32)
    b = r.astype(bf16)
    c = (r - b.astype(f32)).astype(bf16)
    return a, b, c


def _rms(x, g):
    x = x.astype(f32)
    return x * lax.rsqrt(jnp.mean(x * x, axis=-1, keepdims=True) + RMS_EPS) * g


def _sigmoid(x):
    return 1.0 / (1.0 + jnp.exp(-x))


def _softplus(x):
    return jnp.maximum(x, 0.0) + jnp.log(1.0 + jnp.exp(-jnp.abs(x)))


def _rows(name, fn, row_ins, full_ins, row_outs, acc_outs=(), tm=None, comm=None):
    width = lambda a: a.shape[1] if a.ndim == 2 else a.shape[0] * a.shape[2]
    rows = row_ins[0].shape[0] if row_ins[0].ndim == 2 else row_ins[0].shape[1]
    if tm is None:
        tm = _row_tile(rows, max([width(a) for a in row_ins] + [o[0] for o in row_outs]))
    tm = min(tm, rows)
    assert rows % tm == 0, (name, rows, tm)
    n_in = len(row_ins) + len(full_ins)
    n_o, n_a = len(row_outs), len(acc_outs)

    def load(k, ref):
        if k < len(row_ins) and row_ins[k].ndim == 3:
            return jnp.concatenate([ref[h] for h in range(ref.shape[0])], axis=-1)
        return ref[...]

    def body(*refs):
        vals = [load(k, r) for k, r in enumerate(refs[:n_in])]
        outs = fn(*vals)
        o_refs = refs[n_in:n_in + n_o]
        a_refs = refs[n_in + n_o:]
        for k in range(n_o):
            if len(row_outs[k]) == 3:
                n = row_outs[k][2]
                for h in range(row_outs[k][0] // n):
                    o_refs[k][h] = outs[k][:, h * n:(h + 1) * n].astype(o_refs[k].dtype)
            else:
                o_refs[k][...] = outs[k].astype(o_refs[k].dtype)
        if n_a:
            first = pl.program_id(0) == 0

            @pl.when(first)
            def _():
                for k in range(n_a):
                    a_refs[k][...] = outs[n_o + k].astype(a_refs[k].dtype)

            @pl.when(jnp.logical_not(first))
            def _():
                for k in range(n_a):
                    a_refs[k][...] += outs[n_o + k].astype(a_refs[k].dtype)

    by_rows = lambda cols: pl.BlockSpec((tm, cols), lambda i: (i, 0))
    by_heads = lambda h, n: pl.BlockSpec((h, tm, n), lambda i: (0, i, 0))
    in_specs = [by_rows(a.shape[1]) if a.ndim == 2 else by_heads(a.shape[0], a.shape[2]) for a in row_ins]
    in_specs += [pl.BlockSpec(a.shape, lambda i, nd=a.ndim: (0,) * nd) for a in full_ins]
    out_specs = [by_rows(o[0]) if len(o) == 2 else by_heads(o[0] // o[2], o[2]) for o in row_outs]
    out_specs += [pl.BlockSpec(s, lambda i, nd=len(s): (0,) * nd) for s, _ in acc_outs]
    out_shape = [jax.ShapeDtypeStruct((rows, o[0]) if len(o) == 2 else (o[0] // o[2], rows, o[2]), o[1]) for o in row_outs]
    out_shape += [jax.ShapeDtypeStruct(s, d) for s, d in acc_outs]
    return _pcall(body, comm=comm, name=name, grid=(rows // tm,), in_specs=in_specs, out_specs=out_specs, out_shape=out_shape,
                  compiler_params=_params(("arbitrary",)))(*row_ins, *full_ins)


def _pick(n, cands):
    for c in cands:
        if n % c == 0:
            return c
    return n


def _mm(name, a, b, mode, out_dtype, scale=1.0, res=None, bias=None, comm=None, out_blocks=None):
    b_blocks = b.ndim == 3
    if b_blocks:
        assert mode == "nn"
        (m, k), (nb, k2, tn) = a.shape, b.shape
        n = nb * tn
    elif mode == "nn":
        (m, k), (k2, n) = a.shape, b.shape
    elif mode == "nt":
        (m, k), (n, k2) = a.shape, b.shape
    else:
        (k, m), (k2, n) = a.shape, b.shape
    assert k == k2, (name, a.shape, b.shape, mode)
    if not b_blocks:
        tn = n // out_blocks if out_blocks else _pick(n, (512, 256, 128))
    tm = _pick(m, (1024, 512, 256, 128))

    def need(tm_):
        by = tm_ * k * a.dtype.itemsize + tn * k * b.dtype.itemsize + tm_ * tn * (jnp.dtype(out_dtype).itemsize + 4)
        if res is not None:
            by += tm_ * tn * res.dtype.itemsize
        return 2 * by

    while need(tm) > MM_VMEM_BUDGET and tm % 256 == 0:
        tm //= 2
    dims = _dims(mode, 2)

    def body(*refs):
        bv = refs[1][0] if b_blocks else refs[1][...]
        acc = lax.dot_general(refs[0][...].astype(MXU_DTYPE), bv.astype(MXU_DTYPE), dims, preferred_element_type=f32)
        if scale != 1.0:
            acc = acc * scale
        pos = 2
        if bias is not None:
            acc = acc + refs[pos][...]
            pos += 1
        if res is not None:
            acc = acc + refs[pos][...].astype(f32)
            pos += 1
        if out_blocks:
            refs[pos][0] = acc.astype(out_dtype)
        else:
            refs[pos][...] = acc.astype(out_dtype)

    a_spec = pl.BlockSpec((k, tm), lambda i, j: (0, i)) if mode == "tn" else pl.BlockSpec((tm, k), lambda i, j: (i, 0))
    if b_blocks:
        b_spec = pl.BlockSpec((1, k, tn), lambda i, j: (j, 0, 0))
    else:
        b_spec = pl.BlockSpec((tn, k), lambda i, j: (j, 0)) if mode == "nt" else pl.BlockSpec((k, tn), lambda i, j: (0, j))
    in_specs, args = [a_spec, b_spec], [a, b]
    if bias is not None:
        in_specs.append(pl.BlockSpec((1, tn), lambda i, j: (0, j)))
        args.append(bias)
    if res is not None:
        in_specs.append(pl.BlockSpec((tm, tn), lambda i, j: (i, j)))
        args.append(res)
    if out_blocks:
        out_spec, out_shape = pl.BlockSpec((1, tm, tn), lambda i, j: (j, i, 0)), jax.ShapeDtypeStruct((out_blocks, m, tn), out_dtype)
    else:
        out_spec, out_shape = pl.BlockSpec((tm, tn), lambda i, j: (i, j)), jax.ShapeDtypeStruct((m, n), out_dtype)
    return _pcall(body, comm=comm, name=name, grid=(m // tm, n // tn), in_specs=in_specs, out_specs=out_spec, out_shape=out_shape,
                  compiler_params=_params(("parallel", "parallel")))(*args)


def _position():
    return lax.axis_index("x"), lax.axis_index("y"), lax.axis_index("c")


def _gather_comm(shards, after=()):
    n = len(shards)

    def plan(x_refs, o_refs, send_sems, recv_sems, local_sems):
        x, y, c = _position()
        me, sibling = (x, y, c), (x, y, 1 - c)
        chips = [(1 - x, y), (x, 1 - y), (1 - x, 1 - y)]

        def slot(px, py, pc):
            return 4 * px + 2 * py + pc

        def copy(t, k, block, to, src=None):
            dst = o_refs[t].at[slot(*block)]
            return pltpu.make_async_remote_copy(src_ref=dst if src is None else src, dst_ref=dst,
                                                send_sem=send_sems.at[7 * t + k], recv_sem=recv_sems.at[7 * t + k],
                                                device_id=to, device_id_type=MESH)

        mine = [pltpu.make_async_copy(x_refs[t], o_refs[t].at[slot(*me)], local_sems.at[t]) for t in range(n)]
        first = []
        for t in range(n):
            first.append(copy(t, 0, me, sibling, src=x_refs[t]))
            first += [copy(t, 1 + j, me, (*chip, c), src=x_refs[t]) for j, chip in enumerate(chips)]
        return me, sibling, chips, c, copy, mine, first

    def start(*refs):
        _, _, _, _, _, mine, first = plan(*refs)
        for cp in mine + first:
            cp.start()

    def finish(*refs):
        me, sibling, chips, c, copy, mine, first = plan(*refs)
        passed = []
        for t in range(n):
            for j, chip in enumerate(chips):
                copy(t, 1 + j, (*chip, c), me).wait_recv()
                cp = copy(t, 4 + j, (*chip, c), sibling)
                cp.start()
                passed.append(cp)
        for t in range(n):
            copy(t, 0, sibling, me).wait_recv()
            for j, chip in enumerate(chips):
                copy(t, 4 + j, (*chip, 1 - c), me).wait_recv()
        for cp in first + passed:
            cp.wait_send()
        for cp in mine:
            cp.wait()

    outs = [jax.ShapeDtypeStruct((N_DEV,) + s.shape, s.dtype) for s in shards]
    return _Comm(list(shards) + list(after), outs, 7 * n, n, start, finish)


def _sibling_comm(blocks):
    n = len(blocks)

    def copies(g_refs, o_refs, send_sems, recv_sems, _):
        x, y, c = _position()
        return [pltpu.make_async_remote_copy(src_ref=g_refs[t].at[2 * q + 1 - c], dst_ref=o_refs[t].at[q],
                                             send_sem=send_sems.at[4 * t + q], recv_sem=recv_sems.at[4 * t + q],
                                             device_id=(x, y, 1 - c), device_id_type=MESH)
                for t in range(n) for q in range(4)]

    def start(*refs):
        for cp in copies(*refs):
            cp.start()

    def finish(*refs):
        for cp in copies(*refs):
            cp.wait()

    outs = [jax.ShapeDtypeStruct((4,) + g.shape[1:], g.dtype) for g in blocks]
    return _Comm(blocks, outs, 4 * n, 0, start, finish)


def _chips_comm(parts):
    n = len(parts)

    def copies(p_refs, o_refs, send_sems, recv_sems, _):
        x, y, c = _position()
        chips = [(1 - x, y), (x, 1 - y), (1 - x, 1 - y)]
        return [pltpu.make_async_remote_copy(src_ref=p_refs[t].at[2 * px + py], dst_ref=o_refs[t].at[j],
                                             send_sem=send_sems.at[3 * t + j], recv_sem=recv_sems.at[3 * t + j],
                                             device_id=(px, py, c), device_id_type=MESH)
                for t in range(n) for j, (px, py) in enumerate(chips)]

    def start(*refs):
        for cp in copies(*refs):
            cp.start()

    def finish(*refs):
        for cp in copies(*refs):
            cp.wait()

    outs = [jax.ShapeDtypeStruct((3,) + p.shape[1:], p.dtype) for p in parts]
    return _Comm(parts, outs, 3 * n, 0, start, finish)


HBM_SPEC = pl.BlockSpec(memory_space=pltpu.HBM)
SEM_SPEC = pl.BlockSpec(memory_space=pltpu.SEMAPHORE)
DATAFLOW = pltpu.SideEffectType.DATAFLOW_SIDE_EFFECTING


def _chip_exchange_copies(p_refs, o_refs, send_sems, recv_sems):
    x, y, c = _position()
    chips = [(1 - x, y), (x, 1 - y), (1 - x, 1 - y)]
    return [pltpu.make_async_remote_copy(src_ref=p_refs[t].at[2 * px + py], dst_ref=o_refs[t].at[j],
                                         send_sem=send_sems.at[3 * t + j], recv_sem=recv_sems.at[3 * t + j],
                                         device_id=(px, py, c), device_id_type=MESH)
            for t in range(len(p_refs)) for j, (px, py) in enumerate(chips)]


def _chips_start(name, parts, thru):
    n = len(parts)

    def body(*refs):
        for cp in _chip_exchange_copies(refs[:n], refs[n:2 * n], refs[2 * n + 1], refs[2 * n + 2]):
            cp.start()

    lands = [lax.empty((3,) + p.shape[1:], p.dtype) for p in parts]
    args = [pltpu.with_memory_space_constraint(a, pltpu.HBM) for a in list(parts) + lands + [thru]]
    res = pl.pallas_call(body, name=name, in_specs=[HBM_SPEC] * (2 * n + 1),
                         out_specs=[SEM_SPEC, SEM_SPEC] + [HBM_SPEC] * (2 * n + 1),
                         out_shape=[pltpu.SemaphoreType.DMA((3 * n,)), pltpu.SemaphoreType.DMA((3 * n,))]
                         + [pltpu.HBM(a.shape, a.dtype) for a in args],
                         input_output_aliases={i: 2 + i for i in range(2 * n + 1)},
                         compiler_params=pltpu.CompilerParams(has_side_effects=DATAFLOW))(*args)
    return (res[0], res[1], list(res[2:2 + n]), list(res[2 + n:2 + 2 * n])), res[2 + 2 * n]


def _chips_wait(name, send_sems, recv_sems, parts, lands, after):
    n = len(parts)

    def body(*refs):
        for cp in _chip_exchange_copies(refs[:n], refs[n:2 * n], refs[2 * n], refs[2 * n + 1]):
            cp.wait_send()
            cp.wait_recv()

    res = pl.pallas_call(body, name=name, out_shape=[pltpu.HBM(a.shape, a.dtype) for a in parts + lands],
                         in_specs=[HBM_SPEC] * (2 * n) + [SEM_SPEC, SEM_SPEC, pl.BlockSpec(memory_space=pl.ANY)],
                         out_specs=[HBM_SPEC] * (2 * n), input_output_aliases={i: i for i in range(2 * n)},
                         compiler_params=pltpu.CompilerParams(has_side_effects=DATAFLOW))(*parts, *lands, send_sems, recv_sems, after)
    return list(res[:n]), list(res[n:])


ROW_TILE_BYTES = 2 << 20


def _row_tile(r, cols, itemsize=4):
    fits = [t for t in range(8, r + 1, 8) if r % t == 0 and t * cols * itemsize <= ROW_TILE_BYTES]
    return max(fits) if fits else r


def _pair_add(name, g, got, c_idx):
    _, r, cc = g.shape
    tr = _row_tile(r, cc, g.dtype.itemsize)

    def body(c_ref, g_ref, o_ref, out_ref):
        out_ref[...] = (g_ref[...].astype(f32) + o_ref[...].astype(f32)).astype(out_ref.dtype)

    g5 = g.reshape(4, 2, r, cc)
    spec = pltpu.PrefetchScalarGridSpec(
        num_scalar_prefetch=1, grid=(4, r // tr),
        in_specs=[pl.BlockSpec((1, 1, tr, cc), lambda q, i, c_ref: (q, c_ref[0], i, 0)),
                  pl.BlockSpec((1, 1, tr, cc), lambda q, i, c_ref: (q, 0, i, 0))],
        out_specs=pl.BlockSpec((1, 1, tr, cc), lambda q, i, c_ref: (q, 0, i, 0)))
    out = _pcall(body, name=name, grid_spec=spec, out_shape=jax.ShapeDtypeStruct((4, 1, r, cc), g.dtype),
                 compiler_params=_params(("arbitrary", "arbitrary")))(c_idx, g5, got.reshape(4, 1, r, cc))
    return out.reshape(4, r, cc)


def _adam_math(w, g, m, v):
    m2 = ADAM_B1 * m + (1.0 - ADAM_B1) * g
    v2 = ADAM_B2 * v + (1.0 - ADAM_B2) * (g * g)
    m_hat = m2 / (1.0 - ADAM_B1 ** ADAM_STEP)
    v_hat = v2 / (1.0 - ADAM_B2 ** ADAM_STEP)
    delta = -ADAM_LR * (m_hat / (jnp.sqrt(v_hat) + ADAM_EPS) + ADAM_WD * w)
    return delta, m2, v2


def _adam_sharded(name, w, m, v, part, got, chip_idx):
    r, cc = w.shape
    tr = _row_tile(r, cc)

    def body(q_ref, w_ref, m_ref, v_ref, p_ref, o_ref, g_out, d_out, m_out, v_out):
        g = p_ref[0].astype(f32)
        for j in range(3):
            g = g + o_ref[j].astype(f32)
        d, m2, v2 = _adam_math(w_ref[...], g, m_ref[...], v_ref[...])
        g_out[...] = g
        d_out[...] = d
        m_out[...] = m2
        v_out[...] = v2

    row = pl.BlockSpec((tr, cc), lambda i, q_ref: (i, 0))
    spec = pltpu.PrefetchScalarGridSpec(
        num_scalar_prefetch=1, grid=(r // tr,),
        in_specs=[row, row, row, pl.BlockSpec((1, tr, cc), lambda i, q_ref: (q_ref[0], i, 0)),
                  pl.BlockSpec((3, tr, cc), lambda i, q_ref: (0, i, 0))],
        out_specs=[row, row, row, row])
    sh = jax.ShapeDtypeStruct((r, cc), f32)
    return _pcall(body, name=name, grid_spec=spec, out_shape=[sh, sh, sh, sh],
                  compiler_params=_params(("arbitrary",)))(chip_idx, w, m, v, part, got)


def _adam_small(name, w, m, v, parts):
    def body(w_ref, m_ref, v_ref, p_ref, g_out, d_out, m_out, v_out):
        g = p_ref[0]
        for b in range(1, N_DEV):
            g = g + p_ref[b]
        d, m2, v2 = _adam_math(w_ref[...], g, m_ref[...], v_ref[...])
        g_out[...] = g
        d_out[...] = d
        m_out[...] = m2
        v_out[...] = v2

    sh = jax.ShapeDtypeStruct(w.shape, f32)
    return _pcall(body, name=name, out_shape=[sh, sh, sh, sh])(w, m, v, parts)


def _swa_math(n, qa, qb, kap, kac, kbp, kbc, vp, vc, cq, sq, cp, sp, sink):
    g, blk, half = qa.shape
    c3, s3 = cq[None], sq[None]
    q1 = (qa * c3 - qb * s3).reshape(g * blk, half)
    q2 = (qb * c3 + qa * s3).reshape(g * blk, half)
    ck, sk = jnp.concatenate([cp, cq], axis=0), jnp.concatenate([sp, sq], axis=0)
    k1, k2 = jnp.concatenate([kap[0], kac[0]], axis=0), jnp.concatenate([kbp[0], kbc[0]], axis=0)
    k1r, k2r = k1 * ck - k2 * sk, k2 * ck + k1 * sk
    vv = jnp.concatenate([vp[0], vc[0]], axis=0)
    s = (_dot(q1, k1r, "nt") + _dot(q2, k2r, "nt")) * (HEAD_DIM ** -0.5)
    s = s.reshape(g, blk, 2 * blk)
    qi = lax.broadcasted_iota(jnp.int32, (blk, 2 * blk), 0)
    kj = lax.broadcasted_iota(jnp.int32, (blk, 2 * blk), 1)
    valid = (kj > qi) & (kj <= qi + blk) & ((kj >= blk) | (n > 0))
    s = jnp.where(valid[None], s, NEG_INF)
    sink3 = sink.reshape(g, 1, 1)
    mx = jnp.maximum(jnp.max(s, axis=-1, keepdims=True), sink3)
    e = jnp.exp(s - mx)
    z = jnp.sum(e, axis=-1, keepdims=True) + jnp.exp(sink3 - mx)
    p = (e / z).reshape(g * blk, 2 * blk)
    return _dot(p, vv, "nn").reshape(g, blk, 2 * half)


def _swa_specs(hq, kv, blk, half):
    prev = lambda n: jnp.maximum(n - 1, 0)
    q_spec = pl.BlockSpec((hq, blk, half), lambda n: (0, n, 0))
    kc = pl.BlockSpec((kv, blk, half), lambda n: (0, n, 0))
    kp = pl.BlockSpec((kv, blk, half), lambda n: (0, prev(n), 0))
    vc = pl.BlockSpec((kv, blk, 2 * half), lambda n: (0, n, 0))
    vp = pl.BlockSpec((kv, blk, 2 * half), lambda n: (0, prev(n), 0))
    tc = pl.BlockSpec((blk, half), lambda n: (n, 0))
    tp = pl.BlockSpec((blk, half), lambda n: (prev(n), 0))
    sink = pl.BlockSpec((kv, hq // kv, 1), lambda n: (0, 0, 0))
    o_spec = pl.BlockSpec((hq, blk, 2 * half), lambda n: (0, n, 0))
    return q_spec, kc, kp, vc, vp, tc, tp, sink, o_spec


def _swa_fwd(qa, qb, ka, kb, v, cos, sin, sinks, comm=None):
    hq, t, half = qa.shape
    kv = ka.shape[0]
    g, blk = hq // kv, SWA_BLOCK
    q_spec, kc, kp, vc, vp, tc, tp, sink, o_spec = _swa_specs(hq, kv, blk, half)

    def body(qa_r, qb_r, kap, kac, kbp, kbc, vp_r, vc_r, cq, sq, cp, sp, sink_r, o_r):
        tabs = (cq[...], sq[...], cp[...], sp[...])
        for h in range(kv):
            qs, ks = pl.ds(h * g, g), pl.ds(h, 1)
            o_r[qs] = _swa_math(pl.program_id(0), qa_r[qs], qb_r[qs], kap[ks], kac[ks], kbp[ks], kbc[ks], vp_r[ks], vc_r[ks],
                                *tabs, sink_r[ks]).astype(o_r.dtype)

    return _pcall(body, comm=comm, name="swa_fwd", grid=(t // blk,),
                  in_specs=[q_spec, q_spec, kp, kc, kp, kc, vp, vc, tc, tc, tp, tp, sink], out_specs=o_spec,
                  out_shape=jax.ShapeDtypeStruct((hq, t, 2 * half), f32),
                  compiler_params=_params(("arbitrary",)))(qa, qb, ka, ka, kb, kb, v, v, cos, sin, cos, sin, sinks)


def _swa_bwd(qa, qb, ka, kb, v, cos, sin, sinks, do, comm=None):
    hq, t, half = qa.shape
    kv = ka.shape[0]
    g, blk = hq // kv, SWA_BLOCK
    q_spec, kc, kp, vc, vp, tc, tp, sink, o_spec = _swa_specs(hq, kv, blk, half)

    def body(qa_r, qb_r, kap, kac, kbp, kbc, vp_r, vc_r, cq, sq, cp, sp, sink_r, do_r,
             dqa, dqb, dkap, dkac, dkbp, dkbc, dvp, dvc, dsink):
        n = pl.program_id(0)
        tabs = (cq[...], sq[...], cp[...], sp[...])
        fn = lambda a, b, c_, d, e, f_, g_, h_, s_: _swa_math(n, a, b, c_, d, e, f_, g_, h_, *tabs, s_)
        dsinks = []
        for h in range(kv):
            qs, ks = pl.ds(h * g, g), pl.ds(h, 1)
            _, vjp = jax.vjp(fn, qa_r[qs], qb_r[qs], kap[ks], kac[ks], kbp[ks], kbc[ks], vp_r[ks], vc_r[ks], sink_r[ks])
            grads = vjp(do_r[qs])
            dqa[qs] = grads[0]
            dqb[qs] = grads[1]
            for ref, val in zip((dkap, dkac, dkbp, dkbc, dvp, dvc), grads[2:8]):
                ref[ks] = val
            dsinks.append(grads[8])
        dsink_all = jnp.concatenate(dsinks, axis=0)

        @pl.when(n == 0)
        def _():
            dsink[...] = dsink_all

        @pl.when(n > 0)
        def _():
            dsink[...] += dsink_all

    sh = lambda a: jax.ShapeDtypeStruct(a.shape, f32)
    return _pcall(body, comm=comm, name="swa_bwd", grid=(t // blk,),
                  in_specs=[q_spec, q_spec, kp, kc, kp, kc, vp, vc, tc, tc, tp, tp, sink, o_spec],
                  out_specs=[q_spec, q_spec, kc, kc, kc, kc, vc, vc, sink],
                  out_shape=[sh(qa), sh(qb), sh(ka), sh(ka), sh(kb), sh(kb), sh(v), sh(v), sh(sinks)],
                  compiler_params=_params(("arbitrary",)))(qa, qb, ka, ka, kb, kb, v, v, cos, sin, cos, sin, sinks, do)


def _swa_split(za, hq, kv):
    t = za.shape[0]
    half = HEAD_DIM // 2
    tm = _row_tile(t, za.shape[1])

    def body(z_r, qa, qb, ka, kb, v):
        z = z_r[...]
        for h in range(hq):
            qa[h] = z[:, HEAD_DIM * h:HEAD_DIM * h + half]
            qb[h] = z[:, HEAD_DIM * h + half:HEAD_DIM * (h + 1)]
        for h in range(kv):
            o = HEAD_DIM * (hq + h)
            ka[h] = z[:, o:o + half]
            kb[h] = z[:, o + half:o + HEAD_DIM]
            o = HEAD_DIM * (hq + kv + h)
            v[h] = z[:, o:o + HEAD_DIM]

    spec = lambda n, w: pl.BlockSpec((n, tm, w), lambda i: (0, i, 0))
    sh = lambda n, w: jax.ShapeDtypeStruct((n, t, w), f32)
    return _pcall(body, name="swa_split", grid=(t // tm,), in_specs=[pl.BlockSpec((tm, za.shape[1]), lambda i: (i, 0))],
                  out_specs=[spec(hq, half), spec(hq, half), spec(kv, half), spec(kv, half), spec(kv, HEAD_DIM)],
                  out_shape=[sh(hq, half), sh(hq, half), sh(kv, half), sh(kv, half), sh(kv, HEAD_DIM)],
                  compiler_params=_params(("parallel",)))(za)


def _swa_merge(dqa, dqb, dkac, dkap, dkbc, dkbp, dvc, dvp):
    hq, t, half = dqa.shape
    kv = dkac.shape[0]
    blk = SWA_BLOCK
    nb = t // blk
    cols = HEAD_DIM * (hq + 2 * kv)

    def body(qa, qb, kac, kap, kbc, kbp, vc, vp, z_o, s_o):
        i = pl.program_id(0)
        more = (i < nb - 1).astype(f32)
        pieces = []
        for h in range(hq):
            pieces += [qa[h], qb[h]]
        for h in range(kv):
            pieces += [kac[h] + more * kap[h], kbc[h] + more * kbp[h]]
        for h in range(kv):
            pieces.append(vc[h] + more * vp[h])
        z = jnp.concatenate(pieces, axis=-1)
        z_o[...] = z
        colsum = jnp.sum(z, axis=0, keepdims=True)

        @pl.when(i == 0)
        def _():
            s_o[...] = colsum

        @pl.when(i > 0)
        def _():
            s_o[...] += colsum

    cur = lambda n, w: pl.BlockSpec((n, blk, w), lambda i: (0, i, 0))
    nxt = lambda n, w: pl.BlockSpec((n, blk, w), lambda i: (0, jnp.minimum(i + 1, nb - 1), 0))
    return _pcall(body, name="swa_merge", grid=(nb,),
                  in_specs=[cur(hq, half), cur(hq, half), cur(kv, half), nxt(kv, half), cur(kv, half), nxt(kv, half),
                            cur(kv, HEAD_DIM), nxt(kv, HEAD_DIM)],
                  out_specs=[pl.BlockSpec((blk, cols), lambda i: (i, 0)), pl.BlockSpec((1, cols), lambda i: (0, 0))],
                  out_shape=[jax.ShapeDtypeStruct((t, cols), f32), jax.ShapeDtypeStruct((1, cols), f32)],
                  compiler_params=_params(("arbitrary",)))(dqa, dqb, dkac, dkap, dkbc, dkbp, dvc, dvp)


def _xattn_math(q, k, v):
    s = _dot(q, k, "nt") * (q.shape[-1] ** -0.5)
    e = jnp.exp(s - jnp.max(s, axis=-1, keepdims=True))
    p = e / jnp.sum(e, axis=-1, keepdims=True)
    return _dot(p, v, "nn")


def _xattn_fwd(q, kvm):
    t, d = q.shape
    mlen = kvm.shape[0]
    hd = d // XATTN_HEADS
    tq = min(512, t)

    def body(q_r, k_r, v_r, o_r):
        o_r[...] = _xattn_math(q_r[...], k_r[...], v_r[...]).astype(o_r.dtype)

    return _pcall(body, name="xattn_fwd", grid=(XATTN_HEADS, t // tq),
                  in_specs=[pl.BlockSpec((tq, hd), lambda h, i: (i, h)), pl.BlockSpec((mlen, hd), lambda h, i: (0, h)),
                            pl.BlockSpec((mlen, hd), lambda h, i: (0, XATTN_HEADS + h))],
                  out_specs=pl.BlockSpec((tq, hd), lambda h, i: (i, h)), out_shape=jax.ShapeDtypeStruct((t, d), bf16),
                  compiler_params=_params(("parallel", "parallel")))(q, kvm, kvm)


def _xattn_bwd(q, kvm, do):
    t, d = q.shape
    mlen = kvm.shape[0]
    hd = d // XATTN_HEADS
    tq = min(512, t)

    def body(q_r, k_r, v_r, do_r, dq, dk, dv):
        _, vjp = jax.vjp(_xattn_math, q_r[...].astype(f32), k_r[...].astype(f32), v_r[...].astype(f32))
        gq, gk, gv = vjp(do_r[...].astype(f32))
        dq[...] = gq.astype(dq.dtype)
        first = pl.program_id(1) == 0

        @pl.when(first)
        def _():
            dk[...] = gk
            dv[...] = gv

        @pl.when(jnp.logical_not(first))
        def _():
            dk[...] += gk
            dv[...] += gv

    qs = pl.BlockSpec((tq, hd), lambda h, i: (i, h))
    ms = pl.BlockSpec((mlen, hd), lambda h, i: (0, h))
    return _pcall(body, name="xattn_bwd", grid=(XATTN_HEADS, t // tq),
                  in_specs=[qs, ms, pl.BlockSpec((mlen, hd), lambda h, i: (0, XATTN_HEADS + h)), qs],
                  out_specs=[qs, ms, ms],
                  out_shape=[jax.ShapeDtypeStruct((t, d), bf16), jax.ShapeDtypeStruct((mlen, d), f32),
                             jax.ShapeDtypeStruct((mlen, d), f32)],
                  compiler_params=_params(("parallel", "arbitrary")))(q, kvm, kvm, do)


def _chunk_cumsum(lw, reverse=False):
    h, l, _ = lw.shape
    i = lax.broadcasted_iota(jnp.int32, (l, l), 0)
    j = lax.broadcasted_iota(jnp.int32, (l, l), 1)
    tri = jnp.broadcast_to(((i <= j) if reverse else (i >= j)).astype(bf16)[None], (h, l, l))
    out = jnp.zeros(lw.shape, f32)
    for piece in _split3(lw):
        out = out + lax.dot_general(tri, piece, _dims("nn", 3), preferred_element_type=f32)
    return out


def _rwkv_chunk(s0, r, k, v, a, lw, cl, k_k, k_a, r_k, ln_w, ln_b):
    l = r.shape[1]
    kk = k * k_k
    kk = kk / jnp.maximum(jnp.sqrt(jnp.sum(kk * kk, axis=-1, keepdims=True)), 1e-12)
    km = k * (1.0 + (a - 1.0) * k_a)
    av, bv = -kk, kk * a
    p_incl, p_excl, p_inv = jnp.exp(cl), jnp.exp(cl - lw), jnp.exp(-cl)
    at, bh, kh, rt = av * p_excl, bv * p_inv, km * p_inv, r * p_incl
    i = lax.broadcasted_iota(jnp.int32, (l, l), 0)
    j = lax.broadcasted_iota(jnp.int32, (l, l), 1)
    strict, incl = (i > j)[None], (i >= j)[None]
    a_ab = jnp.where(strict, _dot(at, bh, "nt"), 0.0)
    a_ak = jnp.where(strict, _dot(at, kh, "nt"), 0.0)
    a_rb = jnp.where(incl, _dot(rt, bh, "nt"), 0.0)
    a_rk = jnp.where(incl, _dot(rt, kh, "nt"), 0.0)
    rhs = _dot(at, s0, "nt") + _dot(a_ak, v, "nn")
    inv = a_ab + (i == j)[None].astype(f32)
    pw = a_ab
    for _ in range(int(math.log2(l)) - 1):
        pw = _dot(pw, pw, "nn")
        inv = inv + _dot(inv, pw, "nn")
    sa = _dot(inv, rhs, "nn")
    y = _dot(rt, s0, "nt") + _dot(a_rk, v, "nn") + _dot(a_rb, sa, "nn")
    p_last = p_incl[:, l - 1:l, :]
    s_end = s0 * p_last + _dot(v, kh * p_last, "tn") + _dot(sa, bh * p_last, "tn")
    mu = jnp.mean(y, axis=-1, keepdims=True)
    var = jnp.mean(jnp.square(y - mu), axis=-1, keepdims=True)
    out = (y - mu) * lax.rsqrt(var + GN_EPS) * ln_w + ln_b
    out = out + jnp.sum(r * km * r_k, axis=-1, keepdims=True) * v
    return out, s_end


def _rwkv_fwd(r, k, v, a, lw, heads, comm=None):
    h, t, n = r.shape
    l = min(RWKV_CHUNK, t)
    nc = t // l
    seq = pl.BlockSpec((h, l, n), lambda c: (0, c, 0))
    par = pl.BlockSpec((h, 1, n), lambda c: (0, 0, 0))

    def body(r_r, k_r, v_r, a_r, lw_r, p0, p1, p2, p3, p4, y_r, ck_r, s_scr):
        @pl.when(pl.program_id(0) == 0)
        def _():
            s_scr[...] = jnp.zeros_like(s_scr)

        s0 = s_scr[...]
        ck_r[0] = s0
        lw_v = lw_r[...]
        out, s_end = _rwkv_chunk(s0, r_r[...], k_r[...], v_r[...], a_r[...], lw_v, _chunk_cumsum(lw_v),
                                 p0[...], p1[...], p2[...], p3[...], p4[...])
        y_r[...] = out
        s_scr[...] = s_end

    return _pcall(body, comm=comm, name="rwkv_fwd", grid=(nc,), in_specs=[seq] * 5 + [par] * 5,
                  out_specs=[seq, pl.BlockSpec((1, h, n, n), lambda c: (c, 0, 0, 0))],
                  out_shape=[jax.ShapeDtypeStruct((h, t, n), f32), jax.ShapeDtypeStruct((nc, h, n, n), f32)],
                  scratch_shapes=[pltpu.VMEM((h, n, n), f32)],
                  compiler_params=_params(("arbitrary",)))(r, k, v, a, lw, *heads)


def _rwkv_bwd(r, k, v, a, lw, heads, ck, dy, comm=None):
    h, t, n = r.shape
    l = min(RWKV_CHUNK, t)
    nc = t // l
    seq = pl.BlockSpec((h, l, n), lambda c: (0, nc - 1 - c, 0))
    par = pl.BlockSpec((h, 1, n), lambda c: (0, 0, 0))

    def body(r_r, k_r, v_r, a_r, lw_r, p0, p1, p2, p3, p4, ck_r, dy_r,
             dr, dk, dv, da, dlw, g0, g1, g2, g3, g4, ds_scr):
        first = pl.program_id(0) == 0

        @pl.when(first)
        def _():
            ds_scr[...] = jnp.zeros_like(ds_scr)

        lw_v = lw_r[...]
        _, vjp = jax.vjp(_rwkv_chunk, ck_r[0], r_r[...], k_r[...], v_r[...], a_r[...], lw_v, _chunk_cumsum(lw_v),
                         p0[...], p1[...], p2[...], p3[...], p4[...])
        grads = vjp((dy_r[...], ds_scr[...]))
        ds_scr[...] = grads[0]
        dr[...] = grads[1]
        dk[...] = grads[2]
        dv[...] = grads[3]
        da[...] = grads[4]
        dlw[...] = grads[5] + _chunk_cumsum(grads[6], reverse=True)
        acc = (g0, g1, g2, g3, g4)

        @pl.when(first)
        def _():
            for ref, val in zip(acc, grads[7:]):
                ref[...] = val

        @pl.when(jnp.logical_not(first))
        def _():
            for ref, val in zip(acc, grads[7:]):
                ref[...] += val

    seq_sh = jax.ShapeDtypeStruct((h, t, n), f32)
    par_sh = jax.ShapeDtypeStruct((h, 1, n), f32)
    return _pcall(body, comm=comm, name="rwkv_bwd", grid=(nc,),
                  in_specs=[seq] * 5 + [par] * 5 + [pl.BlockSpec((1, h, n, n), lambda c: (nc - 1 - c, 0, 0, 0)), seq],
                  out_specs=[seq] * 5 + [par] * 5, out_shape=[seq_sh] * 5 + [par_sh] * 5,
                  scratch_shapes=[pltpu.VMEM((h, n, n), f32)],
                  compiler_params=_params(("arbitrary",)))(r, k, v, a, lw, *heads, ck, dy)


def _rwkv_pre_math(c, lp, p_rkv, p_rkv_prev, p_l, p_l_prev, mu_rkv, mu_l, w0, a0, decay_up, aaa_up, gate_up):
    dlp, alp, _ = lp
    z = p_rkv + (p_rkv_prev - p_rkv) * mu_rkv
    zl = p_l + (p_l_prev - p_l) * mu_l
    r, k, v = z[:, :c], z[:, c:2 * c], z[:, 2 * c:]
    wd, ad, gd = zl[:, :dlp], zl[:, dlp:dlp + alp], zl[:, dlp + alp:]
    w = -_softplus(-(w0 + _dot(jnp.tanh(wd), decay_up, "nn"))) - 0.5
    a = _sigmoid(a0 + _dot(ad, aaa_up, "nn"))
    g = _dot(_sigmoid(gd), gate_up, "nn")
    return r, k, v, -jnp.exp(w), a, g


def _pad_to(a, n, axis):
    if a.shape[axis] == n:
        return a
    pad = [(0, 0)] * a.ndim
    pad[axis] = (0, n - a.shape[axis])
    return jnp.pad(a, pad)


def _up128(n):
    return -(-n // LANE) * LANE


def _shift_down(p):
    return jnp.concatenate([jnp.zeros((1, p.shape[1]), p.dtype), p[:-1]], axis=0)


def _shift_up(p):
    return jnp.concatenate([p[1:], jnp.zeros((1, p.shape[1]), p.dtype)], axis=0)


def _swiglu(g, u):
    return jax.nn.silu(g) * u


def _ffn_hidden(name, h, w_gate, w_up, comm=None):
    t, d = h.shape
    nb, _, n = w_gate.shape
    tm = _pick(t, (1024, 512, 256, 128))
    ups = list(w_up) if isinstance(w_up, (list, tuple)) else [w_up]
    starts = [sum(p.shape[1] for p in ups[:k]) for k in range(len(ups))]

    def body(h_r, wg_r, *rest):
        up_refs, (g_o, u_o, a_o) = rest[:len(ups)], rest[len(ups):]
        hv = h_r[...]
        g = lax.dot_general(hv, wg_r[0], _dims("nn", 2), preferred_element_type=f32)
        u = None
        for ref, k0 in zip(up_refs, starts):
            part = lax.dot_general(hv[:, k0:k0 + ref.shape[1]], ref[0], _dims("nn", 2), preferred_element_type=f32)
            u = part if u is None else u + part
        g_o[0] = g.astype(bf16)
        u_o[0] = u.astype(bf16)
        a_o[0] = _swiglu(g, u).astype(bf16)

    w_spec = lambda rows: pl.BlockSpec((1, rows, n), lambda i, j: (j, 0, 0))
    o_spec = pl.BlockSpec((1, tm, n), lambda i, j: (j, i, 0))
    sh = jax.ShapeDtypeStruct((nb, t, n), bf16)
    return _pcall(body, comm=comm, name=name, grid=(t // tm, nb),
                  in_specs=[pl.BlockSpec((tm, d), lambda i, j: (i, 0)), w_spec(d)] + [w_spec(p.shape[1]) for p in ups],
                  out_specs=[o_spec, o_spec, o_spec], out_shape=[sh, sh, sh],
                  compiler_params=_params(("parallel", "arbitrary")))(h, w_gate, *ups)


def _ffn_out(name, act, w_down, x, comm=None):
    nb, t, n = act.shape
    d = w_down.shape[2]
    tm, tn = _pick(t, (512, 256, 128)), _pick(d, (512, 256, 128))

    def body(a_r, w_r, x_r, o_r):
        acc = x_r[...]
        for j in range(nb):
            acc = acc + 0.5 * lax.dot_general(a_r[j], w_r[j], _dims("nn", 2), preferred_element_type=f32)
        o_r[...] = acc

    return _pcall(body, comm=comm, name=name, grid=(t // tm, d // tn),
                  in_specs=[pl.BlockSpec((nb, tm, n), lambda i, j: (0, i, 0)), pl.BlockSpec((nb, n, tn), lambda i, j: (0, 0, j)),
                            pl.BlockSpec((tm, tn), lambda i, j: (i, j))],
                  out_specs=pl.BlockSpec((tm, tn), lambda i, j: (i, j)), out_shape=jax.ShapeDtypeStruct((t, d), f32),
                  compiler_params=_params(("parallel", "parallel")))(act, w_down, x)


def _ffn_dhidden(name, dout, w_down, gate, up, comm=None):
    t, d = dout.shape
    nb, n, _ = w_down.shape
    tm = _pick(t, (512, 256, 128))

    def body(d_r, w_r, g_r, u_r, dg_o, du_o):
        dact = 0.5 * lax.dot_general(d_r[...].astype(MXU_DTYPE), w_r[0], _dims("nt", 2), preferred_element_type=f32)
        _, vjp = jax.vjp(_swiglu, g_r[0].astype(f32), u_r[0].astype(f32))
        dg, du = vjp(dact)
        dg_o[0] = dg.astype(bf16)
        du_o[0] = du.astype(bf16)

    o_spec = pl.BlockSpec((1, tm, n), lambda i, j: (j, i, 0))
    sh = jax.ShapeDtypeStruct((nb, t, n), bf16)
    return _pcall(body, comm=comm, name=name, grid=(t // tm, nb),
                  in_specs=[pl.BlockSpec((tm, d), lambda i, j: (i, 0)), pl.BlockSpec((1, n, d), lambda i, j: (j, 0, 0)), o_spec, o_spec],
                  out_specs=[o_spec, o_spec], out_shape=[sh, sh],
                  compiler_params=_params(("parallel", "arbitrary")))(dout, w_down, gate, up)


def _ffn_dw_down(name, act, dout, comm=None):
    nb, t, n = act.shape
    d = dout.shape[1]
    tn = _pick(d, (1024, 512, 256, 128))

    def body(a_r, d_r, o_r):
        acc = lax.dot_general(a_r[0], d_r[...].astype(MXU_DTYPE), _dims("tn", 2), preferred_element_type=f32)
        o_r[0] = (0.5 * acc).astype(bf16)

    return _pcall(body, comm=comm, name=name, grid=(nb, d // tn),
                  in_specs=[pl.BlockSpec((1, t, n), lambda j, i: (j, 0, 0)), pl.BlockSpec((t, tn), lambda j, i: (0, i))],
                  out_specs=pl.BlockSpec((1, n, tn), lambda j, i: (j, 0, i)), out_shape=jax.ShapeDtypeStruct((nb, n, d), bf16),
                  compiler_params=_params(("parallel", "parallel")))(act, dout)


def _ffn_dw_hidden(name, h, dgate, dup, comm=None):
    t, d = h.shape
    nb, _, n = dgate.shape
    tm = _pick(d, (1024, 512, 256, 128))

    def body(h_r, g_r, u_r, dg_o, du_o):
        hv = h_r[...]
        dg_o[0] = lax.dot_general(hv, g_r[0], _dims("tn", 2), preferred_element_type=f32).astype(bf16)
        du_o[0] = lax.dot_general(hv, u_r[0], _dims("tn", 2), preferred_element_type=f32).astype(bf16)

    g_spec = pl.BlockSpec((1, t, n), lambda j, i: (j, 0, 0))
    o_spec = pl.BlockSpec((1, tm, n), lambda j, i: (j, i, 0))
    sh = jax.ShapeDtypeStruct((nb, d, n), bf16)
    return _pcall(body, comm=comm, name=name, grid=(nb, d // tm),
                  in_specs=[pl.BlockSpec((t, tm), lambda j, i: (0, i)), g_spec, g_spec],
                  out_specs=[o_spec, o_spec], out_shape=[sh, sh],
                  compiler_params=_params(("parallel", "parallel")))(h, dgate, dup)


def _ffn_dh(name, dhid, w, res=None, comm=None):
    nb, t, n = dhid.shape
    d = w.shape[1]
    tm, tn = _pick(t, (512, 256, 128)), _pick(d, (512, 256, 128))

    def body(*refs):
        acc = refs[2][...] if res is not None else jnp.zeros((tm, tn), f32)
        for j in range(nb):
            acc = acc + lax.dot_general(refs[0][j], refs[1][j], _dims("nt", 2), preferred_element_type=f32)
        refs[-1][...] = acc

    in_specs = [pl.BlockSpec((nb, tm, n), lambda i, j: (0, i, 0)), pl.BlockSpec((nb, tn, n), lambda i, j: (0, j, 0))]
    args = [dhid, w]
    if res is not None:
        in_specs.append(pl.BlockSpec((tm, tn), lambda i, j: (i, j)))
        args.append(res)
    return _pcall(body, comm=comm, name=name, grid=(t // tm, d // tn), in_specs=in_specs,
                  out_specs=pl.BlockSpec((tm, tn), lambda i, j: (i, j)), out_shape=jax.ShapeDtypeStruct((t, d), f32),
                  compiler_params=_params(("parallel", "parallel")))(*args)


def _lora_bounds(c, lora):
    dl, al, gl = lora
    o1 = 3 * c
    o2, o3 = o1 + dl, o1 + dl + al
    return o1, o2, o3, o3 + gl, (_up128(dl), _up128(al), _up128(gl))


def _win_split(g8, c, lora):
    nb, d, n = g8.shape
    o1, o2, o3, o4, (dlp, alp, glp) = _lora_bounds(c, lora)
    tm = _row_tile(d, nb * n, g8.dtype.itemsize)

    def body(x, rkv_o, lora_o, swa_o):
        w = jnp.concatenate([x[j] for j in range(nb)], axis=-1)
        pad = lambda p, m: p if p.shape[1] == m else jnp.concatenate([p, jnp.zeros((p.shape[0], m - p.shape[1]), p.dtype)], axis=-1)
        rkv_o[...] = w[:, :o1]
        lora_o[...] = jnp.concatenate([pad(w[:, o1:o2], dlp), pad(w[:, o2:o3], alp), pad(w[:, o3:o4], glp)], axis=-1)
        swa_o[...] = w[:, o4:]

    widths = (o1, dlp + alp + glp, nb * n - o4)
    return _pcall(body, name="w_in_split", grid=(d // tm,), in_specs=[pl.BlockSpec((nb, tm, n), lambda i: (0, i, 0))],
                  out_specs=[pl.BlockSpec((tm, wd), lambda i: (i, 0)) for wd in widths],
                  out_shape=[jax.ShapeDtypeStruct((d, wd), g8.dtype) for wd in widths],
                  compiler_params=_params(("parallel",)))(g8)


def _win_merge(dw_rkv, dw_lora, dw_swa, c, lora, nb):
    d = dw_rkv.shape[0]
    o1, o2, o3, o4, (dlp, alp, glp) = _lora_bounds(c, lora)
    dl, al, gl = lora
    total = o4 + dw_swa.shape[1]
    n = total // nb
    tm = _row_tile(d, total, dw_rkv.dtype.itemsize)

    def body(a, b, s, o):
        bv = b[...]
        w = jnp.concatenate([a[...], bv[:, :dl], bv[:, dlp:dlp + al], bv[:, dlp + alp:dlp + alp + gl], s[...]], axis=-1)
        for j in range(nb):
            o[j] = w[:, n * j:n * (j + 1)]

    ins = [dw_rkv, dw_lora, dw_swa]
    return _pcall(body, name="w_in_merge", grid=(d // tm,), in_specs=[pl.BlockSpec((tm, a.shape[1]), lambda i: (i, 0)) for a in ins],
                  out_specs=pl.BlockSpec((nb, tm, n), lambda i: (0, i, 0)), out_shape=jax.ShapeDtypeStruct((nb, d, n), dw_rkv.dtype),
                  compiler_params=_params(("parallel",)))(*ins)


def _norm_bwd(name, x, g_norm, dh, dres, comm=None):
    d = x.shape[1]

    def fn(xb, dhb, drb, g):
        _, vjp = jax.vjp(_rms, xb, g)
        dx, dg = vjp(dhb)
        return drb + dx, dg

    return _rows(name, fn, [x, dh, dres], [g_norm], [(d, f32)], [((1, d), f32)], comm=comm)


def _colsum(name, a):
    return _rows(name, lambda ab: (jnp.sum(ab.astype(f32), axis=0, keepdims=True),), [a], [], [], [((1, a.shape[1]), f32)])[0]


def kernel(x, mem, f1_norm, f1_gate, f1_up, f1_down, mix_norm, w_in, b_in_attn, rw_mu, rw_w0, rw_decay_up, rw_a0, rw_aaa_up, rw_gate_up, rw_k_k, rw_k_a, rw_r_k, rw_lnx_w, rw_lnx_b, attn_sinks, w_out, b_out, xa_norm, mem_norm, w_xq, w_xkv, w_xo, f2_norm, f2_gate, f2_up, f2_down, final_norm, loss_target, m_f1_norm, m_f1_gate, m_f1_up, m_f1_down, m_mix_norm, m_w_in, m_b_in_attn, m_rw_mu, m_rw_w0, m_rw_decay_up, m_rw_a0, m_rw_aaa_up, m_rw_gate_up, m_rw_k_k, m_rw_k_a, m_rw_r_k, m_rw_lnx_w, m_rw_lnx_b, m_attn_sinks, m_w_out, m_b_out, m_xa_norm, m_mem_norm, m_w_xq, m_w_xkv, m_w_xo, m_f2_norm, m_f2_gate, m_f2_up, m_f2_down, m_final_norm, v_f1_norm, v_f1_gate, v_f1_up, v_f1_down, v_mix_norm, v_w_in, v_b_in_attn, v_rw_mu, v_rw_w0, v_rw_decay_up, v_rw_a0, v_rw_aaa_up, v_rw_gate_up, v_rw_k_k, v_rw_k_a, v_rw_r_k, v_rw_lnx_w, v_rw_lnx_b, v_attn_sinks, v_w_out, v_b_out, v_xa_norm, v_mem_norm, v_w_xq, v_w_xkv, v_w_xo, v_f2_norm, v_f2_gate, v_f2_up, v_f2_down, v_final_norm):
    names = ["f1_norm", "f1_gate", "f1_up", "f1_down", "mix_norm", "w_in", "b_in_attn", "rw_mu", "rw_w0", "rw_decay_up",
             "rw_a0", "rw_aaa_up", "rw_gate_up", "rw_k_k", "rw_k_a", "rw_r_k", "rw_lnx_w", "rw_lnx_b", "attn_sinks", "w_out",
             "b_out", "xa_norm", "mem_norm", "w_xq", "w_xkv", "w_xo", "f2_norm", "f2_gate", "f2_up", "f2_down", "final_norm"]
    env = dict(locals())
    w_of = {k: env[k] for k in names}
    m_of = {k: env["m_" + k] for k in names}
    v_of = {k: env["v_" + k] for k in names}
    col_sharded = ["f1_gate", "f1_up", "w_in", "rw_decay_up", "rw_aaa_up", "rw_gate_up", "w_xkv", "f2_gate", "f2_up"]
    row_sharded = ["f1_down", "w_out", "w_xq", "w_xo", "f2_down"]
    sharded = col_sharded + row_sharded
    small = [k for k in names if k not in sharded]

    x0, mem0, tgt = x[0], mem[0], loss_target[0]
    t, d = x0.shape
    c = rw_w0.shape[-1]
    heads = c // HEAD_DIM
    dl, al, gl = rw_decay_up.shape[1], rw_aaa_up.shape[1], rw_gate_up.shape[1]
    dlp, alp, glp = _up128(dl), _up128(al), _up128(gl)
    swa_w = d - c
    hq, kvh = swa_w // HEAD_DIM, (b_in_attn.shape[-1] - swa_w) // (2 * HEAD_DIM)
    my_x, my_y, my_c = _position()
    c_idx = jnp.reshape(my_c, (1,)).astype(jnp.int32)
    chip_idx = jnp.reshape(2 * my_x + my_y, (1,)).astype(jnp.int32)

    shard2d = {k: w_of[k][0] for k in sharded}
    cast = {k: _rows("cast_" + k, lambda a: (a,), [shard2d[k]], [], [(shard2d[k].shape[1], bf16)], tm=_row_tile(*shard2d[k].shape))[0]
            for k in sharded}
    ffn1_keys, ffn2_keys = ["f1_gate", "f1_up", "f1_down"], ["f2_gate", "f2_up", "f2_down"]
    in_keys = ["w_in", "rw_decay_up", "rw_aaa_up", "rw_gate_up"]
    kept_in_blocks = ffn1_keys + ffn2_keys + ["w_in", "w_xkv"]

    def whole(k, g8):
        if k in kept_in_blocks:
            return g8
        if k in col_sharded:
            return g8.transpose(1, 0, 2).reshape(g8.shape[1], N_DEV * g8.shape[2])
        return g8.reshape(N_DEV * g8.shape[1], g8.shape[2])

    def gather_of(keys):
        return _gather_comm([cast[k] for k in keys])

    def wholes(keys, gathered):
        return {k: whole(k, g8) for k, g8 in zip(keys, gathered)}

    (h1,), gathered = _rows("f1_norm", lambda xb, g: (_rms(xb, g),), [x0], [f1_norm], [(d, bf16)], comm=gather_of(ffn1_keys[:2]))
    full = wholes(ffn1_keys[:2], gathered)
    (gate1, up1, act1), gathered = _ffn_hidden("f1_hidden", h1, full["f1_gate"], full["f1_up"], comm=gather_of(["f1_down"]))
    full.update(wholes(["f1_down"], gathered))
    x1, gathered = _ffn_out("f1_out", act1, full["f1_down"], x0, comm=gather_of(in_keys))
    full.update(wholes(in_keys, gathered))
    ffn1_saved = (h1, gate1, up1, act1)
    w_rkv, w_lora, w_swa = _win_split(full["w_in"], c, (dl, al, gl))
    o1, o2, o3, shift_cols, _ = _lora_bounds(c, (dl, al, gl))
    mu_rkv = rw_mu[:, :3 * c]
    mu_l = jnp.concatenate([_pad_to(rw_mu[:, o1:o2], dlp, 1), _pad_to(rw_mu[:, o2:o3], alp, 1),
                            _pad_to(rw_mu[:, o3:shift_cols], glp, 1)], axis=1)
    decay_up = _pad_to(full["rw_decay_up"], dlp, 0).astype(f32)
    aaa_up = _pad_to(full["rw_aaa_up"], alp, 0).astype(f32)
    gate_up = _pad_to(full["rw_gate_up"], glp, 0).astype(f32)
    head_pars = [p.reshape(heads, 1, HEAD_DIM) for p in (rw_k_k, rw_k_a, rw_r_k, rw_lnx_w, rw_lnx_b)]
    final_g = final_norm.reshape(1, d)

    (h2,) = _rows("mix_norm", lambda xb, g: (_rms(xb, g),), [x1], [mix_norm], [(d, bf16)])
    p_rkv, gathered = _mm("in_rkv", h2, w_rkv, "nn", f32, comm=gather_of(["w_out"]))
    full.update(wholes(["w_out"], gathered))
    p_l =_mm("in_lora", h2, w_lora, "nn", f32)
    za = _mm("in_swa", h2, w_swa, "nn", f32, bias=b_in_attn)
    pre_fn = functools.partial(_rwkv_pre_math, c, (dlp, alp, glp))
    pre_rows = [p_rkv, _shift_down(p_rkv), p_l, _shift_down(p_l)]
    pre_full = [mu_rkv, mu_l, rw_w0, rw_a0, decay_up, aaa_up, gate_up]
    (r_h, k_h, v_h, lw_h, a_h, g_t), gathered = _rows("rwkv_pre", pre_fn, pre_rows, pre_full,
                                                       [(c, f32, HEAD_DIM)] * 5 + [(c, f32)], tm=128, comm=gather_of(["w_xq"]))
    full.update(wholes(["w_xq"], gathered))
    seqs = [r_h, k_h, v_h, a_h, lw_h]
    (y_heads, checkpoints), gathered = _rwkv_fwd(*seqs, head_pars, comm=gather_of(["f2_gate"]))
    full.update(wholes(["f2_gate"], gathered))

    pos = jnp.arange(t, dtype=f32)
    inv_freq = ROPE_THETA ** (-jnp.arange(0, HEAD_DIM, 2, dtype=f32) / HEAD_DIM)
    ang = pos[:, None] * inv_freq[None, :]
    cos, sin = jnp.cos(ang), jnp.sin(ang)
    sinks3 = attn_sinks.reshape(kvh, hq // kvh, 1)
    swa_in = (*_swa_split(za, hq, kvh), cos, sin, sinks3)
    y_swa_heads, gathered = _swa_fwd(*swa_in, comm=gather_of(["w_xkv"]))
    full.update(wholes(["w_xkv"], gathered))
    up_rows = cast["f2_up"].shape[0] // 2
    up_halves = [cast["f2_up"][:up_rows], cast["f2_up"][up_rows:]]
    (ycat,) = _rows("mix_cat", lambda yb, gb, sb: (jnp.concatenate([yb * gb, sb], axis=1),), [y_heads, g_t, y_swa_heads], [],
                    [(d, bf16)])
    x2, (f2_up_top,) = _mm("mix_out", ycat, full["w_out"], "nn", f32, res=x1, bias=b_out, comm=_gather_comm(up_halves[:1]))

    (h3,) = _rows("xa_norm", lambda xb, g: (_rms(xb, g),), [x2], [xa_norm], [(d, bf16)])
    (mem_n,) = _rows("mem_norm", lambda xb, g: (_rms(xb, g),), [mem0], [mem_norm], [(d, bf16)])
    q_x, gathered = _mm("xa_q", h3, full["w_xq"], "nn", bf16, comm=gather_of(["w_xo"]))
    full.update(wholes(["w_xo"], gathered))
    kv_x = _mm("xa_kv", mem_n, full["w_xkv"], "nn", bf16)
    o_x = _xattn_fwd(q_x, kv_x)
    x3, (f2_up_bottom,) = _mm("xa_out", o_x, full["w_xo"], "nn", f32, res=x2, comm=_gather_comm(up_halves[1:]))
    full["f2_up"] = jnp.concatenate([f2_up_top, f2_up_bottom], axis=1)
    (h4,) = _rows("f2_norm", lambda xb, g: (_rms(xb, g),), [x3], [f2_norm], [(d, bf16)])
    (gate2, up2, act2), gathered = _ffn_hidden("f2_hidden", h4, full["f2_gate"], [f2_up_top, f2_up_bottom], comm=gather_of(["f2_down"]))
    full.update(wholes(["f2_down"], gathered))
    x4 = _ffn_out("f2_out", act2, full["f2_down"], x3)
    ffn2_saved = (h4, gate2, up2, act2)

    def loss_fn(xb, tb, g):
        def per_row(xv, gv):
            return 0.5 * jnp.mean(jnp.square(_rms(xv, gv) - tb), axis=-1, keepdims=True)

        lrow, vjp = jax.vjp(per_row, xb, g)
        dxb, dgb = vjp(jnp.ones_like(lrow))
        return dxb, dgb, jnp.sum(lrow, axis=0, keepdims=True)

    dx4, d_final, loss_part = _rows("loss", loss_fn, [x4, tgt], [final_g], [(d, f32)], [((1, d), f32), ((1, 1), f32)])
    loss = lax.psum(loss_part[0, 0], ("x", "y", "c"))

    grads, small_g, out = {}, {"final_norm": d_final}, {}

    def pair_sums_of(tag, keys, carrier=None):
        blocks = []
        for k in keys:
            g2 = grads[k]
            rr, cc = shard2d[k].shape
            if k in kept_in_blocks:
                blocks.append(g2)
            else:
                blocks.append(g2.reshape(g2.shape[0], N_DEV, cc).transpose(1, 0, 2) if k in col_sharded else g2.reshape(N_DEV, rr, cc))
        if carrier is None:
            from_sibling = _comm_only("grads_to_sibling_" + tag, _sibling_comm(blocks))
        else:
            carried, from_sibling = carrier(_sibling_comm(blocks))
        pairs = [_pair_add("pair_add_" + k, b, o, c_idx) for k, b, o in zip(keys, blocks, from_sibling)]
        return pairs if carrier is None else (pairs, carried)

    def update(keys, pair_sums, from_chips):
        for k, part, others in zip(keys, pair_sums, from_chips):
            res = _adam_sharded("adam_" + k, shard2d[k], m_of[k][0], v_of[k][0], part, others, chip_idx)
            out[k] = [a.reshape(w_of[k].shape) for a in res]

    def ffn_backward(tag, keys, xin, g_norm, saved, dout, first_comm, start_exchange):
        h, gate, up, act = saved
        k_gate, k_up, k_down = keys
        if first_comm is None:
            grads[k_down], carried = _ffn_dw_down(tag + "_dw_down", act, dout), None
        else:
            grads[k_down], carried = _ffn_dw_down(tag + "_dw_down", act, dout, comm=first_comm)
        down_pairs, (dgate, dup) = pair_sums_of(
            k_down, [k_down], lambda cm: _ffn_dhidden(tag + "_dhidden", dout, full[k_down], gate, up, comm=cm))
        (grads[k_gate], grads[k_up]), from_chips = _ffn_dw_hidden(tag + "_dw_hidden", h, dgate, dup, comm=_chips_comm(down_pairs))
        update([k_down], down_pairs, from_chips)
        hidden_pairs, dh = pair_sums_of(tag + "_hidden", [k_gate, k_up],
                                        lambda cm: _ffn_dh(tag + "_dh1", dgate, full[k_gate], comm=cm))
        pending = None
        if start_exchange:
            pending, dh = _chips_start("grads_to_chips_start_" + tag, hidden_pairs, dh)
        dh = _ffn_dh(tag + "_dh2", dup, full[k_up], res=dh)
        dx, dg_norm = _norm_bwd(tag + "_dnorm", xin, g_norm, dh, dout)
        return dx, dg_norm, hidden_pairs, carried, pending

    xa_keys = ["w_xq", "w_xkv", "w_xo"]
    dx3, small_g["f2_norm"], ffn2_pairs, _, _ = ffn_backward("f2b", ffn2_keys, x3, f2_norm, ffn2_saved, dx4, None, False)

    do_x = _mm("xa_do", dx3, full["w_xo"], "nt", bf16)
    grads["w_xo"] = _mm("xa_dwo", o_x, dx3, "tn", bf16)
    dq_x, dk_x, dv_x = _xattn_bwd(q_x, kv_x, do_x)
    grads["w_xq"] = _mm("xa_dwq", h3, dq_x, "tn", bf16)
    dh3 = _mm("xa_dh", dq_x, full["w_xq"], "nt", f32)
    dkv_x = jnp.concatenate([dk_x, dv_x], axis=1)
    grads["w_xkv"] = _mm("xa_dwkv", mem_n, dkv_x, "tn", bf16, out_blocks=N_DEV)
    dkv_blocks = dkv_x.astype(bf16).reshape(dkv_x.shape[0], N_DEV, -1).transpose(1, 0, 2)
    dmem_n = _ffn_dh("xa_dmem", dkv_blocks, full["w_xkv"])
    _, small_g["mem_norm"] = _norm_bwd("mem_dnorm", mem0, mem_norm, dmem_n, jnp.zeros_like(mem0))
    xa_pairs, (dx2, small_g["xa_norm"]) = pair_sums_of(
        "xa", xa_keys, lambda cm: _norm_bwd("xa_dnorm", x2, xa_norm, dh3, dx3, comm=cm))

    dycat = _mm("mix_dy", dx2, full["w_out"], "nt", f32)
    grads["w_out"] = _mm("mix_dwout", ycat, dx2, "tn", bf16)
    small_g["b_out"] = _colsum("mix_dbout", dx2)
    out_pairs, (dy_heads, dg_t, do_sw) = pair_sums_of("out", ["w_out"], lambda cm: _rows(
        "mix_dgate", lambda db, yb, gb: (db[:, :c] * gb, db[:, :c] * yb, db[:, c:]), [dycat, y_heads, g_t], [],
        [(c, f32, HEAD_DIM), (c, f32), (swa_w, f32, HEAD_DIM)], comm=cm))
    rw_grads, from_chips = _rwkv_bwd(*seqs, head_pars, checkpoints, dy_heads, comm=_chips_comm(ffn2_pairs))
    update(ffn2_keys[:2], ffn2_pairs, from_chips)
    dr_h, dk_h, dv_h, da_h, dlw_h = rw_grads[:5]
    for nm, gh in zip(("rw_k_k", "rw_k_a", "rw_r_k", "rw_lnx_w", "rw_lnx_b"), rw_grads[5:]):
        small_g[nm] = gh.reshape(w_of[nm].shape)

    def pre_bwd(*args):
        _, vjp = jax.vjp(pre_fn, *args[:4], *args[10:])
        return vjp(tuple(args[4:10]))

    pre_cts = [dr_h, dk_h, dv_h, dlw_h, da_h, dg_t]
    pre_out = _rows("rwkv_pre_bwd", pre_bwd, pre_rows + pre_cts, pre_full,
                    [(3 * c, f32), (3 * c, f32), (dlp + alp + glp, f32), (dlp + alp + glp, f32)],
                    [(p.shape, f32) for p in pre_full], tm=128)
    dp_rkv = pre_out[0] + _shift_up(pre_out[1])
    dp_l = pre_out[2] + _shift_up(pre_out[3])
    dmu_rkv, dmu_l, small_g["rw_w0"], small_g["rw_a0"], d_decay_up, d_aaa_up, d_gate_up = pre_out[4:]
    small_g["rw_mu"] = jnp.concatenate([dmu_rkv, dmu_l[:, :dl], dmu_l[:, dlp:dlp + al], dmu_l[:, dlp + alp:dlp + alp + gl]], axis=1)
    grads["rw_decay_up"] = d_decay_up[:dl].astype(bf16)
    grads["rw_aaa_up"] = d_aaa_up[:al].astype(bf16)
    grads["rw_gate_up"] = d_gate_up[:gl].astype(bf16)

    sw, from_chips = _swa_bwd(*swa_in, do_sw, comm=_chips_comm(xa_pairs))
    update(xa_keys, xa_pairs, from_chips)
    small_g["attn_sinks"] = sw[8].reshape(attn_sinks.shape)
    dza, small_g["b_in_attn"] = _swa_merge(sw[0], sw[1], sw[3], sw[2], sw[5], sw[4], sw[7], sw[6])

    dw_rkv = _mm("in_dwrkv", h2, dp_rkv, "tn", bf16)
    dw_l = _mm("in_dwlora", h2, dp_l, "tn", bf16)
    dw_swa = _mm("in_dwswa", h2, dza, "tn", bf16)
    grads["w_in"] = _win_merge(dw_rkv, dw_l, dw_swa, c, (dl, al, gl), N_DEV)
    dh2, from_chips = _mm("in_dh1", dp_rkv, w_rkv, "nt", f32, comm=_chips_comm(out_pairs))
    update(["w_out"], out_pairs, from_chips)
    dh2 = _mm("in_dh2", dp_l, w_lora, "nt", f32, res=dh2)
    in_pairs, dh2 = pair_sums_of("in", in_keys, lambda cm: _mm("in_dh3", dza, w_swa, "nt", f32, res=dh2, comm=cm))
    pending_in, dh2 = _chips_start("grads_to_chips_start_in", in_pairs, dh2)
    dx1, small_g["mix_norm"] = _norm_bwd("mix_dnorm", x1, mix_norm, dh2, dx2)

    dx0, small_g["f1_norm"], _, _, pending = ffn_backward("f1b", ffn1_keys, x0, f1_norm, ffn1_saved, dx1, None, True)
    others_done = _fence("updates_done", [res[1] for res in out.values()] + [dx0] + list(small_g.values()))
    in_pairs, from_chips = _chips_wait("grads_to_chips_wait_in", *pending_in, others_done)
    update(in_keys, in_pairs, from_chips)
    pairs, from_chips = _chips_wait("grads_to_chips_wait_f1b", *pending, others_done)
    update(ffn1_keys[:2], pairs, from_chips)

    sizes = [int(w_of[k].size) for k in small]
    total = sum(sizes)
    cols = -(-total // (8 * LANE)) * LANE

    def pack(parts_of):
        flat = jnp.concatenate([parts_of[k].reshape(-1).astype(f32) for k in small])
        return _pad_to(flat, 8 * cols, 0).reshape(8, cols)

    (all_parts,) = _comm_only("gather_small_grads", _gather_comm([pack(small_g)], after=from_chips))
    res = _adam_small("adam_small", pack(w_of), pack(m_of), pack(v_of), all_parts)
    offs = 0
    flat_res = [a.reshape(-1) for a in res]
    for k, sz in zip(small, sizes):
        out[k] = [a[offs:offs + sz].reshape(w_of[k].shape) for a in flat_res]
        offs += sz

    outs = [loss, dx0.reshape(x.shape)]
    for j in range(4):
        outs += [out[k][j] for k in names]
    return tuple(outs)
```

```python
import functools
import math

import jax
import jax.numpy as jnp
from jax import lax
from jax.experimental import pallas as pl
from jax.experimental.pallas import tpu as pltpu

f32 = jnp.float32
bf16 = jnp.bfloat16
MXU_DTYPE = jnp.bfloat16

HEAD_DIM = 64
SWA_BLOCK = 128
ROPE_THETA = 10000.0
XATTN_HEADS = 4
RMS_EPS = 1e-6
GN_EPS = 64e-5
NEG_INF = -1e30
RWKV_CHUNK = 64

ADAM_LR = 0.001
ADAM_B1 = 0.9
ADAM_B2 = 0.999
ADAM_EPS = 1e-08
ADAM_WD = 0.01
ADAM_STEP = 10

N_DEV = 8
LANE = 128
VMEM_LIMIT_BYTES = 56 * 1024 * 1024
MM_VMEM_BUDGET = 40 * 1024 * 1024
MESH = pl.DeviceIdType.MESH


def _params(sem):
    return pltpu.CompilerParams(dimension_semantics=sem, vmem_limit_bytes=VMEM_LIMIT_BYTES)


class _Comm:
    def __init__(self, ins, outs, n_remote, n_local, start, finish):
        self.ins, self.outs, self.n_remote, self.n_local = list(ins), list(outs), n_remote, max(n_local, 1)
        self.start, self.finish = start, finish


def _pcall(body, comm=None, **kw):
    kw.setdefault("compiler_params", pltpu.CompilerParams(vmem_limit_bytes=VMEM_LIMIT_BYTES))
    if comm is None:
        return pl.pallas_call(body, **kw)
    single = not isinstance(kw["out_shape"], (list, tuple))
    out_shape = [kw["out_shape"]] if single else list(kw["out_shape"])
    out_specs = [kw["out_specs"]] if single else list(kw["out_specs"])
    in_specs, scratch, grid = list(kw["in_specs"]), list(kw.get("scratch_shapes", ())), tuple(kw.get("grid", ()))
    n_in, n_out, n_ci, n_co, n_scr = len(in_specs), len(out_shape), len(comm.ins), len(comm.outs), len(scratch)

    def wrapped(*refs):
        ins, c_ins = refs[:n_in], refs[n_in:n_in + n_ci]
        outs = refs[n_in + n_ci:n_in + n_ci + n_out]
        c_outs = refs[n_in + n_ci + n_out:n_in + n_ci + n_out + n_co]
        rest = refs[n_in + n_ci + n_out + n_co:]
        scr, sems = rest[:n_scr], rest[n_scr:]
        if grid:
            ids = [pl.program_id(k) for k in range(len(grid))]
            first = functools.reduce(jnp.logical_and, [i == 0 for i in ids])
            last = functools.reduce(jnp.logical_and, [i == g - 1 for i, g in zip(ids, grid)])
            pl.when(first)(lambda: comm.start(c_ins, c_outs, *sems))
            body(*ins, *outs, *scr)
            pl.when(last)(lambda: comm.finish(c_ins, c_outs, *sems))
        else:
            comm.start(c_ins, c_outs, *sems)
            body(*ins, *outs, *scr)
            comm.finish(c_ins, c_outs, *sems)

    any_spec = pl.BlockSpec(memory_space=pl.ANY)
    kw.update(in_specs=in_specs + [any_spec] * n_ci, out_specs=out_specs + [any_spec] * n_co,
              out_shape=out_shape + comm.outs,
              scratch_shapes=scratch + [pltpu.SemaphoreType.DMA((comm.n_remote,)), pltpu.SemaphoreType.DMA((comm.n_remote,)),
                                        pltpu.SemaphoreType.DMA((comm.n_local,))])
    if grid:
        kw["compiler_params"] = _params(("arbitrary",) * len(grid))
    call = pl.pallas_call(wrapped, **kw)

    def run(*args):
        res = call(*args, *comm.ins)
        return (res[0] if single else list(res[:n_out])), list(res[n_out:])

    return run


def _fence(name, arrays):
    def body(*refs):
        refs[-1][...] = jnp.zeros(refs[-1].shape, f32)

    return _pcall(body, name=name, in_specs=[pl.BlockSpec(memory_space=pl.ANY)] * len(arrays),
                  out_specs=pl.BlockSpec(memory_space=pltpu.VMEM), out_shape=jax.ShapeDtypeStruct((8, LANE), f32))(*arrays)


def _comm_only(name, comm):
    return _pcall(lambda: None, comm=comm, name=name, in_specs=[], out_specs=[], out_shape=[])()[1]


def _dims(kind, ndim):
    o = ndim - 2
    batch = ((0,), (0,)) if o else ((), ())
    c = {"nn": ((1 + o,), (o,)), "nt": ((1 + o,), (1 + o,)), "tn": ((o,), (o,))}[kind]
    return (c, batch)


def _dot_raw(x, y, kind):
    return lax.dot_general(x.astype(MXU_DTYPE), y.astype(MXU_DTYPE), _dims(kind, x.ndim), preferred_element_type=f32)


@functools.partial(jax.custom_vjp, nondiff_argnums=(2,))
def _dot(x, y, kind):
    return _dot_raw(x, y, kind)


def _dot_fwd(x, y, kind):
    return _dot_raw(x, y, kind), (x, y)


def _dot_bwd(kind, res, g):
    x, y = res
    if kind == "nn":
        dx, dy = _dot(g, y, "nt"), _dot(x, g, "tn")
    elif kind == "nt":
        dx, dy = _dot(g, y, "nn"), _dot(g, x, "tn")
    else:
        dx, dy = _dot(y, g, "nt"), _dot(x, g, "nn")
    return dx.astype(x.dtype), dy.astype(y.dtype)


_dot.defvjp(_dot_fwd, _dot_bwd)


def _split3(x):
    a = x.astype(bf16)
    r = x - a.astype(f32)
    b = r.astype(bf16)
    c = (r - b.astype(f32)).astype(bf16)
    return a, b, c


def _rms(x, g):
    x = x.astype(f32)
    return x * lax.rsqrt(jnp.mean(x * x, axis=-1, keepdims=True) + RMS_EPS) * g


def _sigmoid(x):
    return 1.0 / (1.0 + jnp.exp(-x))


def _softplus(x):
    return jnp.maximum(x, 0.0) + jnp.log(1.0 + jnp.exp(-jnp.abs(x)))


def _rows(name, fn, row_ins, full_ins, row_outs, acc_outs=(), tm=None, comm=None):
    width = lambda a: a.shape[1] if a.ndim == 2 else a.shape[0] * a.shape[2]
    rows = row_ins[0].shape[0] if row_ins[0].ndim == 2 else row_ins[0].shape[1]
    if tm is None:
        tm = _row_tile(rows, max([width(a) for a in row_ins] + [o[0] for o in row_outs]))
    tm = min(tm, rows)
    assert rows % tm == 0, (name, rows, tm)
    n_in = len(row_ins) + len(full_ins)
    n_o, n_a = len(row_outs), len(acc_outs)

    def load(k, ref):
        if k < len(row_ins) and row_ins[k].ndim == 3:
            return jnp.concatenate([ref[h] for h in range(ref.shape[0])], axis=-1)
        return ref[...]

    def body(*refs):
        vals = [load(k, r) for k, r in enumerate(refs[:n_in])]
        outs = fn(*vals)
        o_refs = refs[n_in:n_in + n_o]
        a_refs = refs[n_in + n_o:]
        for k in range(n_o):
            if len(row_outs[k]) == 3:
                n = row_outs[k][2]
                for h in range(row_outs[k][0] // n):
                    o_refs[k][h] = outs[k][:, h * n:(h + 1) * n].astype(o_refs[k].dtype)
            else:
                o_refs[k][...] = outs[k].astype(o_refs[k].dtype)
        if n_a:
            first = pl.program_id(0) == 0

            @pl.when(first)
            def _():
                for k in range(n_a):
                    a_refs[k][...] = outs[n_o + k].astype(a_refs[k].dtype)

            @pl.when(jnp.logical_not(first))
            def _():
                for k in range(n_a):
                    a_refs[k][...] += outs[n_o + k].astype(a_refs[k].dtype)

    by_rows = lambda cols: pl.BlockSpec((tm, cols), lambda i: (i, 0))
    by_heads = lambda h, n: pl.BlockSpec((h, tm, n), lambda i: (0, i, 0))
    in_specs = [by_rows(a.shape[1]) if a.ndim == 2 else by_heads(a.shape[0], a.shape[2]) for a in row_ins]
    in_specs += [pl.BlockSpec(a.shape, lambda i, nd=a.ndim: (0,) * nd) for a in full_ins]
    out_specs = [by_rows(o[0]) if len(o) == 2 else by_heads(o[0] // o[2], o[2]) for o in row_outs]
    out_specs += [pl.BlockSpec(s, lambda i, nd=len(s): (0,) * nd) for s, _ in acc_outs]
    out_shape = [jax.ShapeDtypeStruct((rows, o[0]) if len(o) == 2 else (o[0] // o[2], rows, o[2]), o[1]) for o in row_outs]
    out_shape += [jax.ShapeDtypeStruct(s, d) for s, d in acc_outs]
    return _pcall(body, comm=comm, name=name, grid=(rows // tm,), in_specs=in_specs, out_specs=out_specs, out_shape=out_shape,
                  compiler_params=_params(("arbitrary",)))(*row_ins, *full_ins)


def _pick(n, cands):
    for c in cands:
        if n % c == 0:
            return c
    return n


def _mm(name, a, b, mode, out_dtype, scale=1.0, res=None, bias=None, comm=None, out_blocks=None):
    b_blocks = b.ndim == 3
    if b_blocks:
        assert mode == "nn"
        (m, k), (nb, k2, tn) = a.shape, b.shape
        n = nb * tn
    elif mode == "nn":
        (m, k), (k2, n) = a.shape, b.shape
    elif mode == "nt":
        (m, k), (n, k2) = a.shape, b.shape
    else:
        (k, m), (k2, n) = a.shape, b.shape
    assert k == k2, (name, a.shape, b.shape, mode)
    if not b_blocks:
        tn = n // out_blocks if out_blocks else _pick(n, (512, 256, 128))
    tm = _pick(m, (1024, 512, 256, 128))

    def need(tm_):
        by = tm_ * k * a.dtype.itemsize + tn * k * b.dtype.itemsize + tm_ * tn * (jnp.dtype(out_dtype).itemsize + 4)
        if res is not None:
            by += tm_ * tn * res.dtype.itemsize
        return 2 * by

    while need(tm) > MM_VMEM_BUDGET and tm % 256 == 0:
        tm //= 2
    dims = _dims(mode, 2)

    def body(*refs):
        bv = refs[1][0] if b_blocks else refs[1][...]
        acc = lax.dot_general(refs[0][...].astype(MXU_DTYPE), bv.astype(MXU_DTYPE), dims, preferred_element_type=f32)
        if scale != 1.0:
            acc = acc * scale
        pos = 2
        if bias is not None:
            acc = acc + refs[pos][...]
            pos += 1
        if res is not None:
            acc = acc + refs[pos][...].astype(f32)
            pos += 1
        if out_blocks:
            refs[pos][0] = acc.astype(out_dtype)
        else:
            refs[pos][...] = acc.astype(out_dtype)

    a_spec = pl.BlockSpec((k, tm), lambda i, j: (0, i)) if mode == "tn" else pl.BlockSpec((tm, k), lambda i, j: (i, 0))
    if b_blocks:
        b_spec = pl.BlockSpec((1, k, tn), lambda i, j: (j, 0, 0))
    else:
        b_spec = pl.BlockSpec((tn, k), lambda i, j: (j, 0)) if mode == "nt" else pl.BlockSpec((k, tn), lambda i, j: (0, j))
    in_specs, args = [a_spec, b_spec], [a, b]
    if bias is not None:
        in_specs.append(pl.BlockSpec((1, tn), lambda i, j: (0, j)))
        args.append(bias)
    if res is not None:
        in_specs.append(pl.BlockSpec((tm, tn), lambda i, j: (i, j)))
        args.append(res)
    if out_blocks:
        out_spec, out_shape = pl.BlockSpec((1, tm, tn), lambda i, j: (j, i, 0)), jax.ShapeDtypeStruct((out_blocks, m, tn), out_dtype)
    else:
        out_spec, out_shape = pl.BlockSpec((tm, tn), lambda i, j: (i, j)), jax.ShapeDtypeStruct((m, n), out_dtype)
    return _pcall(body, comm=comm, name=name, grid=(m // tm, n // tn), in_specs=in_specs, out_specs=out_spec, out_shape=out_shape,
                  compiler_params=_params(("parallel", "parallel")))(*args)


def _position():
    return lax.axis_index("x"), lax.axis_index("y"), lax.axis_index("c")


def _gather_comm(shards, after=()):
    n = len(shards)

    def plan(x_refs, o_refs, send_sems, recv_sems, local_sems):
        x, y, c = _position()
        me, sibling = (x, y, c), (x, y, 1 - c)
        chips = [(1 - x, y), (x, 1 - y), (1 - x, 1 - y)]

        def slot(px, py, pc):
            return 4 * px + 2 * py + pc

        def copy(t, k, block, to, src=None):
            dst = o_refs[t].at[slot(*block)]
            return pltpu.make_async_remote_copy(src_ref=dst if src is None else src, dst_ref=dst,
                                                send_sem=send_sems.at[7 * t + k], recv_sem=recv_sems.at[7 * t + k],
                                                device_id=to, device_id_type=MESH)

        mine = [pltpu.make_async_copy(x_refs[t], o_refs[t].at[slot(*me)], local_sems.at[t]) for t in range(n)]
        first = []
        for t in range(n):
            first.append(copy(t, 0, me, sibling, src=x_refs[t]))
            first += [copy(t, 1 + j, me, (*chip, c), src=x_refs[t]) for j, chip in enumerate(chips)]
        return me, sibling, chips, c, copy, mine, first

    def start(*refs):
        _, _, _, _, _, mine, first = plan(*refs)
        for cp in mine + first:
            cp.start()

    def finish(*refs):
        me, sibling, chips, c, copy, mine, first = plan(*refs)
        passed = []
        for t in range(n):
            for j, chip in enumerate(chips):
                copy(t, 1 + j, (*chip, c), me).wait_recv()
                cp = copy(t, 4 + j, (*chip, c), sibling)
                cp.start()
                passed.append(cp)
        for t in range(n):
            copy(t, 0, sibling, me).wait_recv()
            for j, chip in enumerate(chips):
                copy(t, 4 + j, (*chip, 1 - c), me).wait_recv()
        for cp in first + passed:
            cp.wait_send()
        for cp in mine:
            cp.wait()

    outs = [jax.ShapeDtypeStruct((N_DEV,) + s.shape, s.dtype) for s in shards]
    return _Comm(list(shards) + list(after), outs, 7 * n, n, start, finish)


def _sibling_comm(blocks):
    n = len(blocks)

    def copies(g_refs, o_refs, send_sems, recv_sems, _):
        x, y, c = _position()
        return [pltpu.make_async_remote_copy(src_ref=g_refs[t].at[2 * q + 1 - c], dst_ref=o_refs[t].at[q],
                                             send_sem=send_sems.at[4 * t + q], recv_sem=recv_sems.at[4 * t + q],
                                             device_id=(x, y, 1 - c), device_id_type=MESH)
                for t in range(n) for q in range(4)]

    def start(*refs):
        for cp in copies(*refs):
            cp.start()

    def finish(*refs):
        for cp in copies(*refs):
            cp.wait()

    outs = [jax.ShapeDtypeStruct((4,) + g.shape[1:], g.dtype) for g in blocks]
    return _Comm(blocks, outs, 4 * n, 0, start, finish)


def _chips_comm(parts):
    n = len(parts)

    def copies(p_refs, o_refs, send_sems, recv_sems, _):
        x, y, c = _position()
        chips = [(1 - x, y), (x, 1 - y), (1 - x, 1 - y)]
        return [pltpu.make_async_remote_copy(src_ref=p_refs[t].at[2 * px + py], dst_ref=o_refs[t].at[j],
                                             send_sem=send_sems.at[3 * t + j], recv_sem=recv_sems.at[3 * t + j],
                                             device_id=(px, py, c), device_id_type=MESH)
                for t in range(n) for j, (px, py) in enumerate(chips)]

    def start(*refs):
        for cp in copies(*refs):
            cp.start()

    def finish(*refs):
        for cp in copies(*refs):
            cp.wait()

    outs = [jax.ShapeDtypeStruct((3,) + p.shape[1:], p.dtype) for p in parts]
    return _Comm(parts, outs, 3 * n, 0, start, finish)


HBM_SPEC = pl.BlockSpec(memory_space=pltpu.HBM)
SEM_SPEC = pl.BlockSpec(memory_space=pltpu.SEMAPHORE)
DATAFLOW = pltpu.SideEffectType.DATAFLOW_SIDE_EFFECTING


def _chip_exchange_copies(p_refs, o_refs, send_sems, recv_sems):
    x, y, c = _position()
    chips = [(1 - x, y), (x, 1 - y), (1 - x, 1 - y)]
    return [pltpu.make_async_remote_copy(src_ref=p_refs[t].at[2 * px + py], dst_ref=o_refs[t].at[j],
                                         send_sem=send_sems.at[3 * t + j], recv_sem=recv_sems.at[3 * t + j],
                                         device_id=(px, py, c), device_id_type=MESH)
            for t in range(len(p_refs)) for j, (px, py) in enumerate(chips)]


def _chips_start(name, parts, thru):
    n = len(parts)

    def body(*refs):
        for cp in _chip_exchange_copies(refs[:n], refs[n:2 * n], refs[2 * n + 1], refs[2 * n + 2]):
            cp.start()

    lands = [lax.empty((3,) + p.shape[1:], p.dtype) for p in parts]
    args = [pltpu.with_memory_space_constraint(a, pltpu.HBM) for a in list(parts) + lands + [thru]]
    res = pl.pallas_call(body, name=name, in_specs=[HBM_SPEC] * (2 * n + 1),
                         out_specs=[SEM_SPEC, SEM_SPEC] + [HBM_SPEC] * (2 * n + 1),
                         out_shape=[pltpu.SemaphoreType.DMA((3 * n,)), pltpu.SemaphoreType.DMA((3 * n,))]
                         + [pltpu.HBM(a.shape, a.dtype) for a in args],
                         input_output_aliases={i: 2 + i for i in range(2 * n + 1)},
                         compiler_params=pltpu.CompilerParams(has_side_effects=DATAFLOW))(*args)
    return (res[0], res[1], list(res[2:2 + n]), list(res[2 + n:2 + 2 * n])), res[2 + 2 * n]


def _chips_wait(name, send_sems, recv_sems, parts, lands, after):
    n = len(parts)

    def body(*refs):
        for cp in _chip_exchange_copies(refs[:n], refs[n:2 * n], refs[2 * n], refs[2 * n + 1]):
            cp.wait_send()
            cp.wait_recv()

    res = pl.pallas_call(body, name=name, out_shape=[pltpu.HBM(a.shape, a.dtype) for a in parts + lands],
                         in_specs=[HBM_SPEC] * (2 * n) + [SEM_SPEC, SEM_SPEC, pl.BlockSpec(memory_space=pl.ANY)],
                         out_specs=[HBM_SPEC] * (2 * n), input_output_aliases={i: i for i in range(2 * n)},
                         compiler_params=pltpu.CompilerParams(has_side_effects=DATAFLOW))(*parts, *lands, send_sems, recv_sems, after)
    return list(res[:n]), list(res[n:])


ROW_TILE_BYTES = 2 << 20


def _row_tile(r, cols, itemsize=4):
    fits = [t for t in range(8, r + 1, 8) if r % t == 0 and t * cols * itemsize <= ROW_TILE_BYTES]
    return max(fits) if fits else r


def _pair_add(name, g, got, c_idx):
    _, r, cc = g.shape
    tr = _row_tile(r, cc, g.dtype.itemsize)

    def body(c_ref, g_ref, o_ref, out_ref):
        out_ref[...] = (g_ref[...].astype(f32) + o_ref[...].astype(f32)).astype(out_ref.dtype)

    g5 = g.reshape(4, 2, r, cc)
    spec = pltpu.PrefetchScalarGridSpec(
        num_scalar_prefetch=1, grid=(4, r // tr),
        in_specs=[pl.BlockSpec((1, 1, tr, cc), lambda q, i, c_ref: (q, c_ref[0], i, 0)),
                  pl.BlockSpec((1, 1, tr, cc), lambda q, i, c_ref: (q, 0, i, 0))],
        out_specs=pl.BlockSpec((1, 1, tr, cc), lambda q, i, c_ref: (q, 0, i, 0)))
    out = _pcall(body, name=name, grid_spec=spec, out_shape=jax.ShapeDtypeStruct((4, 1, r, cc), g.dtype),
                 compiler_params=_params(("arbitrary", "arbitrary")))(c_idx, g5, got.reshape(4, 1, r, cc))
    return out.reshape(4, r, cc)


def _adam_math(w, g, m, v):
    m2 = ADAM_B1 * m + (1.0 - ADAM_B1) * g
    v2 = ADAM_B2 * v + (1.0 - ADAM_B2) * (g * g)
    m_hat = m2 / (1.0 - ADAM_B1 ** ADAM_STEP)
    v_hat = v2 / (1.0 - ADAM_B2 ** ADAM_STEP)
    delta = -ADAM_LR * (m_hat / (jnp.sqrt(v_hat) + ADAM_EPS) + ADAM_WD * w)
    return delta, m2, v2


def _adam_sharded(name, w, m, v, part, got, chip_idx):
    r, cc = w.shape
    tr = _row_tile(r, cc)

    def body(q_ref, w_ref, m_ref, v_ref, p_ref, o_ref, g_out, d_out, m_out, v_out):
        g = p_ref[0].astype(f32)
        for j in range(3):
            g = g + o_ref[j].astype(f32)
        d, m2, v2 = _adam_math(w_ref[...], g, m_ref[...], v_ref[...])
        g_out[...] = g
        d_out[...] = d
        m_out[...] = m2
        v_out[...] = v2

    row = pl.BlockSpec((tr, cc), lambda i, q_ref: (i, 0))
    spec = pltpu.PrefetchScalarGridSpec(
        num_scalar_prefetch=1, grid=(r // tr,),
        in_specs=[row, row, row, pl.BlockSpec((1, tr, cc), lambda i, q_ref: (q_ref[0], i, 0)),
                  pl.BlockSpec((3, tr, cc), lambda i, q_ref: (0, i, 0))],
        out_specs=[row, row, row, row])
    sh = jax.ShapeDtypeStruct((r, cc), f32)
    return _pcall(body, name=name, grid_spec=spec, out_shape=[sh, sh, sh, sh],
                  compiler_params=_params(("arbitrary",)))(chip_idx, w, m, v, part, got)


def _adam_small(name, w, m, v, parts):
    def body(w_ref, m_ref, v_ref, p_ref, g_out, d_out, m_out, v_out):
        g = p_ref[0]
        for b in range(1, N_DEV):
            g = g + p_ref[b]
        d, m2, v2 = _adam_math(w_ref[...], g, m_ref[...], v_ref[...])
        g_out[...] = g
        d_out[...] = d
        m_out[...] = m2
        v_out[...] = v2

    sh = jax.ShapeDtypeStruct(w.shape, f32)
    return _pcall(body, name=name, out_shape=[sh, sh, sh, sh])(w, m, v, parts)


def _swa_math(n, qa, qb, kap, kac, kbp, kbc, vp, vc, cq, sq, cp, sp, sink):
    g, blk, half = qa.shape
    c3, s3 = cq[None], sq[None]
    q1 = (qa * c3 - qb * s3).reshape(g * blk, half)
    q2 = (qb * c3 + qa * s3).reshape(g * blk, half)
    ck, sk = jnp.concatenate([cp, cq], axis=0), jnp.concatenate([sp, sq], axis=0)
    k1, k2 = jnp.concatenate([kap[0], kac[0]], axis=0), jnp.concatenate([kbp[0], kbc[0]], axis=0)
    k1r, k2r = k1 * ck - k2 * sk, k2 * ck + k1 * sk
    vv = jnp.concatenate([vp[0], vc[0]], axis=0)
    s = (_dot(q1, k1r, "nt") + _dot(q2, k2r, "nt")) * (HEAD_DIM ** -0.5)
    s = s.reshape(g, blk, 2 * blk)
    qi = lax.broadcasted_iota(jnp.int32, (blk, 2 * blk), 0)
    kj = lax.broadcasted_iota(jnp.int32, (blk, 2 * blk), 1)
    valid = (kj > qi) & (kj <= qi + blk) & ((kj >= blk) | (n > 0))
    s = jnp.where(valid[None], s, NEG_INF)
    sink3 = sink.reshape(g, 1, 1)
    mx = jnp.maximum(jnp.max(s, axis=-1, keepdims=True), sink3)
    e = jnp.exp(s - mx)
    z = jnp.sum(e, axis=-1, keepdims=True) + jnp.exp(sink3 - mx)
    p = (e / z).reshape(g * blk, 2 * blk)
    return _dot(p, vv, "nn").reshape(g, blk, 2 * half)


def _swa_specs(hq, kv, blk, half):
    prev = lambda n: jnp.maximum(n - 1, 0)
    q_spec = pl.BlockSpec((hq, blk, half), lambda n: (0, n, 0))
    kc = pl.BlockSpec((kv, blk, half), lambda n: (0, n, 0))
    kp = pl.BlockSpec((kv, blk, half), lambda n: (0, prev(n), 0))
    vc = pl.BlockSpec((kv, blk, 2 * half), lambda n: (0, n, 0))
    vp = pl.BlockSpec((kv, blk, 2 * half), lambda n: (0, prev(n), 0))
    tc = pl.BlockSpec((blk, half), lambda n: (n, 0))
    tp = pl.BlockSpec((blk, half), lambda n: (prev(n), 0))
    sink = pl.BlockSpec((kv, hq // kv, 1), lambda n: (0, 0, 0))
    o_spec = pl.BlockSpec((hq, blk, 2 * half), lambda n: (0, n, 0))
    return q_spec, kc, kp, vc, vp, tc, tp, sink, o_spec


def _swa_fwd(qa, qb, ka, kb, v, cos, sin, sinks, comm=None):
    hq, t, half = qa.shape
    kv = ka.shape[0]
    g, blk = hq // kv, SWA_BLOCK
    q_spec, kc, kp, vc, vp, tc, tp, sink, o_spec = _swa_specs(hq, kv, blk, half)

    def body(qa_r, qb_r, kap, kac, kbp, kbc, vp_r, vc_r, cq, sq, cp, sp, sink_r, o_r):
        tabs = (cq[...], sq[...], cp[...], sp[...])
        for h in range(kv):
            qs, ks = pl.ds(h * g, g), pl.ds(h, 1)
            o_r[qs] = _swa_math(pl.program_id(0), qa_r[qs], qb_r[qs], kap[ks], kac[ks], kbp[ks], kbc[ks], vp_r[ks], vc_r[ks],
                                *tabs, sink_r[ks]).astype(o_r.dtype)

    return _pcall(body, comm=comm, name="swa_fwd", grid=(t // blk,),
                  in_specs=[q_spec, q_spec, kp, kc, kp, kc, vp, vc, tc, tc, tp, tp, sink], out_specs=o_spec,
                  out_shape=jax.ShapeDtypeStruct((hq, t, 2 * half), f32),
                  compiler_params=_params(("arbitrary",)))(qa, qb, ka, ka, kb, kb, v, v, cos, sin, cos, sin, sinks)


def _swa_bwd(qa, qb, ka, kb, v, cos, sin, sinks, do, comm=None):
    hq, t, half = qa.shape
    kv = ka.shape[0]
    g, blk = hq // kv, SWA_BLOCK
    q_spec, kc, kp, vc, vp, tc, tp, sink, o_spec = _swa_specs(hq, kv, blk, half)

    def body(qa_r, qb_r, kap, kac, kbp, kbc, vp_r, vc_r, cq, sq, cp, sp, sink_r, do_r,
             dqa, dqb, dkap, dkac, dkbp, dkbc, dvp, dvc, dsink):
        n = pl.program_id(0)
        tabs = (cq[...], sq[...], cp[...], sp[...])
        fn = lambda a, b, c_, d, e, f_, g_, h_, s_: _swa_math(n, a, b, c_, d, e, f_, g_, h_, *tabs, s_)
        dsinks = []
        for h in range(kv):
            qs, ks = pl.ds(h * g, g), pl.ds(h, 1)
            _, vjp = jax.vjp(fn, qa_r[qs], qb_r[qs], kap[ks], kac[ks], kbp[ks], kbc[ks], vp_r[ks], vc_r[ks], sink_r[ks])
            grads = vjp(do_r[qs])
            dqa[qs] = grads[0]
            dqb[qs] = grads[1]
            for ref, val in zip((dkap, dkac, dkbp, dkbc, dvp, dvc), grads[2:8]):
                ref[ks] = val
            dsinks.append(grads[8])
        dsink_all = jnp.concatenate(dsinks, axis=0)

        @pl.when(n == 0)
        def _():
            dsink[...] = dsink_all

        @pl.when(n > 0)
        def _():
            dsink[...] += dsink_all

    sh = lambda a: jax.ShapeDtypeStruct(a.shape, f32)
    return _pcall(body, comm=comm, name="swa_bwd", grid=(t // blk,),
                  in_specs=[q_spec, q_spec, kp, kc, kp, kc, vp, vc, tc, tc, tp, tp, sink, o_spec],
                  out_specs=[q_spec, q_spec, kc, kc, kc, kc, vc, vc, sink],
                  out_shape=[sh(qa), sh(qb), sh(ka), sh(ka), sh(kb), sh(kb), sh(v), sh(v), sh(sinks)],
                  compiler_params=_params(("arbitrary",)))(qa, qb, ka, ka, kb, kb, v, v, cos, sin, cos, sin, sinks, do)


def _swa_split(za, hq, kv):
    t = za.shape[0]
    half = HEAD_DIM // 2
    tm = _row_tile(t, za.shape[1])

    def body(z_r, qa, qb, ka, kb, v):
        z = z_r[...]
        for h in range(hq):
            qa[h] = z[:, HEAD_DIM * h:HEAD_DIM * h + half]
            qb[h] = z[:, HEAD_DIM * h + half:HEAD_DIM * (h + 1)]
        for h in range(kv):
            o = HEAD_DIM * (hq + h)
            ka[h] = z[:, o:o + half]
            kb[h] = z[:, o + half:o + HEAD_DIM]
            o = HEAD_DIM * (hq + kv + h)
            v[h] = z[:, o:o + HEAD_DIM]

    spec = lambda n, w: pl.BlockSpec((n, tm, w), lambda i: (0, i, 0))
    sh = lambda n, w: jax.ShapeDtypeStruct((n, t, w), f32)
    return _pcall(body, name="swa_split", grid=(t // tm,), in_specs=[pl.BlockSpec((tm, za.shape[1]), lambda i: (i, 0))],
                  out_specs=[spec(hq, half), spec(hq, half), spec(kv, half), spec(kv, half), spec(kv, HEAD_DIM)],
                  out_shape=[sh(hq, half), sh(hq, half), sh(kv, half), sh(kv, half), sh(kv, HEAD_DIM)],
                  compiler_params=_params(("parallel",)))(za)


def _swa_merge(dqa, dqb, dkac, dkap, dkbc, dkbp, dvc, dvp):
    hq, t, half = dqa.shape
    kv = dkac.shape[0]
    blk = SWA_BLOCK
    nb = t // blk
    cols = HEAD_DIM * (hq + 2 * kv)

    def body(qa, qb, kac, kap, kbc, kbp, vc, vp, z_o, s_o):
        i = pl.program_id(0)
        more = (i < nb - 1).astype(f32)
        pieces = []
        for h in range(hq):
            pieces += [qa[h], qb[h]]
        for h in range(kv):
            pieces += [kac[h] + more * kap[h], kbc[h] + more * kbp[h]]
        for h in range(kv):
            pieces.append(vc[h] + more * vp[h])
        z = jnp.concatenate(pieces, axis=-1)
        z_o[...] = z
        colsum = jnp.sum(z, axis=0, keepdims=True)

        @pl.when(i == 0)
        def _():
            s_o[...] = colsum

        @pl.when(i > 0)
        def _():
            s_o[...] += colsum

    cur = lambda n, w: pl.BlockSpec((n, blk, w), lambda i: (0, i, 0))
    nxt = lambda n, w: pl.BlockSpec((n, blk, w), lambda i: (0, jnp.minimum(i + 1, nb - 1), 0))
    return _pcall(body, name="swa_merge", grid=(nb,),
                  in_specs=[cur(hq, half), cur(hq, half), cur(kv, half), nxt(kv, half), cur(kv, half), nxt(kv, half),
                            cur(kv, HEAD_DIM), nxt(kv, HEAD_DIM)],
                  out_specs=[pl.BlockSpec((blk, cols), lambda i: (i, 0)), pl.BlockSpec((1, cols), lambda i: (0, 0))],
                  out_shape=[jax.ShapeDtypeStruct((t, cols), f32), jax.ShapeDtypeStruct((1, cols), f32)],
                  compiler_params=_params(("arbitrary",)))(dqa, dqb, dkac, dkap, dkbc, dkbp, dvc, dvp)


def _xattn_math(q, k, v):
    s = _dot(q, k, "nt") * (q.shape[-1] ** -0.5)
    e = jnp.exp(s - jnp.max(s, axis=-1, keepdims=True))
    p = e / jnp.sum(e, axis=-1, keepdims=True)
    return _dot(p, v, "nn")


def _xattn_fwd(q, kvm):
    t, d = q.shape
    mlen = kvm.shape[0]
    hd = d // XATTN_HEADS
    tq = min(512, t)

    def body(q_r, k_r, v_r, o_r):
        o_r[...] = _xattn_math(q_r[...], k_r[...], v_r[...]).astype(o_r.dtype)

    return _pcall(body, name="xattn_fwd", grid=(XATTN_HEADS, t // tq),
                  in_specs=[pl.BlockSpec((tq, hd), lambda h, i: (i, h)), pl.BlockSpec((mlen, hd), lambda h, i: (0, h)),
                            pl.BlockSpec((mlen, hd), lambda h, i: (0, XATTN_HEADS + h))],
                  out_specs=pl.BlockSpec((tq, hd), lambda h, i: (i, h)), out_shape=jax.ShapeDtypeStruct((t, d), bf16),
                  compiler_params=_params(("parallel", "parallel")))(q, kvm, kvm)


def _xattn_bwd(q, kvm, do):
    t, d = q.shape
    mlen = kvm.shape[0]
    hd = d // XATTN_HEADS
    tq = min(512, t)

    def body(q_r, k_r, v_r, do_r, dq, dk, dv):
        _, vjp = jax.vjp(_xattn_math, q_r[...].astype(f32), k_r[...].astype(f32), v_r[...].astype(f32))
        gq, gk, gv = vjp(do_r[...].astype(f32))
        dq[...] = gq.astype(dq.dtype)
        first = pl.program_id(1) == 0

        @pl.when(first)
        def _():
            dk[...] = gk
            dv[...] = gv

        @pl.when(jnp.logical_not(first))
        def _():
            dk[...] += gk
            dv[...] += gv

    qs = pl.BlockSpec((tq, hd), lambda h, i: (i, h))
    ms = pl.BlockSpec((mlen, hd), lambda h, i: (0, h))
    return _pcall(body, name="xattn_bwd", grid=(XATTN_HEADS, t // tq),
                  in_specs=[qs, ms, pl.BlockSpec((mlen, hd), lambda h, i: (0, XATTN_HEADS + h)), qs],
                  out_specs=[qs, ms, ms],
                  out_shape=[jax.ShapeDtypeStruct((t, d), bf16), jax.ShapeDtypeStruct((mlen, d), f32),
                             jax.ShapeDtypeStruct((mlen, d), f32)],
                  compiler_params=_params(("parallel", "arbitrary")))(q, kvm, kvm, do)


def _chunk_cumsum(lw, reverse=False):
    h, l, _ = lw.shape
    i = lax.broadcasted_iota(jnp.int32, (l, l), 0)
    j = lax.broadcasted_iota(jnp.int32, (l, l), 1)
    tri = jnp.broadcast_to(((i <= j) if reverse else (i >= j)).astype(bf16)[None], (h, l, l))
    out = jnp.zeros(lw.shape, f32)
    for piece in _split3(lw):
        out = out + lax.dot_general(tri, piece, _dims("nn", 3), preferred_element_type=f32)
    return out


def _rwkv_chunk(s0, r, k, v, a, lw, cl, k_k, k_a, r_k, ln_w, ln_b):
    l = r.shape[1]
    kk = k * k_k
    kk = kk / jnp.maximum(jnp.sqrt(jnp.sum(kk * kk, axis=-1, keepdims=True)), 1e-12)
    km = k * (1.0 + (a - 1.0) * k_a)
    av, bv = -kk, kk * a
    p_incl, p_excl, p_inv = jnp.exp(cl), jnp.exp(cl - lw), jnp.exp(-cl)
    at, bh, kh, rt = av * p_excl, bv * p_inv, km * p_inv, r * p_incl
    i = lax.broadcasted_iota(jnp.int32, (l, l), 0)
    j = lax.broadcasted_iota(jnp.int32, (l, l), 1)
    strict, incl = (i > j)[None], (i >= j)[None]
    a_ab = jnp.where(strict, _dot(at, bh, "nt"), 0.0)
    a_ak = jnp.where(strict, _dot(at, kh, "nt"), 0.0)
    a_rb = jnp.where(incl, _dot(rt, bh, "nt"), 0.0)
    a_rk = jnp.where(incl, _dot(rt, kh, "nt"), 0.0)
    rhs = _dot(at, s0, "nt") + _dot(a_ak, v, "nn")
    inv = a_ab + (i == j)[None].astype(f32)
    pw = a_ab
    for _ in range(int(math.log2(l)) - 1):
        pw = _dot(pw, pw, "nn")
        inv = inv + _dot(inv, pw, "nn")
    sa = _dot(inv, rhs, "nn")
    y = _dot(rt, s0, "nt") + _dot(a_rk, v, "nn") + _dot(a_rb, sa, "nn")
    p_last = p_incl[:, l - 1:l, :]
    s_end = s0 * p_last + _dot(v, kh * p_last, "tn") + _dot(sa, bh * p_last, "tn")
    mu = jnp.mean(y, axis=-1, keepdims=True)
    var = jnp.mean(jnp.square(y - mu), axis=-1, keepdims=True)
    out = (y - mu) * lax.rsqrt(var + GN_EPS) * ln_w + ln_b
    out = out + jnp.sum(r * km * r_k, axis=-1, keepdims=True) * v
    return out, s_end


def _rwkv_fwd(r, k, v, a, lw, heads, comm=None):
    h, t, n = r.shape
    l = min(RWKV_CHUNK, t)
    nc = t // l
    seq = pl.BlockSpec((h, l, n), lambda c: (0, c, 0))
    par = pl.BlockSpec((h, 1, n), lambda c: (0, 0, 0))

    def body(r_r, k_r, v_r, a_r, lw_r, p0, p1, p2, p3, p4, y_r, ck_r, s_scr):
        @pl.when(pl.program_id(0) == 0)
        def _():
            s_scr[...] = jnp.zeros_like(s_scr)

        s0 = s_scr[...]
        ck_r[0] = s0
        lw_v = lw_r[...]
        out, s_end = _rwkv_chunk(s0, r_r[...], k_r[...], v_r[...], a_r[...], lw_v, _chunk_cumsum(lw_v),
                                 p0[...], p1[...], p2[...], p3[...], p4[...])
        y_r[...] = out
        s_scr[...] = s_end

    return _pcall(body, comm=comm, name="rwkv_fwd", grid=(nc,), in_specs=[seq] * 5 + [par] * 5,
                  out_specs=[seq, pl.BlockSpec((1, h, n, n), lambda c: (c, 0, 0, 0))],
                  out_shape=[jax.ShapeDtypeStruct((h, t, n), f32), jax.ShapeDtypeStruct((nc, h, n, n), f32)],
                  scratch_shapes=[pltpu.VMEM((h, n, n), f32)],
                  compiler_params=_params(("arbitrary",)))(r, k, v, a, lw, *heads)


def _rwkv_bwd(r, k, v, a, lw, heads, ck, dy, comm=None):
    h, t, n = r.shape
    l = min(RWKV_CHUNK, t)
    nc = t // l
    seq = pl.BlockSpec((h, l, n), lambda c: (0, nc - 1 - c, 0))
    par = pl.BlockSpec((h, 1, n), lambda c: (0, 0, 0))

    def body(r_r, k_r, v_r, a_r, lw_r, p0, p1, p2, p3, p4, ck_r, dy_r,
             dr, dk, dv, da, dlw, g0, g1, g2, g3, g4, ds_scr):
        first = pl.program_id(0) == 0

        @pl.when(first)
        def _():
            ds_scr[...] = jnp.zeros_like(ds_scr)

        lw_v = lw_r[...]
        _, vjp = jax.vjp(_rwkv_chunk, ck_r[0], r_r[...], k_r[...], v_r[...], a_r[...], lw_v, _chunk_cumsum(lw_v),
                         p0[...], p1[...], p2[...], p3[...], p4[...])
        grads = vjp((dy_r[...], ds_scr[...]))
        ds_scr[...] = grads[0]
        dr[...] = grads[1]
        dk[...] = grads[2]
        dv[...] = grads[3]
        da[...] = grads[4]
        dlw[...] = grads[5] + _chunk_cumsum(grads[6], reverse=True)
        acc = (g0, g1, g2, g3, g4)

        @pl.when(first)
        def _():
            for ref, val in zip(acc, grads[7:]):
                ref[...] = val

        @pl.when(jnp.logical_not(first))
        def _():
            for ref, val in zip(acc, grads[7:]):
                ref[...] += val

    seq_sh = jax.ShapeDtypeStruct((h, t, n), f32)
    par_sh = jax.ShapeDtypeStruct((h, 1, n), f32)
    return _pcall(body, comm=comm, name="rwkv_bwd", grid=(nc,),
                  in_specs=[seq] * 5 + [par] * 5 + [pl.BlockSpec((1, h, n, n), lambda c: (nc - 1 - c, 0, 0, 0)), seq],
                  out_specs=[seq] * 5 + [par] * 5, out_shape=[seq_sh] * 5 + [par_sh] * 5,
                  scratch_shapes=[pltpu.VMEM((h, n, n), f32)],
                  compiler_params=_params(("arbitrary",)))(r, k, v, a, lw, *heads, ck, dy)


def _rwkv_pre_math(c, lp, p_rkv, p_rkv_prev, p_l, p_l_prev, mu_rkv, mu_l, w0, a0, decay_up, aaa_up, gate_up):
    dlp, alp, _ = lp
    z = p_rkv + (p_rkv_prev - p_rkv) * mu_rkv
    zl = p_l + (p_l_prev - p_l) * mu_l
    r, k, v = z[:, :c], z[:, c:2 * c], z[:, 2 * c:]
    wd, ad, gd = zl[:, :dlp], zl[:, dlp:dlp + alp], zl[:, dlp + alp:]
    w = -_softplus(-(w0 + _dot(jnp.tanh(wd), decay_up, "nn"))) - 0.5
    a = _sigmoid(a0 + _dot(ad, aaa_up, "nn"))
    g = _dot(_sigmoid(gd), gate_up, "nn")
    return r, k, v, -jnp.exp(w), a, g


def _pad_to(a, n, axis):
    if a.shape[axis] == n:
        return a
    pad = [(0, 0)] * a.ndim
    pad[axis] = (0, n - a.shape[axis])
    return jnp.pad(a, pad)


def _up128(n):
    return -(-n // LANE) * LANE


def _shift_down(p):
    return jnp.concatenate([jnp.zeros((1, p.shape[1]), p.dtype), p[:-1]], axis=0)


def _shift_up(p):
    return jnp.concatenate([p[1:], jnp.zeros((1, p.shape[1]), p.dtype)], axis=0)


def _swiglu(g, u):
    return jax.nn.silu(g) * u


def _ffn_hidden(name, h, w_gate, w_up, comm=None):
    t, d = h.shape
    nb, _, n = w_gate.shape
    tm = _pick(t, (1024, 512, 256, 128))

    def body(h_r, wg_r, wu_r, g_o, u_o, a_o):
        hv = h_r[...]
        g = lax.dot_general(hv, wg_r[0], _dims("nn", 2), preferred_element_type=f32)
        u = lax.dot_general(hv, wu_r[0], _dims("nn", 2), preferred_element_type=f32)
        g_o[0] = g.astype(bf16)
        u_o[0] = u.astype(bf16)
        a_o[0] = _swiglu(g, u).astype(bf16)

    w_spec = pl.BlockSpec((1, d, n), lambda i, j: (j, 0, 0))
    o_spec = pl.BlockSpec((1, tm, n), lambda i, j: (j, i, 0))
    sh = jax.ShapeDtypeStruct((nb, t, n), bf16)
    return _pcall(body, comm=comm, name=name, grid=(t // tm, nb),
                  in_specs=[pl.BlockSpec((tm, d), lambda i, j: (i, 0)), w_spec, w_spec],
                  out_specs=[o_spec, o_spec, o_spec], out_shape=[sh, sh, sh],
                  compiler_params=_params(("parallel", "arbitrary")))(h, w_gate, w_up)


def _ffn_out(name, act, w_down, x, comm=None):
    nb, t, n = act.shape
    d = w_down.shape[2]
    tm, tn = _pick(t, (512, 256, 128)), _pick(d, (512, 256, 128))

    def body(a_r, w_r, x_r, o_r):
        acc = x_r[...]
        for j in range(nb):
            acc = acc + 0.5 * lax.dot_general(a_r[j], w_r[j], _dims("nn", 2), preferred_element_type=f32)
        o_r[...] = acc

    return _pcall(body, comm=comm, name=name, grid=(t // tm, d // tn),
                  in_specs=[pl.BlockSpec((nb, tm, n), lambda i, j: (0, i, 0)), pl.BlockSpec((nb, n, tn), lambda i, j: (0, 0, j)),
                            pl.BlockSpec((tm, tn), lambda i, j: (i, j))],
                  out_specs=pl.BlockSpec((tm, tn), lambda i, j: (i, j)), out_shape=jax.ShapeDtypeStruct((t, d), f32),
                  compiler_params=_params(("parallel", "parallel")))(act, w_down, x)


def _ffn_dhidden(name, dout, w_down, gate, up, comm=None):
    t, d = dout.shape
    nb, n, _ = w_down.shape
    tm = _pick(t, (512, 256, 128))

    def body(d_r, w_r, g_r, u_r, dg_o, du_o):
        dact = 0.5 * lax.dot_general(d_r[...].astype(MXU_DTYPE), w_r[0], _dims("nt", 2), preferred_element_type=f32)
        _, vjp = jax.vjp(_swiglu, g_r[0].astype(f32), u_r[0].astype(f32))
        dg, du = vjp(dact)
        dg_o[0] = dg.astype(bf16)
        du_o[0] = du.astype(bf16)

    o_spec = pl.BlockSpec((1, tm, n), lambda i, j: (j, i, 0))
    sh = jax.ShapeDtypeStruct((nb, t, n), bf16)
    return _pcall(body, comm=comm, name=name, grid=(t // tm, nb),
                  in_specs=[pl.BlockSpec((tm, d), lambda i, j: (i, 0)), pl.BlockSpec((1, n, d), lambda i, j: (j, 0, 0)), o_spec, o_spec],
                  out_specs=[o_spec, o_spec], out_shape=[sh, sh],
                  compiler_params=_params(("parallel", "arbitrary")))(dout, w_down, gate, up)


def _ffn_dw_down(name, act, dout, comm=None):
    nb, t, n = act.shape
    d = dout.shape[1]
    tn = _pick(d, (1024, 512, 256, 128))

    def body(a_r, d_r, o_r):
        acc = lax.dot_general(a_r[0], d_r[...].astype(MXU_DTYPE), _dims("tn", 2), preferred_element_type=f32)
        o_r[0] = (0.5 * acc).astype(bf16)

    return _pcall(body, comm=comm, name=name, grid=(nb, d // tn),
                  in_specs=[pl.BlockSpec((1, t, n), lambda j, i: (j, 0, 0)), pl.BlockSpec((t, tn), lambda j, i: (0, i))],
                  out_specs=pl.BlockSpec((1, n, tn), lambda j, i: (j, 0, i)), out_shape=jax.ShapeDtypeStruct((nb, n, d), bf16),
                  compiler_params=_params(("parallel", "parallel")))(act, dout)


def _ffn_dw_hidden(name, h, dgate, dup, comm=None):
    t, d = h.shape
    nb, _, n = dgate.shape
    tm = _pick(d, (1024, 512, 256, 128))

    def body(h_r, g_r, u_r, dg_o, du_o):
        hv = h_r[...]
        dg_o[0] = lax.dot_general(hv, g_r[0], _dims("tn", 2), preferred_element_type=f32).astype(bf16)
        du_o[0] = lax.dot_general(hv, u_r[0], _dims("tn", 2), preferred_element_type=f32).astype(bf16)

    g_spec = pl.BlockSpec((1, t, n), lambda j, i: (j, 0, 0))
    o_spec = pl.BlockSpec((1, tm, n), lambda j, i: (j, i, 0))
    sh = jax.ShapeDtypeStruct((nb, d, n), bf16)
    return _pcall(body, comm=comm, name=name, grid=(nb, d // tm),
                  in_specs=[pl.BlockSpec((t, tm), lambda j, i: (0, i)), g_spec, g_spec],
                  out_specs=[o_spec, o_spec], out_shape=[sh, sh],
                  compiler_params=_params(("parallel", "parallel")))(h, dgate, dup)


def _ffn_dh(name, dhid, w, res=None, comm=None):
    nb, t, n = dhid.shape
    d = w.shape[1]
    tm, tn = _pick(t, (512, 256, 128)), _pick(d, (512, 256, 128))

    def body(*refs):
        acc = refs[2][...] if res is not None else jnp.zeros((tm, tn), f32)
        for j in range(nb):
            acc = acc + lax.dot_general(refs[0][j], refs[1][j], _dims("nt", 2), preferred_element_type=f32)
        refs[-1][...] = acc

    in_specs = [pl.BlockSpec((nb, tm, n), lambda i, j: (0, i, 0)), pl.BlockSpec((nb, tn, n), lambda i, j: (0, j, 0))]
    args = [dhid, w]
    if res is not None:
        in_specs.append(pl.BlockSpec((tm, tn), lambda i, j: (i, j)))
        args.append(res)
    return _pcall(body, comm=comm, name=name, grid=(t // tm, d // tn), in_specs=in_specs,
                  out_specs=pl.BlockSpec((tm, tn), lambda i, j: (i, j)), out_shape=jax.ShapeDtypeStruct((t, d), f32),
                  compiler_params=_params(("parallel", "parallel")))(*args)


def _lora_bounds(c, lora):
    dl, al, gl = lora
    o1 = 3 * c
    o2, o3 = o1 + dl, o1 + dl + al
    return o1, o2, o3, o3 + gl, (_up128(dl), _up128(al), _up128(gl))


def _win_split(g8, c, lora, comm=None):
    nb, d, n = g8.shape
    o1, o2, o3, o4, (dlp, alp, glp) = _lora_bounds(c, lora)
    tm = _row_tile(d, nb * n, g8.dtype.itemsize)

    def body(x, rkv_o, lora_o, swa_o):
        w = jnp.concatenate([x[j] for j in range(nb)], axis=-1)
        pad = lambda p, m: p if p.shape[1] == m else jnp.concatenate([p, jnp.zeros((p.shape[0], m - p.shape[1]), p.dtype)], axis=-1)
        rkv_o[...] = w[:, :o1]
        lora_o[...] = jnp.concatenate([pad(w[:, o1:o2], dlp), pad(w[:, o2:o3], alp), pad(w[:, o3:o4], glp)], axis=-1)
        swa_o[...] = w[:, o4:]

    widths = (o1, dlp + alp + glp, nb * n - o4)
    return _pcall(body, comm=comm, name="w_in_split", grid=(d // tm,), in_specs=[pl.BlockSpec((nb, tm, n), lambda i: (0, i, 0))],
                  out_specs=[pl.BlockSpec((tm, wd), lambda i: (i, 0)) for wd in widths],
                  out_shape=[jax.ShapeDtypeStruct((d, wd), g8.dtype) for wd in widths],
                  compiler_params=_params(("parallel",)))(g8)


def _win_merge(dw_rkv, dw_lora, dw_swa, c, lora, nb):
    d = dw_rkv.shape[0]
    o1, o2, o3, o4, (dlp, alp, glp) = _lora_bounds(c, lora)
    dl, al, gl = lora
    total = o4 + dw_swa.shape[1]
    n = total // nb
    tm = _row_tile(d, total, dw_rkv.dtype.itemsize)

    def body(a, b, s, o):
        bv = b[...]
        w = jnp.concatenate([a[...], bv[:, :dl], bv[:, dlp:dlp + al], bv[:, dlp + alp:dlp + alp + gl], s[...]], axis=-1)
        for j in range(nb):
            o[j] = w[:, n * j:n * (j + 1)]

    ins = [dw_rkv, dw_lora, dw_swa]
    return _pcall(body, name="w_in_merge", grid=(d // tm,), in_specs=[pl.BlockSpec((tm, a.shape[1]), lambda i: (i, 0)) for a in ins],
                  out_specs=pl.BlockSpec((nb, tm, n), lambda i: (0, i, 0)), out_shape=jax.ShapeDtypeStruct((nb, d, n), dw_rkv.dtype),
                  compiler_params=_params(("parallel",)))(*ins)


def _norm_bwd(name, x, g_norm, dh, dres, comm=None):
    d = x.shape[1]

    def fn(xb, dhb, drb, g):
        _, vjp = jax.vjp(_rms, xb, g)
        dx, dg = vjp(dhb)
        return drb + dx, dg

    return _rows(name, fn, [x, dh, dres], [g_norm], [(d, f32)], [((1, d), f32)], comm=comm)


def _colsum(name, a):
    return _rows(name, lambda ab: (jnp.sum(ab.astype(f32), axis=0, keepdims=True),), [a], [], [], [((1, a.shape[1]), f32)])[0]


def kernel(x, mem, f1_norm, f1_gate, f1_up, f1_down, mix_norm, w_in, b_in_attn, rw_mu, rw_w0, rw_decay_up, rw_a0, rw_aaa_up, rw_gate_up, rw_k_k, rw_k_a, rw_r_k, rw_lnx_w, rw_lnx_b, attn_sinks, w_out, b_out, xa_norm, mem_norm, w_xq, w_xkv, w_xo, f2_norm, f2_gate, f2_up, f2_down, final_norm, loss_target, m_f1_norm, m_f1_gate, m_f1_up, m_f1_down, m_mix_norm, m_w_in, m_b_in_attn, m_rw_mu, m_rw_w0, m_rw_decay_up, m_rw_a0, m_rw_aaa_up, m_rw_gate_up, m_rw_k_k, m_rw_k_a, m_rw_r_k, m_rw_lnx_w, m_rw_lnx_b, m_attn_sinks, m_w_out, m_b_out, m_xa_norm, m_mem_norm, m_w_xq, m_w_xkv, m_w_xo, m_f2_norm, m_f2_gate, m_f2_up, m_f2_down, m_final_norm, v_f1_norm, v_f1_gate, v_f1_up, v_f1_down, v_mix_norm, v_w_in, v_b_in_attn, v_rw_mu, v_rw_w0, v_rw_decay_up, v_rw_a0, v_rw_aaa_up, v_rw_gate_up, v_rw_k_k, v_rw_k_a, v_rw_r_k, v_rw_lnx_w, v_rw_lnx_b, v_attn_sinks, v_w_out, v_b_out, v_xa_norm, v_mem_norm, v_w_xq, v_w_xkv, v_w_xo, v_f2_norm, v_f2_gate, v_f2_up, v_f2_down, v_final_norm):
    names = ["f1_norm", "f1_gate", "f1_up", "f1_down", "mix_norm", "w_in", "b_in_attn", "rw_mu", "rw_w0", "rw_decay_up",
             "rw_a0", "rw_aaa_up", "rw_gate_up", "rw_k_k", "rw_k_a", "rw_r_k", "rw_lnx_w", "rw_lnx_b", "attn_sinks", "w_out",
             "b_out", "xa_norm", "mem_norm", "w_xq", "w_xkv", "w_xo", "f2_norm", "f2_gate", "f2_up", "f2_down", "final_norm"]
    env = dict(locals())
    w_of = {k: env[k] for k in names}
    m_of = {k: env["m_" + k] for k in names}
    v_of = {k: env["v_" + k] for k in names}
    col_sharded = ["f1_gate", "f1_up", "w_in", "rw_decay_up", "rw_aaa_up", "rw_gate_up", "w_xkv", "f2_gate", "f2_up"]
    row_sharded = ["f1_down", "w_out", "w_xq", "w_xo", "f2_down"]
    sharded = col_sharded + row_sharded
    small = [k for k in names if k not in sharded]

    x0, mem0, tgt = x[0], mem[0], loss_target[0]
    t, d = x0.shape
    c = rw_w0.shape[-1]
    heads = c // HEAD_DIM
    dl, al, gl = rw_decay_up.shape[1], rw_aaa_up.shape[1], rw_gate_up.shape[1]
    dlp, alp, glp = _up128(dl), _up128(al), _up128(gl)
    swa_w = d - c
    hq, kvh = swa_w // HEAD_DIM, (b_in_attn.shape[-1] - swa_w) // (2 * HEAD_DIM)
    my_x, my_y, my_c = _position()
    c_idx = jnp.reshape(my_c, (1,)).astype(jnp.int32)
    chip_idx = jnp.reshape(2 * my_x + my_y, (1,)).astype(jnp.int32)

    shard2d = {k: w_of[k][0] for k in sharded}
    cast = {k: _rows("cast_" + k, lambda a: (a,), [shard2d[k]], [], [(shard2d[k].shape[1], bf16)], tm=_row_tile(*shard2d[k].shape))[0]
            for k in sharded}
    ffn1_keys, ffn2_keys = ["f1_gate", "f1_up", "f1_down"], ["f2_gate", "f2_up", "f2_down"]
    in_keys = ["w_in", "rw_decay_up", "rw_aaa_up", "rw_gate_up"]
    kept_in_blocks = ffn1_keys + ffn2_keys + ["w_in", "w_xkv"]

    def whole(k, g8):
        if k in kept_in_blocks:
            return g8
        if k in col_sharded:
            return g8.transpose(1, 0, 2).reshape(g8.shape[1], N_DEV * g8.shape[2])
        return g8.reshape(N_DEV * g8.shape[1], g8.shape[2])

    def gather_of(keys):
        return _gather_comm([cast[k] for k in keys])

    def wholes(keys, gathered):
        return {k: whole(k, g8) for k, g8 in zip(keys, gathered)}

    (h1,), gathered = _rows("f1_norm", lambda xb, g: (_rms(xb, g),), [x0], [f1_norm], [(d, bf16)], comm=gather_of(ffn1_keys[:2]))
    full = wholes(ffn1_keys[:2], gathered)
    (gate1, up1, act1), gathered = _ffn_hidden("f1_hidden", h1, full["f1_gate"], full["f1_up"], comm=gather_of(["f1_down"]))
    full.update(wholes(["f1_down"], gathered))
    x1, gathered = _ffn_out("f1_out", act1, full["f1_down"], x0, comm=gather_of(in_keys))
    full.update(wholes(in_keys, gathered))
    ffn1_saved = (h1, gate1, up1, act1)
    (w_rkv, w_lora, w_swa), gathered = _win_split(full["w_in"], c, (dl, al, gl), comm=gather_of(["w_out"]))
    full.update(wholes(["w_out"], gathered))
    o1, o2, o3, shift_cols, _ = _lora_bounds(c, (dl, al, gl))
    mu_rkv = rw_mu[:, :3 * c]
    mu_l = jnp.concatenate([_pad_to(rw_mu[:, o1:o2], dlp, 1), _pad_to(rw_mu[:, o2:o3], alp, 1),
                            _pad_to(rw_mu[:, o3:shift_cols], glp, 1)], axis=1)
    decay_up = _pad_to(full["rw_decay_up"], dlp, 0).astype(f32)
    aaa_up = _pad_to(full["rw_aaa_up"], alp, 0).astype(f32)
    gate_up = _pad_to(full["rw_gate_up"], glp, 0).astype(f32)
    head_pars = [p.reshape(heads, 1, HEAD_DIM) for p in (rw_k_k, rw_k_a, rw_r_k, rw_lnx_w, rw_lnx_b)]
    final_g = final_norm.reshape(1, d)

    (h2,) = _rows("mix_norm", lambda xb, g: (_rms(xb, g),), [x1], [mix_norm], [(d, bf16)])
    p_rkv, gathered = _mm("in_rkv", h2, w_rkv, "nn", f32, comm=gather_of(["w_xq"]))
    full.update(wholes(["w_xq"], gathered))
    p_l =_mm("in_lora", h2, w_lora, "nn", f32)
    za = _mm("in_swa", h2, w_swa, "nn", f32, bias=b_in_attn)
    pre_fn = functools.partial(_rwkv_pre_math, c, (dlp, alp, glp))
    pre_rows = [p_rkv, _shift_down(p_rkv), p_l, _shift_down(p_l)]
    pre_full = [mu_rkv, mu_l, rw_w0, rw_a0, decay_up, aaa_up, gate_up]
    (r_h, k_h, v_h, lw_h, a_h, g_t), gathered = _rows("rwkv_pre", pre_fn, pre_rows, pre_full,
                                                       [(c, f32, HEAD_DIM)] * 5 + [(c, f32)], tm=128, comm=gather_of(["w_xo"]))
    full.update(wholes(["w_xo"], gathered))
    seqs = [r_h, k_h, v_h, a_h, lw_h]
    (y_heads, checkpoints), gathered = _rwkv_fwd(*seqs, head_pars, comm=gather_of(["f2_gate"]))
    full.update(wholes(["f2_gate"], gathered))

    pos = jnp.arange(t, dtype=f32)
    inv_freq = ROPE_THETA ** (-jnp.arange(0, HEAD_DIM, 2, dtype=f32) / HEAD_DIM)
    ang = pos[:, None] * inv_freq[None, :]
    cos, sin = jnp.cos(ang), jnp.sin(ang)
    sinks3 = attn_sinks.reshape(kvh, hq // kvh, 1)
    swa_in = (*_swa_split(za, hq, kvh), cos, sin, sinks3)
    y_swa_heads, gathered = _swa_fwd(*swa_in, comm=gather_of(["f2_up"]))
    full.update(wholes(["f2_up"], gathered))
    (ycat,) = _rows("mix_cat", lambda yb, gb, sb: (jnp.concatenate([yb * gb, sb], axis=1),), [y_heads, g_t, y_swa_heads], [],
                    [(d, bf16)])
    x2, gathered = _mm("mix_out", ycat, full["w_out"], "nn", f32, res=x1, bias=b_out, comm=gather_of(["w_xkv"]))
    full.update(wholes(["w_xkv"], gathered))

    (h3,) = _rows("xa_norm", lambda xb, g: (_rms(xb, g),), [x2], [xa_norm], [(d, bf16)])
    (mem_n,) = _rows("mem_norm", lambda xb, g: (_rms(xb, g),), [mem0], [mem_norm], [(d, bf16)])
    q_x = _mm("xa_q", h3, full["w_xq"], "nn", bf16)
    kv_x = _mm("xa_kv", mem_n, full["w_xkv"], "nn", bf16)
    o_x = _xattn_fwd(q_x, kv_x)
    x3 = _mm("xa_out", o_x, full["w_xo"], "nn", f32, res=x2)
    (h4,) = _rows("f2_norm", lambda xb, g: (_rms(xb, g),), [x3], [f2_norm], [(d, bf16)])
    (gate2, up2, act2), gathered = _ffn_hidden("f2_hidden", h4, full["f2_gate"], full["f2_up"], comm=gather_of(["f2_down"]))
    full.update(wholes(["f2_down"], gathered))
    x4 = _ffn_out("f2_out", act2, full["f2_down"], x3)
    ffn2_saved = (h4, gate2, up2, act2)

    def loss_fn(xb, tb, g):
        def per_row(xv, gv):
            return 0.5 * jnp.mean(jnp.square(_rms(xv, gv) - tb), axis=-1, keepdims=True)

        lrow, vjp = jax.vjp(per_row, xb, g)
        dxb, dgb = vjp(jnp.ones_like(lrow))
        return dxb, dgb, jnp.sum(lrow, axis=0, keepdims=True)

    dx4, d_final, loss_part = _rows("loss", loss_fn, [x4, tgt], [final_g], [(d, f32)], [((1, d), f32), ((1, 1), f32)])
    loss = lax.psum(loss_part[0, 0], ("x", "y", "c"))

    grads, small_g, out = {}, {"final_norm": d_final}, {}

    def pair_sums_of(tag, keys, carrier=None):
        blocks = []
        for k in keys:
            g2 = grads[k]
            rr, cc = shard2d[k].shape
            if k in kept_in_blocks:
                blocks.append(g2)
            else:
                blocks.append(g2.reshape(g2.shape[0], N_DEV, cc).transpose(1, 0, 2) if k in col_sharded else g2.reshape(N_DEV, rr, cc))
        if carrier is None:
            from_sibling = _comm_only("grads_to_sibling_" + tag, _sibling_comm(blocks))
        else:
            carried, from_sibling = carrier(_sibling_comm(blocks))
        pairs = [_pair_add("pair_add_" + k, b, o, c_idx) for k, b, o in zip(keys, blocks, from_sibling)]
        return pairs if carrier is None else (pairs, carried)

    def update(keys, pair_sums, from_chips):
        for k, part, others in zip(keys, pair_sums, from_chips):
            res = _adam_sharded("adam_" + k, shard2d[k], m_of[k][0], v_of[k][0], part, others, chip_idx)
            out[k] = [a.reshape(w_of[k].shape) for a in res]

    def ffn_backward(tag, keys, xin, g_norm, saved, dout, first_comm, start_exchange):
        h, gate, up, act = saved
        k_gate, k_up, k_down = keys
        if first_comm is None:
            grads[k_down], carried = _ffn_dw_down(tag + "_dw_down", act, dout), None
        else:
            grads[k_down], carried = _ffn_dw_down(tag + "_dw_down", act, dout, comm=first_comm)
        down_pairs, (dgate, dup) = pair_sums_of(
            k_down, [k_down], lambda cm: _ffn_dhidden(tag + "_dhidden", dout, full[k_down], gate, up, comm=cm))
        (grads[k_gate], grads[k_up]), from_chips = _ffn_dw_hidden(tag + "_dw_hidden", h, dgate, dup, comm=_chips_comm(down_pairs))
        update([k_down], down_pairs, from_chips)
        hidden_pairs, dh = pair_sums_of(tag + "_hidden", [k_gate, k_up],
                                        lambda cm: _ffn_dh(tag + "_dh1", dgate, full[k_gate], comm=cm))
        pending = None
        if start_exchange:
            pending, dh = _chips_start("grads_to_chips_start_" + tag, hidden_pairs, dh)
        dh = _ffn_dh(tag + "_dh2", dup, full[k_up], res=dh)
        dx, dg_norm = _norm_bwd(tag + "_dnorm", xin, g_norm, dh, dout)
        return dx, dg_norm, hidden_pairs, carried, pending

    xa_keys = ["w_xq", "w_xkv", "w_xo"]
    dx3, small_g["f2_norm"], ffn2_pairs, _, _ = ffn_backward("f2b", ffn2_keys, x3, f2_norm, ffn2_saved, dx4, None, False)

    do_x = _mm("xa_do", dx3, full["w_xo"], "nt", bf16)
    grads["w_xo"] = _mm("xa_dwo", o_x, dx3, "tn", bf16)
    dq_x, dk_x, dv_x = _xattn_bwd(q_x, kv_x, do_x)
    grads["w_xq"] = _mm("xa_dwq", h3, dq_x, "tn", bf16)
    dh3 = _mm("xa_dh", dq_x, full["w_xq"], "nt", f32)
    dkv_x = jnp.concatenate([dk_x, dv_x], axis=1)
    grads["w_xkv"] = _mm("xa_dwkv", mem_n, dkv_x, "tn", bf16, out_blocks=N_DEV)
    dkv_blocks = dkv_x.astype(bf16).reshape(dkv_x.shape[0], N_DEV, -1).transpose(1, 0, 2)
    dmem_n = _ffn_dh("xa_dmem", dkv_blocks, full["w_xkv"])
    _, small_g["mem_norm"] = _norm_bwd("mem_dnorm", mem0, mem_norm, dmem_n, jnp.zeros_like(mem0))
    xa_pairs, (dx2, small_g["xa_norm"]) = pair_sums_of(
        "xa", xa_keys, lambda cm: _norm_bwd("xa_dnorm", x2, xa_norm, dh3, dx3, comm=cm))

    dycat = _mm("mix_dy", dx2, full["w_out"], "nt", f32)
    grads["w_out"] = _mm("mix_dwout", ycat, dx2, "tn", bf16)
    small_g["b_out"] = _colsum("mix_dbout", dx2)
    out_pairs, (dy_heads, dg_t, do_sw) = pair_sums_of("out", ["w_out"], lambda cm: _rows(
        "mix_dgate", lambda db, yb, gb: (db[:, :c] * gb, db[:, :c] * yb, db[:, c:]), [dycat, y_heads, g_t], [],
        [(c, f32, HEAD_DIM), (c, f32), (swa_w, f32, HEAD_DIM)], comm=cm))
    rw_grads, from_chips = _rwkv_bwd(*seqs, head_pars, checkpoints, dy_heads, comm=_chips_comm(ffn2_pairs))
    update(ffn2_keys[:2], ffn2_pairs, from_chips)
    dr_h, dk_h, dv_h, da_h, dlw_h = rw_grads[:5]
    for nm, gh in zip(("rw_k_k", "rw_k_a", "rw_r_k", "rw_lnx_w", "rw_lnx_b"), rw_grads[5:]):
        small_g[nm] = gh.reshape(w_of[nm].shape)

    def pre_bwd(*args):
        _, vjp = jax.vjp(pre_fn, *args[:4], *args[10:])
        return vjp(tuple(args[4:10]))

    pre_cts = [dr_h, dk_h, dv_h, dlw_h, da_h, dg_t]
    pre_out = _rows("rwkv_pre_bwd", pre_bwd, pre_rows + pre_cts, pre_full,
                    [(3 * c, f32), (3 * c, f32), (dlp + alp + glp, f32), (dlp + alp + glp, f32)],
                    [(p.shape, f32) for p in pre_full], tm=128)
    dp_rkv = pre_out[0] + _shift_up(pre_out[1])
    dp_l = pre_out[2] + _shift_up(pre_out[3])
    dmu_rkv, dmu_l, small_g["rw_w0"], small_g["rw_a0"], d_decay_up, d_aaa_up, d_gate_up = pre_out[4:]
    small_g["rw_mu"] = jnp.concatenate([dmu_rkv, dmu_l[:, :dl], dmu_l[:, dlp:dlp + al], dmu_l[:, dlp + alp:dlp + alp + gl]], axis=1)
    grads["rw_decay_up"] = d_decay_up[:dl].astype(bf16)
    grads["rw_aaa_up"] = d_aaa_up[:al].astype(bf16)
    grads["rw_gate_up"] = d_gate_up[:gl].astype(bf16)

    sw, from_chips = _swa_bwd(*swa_in, do_sw, comm=_chips_comm(xa_pairs))
    update(xa_keys, xa_pairs, from_chips)
    small_g["attn_sinks"] = sw[8].reshape(attn_sinks.shape)
    dza, small_g["b_in_attn"] = _swa_merge(sw[0], sw[1], sw[3], sw[2], sw[5], sw[4], sw[7], sw[6])

    dw_rkv = _mm("in_dwrkv", h2, dp_rkv, "tn", bf16)
    dw_l = _mm("in_dwlora", h2, dp_l, "tn", bf16)
    dw_swa = _mm("in_dwswa", h2, dza, "tn", bf16)
    grads["w_in"] = _win_merge(dw_rkv, dw_l, dw_swa, c, (dl, al, gl), N_DEV)
    dh2, from_chips = _mm("in_dh1", dp_rkv, w_rkv, "nt", f32, comm=_chips_comm(out_pairs))
    update(["w_out"], out_pairs, from_chips)
    dh2 = _mm("in_dh2", dp_l, w_lora, "nt", f32, res=dh2)
    in_pairs, dh2 = pair_sums_of("in", in_keys, lambda cm: _mm("in_dh3", dza, w_swa, "nt", f32, res=dh2, comm=cm))
    pending_in, dh2 = _chips_start("grads_to_chips_start_in", in_pairs, dh2)
    dx1, small_g["mix_norm"] = _norm_bwd("mix_dnorm", x1, mix_norm, dh2, dx2)

    dx0, small_g["f1_norm"], _, _, pending = ffn_backward("f1b", ffn1_keys, x0, f1_norm, ffn1_saved, dx1, None, True)
    others_done = _fence("updates_done", [res[1] for res in out.values()] + [dx0] + list(small_g.values()))
    in_pairs, from_chips = _chips_wait("grads_to_chips_wait_in", *pending_in, others_done)
    update(in_keys, in_pairs, from_chips)
    pairs, from_chips = _chips_wait("grads_to_chips_wait_f1b", *pending, others_done)
    update(ffn1_keys[:2], pairs, from_chips)

    sizes = [int(w_of[k].size) for k in small]
    total = sum(sizes)
    cols = -(-total // (8 * LANE)) * LANE

    def pack(parts_of):
        flat = jnp.concatenate([parts_of[k].reshape(-1).astype(f32) for k in small])
        return _pad_to(flat, 8 * cols, 0).reshape(8, cols)

    (all_parts,) = _comm_only("gather_small_grads", _gather_comm([pack(small_g)], after=from_chips))
    res = _adam_small("adam_small", pack(w_of), pack(m_of), pack(v_of), all_parts)
    offs = 0
    flat_res = [a.reshape(-1) for a in res]
    for k, sz in zip(small, sizes):
        out[k] = [a[offs:offs + sz].reshape(w_of[k].shape) for a in flat_res]
        offs += sz

    outs = [loss, dx0.reshape(x.shape)]
    for j in range(4):
        outs += [out[k][j] for k in names]
    return tuple(outs)
```

```python
import functools
import math

import jax
import jax.numpy as jnp
from jax import lax
from jax.experimental import pallas as pl
from jax.experimental.pallas import tpu as pltpu

f32 = jnp.float32
bf16 = jnp.bfloat16
MXU_DTYPE = jnp.bfloat16

HEAD_DIM = 64
SWA_BLOCK = 128
ROPE_THETA = 10000.0
XATTN_HEADS = 4
RMS_EPS = 1e-6
GN_EPS = 64e-5
NEG_INF = -1e30
RWKV_CHUNK = 64

ADAM_LR = 0.001
ADAM_B1 = 0.9
ADAM_B2 = 0.999
ADAM_EPS = 1e-08
ADAM_WD = 0.01
ADAM_STEP = 10

N_DEV = 8
LANE = 128
VMEM_LIMIT_BYTES = 56 * 1024 * 1024
MM_VMEM_BUDGET = 40 * 1024 * 1024
MESH = pl.DeviceIdType.MESH


def _params(sem):
    return pltpu.CompilerParams(dimension_semantics=sem, vmem_limit_bytes=VMEM_LIMIT_BYTES)


class _Comm:
    def __init__(self, ins, outs, n_remote, n_local, start, finish):
        self.ins, self.outs, self.n_remote, self.n_local = list(ins), list(outs), n_remote, max(n_local, 1)
        self.start, self.finish = start, finish


def _pcall(body, comm=None, **kw):
    kw.setdefault("compiler_params", pltpu.CompilerParams(vmem_limit_bytes=VMEM_LIMIT_BYTES))
    if comm is None:
        return pl.pallas_call(body, **kw)
    single = not isinstance(kw["out_shape"], (list, tuple))
    out_shape = [kw["out_shape"]] if single else list(kw["out_shape"])
    out_specs = [kw["out_specs"]] if single else list(kw["out_specs"])
    in_specs, scratch, grid = list(kw["in_specs"]), list(kw.get("scratch_shapes", ())), tuple(kw.get("grid", ()))
    n_in, n_out, n_ci, n_co, n_scr = len(in_specs), len(out_shape), len(comm.ins), len(comm.outs), len(scratch)

    def wrapped(*refs):
        ins, c_ins = refs[:n_in], refs[n_in:n_in + n_ci]
        outs = refs[n_in + n_ci:n_in + n_ci + n_out]
        c_outs = refs[n_in + n_ci + n_out:n_in + n_ci + n_out + n_co]
        rest = refs[n_in + n_ci + n_out + n_co:]
        scr, sems = rest[:n_scr], rest[n_scr:]
        if grid:
            ids = [pl.program_id(k) for k in range(len(grid))]
            first = functools.reduce(jnp.logical_and, [i == 0 for i in ids])
            last = functools.reduce(jnp.logical_and, [i == g - 1 for i, g in zip(ids, grid)])
            pl.when(first)(lambda: comm.start(c_ins, c_outs, *sems))
            body(*ins, *outs, *scr)
            pl.when(last)(lambda: comm.finish(c_ins, c_outs, *sems))
        else:
            comm.start(c_ins, c_outs, *sems)
            body(*ins, *outs, *scr)
            comm.finish(c_ins, c_outs, *sems)

    any_spec = pl.BlockSpec(memory_space=pl.ANY)
    kw.update(in_specs=in_specs + [any_spec] * n_ci, out_specs=out_specs + [any_spec] * n_co,
              out_shape=out_shape + comm.outs,
              scratch_shapes=scratch + [pltpu.SemaphoreType.DMA((comm.n_remote,)), pltpu.SemaphoreType.DMA((comm.n_remote,)),
                                        pltpu.SemaphoreType.DMA((comm.n_local,))])
    if grid:
        kw["compiler_params"] = _params(("arbitrary",) * len(grid))
    call = pl.pallas_call(wrapped, **kw)

    def run(*args):
        res = call(*args, *comm.ins)
        return (res[0] if single else list(res[:n_out])), list(res[n_out:])

    return run


def _fence(name, arrays):
    def body(*refs):
        refs[-1][...] = jnp.zeros(refs[-1].shape, f32)

    return _pcall(body, name=name, in_specs=[pl.BlockSpec(memory_space=pl.ANY)] * len(arrays),
                  out_specs=pl.BlockSpec(memory_space=pltpu.VMEM), out_shape=jax.ShapeDtypeStruct((8, LANE), f32))(*arrays)


def _comm_only(name, comm):
    return _pcall(lambda: None, comm=comm, name=name, in_specs=[], out_specs=[], out_shape=[])()[1]


def _dims(kind, ndim):
    o = ndim - 2
    batch = ((0,), (0,)) if o else ((), ())
    c = {"nn": ((1 + o,), (o,)), "nt": ((1 + o,), (1 + o,)), "tn": ((o,), (o,))}[kind]
    return (c, batch)


def _dot_raw(x, y, kind):
    return lax.dot_general(x.astype(MXU_DTYPE), y.astype(MXU_DTYPE), _dims(kind, x.ndim), preferred_element_type=f32)


@functools.partial(jax.custom_vjp, nondiff_argnums=(2,))
def _dot(x, y, kind):
    return _dot_raw(x, y, kind)


def _dot_fwd(x, y, kind):
    return _dot_raw(x, y, kind), (x, y)


def _dot_bwd(kind, res, g):
    x, y = res
    if kind == "nn":
        dx, dy = _dot(g, y, "nt"), _dot(x, g, "tn")
    elif kind == "nt":
        dx, dy = _dot(g, y, "nn"), _dot(g, x, "tn")
    else:
        dx, dy = _dot(y, g, "nt"), _dot(x, g, "nn")
    return dx.astype(x.dtype), dy.astype(y.dtype)


_dot.defvjp(_dot_fwd, _dot_bwd)


def _split3(x):
    a = x.astype(bf16)
    r = x - a.astype(f32)
    b = r.astype(bf16)
    c = (r - b.astype(f32)).astype(bf16)
    return a, b, c


def _rms(x, g):
    x = x.astype(f32)
    return x * lax.rsqrt(jnp.mean(x * x, axis=-1, keepdims=True) + RMS_EPS) * g


def _sigmoid(x):
    return 1.0 / (1.0 + jnp.exp(-x))


def _softplus(x):
    return jnp.maximum(x, 0.0) + jnp.log(1.0 + jnp.exp(-jnp.abs(x)))


def _rows(name, fn, row_ins, full_ins, row_outs, acc_outs=(), tm=None, comm=None):
    width = lambda a: a.shape[1] if a.ndim == 2 else a.shape[0] * a.shape[2]
    rows = row_ins[0].shape[0] if row_ins[0].ndim == 2 else row_ins[0].shape[1]
    if tm is None:
        tm = _row_tile(rows, max([width(a) for a in row_ins] + [o[0] for o in row_outs]))
    tm = min(tm, rows)
    assert rows % tm == 0, (name, rows, tm)
    n_in = len(row_ins) + len(full_ins)
    n_o, n_a = len(row_outs), len(acc_outs)

    def load(k, ref):
        if k < len(row_ins) and row_ins[k].ndim == 3:
            return jnp.concatenate([ref[h] for h in range(ref.shape[0])], axis=-1)
        return ref[...]

    def body(*refs):
        vals = [load(k, r) for k, r in enumerate(refs[:n_in])]
        outs = fn(*vals)
        o_refs = refs[n_in:n_in + n_o]
        a_refs = refs[n_in + n_o:]
        for k in range(n_o):
            if len(row_outs[k]) == 3:
                n = row_outs[k][2]
                for h in range(row_outs[k][0] // n):
                    o_refs[k][h] = outs[k][:, h * n:(h + 1) * n].astype(o_refs[k].dtype)
            else:
                o_refs[k][...] = outs[k].astype(o_refs[k].dtype)
        if n_a:
            first = pl.program_id(0) == 0

            @pl.when(first)
            def _():
                for k in range(n_a):
                    a_refs[k][...] = outs[n_o + k].astype(a_refs[k].dtype)

            @pl.when(jnp.logical_not(first))
            def _():
                for k in range(n_a):
                    a_refs[k][...] += outs[n_o + k].astype(a_refs[k].dtype)

    by_rows = lambda cols: pl.BlockSpec((tm, cols), lambda i: (i, 0))
    by_heads = lambda h, n: pl.BlockSpec((h, tm, n), lambda i: (0, i, 0))
    in_specs = [by_rows(a.shape[1]) if a.ndim == 2 else by_heads(a.shape[0], a.shape[2]) for a in row_ins]
    in_specs += [pl.BlockSpec(a.shape, lambda i, nd=a.ndim: (0,) * nd) for a in full_ins]
    out_specs = [by_rows(o[0]) if len(o) == 2 else by_heads(o[0] // o[2], o[2]) for o in row_outs]
    out_specs += [pl.BlockSpec(s, lambda i, nd=len(s): (0,) * nd) for s, _ in acc_outs]
    out_shape = [jax.ShapeDtypeStruct((rows, o[0]) if len(o) == 2 else (o[0] // o[2], rows, o[2]), o[1]) for o in row_outs]
    out_shape += [jax.ShapeDtypeStruct(s, d) for s, d in acc_outs]
    return _pcall(body, comm=comm, name=name, grid=(rows // tm,), in_specs=in_specs, out_specs=out_specs, out_shape=out_shape,
                  compiler_params=_params(("arbitrary",)))(*row_ins, *full_ins)


def _pick(n, cands):
    for c in cands:
        if n % c == 0:
            return c
    return n


def _mm(name, a, b, mode, out_dtype, scale=1.0, res=None, bias=None, comm=None, out_blocks=None):
    b_blocks = b.ndim == 3
    if b_blocks:
        assert mode == "nn"
        (m, k), (nb, k2, tn) = a.shape, b.shape
        n = nb * tn
    elif mode == "nn":
        (m, k), (k2, n) = a.shape, b.shape
    elif mode == "nt":
        (m, k), (n, k2) = a.shape, b.shape
    else:
        (k, m), (k2, n) = a.shape, b.shape
    assert k == k2, (name, a.shape, b.shape, mode)
    if not b_blocks:
        tn = n // out_blocks if out_blocks else _pick(n, (512, 256, 128))
    tm = _pick(m, (1024, 512, 256, 128))

    def need(tm_):
        by = tm_ * k * a.dtype.itemsize + tn * k * b.dtype.itemsize + tm_ * tn * (jnp.dtype(out_dtype).itemsize + 4)
        if res is not None:
            by += tm_ * tn * res.dtype.itemsize
        return 2 * by

    while need(tm) > MM_VMEM_BUDGET and tm % 256 == 0:
        tm //= 2
    dims = _dims(mode, 2)

    def body(*refs):
        bv = refs[1][0] if b_blocks else refs[1][...]
        acc = lax.dot_general(refs[0][...].astype(MXU_DTYPE), bv.astype(MXU_DTYPE), dims, preferred_element_type=f32)
        if scale != 1.0:
            acc = acc * scale
        pos = 2
        if bias is not None:
            acc = acc + refs[pos][...]
            pos += 1
        if res is not None:
            acc = acc + refs[pos][...].astype(f32)
            pos += 1
        if out_blocks:
            refs[pos][0] = acc.astype(out_dtype)
        else:
            refs[pos][...] = acc.astype(out_dtype)

    a_spec = pl.BlockSpec((k, tm), lambda i, j: (0, i)) if mode == "tn" else pl.BlockSpec((tm, k), lambda i, j: (i, 0))
    if b_blocks:
        b_spec = pl.BlockSpec((1, k, tn), lambda i, j: (j, 0, 0))
    else:
        b_spec = pl.BlockSpec((tn, k), lambda i, j: (j, 0)) if mode == "nt" else pl.BlockSpec((k, tn), lambda i, j: (0, j))
    in_specs, args = [a_spec, b_spec], [a, b]
    if bias is not None:
        in_specs.append(pl.BlockSpec((1, tn), lambda i, j: (0, j)))
        args.append(bias)
    if res is not None:
        in_specs.append(pl.BlockSpec((tm, tn), lambda i, j: (i, j)))
        args.append(res)
    if out_blocks:
        out_spec, out_shape = pl.BlockSpec((1, tm, tn), lambda i, j: (j, i, 0)), jax.ShapeDtypeStruct((out_blocks, m, tn), out_dtype)
    else:
        out_spec, out_shape = pl.BlockSpec((tm, tn), lambda i, j: (i, j)), jax.ShapeDtypeStruct((m, n), out_dtype)
    return _pcall(body, comm=comm, name=name, grid=(m // tm, n // tn), in_specs=in_specs, out_specs=out_spec, out_shape=out_shape,
                  compiler_params=_params(("parallel", "parallel")))(*args)


def _position():
    return lax.axis_index("x"), lax.axis_index("y"), lax.axis_index("c")


def _gather_comm(shards, after=()):
    n = len(shards)

    def plan(x_refs, o_refs, send_sems, recv_sems, local_sems):
        x, y, c = _position()
        me, sibling = (x, y, c), (x, y, 1 - c)
        chips = [(1 - x, y), (x, 1 - y), (1 - x, 1 - y)]

        def slot(px, py, pc):
            return 4 * px + 2 * py + pc

        def copy(t, k, block, to, src=None):
            dst = o_refs[t].at[slot(*block)]
            return pltpu.make_async_remote_copy(src_ref=dst if src is None else src, dst_ref=dst,
                                                send_sem=send_sems.at[7 * t + k], recv_sem=recv_sems.at[7 * t + k],
                                                device_id=to, device_id_type=MESH)

        mine = [pltpu.make_async_copy(x_refs[t], o_refs[t].at[slot(*me)], local_sems.at[t]) for t in range(n)]
        first = []
        for t in range(n):
            first.append(copy(t, 0, me, sibling, src=x_refs[t]))
            first += [copy(t, 1 + j, me, (*chip, c), src=x_refs[t]) for j, chip in enumerate(chips)]
        return me, sibling, chips, c, copy, mine, first

    def start(*refs):
        _, _, _, _, _, mine, first = plan(*refs)
        for cp in mine + first:
            cp.start()

    def finish(*refs):
        me, sibling, chips, c, copy, mine, first = plan(*refs)
        passed = []
        for t in range(n):
            for j, chip in enumerate(chips):
                copy(t, 1 + j, (*chip, c), me).wait_recv()
                cp = copy(t, 4 + j, (*chip, c), sibling)
                cp.start()
                passed.append(cp)
        for t in range(n):
            copy(t, 0, sibling, me).wait_recv()
            for j, chip in enumerate(chips):
                copy(t, 4 + j, (*chip, 1 - c), me).wait_recv()
        for cp in first + passed:
            cp.wait_send()
        for cp in mine:
            cp.wait()

    outs = [jax.ShapeDtypeStruct((N_DEV,) + s.shape, s.dtype) for s in shards]
    return _Comm(list(shards) + list(after), outs, 7 * n, n, start, finish)


def _sibling_comm(blocks):
    n = len(blocks)

    def copies(g_refs, o_refs, send_sems, recv_sems, _):
        x, y, c = _position()
        return [pltpu.make_async_remote_copy(src_ref=g_refs[t].at[2 * q + 1 - c], dst_ref=o_refs[t].at[q],
                                             send_sem=send_sems.at[4 * t + q], recv_sem=recv_sems.at[4 * t + q],
                                             device_id=(x, y, 1 - c), device_id_type=MESH)
                for t in range(n) for q in range(4)]

    def start(*refs):
        for cp in copies(*refs):
            cp.start()

    def finish(*refs):
        for cp in copies(*refs):
            cp.wait()

    outs = [jax.ShapeDtypeStruct((4,) + g.shape[1:], g.dtype) for g in blocks]
    return _Comm(blocks, outs, 4 * n, 0, start, finish)


def _chips_comm(parts):
    n = len(parts)

    def copies(p_refs, o_refs, send_sems, recv_sems, _):
        x, y, c = _position()
        chips = [(1 - x, y), (x, 1 - y), (1 - x, 1 - y)]
        return [pltpu.make_async_remote_copy(src_ref=p_refs[t].at[2 * px + py], dst_ref=o_refs[t].at[j],
                                             send_sem=send_sems.at[3 * t + j], recv_sem=recv_sems.at[3 * t + j],
                                             device_id=(px, py, c), device_id_type=MESH)
                for t in range(n) for j, (px, py) in enumerate(chips)]

    def start(*refs):
        for cp in copies(*refs):
            cp.start()

    def finish(*refs):
        for cp in copies(*refs):
            cp.wait()

    outs = [jax.ShapeDtypeStruct((3,) + p.shape[1:], p.dtype) for p in parts]
    return _Comm(parts, outs, 3 * n, 0, start, finish)


HBM_SPEC = pl.BlockSpec(memory_space=pltpu.HBM)
SEM_SPEC = pl.BlockSpec(memory_space=pltpu.SEMAPHORE)
DATAFLOW = pltpu.SideEffectType.DATAFLOW_SIDE_EFFECTING


def _chip_exchange_copies(p_refs, o_refs, send_sems, recv_sems):
    x, y, c = _position()
    chips = [(1 - x, y), (x, 1 - y), (1 - x, 1 - y)]
    return [pltpu.make_async_remote_copy(src_ref=p_refs[t].at[2 * px + py], dst_ref=o_refs[t].at[j],
                                         send_sem=send_sems.at[3 * t + j], recv_sem=recv_sems.at[3 * t + j],
                                         device_id=(px, py, c), device_id_type=MESH)
            for t in range(len(p_refs)) for j, (px, py) in enumerate(chips)]


def _chips_start(name, parts, thru):
    n = len(parts)

    def body(*refs):
        for cp in _chip_exchange_copies(refs[:n], refs[n:2 * n], refs[2 * n + 1], refs[2 * n + 2]):
            cp.start()

    lands = [lax.empty((3,) + p.shape[1:], p.dtype) for p in parts]
    args = [pltpu.with_memory_space_constraint(a, pltpu.HBM) for a in list(parts) + lands + [thru]]
    res = pl.pallas_call(body, name=name, in_specs=[HBM_SPEC] * (2 * n + 1),
                         out_specs=[SEM_SPEC, SEM_SPEC] + [HBM_SPEC] * (2 * n + 1),
                         out_shape=[pltpu.SemaphoreType.DMA((3 * n,)), pltpu.SemaphoreType.DMA((3 * n,))]
                         + [pltpu.HBM(a.shape, a.dtype) for a in args],
                         input_output_aliases={i: 2 + i for i in range(2 * n + 1)},
                         compiler_params=pltpu.CompilerParams(has_side_effects=DATAFLOW))(*args)
    return (res[0], res[1], list(res[2:2 + n]), list(res[2 + n:2 + 2 * n])), res[2 + 2 * n]


def _chips_wait(name, send_sems, recv_sems, parts, lands, after):
    n = len(parts)

    def body(*refs):
        for cp in _chip_exchange_copies(refs[:n], refs[n:2 * n], refs[2 * n], refs[2 * n + 1]):
            cp.wait_send()
            cp.wait_recv()

    res = pl.pallas_call(body, name=name, out_shape=[pltpu.HBM(a.shape, a.dtype) for a in parts + lands],
                         in_specs=[HBM_SPEC] * (2 * n) + [SEM_SPEC, SEM_SPEC, pl.BlockSpec(memory_space=pl.ANY)],
                         out_specs=[HBM_SPEC] * (2 * n), input_output_aliases={i: i for i in range(2 * n)},
                         compiler_params=pltpu.CompilerParams(has_side_effects=DATAFLOW))(*parts, *lands, send_sems, recv_sems, after)
    return list(res[:n]), list(res[n:])


ROW_TILE_BYTES = 2 << 20


def _row_tile(r, cols, itemsize=4):
    fits = [t for t in range(8, r + 1, 8) if r % t == 0 and t * cols * itemsize <= ROW_TILE_BYTES]
    return max(fits) if fits else r


def _pair_add(name, g, got, c_idx):
    _, r, cc = g.shape
    tr = _row_tile(r, cc, g.dtype.itemsize)

    def body(c_ref, g_ref, o_ref, out_ref):
        out_ref[...] = (g_ref[...].astype(f32) + o_ref[...].astype(f32)).astype(out_ref.dtype)

    g5 = g.reshape(4, 2, r, cc)
    spec = pltpu.PrefetchScalarGridSpec(
        num_scalar_prefetch=1, grid=(4, r // tr),
        in_specs=[pl.BlockSpec((1, 1, tr, cc), lambda q, i, c_ref: (q, c_ref[0], i, 0)),
                  pl.BlockSpec((1, 1, tr, cc), lambda q, i, c_ref: (q, 0, i, 0))],
        out_specs=pl.BlockSpec((1, 1, tr, cc), lambda q, i, c_ref: (q, 0, i, 0)))
    out = _pcall(body, name=name, grid_spec=spec, out_shape=jax.ShapeDtypeStruct((4, 1, r, cc), g.dtype),
                 compiler_params=_params(("arbitrary", "arbitrary")))(c_idx, g5, got.reshape(4, 1, r, cc))
    return out.reshape(4, r, cc)


def _adam_math(w, g, m, v):
    m2 = ADAM_B1 * m + (1.0 - ADAM_B1) * g
    v2 = ADAM_B2 * v + (1.0 - ADAM_B2) * (g * g)
    m_hat = m2 / (1.0 - ADAM_B1 ** ADAM_STEP)
    v_hat = v2 / (1.0 - ADAM_B2 ** ADAM_STEP)
    delta = -ADAM_LR * (m_hat / (jnp.sqrt(v_hat) + ADAM_EPS) + ADAM_WD * w)
    return delta, m2, v2


def _adam_sharded(name, w, m, v, part, got, chip_idx):
    r, cc = w.shape
    tr = _row_tile(r, cc)

    def body(q_ref, w_ref, m_ref, v_ref, p_ref, o_ref, g_out, d_out, m_out, v_out):
        g = p_ref[0].astype(f32)
        for j in range(3):
            g = g + o_ref[j].astype(f32)
        d, m2, v2 = _adam_math(w_ref[...], g, m_ref[...], v_ref[...])
        g_out[...] = g
        d_out[...] = d
        m_out[...] = m2
        v_out[...] = v2

    row = pl.BlockSpec((tr, cc), lambda i, q_ref: (i, 0))
    spec = pltpu.PrefetchScalarGridSpec(
        num_scalar_prefetch=1, grid=(r // tr,),
        in_specs=[row, row, row, pl.BlockSpec((1, tr, cc), lambda i, q_ref: (q_ref[0], i, 0)),
                  pl.BlockSpec((3, tr, cc), lambda i, q_ref: (0, i, 0))],
        out_specs=[row, row, row, row])
    sh = jax.ShapeDtypeStruct((r, cc), f32)
    return _pcall(body, name=name, grid_spec=spec, out_shape=[sh, sh, sh, sh],
                  compiler_params=_params(("arbitrary",)))(chip_idx, w, m, v, part, got)


def _adam_small(name, w, m, v, parts):
    def body(w_ref, m_ref, v_ref, p_ref, g_out, d_out, m_out, v_out):
        g = p_ref[0]
        for b in range(1, N_DEV):
            g = g + p_ref[b]
        d, m2, v2 = _adam_math(w_ref[...], g, m_ref[...], v_ref[...])
        g_out[...] = g
        d_out[...] = d
        m_out[...] = m2
        v_out[...] = v2

    sh = jax.ShapeDtypeStruct(w.shape, f32)
    return _pcall(body, name=name, out_shape=[sh, sh, sh, sh])(w, m, v, parts)


def _swa_math(n, qa, qb, kap, kac, kbp, kbc, vp, vc, cq, sq, cp, sp, sink):
    g, blk, half = qa.shape
    c3, s3 = cq[None], sq[None]
    q1 = (qa * c3 - qb * s3).reshape(g * blk, half)
    q2 = (qb * c3 + qa * s3).reshape(g * blk, half)
    ck, sk = jnp.concatenate([cp, cq], axis=0), jnp.concatenate([sp, sq], axis=0)
    k1, k2 = jnp.concatenate([kap[0], kac[0]], axis=0), jnp.concatenate([kbp[0], kbc[0]], axis=0)
    k1r, k2r = k1 * ck - k2 * sk, k2 * ck + k1 * sk
    vv = jnp.concatenate([vp[0], vc[0]], axis=0)
    s = (_dot(q1, k1r, "nt") + _dot(q2, k2r, "nt")) * (HEAD_DIM ** -0.5)
    s = s.reshape(g, blk, 2 * blk)
    qi = lax.broadcasted_iota(jnp.int32, (blk, 2 * blk), 0)
    kj = lax.broadcasted_iota(jnp.int32, (blk, 2 * blk), 1)
    valid = (kj > qi) & (kj <= qi + blk) & ((kj >= blk) | (n > 0))
    s = jnp.where(valid[None], s, NEG_INF)
    sink3 = sink.reshape(g, 1, 1)
    mx = jnp.maximum(jnp.max(s, axis=-1, keepdims=True), sink3)
    e = jnp.exp(s - mx)
    z = jnp.sum(e, axis=-1, keepdims=True) + jnp.exp(sink3 - mx)
    p = (e / z).reshape(g * blk, 2 * blk)
    return _dot(p, vv, "nn").reshape(g, blk, 2 * half)


def _swa_specs(hq, kv, blk, half):
    prev = lambda n: jnp.maximum(n - 1, 0)
    q_spec = pl.BlockSpec((hq, blk, half), lambda n: (0, n, 0))
    kc = pl.BlockSpec((kv, blk, half), lambda n: (0, n, 0))
    kp = pl.BlockSpec((kv, blk, half), lambda n: (0, prev(n), 0))
    vc = pl.BlockSpec((kv, blk, 2 * half), lambda n: (0, n, 0))
    vp = pl.BlockSpec((kv, blk, 2 * half), lambda n: (0, prev(n), 0))
    tc = pl.BlockSpec((blk, half), lambda n: (n, 0))
    tp = pl.BlockSpec((blk, half), lambda n: (prev(n), 0))
    sink = pl.BlockSpec((kv, hq // kv, 1), lambda n: (0, 0, 0))
    o_spec = pl.BlockSpec((hq, blk, 2 * half), lambda n: (0, n, 0))
    return q_spec, kc, kp, vc, vp, tc, tp, sink, o_spec


def _swa_fwd(qa, qb, ka, kb, v, cos, sin, sinks, comm=None):
    hq, t, half = qa.shape
    kv = ka.shape[0]
    g, blk = hq // kv, SWA_BLOCK
    q_spec, kc, kp, vc, vp, tc, tp, sink, o_spec = _swa_specs(hq, kv, blk, half)

    def body(qa_r, qb_r, kap, kac, kbp, kbc, vp_r, vc_r, cq, sq, cp, sp, sink_r, o_r):
        tabs = (cq[...], sq[...], cp[...], sp[...])
        for h in range(kv):
            qs, ks = pl.ds(h * g, g), pl.ds(h, 1)
            o_r[qs] = _swa_math(pl.program_id(0), qa_r[qs], qb_r[qs], kap[ks], kac[ks], kbp[ks], kbc[ks], vp_r[ks], vc_r[ks],
                                *tabs, sink_r[ks]).astype(o_r.dtype)

    return _pcall(body, comm=comm, name="swa_fwd", grid=(t // blk,),
                  in_specs=[q_spec, q_spec, kp, kc, kp, kc, vp, vc, tc, tc, tp, tp, sink], out_specs=o_spec,
                  out_shape=jax.ShapeDtypeStruct((hq, t, 2 * half), f32),
                  compiler_params=_params(("arbitrary",)))(qa, qb, ka, ka, kb, kb, v, v, cos, sin, cos, sin, sinks)


def _swa_bwd(qa, qb, ka, kb, v, cos, sin, sinks, do, comm=None):
    hq, t, half = qa.shape
    kv = ka.shape[0]
    g, blk = hq // kv, SWA_BLOCK
    q_spec, kc, kp, vc, vp, tc, tp, sink, o_spec = _swa_specs(hq, kv, blk, half)

    def body(qa_r, qb_r, kap, kac, kbp, kbc, vp_r, vc_r, cq, sq, cp, sp, sink_r, do_r,
             dqa, dqb, dkap, dkac, dkbp, dkbc, dvp, dvc, dsink):
        n = pl.program_id(0)
        tabs = (cq[...], sq[...], cp[...], sp[...])
        fn = lambda a, b, c_, d, e, f_, g_, h_, s_: _swa_math(n, a, b, c_, d, e, f_, g_, h_, *tabs, s_)
        dsinks = []
        for h in range(kv):
            qs, ks = pl.ds(h * g, g), pl.ds(h, 1)
            _, vjp = jax.vjp(fn, qa_r[qs], qb_r[qs], kap[ks], kac[ks], kbp[ks], kbc[ks], vp_r[ks], vc_r[ks], sink_r[ks])
            grads = vjp(do_r[qs])
            dqa[qs] = grads[0]
            dqb[qs] = grads[1]
            for ref, val in zip((dkap, dkac, dkbp, dkbc, dvp, dvc), grads[2:8]):
                ref[ks] = val
            dsinks.append(grads[8])
        dsink_all = jnp.concatenate(dsinks, axis=0)

        @pl.when(n == 0)
        def _():
            dsink[...] = dsink_all

        @pl.when(n > 0)
        def _():
            dsink[...] += dsink_all

    sh = lambda a: jax.ShapeDtypeStruct(a.shape, f32)
    return _pcall(body, comm=comm, name="swa_bwd", grid=(t // blk,),
                  in_specs=[q_spec, q_spec, kp, kc, kp, kc, vp, vc, tc, tc, tp, tp, sink, o_spec],
                  out_specs=[q_spec, q_spec, kc, kc, kc, kc, vc, vc, sink],
                  out_shape=[sh(qa), sh(qb), sh(ka), sh(ka), sh(kb), sh(kb), sh(v), sh(v), sh(sinks)],
                  compiler_params=_params(("arbitrary",)))(qa, qb, ka, ka, kb, kb, v, v, cos, sin, cos, sin, sinks, do)


def _swa_split(za, hq, kv):
    t = za.shape[0]
    half = HEAD_DIM // 2
    tm = _row_tile(t, za.shape[1])

    def body(z_r, qa, qb, ka, kb, v):
        z = z_r[...]
        for h in range(hq):
            qa[h] = z[:, HEAD_DIM * h:HEAD_DIM * h + half]
            qb[h] = z[:, HEAD_DIM * h + half:HEAD_DIM * (h + 1)]
        for h in range(kv):
            o = HEAD_DIM * (hq + h)
            ka[h] = z[:, o:o + half]
            kb[h] = z[:, o + half:o + HEAD_DIM]
            o = HEAD_DIM * (hq + kv + h)
            v[h] = z[:, o:o + HEAD_DIM]

    spec = lambda n, w: pl.BlockSpec((n, tm, w), lambda i: (0, i, 0))
    sh = lambda n, w: jax.ShapeDtypeStruct((n, t, w), f32)
    return _pcall(body, name="swa_split", grid=(t // tm,), in_specs=[pl.BlockSpec((tm, za.shape[1]), lambda i: (i, 0))],
                  out_specs=[spec(hq, half), spec(hq, half), spec(kv, half), spec(kv, half), spec(kv, HEAD_DIM)],
                  out_shape=[sh(hq, half), sh(hq, half), sh(kv, half), sh(kv, half), sh(kv, HEAD_DIM)],
                  compiler_params=_params(("parallel",)))(za)


def _swa_merge(dqa, dqb, dkac, dkap, dkbc, dkbp, dvc, dvp):
    hq, t, half = dqa.shape
    kv = dkac.shape[0]
    blk = SWA_BLOCK
    nb = t // blk
    cols = HEAD_DIM * (hq + 2 * kv)

    def body(qa, qb, kac, kap, kbc, kbp, vc, vp, z_o, s_o):
        i = pl.program_id(0)
        more = (i < nb - 1).astype(f32)
        pieces = []
        for h in range(hq):
            pieces += [qa[h], qb[h]]
        for h in range(kv):
            pieces += [kac[h] + more * kap[h], kbc[h] + more * kbp[h]]
        for h in range(kv):
            pieces.append(vc[h] + more * vp[h])
        z = jnp.concatenate(pieces, axis=-1)
        z_o[...] = z
        colsum = jnp.sum(z, axis=0, keepdims=True)

        @pl.when(i == 0)
        def _():
            s_o[...] = colsum

        @pl.when(i > 0)
        def _():
            s_o[...] += colsum

    cur = lambda n, w: pl.BlockSpec((n, blk, w), lambda i: (0, i, 0))
    nxt = lambda n, w: pl.BlockSpec((n, blk, w), lambda i: (0, jnp.minimum(i + 1, nb - 1), 0))
    return _pcall(body, name="swa_merge", grid=(nb,),
                  in_specs=[cur(hq, half), cur(hq, half), cur(kv, half), nxt(kv, half), cur(kv, half), nxt(kv, half),
                            cur(kv, HEAD_DIM), nxt(kv, HEAD_DIM)],
                  out_specs=[pl.BlockSpec((blk, cols), lambda i: (i, 0)), pl.BlockSpec((1, cols), lambda i: (0, 0))],
                  out_shape=[jax.ShapeDtypeStruct((t, cols), f32), jax.ShapeDtypeStruct((1, cols), f32)],
                  compiler_params=_params(("arbitrary",)))(dqa, dqb, dkac, dkap, dkbc, dkbp, dvc, dvp)


def _xattn_math(q, k, v):
    s = _dot(q, k, "nt") * (q.shape[-1] ** -0.5)
    e = jnp.exp(s - jnp.max(s, axis=-1, keepdims=True))
    p = e / jnp.sum(e, axis=-1, keepdims=True)
    return _dot(p, v, "nn")


def _xattn_fwd(q, kvm):
    t, d = q.shape
    mlen = kvm.shape[0]
    hd = d // XATTN_HEADS
    tq = min(512, t)

    def body(q_r, k_r, v_r, o_r):
        o_r[...] = _xattn_math(q_r[...], k_r[...], v_r[...]).astype(o_r.dtype)

    return _pcall(body, name="xattn_fwd", grid=(XATTN_HEADS, t // tq),
                  in_specs=[pl.BlockSpec((tq, hd), lambda h, i: (i, h)), pl.BlockSpec((mlen, hd), lambda h, i: (0, h)),
                            pl.BlockSpec((mlen, hd), lambda h, i: (0, XATTN_HEADS + h))],
                  out_specs=pl.BlockSpec((tq, hd), lambda h, i: (i, h)), out_shape=jax.ShapeDtypeStruct((t, d), bf16),
                  compiler_params=_params(("parallel", "parallel")))(q, kvm, kvm)


def _xattn_bwd(q, kvm, do):
    t, d = q.shape
    mlen = kvm.shape[0]
    hd = d // XATTN_HEADS
    tq = min(512, t)

    def body(q_r, k_r, v_r, do_r, dq, dk, dv):
        _, vjp = jax.vjp(_xattn_math, q_r[...].astype(f32), k_r[...].astype(f32), v_r[...].astype(f32))
        gq, gk, gv = vjp(do_r[...].astype(f32))
        dq[...] = gq.astype(dq.dtype)
        first = pl.program_id(1) == 0

        @pl.when(first)
        def _():
            dk[...] = gk
            dv[...] = gv

        @pl.when(jnp.logical_not(first))
        def _():
            dk[...] += gk
            dv[...] += gv

    qs = pl.BlockSpec((tq, hd), lambda h, i: (i, h))
    ms = pl.BlockSpec((mlen, hd), lambda h, i: (0, h))
    return _pcall(body, name="xattn_bwd", grid=(XATTN_HEADS, t // tq),
                  in_specs=[qs, ms, pl.BlockSpec((mlen, hd), lambda h, i: (0, XATTN_HEADS + h)), qs],
                  out_specs=[qs, ms, ms],
                  out_shape=[jax.ShapeDtypeStruct((t, d), bf16), jax.ShapeDtypeStruct((mlen, d), f32),
                             jax.ShapeDtypeStruct((mlen, d), f32)],
                  compiler_params=_params(("parallel", "arbitrary")))(q, kvm, kvm, do)


def _chunk_cumsum(lw, reverse=False):
    h, l, _ = lw.shape
    i = lax.broadcasted_iota(jnp.int32, (l, l), 0)
    j = lax.broadcasted_iota(jnp.int32, (l, l), 1)
    tri = jnp.broadcast_to(((i <= j) if reverse else (i >= j)).astype(bf16)[None], (h, l, l))
    out = jnp.zeros(lw.shape, f32)
    for piece in _split3(lw):
        out = out + lax.dot_general(tri, piece, _dims("nn", 3), preferred_element_type=f32)
    return out


def _rwkv_chunk(s0, r, k, v, a, lw, cl, k_k, k_a, r_k, ln_w, ln_b):
    l = r.shape[1]
    kk = k * k_k
    kk = kk / jnp.maximum(jnp.sqrt(jnp.sum(kk * kk, axis=-1, keepdims=True)), 1e-12)
    km = k * (1.0 + (a - 1.0) * k_a)
    av, bv = -kk, kk * a
    p_incl, p_excl, p_inv = jnp.exp(cl), jnp.exp(cl - lw), jnp.exp(-cl)
    at, bh, kh, rt = av * p_excl, bv * p_inv, km * p_inv, r * p_incl
    i = lax.broadcasted_iota(jnp.int32, (l, l), 0)
    j = lax.broadcasted_iota(jnp.int32, (l, l), 1)
    strict, incl = (i > j)[None], (i >= j)[None]
    a_ab = jnp.where(strict, _dot(at, bh, "nt"), 0.0)
    a_ak = jnp.where(strict, _dot(at, kh, "nt"), 0.0)
    a_rb = jnp.where(incl, _dot(rt, bh, "nt"), 0.0)
    a_rk = jnp.where(incl, _dot(rt, kh, "nt"), 0.0)
    rhs = _dot(at, s0, "nt") + _dot(a_ak, v, "nn")
    inv = a_ab + (i == j)[None].astype(f32)
    pw = a_ab
    for _ in range(int(math.log2(l)) - 1):
        pw = _dot(pw, pw, "nn")
        inv = inv + _dot(inv, pw, "nn")
    sa = _dot(inv, rhs, "nn")
    y = _dot(rt, s0, "nt") + _dot(a_rk, v, "nn") + _dot(a_rb, sa, "nn")
    p_last = p_incl[:, l - 1:l, :]
    s_end = s0 * p_last + _dot(v, kh * p_last, "tn") + _dot(sa, bh * p_last, "tn")
    mu = jnp.mean(y, axis=-1, keepdims=True)
    var = jnp.mean(jnp.square(y - mu), axis=-1, keepdims=True)
    out = (y - mu) * lax.rsqrt(var + GN_EPS) * ln_w + ln_b
    out = out + jnp.sum(r * km * r_k, axis=-1, keepdims=True) * v
    return out, s_end


def _rwkv_fwd(r, k, v, a, lw, heads, comm=None):
    h, t, n = r.shape
    l = min(RWKV_CHUNK, t)
    nc = t // l
    seq = pl.BlockSpec((h, l, n), lambda c: (0, c, 0))
    par = pl.BlockSpec((h, 1, n), lambda c: (0, 0, 0))

    def body(r_r, k_r, v_r, a_r, lw_r, p0, p1, p2, p3, p4, y_r, ck_r, s_scr):
        @pl.when(pl.program_id(0) == 0)
        def _():
            s_scr[...] = jnp.zeros_like(s_scr)

        s0 = s_scr[...]
        ck_r[0] = s0
        lw_v = lw_r[...]
        out, s_end = _rwkv_chunk(s0, r_r[...], k_r[...], v_r[...], a_r[...], lw_v, _chunk_cumsum(lw_v),
                                 p0[...], p1[...], p2[...], p3[...], p4[...])
        y_r[...] = out
        s_scr[...] = s_end

    return _pcall(body, comm=comm, name="rwkv_fwd", grid=(nc,), in_specs=[seq] * 5 + [par] * 5,
                  out_specs=[seq, pl.BlockSpec((1, h, n, n), lambda c: (c, 0, 0, 0))],
                  out_shape=[jax.ShapeDtypeStruct((h, t, n), f32), jax.ShapeDtypeStruct((nc, h, n, n), f32)],
                  scratch_shapes=[pltpu.VMEM((h, n, n), f32)],
                  compiler_params=_params(("arbitrary",)))(r, k, v, a, lw, *heads)


def _rwkv_bwd(r, k, v, a, lw, heads, ck, dy, comm=None):
    h, t, n = r.shape
    l = min(RWKV_CHUNK, t)
    nc = t // l
    seq = pl.BlockSpec((h, l, n), lambda c: (0, nc - 1 - c, 0))
    par = pl.BlockSpec((h, 1, n), lambda c: (0, 0, 0))

    def body(r_r, k_r, v_r, a_r, lw_r, p0, p1, p2, p3, p4, ck_r, dy_r,
             dr, dk, dv, da, dlw, g0, g1, g2, g3, g4, ds_scr):
        first = pl.program_id(0) == 0

        @pl.when(first)
        def _():
            ds_scr[...] = jnp.zeros_like(ds_scr)

        lw_v = lw_r[...]
        _, vjp = jax.vjp(_rwkv_chunk, ck_r[0], r_r[...], k_r[...], v_r[...], a_r[...], lw_v, _chunk_cumsum(lw_v),
                         p0[...], p1[...], p2[...], p3[...], p4[...])
        grads = vjp((dy_r[...], ds_scr[...]))
        ds_scr[...] = grads[0]
        dr[...] = grads[1]
        dk[...] = grads[2]
        dv[...] = grads[3]
        da[...] = grads[4]
        dlw[...] = grads[5] + _chunk_cumsum(grads[6], reverse=True)
        acc = (g0, g1, g2, g3, g4)

        @pl.when(first)
        def _():
            for ref, val in zip(acc, grads[7:]):
                ref[...] = val

        @pl.when(jnp.logical_not(first))
        def _():
            for ref, val in zip(acc, grads[7:]):
                ref[...] += val

    seq_sh = jax.ShapeDtypeStruct((h, t, n), f32)
    par_sh = jax.ShapeDtypeStruct((h, 1, n), f32)
    return _pcall(body, comm=comm, name="rwkv_bwd", grid=(nc,),
                  in_specs=[seq] * 5 + [par] * 5 + [pl.BlockSpec((1, h, n, n), lambda c: (nc - 1 - c, 0, 0, 0)), seq],
                  out_specs=[seq] * 5 + [par] * 5, out_shape=[seq_sh] * 5 + [par_sh] * 5,
                  scratch_shapes=[pltpu.VMEM((h, n, n), f32)],
                  compiler_params=_params(("arbitrary",)))(r, k, v, a, lw, *heads, ck, dy)


def _rwkv_pre_math(c, lp, p_rkv, p_rkv_prev, p_l, p_l_prev, mu_rkv, mu_l, w0, a0, decay_up, aaa_up, gate_up):
    dlp, alp, _ = lp
    z = p_rkv + (p_rkv_prev - p_rkv) * mu_rkv
    zl = p_l + (p_l_prev - p_l) * mu_l
    r, k, v = z[:, :c], z[:, c:2 * c], z[:, 2 * c:]
    wd, ad, gd = zl[:, :dlp], zl[:, dlp:dlp + alp], zl[:, dlp + alp:]
    w = -_softplus(-(w0 + _dot(jnp.tanh(wd), decay_up, "nn"))) - 0.5
    a = _sigmoid(a0 + _dot(ad, aaa_up, "nn"))
    g = _dot(_sigmoid(gd), gate_up, "nn")
    return r, k, v, -jnp.exp(w), a, g


def _pad_to(a, n, axis):
    if a.shape[axis] == n:
        return a
    pad = [(0, 0)] * a.ndim
    pad[axis] = (0, n - a.shape[axis])
    return jnp.pad(a, pad)


def _up128(n):
    return -(-n // LANE) * LANE


def _shift_down(p):
    return jnp.concatenate([jnp.zeros((1, p.shape[1]), p.dtype), p[:-1]], axis=0)


def _shift_up(p):
    return jnp.concatenate([p[1:], jnp.zeros((1, p.shape[1]), p.dtype)], axis=0)


def _swiglu(g, u):
    return jax.nn.silu(g) * u


def _ffn_hidden(name, h, w_gate, w_up, comm=None):
    t, d = h.shape
    nb, _, n = w_gate.shape
    tm = _pick(t, (1024, 512, 256, 128))

    def body(h_r, wg_r, wu_r, g_o, u_o, a_o):
        hv = h_r[...]
        g = lax.dot_general(hv, wg_r[0], _dims("nn", 2), preferred_element_type=f32)
        u = lax.dot_general(hv, wu_r[0], _dims("nn", 2), preferred_element_type=f32)
        g_o[0] = g.astype(bf16)
        u_o[0] = u.astype(bf16)
        a_o[0] = _swiglu(g, u).astype(bf16)

    w_spec = pl.BlockSpec((1, d, n), lambda i, j: (j, 0, 0))
    o_spec = pl.BlockSpec((1, tm, n), lambda i, j: (j, i, 0))
    sh = jax.ShapeDtypeStruct((nb, t, n), bf16)
    return _pcall(body, comm=comm, name=name, grid=(t // tm, nb),
                  in_specs=[pl.BlockSpec((tm, d), lambda i, j: (i, 0)), w_spec, w_spec],
                  out_specs=[o_spec, o_spec, o_spec], out_shape=[sh, sh, sh],
                  compiler_params=_params(("parallel", "arbitrary")))(h, w_gate, w_up)


def _ffn_out(name, act, w_down, x, comm=None):
    nb, t, n = act.shape
    d = w_down.shape[2]
    tm, tn = _pick(t, (512, 256, 128)), _pick(d, (512, 256, 128))

    def body(a_r, w_r, x_r, o_r):
        acc = x_r[...]
        for j in range(nb):
            acc = acc + 0.5 * lax.dot_general(a_r[j], w_r[j], _dims("nn", 2), preferred_element_type=f32)
        o_r[...] = acc

    return _pcall(body, comm=comm, name=name, grid=(t // tm, d // tn),
                  in_specs=[pl.BlockSpec((nb, tm, n), lambda i, j: (0, i, 0)), pl.BlockSpec((nb, n, tn), lambda i, j: (0, 0, j)),
                            pl.BlockSpec((tm, tn), lambda i, j: (i, j))],
                  out_specs=pl.BlockSpec((tm, tn), lambda i, j: (i, j)), out_shape=jax.ShapeDtypeStruct((t, d), f32),
                  compiler_params=_params(("parallel", "parallel")))(act, w_down, x)


def _ffn_dhidden(name, dout, w_down, gate, up, comm=None):
    t, d = dout.shape
    nb, n, _ = w_down.shape
    tm = _pick(t, (512, 256, 128))

    def body(d_r, w_r, g_r, u_r, dg_o, du_o):
        dact = 0.5 * lax.dot_general(d_r[...].astype(MXU_DTYPE), w_r[0], _dims("nt", 2), preferred_element_type=f32)
        g, u = g_r[0].astype(f32), u_r[0].astype(f32)
        s = _sigmoid(g)
        gs = g * s
        dg_o[0] = (dact * u * (s + gs * (1.0 - s))).astype(bf16)
        du_o[0] = (dact * gs).astype(bf16)

    o_spec = pl.BlockSpec((1, tm, n), lambda i, j: (j, i, 0))
    sh = jax.ShapeDtypeStruct((nb, t, n), bf16)
    return _pcall(body, comm=comm, name=name, grid=(t // tm, nb),
                  in_specs=[pl.BlockSpec((tm, d), lambda i, j: (i, 0)), pl.BlockSpec((1, n, d), lambda i, j: (j, 0, 0)), o_spec, o_spec],
                  out_specs=[o_spec, o_spec], out_shape=[sh, sh],
                  compiler_params=_params(("parallel", "arbitrary")))(dout, w_down, gate, up)


def _ffn_dw_down(name, act, dout, comm=None):
    nb, t, n = act.shape
    d = dout.shape[1]
    tn = _pick(d, (1024, 512, 256, 128))

    def body(a_r, d_r, o_r):
        acc = lax.dot_general(a_r[0], d_r[...].astype(MXU_DTYPE), _dims("tn", 2), preferred_element_type=f32)
        o_r[0] = (0.5 * acc).astype(bf16)

    return _pcall(body, comm=comm, name=name, grid=(nb, d // tn),
                  in_specs=[pl.BlockSpec((1, t, n), lambda j, i: (j, 0, 0)), pl.BlockSpec((t, tn), lambda j, i: (0, i))],
                  out_specs=pl.BlockSpec((1, n, tn), lambda j, i: (j, 0, i)), out_shape=jax.ShapeDtypeStruct((nb, n, d), bf16),
                  compiler_params=_params(("parallel", "parallel")))(act, dout)


def _ffn_dw_hidden(name, h, dgate, dup, comm=None):
    t, d = h.shape
    nb, _, n = dgate.shape
    tm = _pick(d, (1024, 512, 256, 128))

    def body(h_r, g_r, u_r, dg_o, du_o):
        hv = h_r[...]
        dg_o[0] = lax.dot_general(hv, g_r[0], _dims("tn", 2), preferred_element_type=f32).astype(bf16)
        du_o[0] = lax.dot_general(hv, u_r[0], _dims("tn", 2), preferred_element_type=f32).astype(bf16)

    g_spec = pl.BlockSpec((1, t, n), lambda j, i: (j, 0, 0))
    o_spec = pl.BlockSpec((1, tm, n), lambda j, i: (j, i, 0))
    sh = jax.ShapeDtypeStruct((nb, d, n), bf16)
    return _pcall(body, comm=comm, name=name, grid=(nb, d // tm),
                  in_specs=[pl.BlockSpec((t, tm), lambda j, i: (0, i)), g_spec, g_spec],
                  out_specs=[o_spec, o_spec], out_shape=[sh, sh],
                  compiler_params=_params(("parallel", "parallel")))(h, dgate, dup)


def _ffn_dh(name, dhid, w, res=None, comm=None):
    nb, t, n = dhid.shape
    d = w.shape[1]
    tm, tn = _pick(t, (512, 256, 128)), _pick(d, (512, 256, 128))

    def body(*refs):
        acc = refs[2][...] if res is not None else jnp.zeros((tm, tn), f32)
        for j in range(nb):
            acc = acc + lax.dot_general(refs[0][j], refs[1][j], _dims("nt", 2), preferred_element_type=f32)
        refs[-1][...] = acc

    in_specs = [pl.BlockSpec((nb, tm, n), lambda i, j: (0, i, 0)), pl.BlockSpec((nb, tn, n), lambda i, j: (0, j, 0))]
    args = [dhid, w]
    if res is not None:
        in_specs.append(pl.BlockSpec((tm, tn), lambda i, j: (i, j)))
        args.append(res)
    return _pcall(body, comm=comm, name=name, grid=(t // tm, d // tn), in_specs=in_specs,
                  out_specs=pl.BlockSpec((tm, tn), lambda i, j: (i, j)), out_shape=jax.ShapeDtypeStruct((t, d), f32),
                  compiler_params=_params(("parallel", "parallel")))(*args)


def _lora_bounds(c, lora):
    dl, al, gl = lora
    o1 = 3 * c
    o2, o3 = o1 + dl, o1 + dl + al
    return o1, o2, o3, o3 + gl, (_up128(dl), _up128(al), _up128(gl))


def _win_split(g8, c, lora, comm=None):
    nb, d, n = g8.shape
    o1, o2, o3, o4, (dlp, alp, glp) = _lora_bounds(c, lora)
    tm = _row_tile(d, nb * n, g8.dtype.itemsize)

    def body(x, rkv_o, lora_o, swa_o):
        w = jnp.concatenate([x[j] for j in range(nb)], axis=-1)
        pad = lambda p, m: p if p.shape[1] == m else jnp.concatenate([p, jnp.zeros((p.shape[0], m - p.shape[1]), p.dtype)], axis=-1)
        rkv_o[...] = w[:, :o1]
        lora_o[...] = jnp.concatenate([pad(w[:, o1:o2], dlp), pad(w[:, o2:o3], alp), pad(w[:, o3:o4], glp)], axis=-1)
        swa_o[...] = w[:, o4:]

    widths = (o1, dlp + alp + glp, nb * n - o4)
    return _pcall(body, comm=comm, name="w_in_split", grid=(d // tm,), in_specs=[pl.BlockSpec((nb, tm, n), lambda i: (0, i, 0))],
                  out_specs=[pl.BlockSpec((tm, wd), lambda i: (i, 0)) for wd in widths],
                  out_shape=[jax.ShapeDtypeStruct((d, wd), g8.dtype) for wd in widths],
                  compiler_params=_params(("parallel",)))(g8)


def _win_merge(dw_rkv, dw_lora, dw_swa, c, lora, nb):
    d = dw_rkv.shape[0]
    o1, o2, o3, o4, (dlp, alp, glp) = _lora_bounds(c, lora)
    dl, al, gl = lora
    total = o4 + dw_swa.shape[1]
    n = total // nb
    tm = _row_tile(d, total, dw_rkv.dtype.itemsize)

    def body(a, b, s, o):
        bv = b[...]
        w = jnp.concatenate([a[...], bv[:, :dl], bv[:, dlp:dlp + al], bv[:, dlp + alp:dlp + alp + gl], s[...]], axis=-1)
        for j in range(nb):
            o[j] = w[:, n * j:n * (j + 1)]

    ins = [dw_rkv, dw_lora, dw_swa]
    return _pcall(body, name="w_in_merge", grid=(d // tm,), in_specs=[pl.BlockSpec((tm, a.shape[1]), lambda i: (i, 0)) for a in ins],
                  out_specs=pl.BlockSpec((nb, tm, n), lambda i: (0, i, 0)), out_shape=jax.ShapeDtypeStruct((nb, d, n), dw_rkv.dtype),
                  compiler_params=_params(("parallel",)))(*ins)


def _norm_bwd(name, x, g_norm, dh, dres, comm=None):
    d = x.shape[1]

    def fn(xb, dhb, drb, g):
        _, vjp = jax.vjp(_rms, xb, g)
        dx, dg = vjp(dhb)
        return drb + dx, dg

    return _rows(name, fn, [x, dh, dres], [g_norm], [(d, f32)], [((1, d), f32)], comm=comm)


def _colsum(name, a):
    return _rows(name, lambda ab: (jnp.sum(ab.astype(f32), axis=0, keepdims=True),), [a], [], [], [((1, a.shape[1]), f32)])[0]


def kernel(x, mem, f1_norm, f1_gate, f1_up, f1_down, mix_norm, w_in, b_in_attn, rw_mu, rw_w0, rw_decay_up, rw_a0, rw_aaa_up, rw_gate_up, rw_k_k, rw_k_a, rw_r_k, rw_lnx_w, rw_lnx_b, attn_sinks, w_out, b_out, xa_norm, mem_norm, w_xq, w_xkv, w_xo, f2_norm, f2_gate, f2_up, f2_down, final_norm, loss_target, m_f1_norm, m_f1_gate, m_f1_up, m_f1_down, m_mix_norm, m_w_in, m_b_in_attn, m_rw_mu, m_rw_w0, m_rw_decay_up, m_rw_a0, m_rw_aaa_up, m_rw_gate_up, m_rw_k_k, m_rw_k_a, m_rw_r_k, m_rw_lnx_w, m_rw_lnx_b, m_attn_sinks, m_w_out, m_b_out, m_xa_norm, m_mem_norm, m_w_xq, m_w_xkv, m_w_xo, m_f2_norm, m_f2_gate, m_f2_up, m_f2_down, m_final_norm, v_f1_norm, v_f1_gate, v_f1_up, v_f1_down, v_mix_norm, v_w_in, v_b_in_attn, v_rw_mu, v_rw_w0, v_rw_decay_up, v_rw_a0, v_rw_aaa_up, v_rw_gate_up, v_rw_k_k, v_rw_k_a, v_rw_r_k, v_rw_lnx_w, v_rw_lnx_b, v_attn_sinks, v_w_out, v_b_out, v_xa_norm, v_mem_norm, v_w_xq, v_w_xkv, v_w_xo, v_f2_norm, v_f2_gate, v_f2_up, v_f2_down, v_final_norm):
    names = ["f1_norm", "f1_gate", "f1_up", "f1_down", "mix_norm", "w_in", "b_in_attn", "rw_mu", "rw_w0", "rw_decay_up",
             "rw_a0", "rw_aaa_up", "rw_gate_up", "rw_k_k", "rw_k_a", "rw_r_k", "rw_lnx_w", "rw_lnx_b", "attn_sinks", "w_out",
             "b_out", "xa_norm", "mem_norm", "w_xq", "w_xkv", "w_xo", "f2_norm", "f2_gate", "f2_up", "f2_down", "final_norm"]
    env = dict(locals())
    w_of = {k: env[k] for k in names}
    m_of = {k: env["m_" + k] for k in names}
    v_of = {k: env["v_" + k] for k in names}
    col_sharded = ["f1_gate", "f1_up", "w_in", "rw_decay_up", "rw_aaa_up", "rw_gate_up", "w_xkv", "f2_gate", "f2_up"]
    row_sharded = ["f1_down", "w_out", "w_xq", "w_xo", "f2_down"]
    sharded = col_sharded + row_sharded
    small = [k for k in names if k not in sharded]

    x0, mem0, tgt = x[0], mem[0], loss_target[0]
    t, d = x0.shape
    c = rw_w0.shape[-1]
    heads = c // HEAD_DIM
    dl, al, gl = rw_decay_up.shape[1], rw_aaa_up.shape[1], rw_gate_up.shape[1]
    dlp, alp, glp = _up128(dl), _up128(al), _up128(gl)
    swa_w = d - c
    hq, kvh = swa_w // HEAD_DIM, (b_in_attn.shape[-1] - swa_w) // (2 * HEAD_DIM)
    my_x, my_y, my_c = _position()
    c_idx = jnp.reshape(my_c, (1,)).astype(jnp.int32)
    chip_idx = jnp.reshape(2 * my_x + my_y, (1,)).astype(jnp.int32)

    shard2d = {k: w_of[k][0] for k in sharded}
    cast = {k: _rows("cast_" + k, lambda a: (a,), [shard2d[k]], [], [(shard2d[k].shape[1], bf16)], tm=_row_tile(*shard2d[k].shape))[0]
            for k in sharded}
    ffn1_keys, ffn2_keys = ["f1_gate", "f1_up", "f1_down"], ["f2_gate", "f2_up", "f2_down"]
    in_keys = ["w_in", "rw_decay_up", "rw_aaa_up", "rw_gate_up"]
    kept_in_blocks = ffn1_keys + ffn2_keys + ["w_in", "w_xkv"]

    def whole(k, g8):
        if k in kept_in_blocks:
            return g8
        if k in col_sharded:
            return g8.transpose(1, 0, 2).reshape(g8.shape[1], N_DEV * g8.shape[2])
        return g8.reshape(N_DEV * g8.shape[1], g8.shape[2])

    def gather_of(keys):
        return _gather_comm([cast[k] for k in keys])

    def wholes(keys, gathered):
        return {k: whole(k, g8) for k, g8 in zip(keys, gathered)}

    (h1,), gathered = _rows("f1_norm", lambda xb, g: (_rms(xb, g),), [x0], [f1_norm], [(d, bf16)], comm=gather_of(ffn1_keys[:2]))
    full = wholes(ffn1_keys[:2], gathered)
    (gate1, up1, act1), gathered = _ffn_hidden("f1_hidden", h1, full["f1_gate"], full["f1_up"], comm=gather_of(["f1_down"]))
    full.update(wholes(["f1_down"], gathered))
    x1, gathered = _ffn_out("f1_out", act1, full["f1_down"], x0, comm=gather_of(in_keys))
    full.update(wholes(in_keys, gathered))
    ffn1_saved = (h1, gate1, up1, act1)
    (w_rkv, w_lora, w_swa), gathered = _win_split(full["w_in"], c, (dl, al, gl), comm=gather_of(["w_out"]))
    full.update(wholes(["w_out"], gathered))
    o1, o2, o3, shift_cols, _ = _lora_bounds(c, (dl, al, gl))
    mu_rkv = rw_mu[:, :3 * c]
    mu_l = jnp.concatenate([_pad_to(rw_mu[:, o1:o2], dlp, 1), _pad_to(rw_mu[:, o2:o3], alp, 1),
                            _pad_to(rw_mu[:, o3:shift_cols], glp, 1)], axis=1)
    decay_up = _pad_to(full["rw_decay_up"], dlp, 0).astype(f32)
    aaa_up = _pad_to(full["rw_aaa_up"], alp, 0).astype(f32)
    gate_up = _pad_to(full["rw_gate_up"], glp, 0).astype(f32)
    head_pars = [p.reshape(heads, 1, HEAD_DIM) for p in (rw_k_k, rw_k_a, rw_r_k, rw_lnx_w, rw_lnx_b)]
    final_g = final_norm.reshape(1, d)

    (h2,) = _rows("mix_norm", lambda xb, g: (_rms(xb, g),), [x1], [mix_norm], [(d, bf16)])
    p_rkv, gathered = _mm("in_rkv", h2, w_rkv, "nn", f32, comm=gather_of(["w_xq"]))
    full.update(wholes(["w_xq"], gathered))
    p_l =_mm("in_lora", h2, w_lora, "nn", f32)
    za = _mm("in_swa", h2, w_swa, "nn", f32, bias=b_in_attn)
    pre_fn = functools.partial(_rwkv_pre_math, c, (dlp, alp, glp))
    pre_rows = [p_rkv, _shift_down(p_rkv), p_l, _shift_down(p_l)]
    pre_full = [mu_rkv, mu_l, rw_w0, rw_a0, decay_up, aaa_up, gate_up]
    (r_h, k_h, v_h, lw_h, a_h, g_t), gathered = _rows("rwkv_pre", pre_fn, pre_rows, pre_full,
                                                       [(c, f32, HEAD_DIM)] * 5 + [(c, f32)], tm=128, comm=gather_of(["w_xo"]))
    full.update(wholes(["w_xo"], gathered))
    seqs = [r_h, k_h, v_h, a_h, lw_h]
    (y_heads, checkpoints), gathered = _rwkv_fwd(*seqs, head_pars, comm=gather_of(["f2_gate"]))
    full.update(wholes(["f2_gate"], gathered))

    pos = jnp.arange(t, dtype=f32)
    inv_freq = ROPE_THETA ** (-jnp.arange(0, HEAD_DIM, 2, dtype=f32) / HEAD_DIM)
    ang = pos[:, None] * inv_freq[None, :]
    cos, sin = jnp.cos(ang), jnp.sin(ang)
    sinks3 = attn_sinks.reshape(kvh, hq // kvh, 1)
    swa_in = (*_swa_split(za, hq, kvh), cos, sin, sinks3)
    y_swa_heads, gathered = _swa_fwd(*swa_in, comm=gather_of(["f2_up"]))
    full.update(wholes(["f2_up"], gathered))
    (ycat,) = _rows("mix_cat", lambda yb, gb, sb: (jnp.concatenate([yb * gb, sb], axis=1),), [y_heads, g_t, y_swa_heads], [],
                    [(d, bf16)])
    x2, gathered = _mm("mix_out", ycat, full["w_out"], "nn", f32, res=x1, bias=b_out, comm=gather_of(["w_xkv"]))
    full.update(wholes(["w_xkv"], gathered))

    (h3,) = _rows("xa_norm", lambda xb, g: (_rms(xb, g),), [x2], [xa_norm], [(d, bf16)])
    (mem_n,) = _rows("mem_norm", lambda xb, g: (_rms(xb, g),), [mem0], [mem_norm], [(d, bf16)])
    q_x = _mm("xa_q", h3, full["w_xq"], "nn", bf16)
    kv_x = _mm("xa_kv", mem_n, full["w_xkv"], "nn", bf16)
    o_x = _xattn_fwd(q_x, kv_x)
    x3 = _mm("xa_out", o_x, full["w_xo"], "nn", f32, res=x2)
    (h4,) = _rows("f2_norm", lambda xb, g: (_rms(xb, g),), [x3], [f2_norm], [(d, bf16)])
    (gate2, up2, act2), gathered = _ffn_hidden("f2_hidden", h4, full["f2_gate"], full["f2_up"], comm=gather_of(["f2_down"]))
    full.update(wholes(["f2_down"], gathered))
    x4 = _ffn_out("f2_out", act2, full["f2_down"], x3)
    ffn2_saved = (h4, gate2, up2, act2)

    def loss_fn(xb, tb, g):
        def per_row(xv, gv):
            return 0.5 * jnp.mean(jnp.square(_rms(xv, gv) - tb), axis=-1, keepdims=True)

        lrow, vjp = jax.vjp(per_row, xb, g)
        dxb, dgb = vjp(jnp.ones_like(lrow))
        return dxb, dgb, jnp.sum(lrow, axis=0, keepdims=True)

    dx4, d_final, loss_part = _rows("loss", loss_fn, [x4, tgt], [final_g], [(d, f32)], [((1, d), f32), ((1, 1), f32)])
    loss = lax.psum(loss_part[0, 0], ("x", "y", "c"))

    grads, small_g, out = {}, {"final_norm": d_final}, {}

    def pair_sums_of(tag, keys, carrier=None):
        blocks = []
        for k in keys:
            g2 = grads[k]
            rr, cc = shard2d[k].shape
            if k in kept_in_blocks:
                blocks.append(g2)
            else:
                blocks.append(g2.reshape(g2.shape[0], N_DEV, cc).transpose(1, 0, 2) if k in col_sharded else g2.reshape(N_DEV, rr, cc))
        if carrier is None:
            from_sibling = _comm_only("grads_to_sibling_" + tag, _sibling_comm(blocks))
        else:
            carried, from_sibling = carrier(_sibling_comm(blocks))
        pairs = [_pair_add("pair_add_" + k, b, o, c_idx) for k, b, o in zip(keys, blocks, from_sibling)]
        return pairs if carrier is None else (pairs, carried)

    def update(keys, pair_sums, from_chips):
        for k, part, others in zip(keys, pair_sums, from_chips):
            res = _adam_sharded("adam_" + k, shard2d[k], m_of[k][0], v_of[k][0], part, others, chip_idx)
            out[k] = [a.reshape(w_of[k].shape) for a in res]

    def ffn_backward(tag, keys, xin, g_norm, saved, dout, first_comm, start_exchange):
        h, gate, up, act = saved
        k_gate, k_up, k_down = keys
        if first_comm is None:
            grads[k_down], carried = _ffn_dw_down(tag + "_dw_down", act, dout), None
        else:
            grads[k_down], carried = _ffn_dw_down(tag + "_dw_down", act, dout, comm=first_comm)
        down_pairs, (dgate, dup) = pair_sums_of(
            k_down, [k_down], lambda cm: _ffn_dhidden(tag + "_dhidden", dout, full[k_down], gate, up, comm=cm))
        (grads[k_gate], grads[k_up]), from_chips = _ffn_dw_hidden(tag + "_dw_hidden", h, dgate, dup, comm=_chips_comm(down_pairs))
        update([k_down], down_pairs, from_chips)
        hidden_pairs, dh = pair_sums_of(tag + "_hidden", [k_gate, k_up],
                                        lambda cm: _ffn_dh(tag + "_dh1", dgate, full[k_gate], comm=cm))
        pending = None
        if start_exchange:
            pending, dh = _chips_start("grads_to_chips_start_" + tag, hidden_pairs, dh)
        dh = _ffn_dh(tag + "_dh2", dup, full[k_up], res=dh)
        dx, dg_norm = _norm_bwd(tag + "_dnorm", xin, g_norm, dh, dout)
        return dx, dg_norm, hidden_pairs, carried, pending

    xa_keys = ["w_xq", "w_xkv", "w_xo"]
    dx3, small_g["f2_norm"], ffn2_pairs, _, _ = ffn_backward("f2b", ffn2_keys, x3, f2_norm, ffn2_saved, dx4, None, False)

    do_x = _mm("xa_do", dx3, full["w_xo"], "nt", bf16)
    grads["w_xo"] = _mm("xa_dwo", o_x, dx3, "tn", bf16)
    dq_x, dk_x, dv_x = _xattn_bwd(q_x, kv_x, do_x)
    grads["w_xq"] = _mm("xa_dwq", h3, dq_x, "tn", bf16)
    dh3 = _mm("xa_dh", dq_x, full["w_xq"], "nt", f32)
    dkv_x = jnp.concatenate([dk_x, dv_x], axis=1)
    grads["w_xkv"] = _mm("xa_dwkv", mem_n, dkv_x, "tn", bf16, out_blocks=N_DEV)
    dkv_blocks = dkv_x.astype(bf16).reshape(dkv_x.shape[0], N_DEV, -1).transpose(1, 0, 2)
    dmem_n = _ffn_dh("xa_dmem", dkv_blocks, full["w_xkv"])
    _, small_g["mem_norm"] = _norm_bwd("mem_dnorm", mem0, mem_norm, dmem_n, jnp.zeros_like(mem0))
    xa_pairs, (dx2, small_g["xa_norm"]) = pair_sums_of(
        "xa", xa_keys, lambda cm: _norm_bwd("xa_dnorm", x2, xa_norm, dh3, dx3, comm=cm))

    dycat = _mm("mix_dy", dx2, full["w_out"], "nt", f32)
    grads["w_out"] = _mm("mix_dwout", ycat, dx2, "tn", bf16)
    small_g["b_out"] = _colsum("mix_dbout", dx2)
    out_pairs, (dy_heads, dg_t, do_sw) = pair_sums_of("out", ["w_out"], lambda cm: _rows(
        "mix_dgate", lambda db, yb, gb: (db[:, :c] * gb, db[:, :c] * yb, db[:, c:]), [dycat, y_heads, g_t], [],
        [(c, f32, HEAD_DIM), (c, f32), (swa_w, f32, HEAD_DIM)], comm=cm))
    rw_grads, from_chips = _rwkv_bwd(*seqs, head_pars, checkpoints, dy_heads, comm=_chips_comm(ffn2_pairs))
    update(ffn2_keys[:2], ffn2_pairs, from_chips)
    dr_h, dk_h, dv_h, da_h, dlw_h = rw_grads[:5]
    for nm, gh in zip(("rw_k_k", "rw_k_a", "rw_r_k", "rw_lnx_w", "rw_lnx_b"), rw_grads[5:]):
        small_g[nm] = gh.reshape(w_of[nm].shape)

    def pre_bwd(*args):
        _, vjp = jax.vjp(pre_fn, *args[:4], *args[10:])
        return vjp(tuple(args[4:10]))

    pre_cts = [dr_h, dk_h, dv_h, dlw_h, da_h, dg_t]
    pre_out = _rows("rwkv_pre_bwd", pre_bwd, pre_rows + pre_cts, pre_full,
                    [(3 * c, f32), (3 * c, f32), (dlp + alp + glp, f32), (dlp + alp + glp, f32)],
                    [(p.shape, f32) for p in pre_full], tm=128)
    dp_rkv = pre_out[0] + _shift_up(pre_out[1])
    dp_l = pre_out[2] + _shift_up(pre_out[3])
    dmu_rkv, dmu_l, small_g["rw_w0"], small_g["rw_a0"], d_decay_up, d_aaa_up, d_gate_up = pre_out[4:]
    small_g["rw_mu"] = jnp.concatenate([dmu_rkv, dmu_l[:, :dl], dmu_l[:, dlp:dlp + al], dmu_l[:, dlp + alp:dlp + alp + gl]], axis=1)
    grads["rw_decay_up"] = d_decay_up[:dl].astype(bf16)
    grads["rw_aaa_up"] = d_aaa_up[:al].astype(bf16)
    grads["rw_gate_up"] = d_gate_up[:gl].astype(bf16)

    sw, from_chips = _swa_bwd(*swa_in, do_sw, comm=_chips_comm(xa_pairs))
    update(xa_keys, xa_pairs, from_chips)
    small_g["attn_sinks"] = sw[8].reshape(attn_sinks.shape)
    dza, small_g["b_in_attn"] = _swa_merge(sw[0], sw[1], sw[3], sw[2], sw[5], sw[4], sw[7], sw[6])

    dw_rkv = _mm("in_dwrkv", h2, dp_rkv, "tn", bf16)
    dw_l = _mm("in_dwlora", h2, dp_l, "tn", bf16)
    dw_swa = _mm("in_dwswa", h2, dza, "tn", bf16)
    grads["w_in"] = _win_merge(dw_rkv, dw_l, dw_swa, c, (dl, al, gl), N_DEV)
    dh2, from_chips = _mm("in_dh1", dp_rkv, w_rkv, "nt", f32, comm=_chips_comm(out_pairs))
    update(["w_out"], out_pairs, from_chips)
    dh2 = _mm("in_dh2", dp_l, w_lora, "nt", f32, res=dh2)
    in_pairs, dh2 = pair_sums_of("in", in_keys, lambda cm: _mm("in_dh3", dza, w_swa, "nt", f32, res=dh2, comm=cm))
    pending_in, dh2 = _chips_start("grads_to_chips_start_in", in_pairs, dh2)
    dx1, small_g["mix_norm"] = _norm_bwd("mix_dnorm", x1, mix_norm, dh2, dx2)

    dx0, small_g["f1_norm"], _, _, pending = ffn_backward("f1b", ffn1_keys, x0, f1_norm, ffn1_saved, dx1, None, True)
    others_done = _fence("updates_done", [res[1] for res in out.values()] + [dx0] + list(small_g.values()))
    in_pairs, from_chips = _chips_wait("grads_to_chips_wait_in", *pending_in, others_done)
    update(in_keys, in_pairs, from_chips)
    pairs, from_chips = _chips_wait("grads_to_chips_wait_f1b", *pending, others_done)
    update(ffn1_keys[:2], pairs, from_chips)

    sizes = [int(w_of[k].size) for k in small]
    total = sum(sizes)
    cols = -(-total // (8 * LANE)) * LANE

    def pack(parts_of):
        flat = jnp.concatenate([parts_of[k].reshape(-1).astype(f32) for k in small])
        return _pad_to(flat, 8 * cols, 0).reshape(8, cols)

    (all_parts,) = _comm_only("gather_small_grads", _gather_comm([pack(small_g)], after=from_chips))
    res = _adam_small("adam_small", pack(w_of), pack(m_of), pack(v_of), all_parts)
    offs = 0
    flat_res = [a.reshape(-1) for a in res]
    for k, sz in zip(small, sizes):
        out[k] = [a[offs:offs + sz].reshape(w_of[k].shape) for a in flat_res]
        offs += sz

    outs = [loss, dx0.reshape(x.shape)]
    for j in range(4):
        outs += [out[k][j] for k in names]
    return tuple(outs)
```
